```python
import jax, jax.numpy as jnp
from jax import lax
import numpy as np

D_MODEL = 1024
BATCH = 32
SEQ = 256
DEPTH = 2
DEC_BATCH = 2
DEC_SEQ = 2048
PAST_LEN = 512

GRID_W = 64
HEAD_DIM = 64
NA_HEADS = 8
NA_WIN_R = 8
NA_WIN_C = 16
NA_QC = 16
NA_KC = NA_QC + NA_WIN_C
GQA_HEADS = 8
GQA_KV_HEADS = 2
QKV_WIDTH = 3 * NA_HEADS * HEAD_DIM + (GQA_HEADS + 2 * GQA_KV_HEADS) * HEAD_DIM
ATTN_OUT_WIDTH = (NA_HEADS + GQA_HEADS) * HEAD_DIM
ROPE_BASE = 10000.0
Q_BLOCK = 128
SGU_CHUNK = 128
SGU_GROUPS = 8
SGU_WIDTH = 2 * D_MODEL
FFN_HIDDEN = 2816
N_MOD = 9
N_ATTN_LAYERS = (DEPTH + 1) // 2
N_SGU_LAYERS = DEPTH // 2
EPS = 1e-6
NEG_INF = -1e30

kernel_name = 'hybrid_na_gqa_gmlp_diffusion_step'


def rms_norm(x, g):
    xf = x.astype(jnp.float32)
    y = xf * lax.rsqrt(jnp.mean(xf * xf, axis=-1, keepdims=True) + EPS)
    return (y * g.astype(jnp.float32)).astype(x.dtype)


def modulation(cond, w, b):
    m = jax.nn.silu(cond) @ w + b
    return m.reshape(cond.shape[0], 1, N_MOD, D_MODEL)


def adaln_input(h, mod, j, g):
    return rms_norm(h, g) * (1 + mod[:, :, 3 * j + 1]) + mod[:, :, 3 * j]


def adaln_gate(mod, j):
    return mod[:, :, 3 * j + 2]


def swiglu(x, w_gu, w_down):
    gate, up = jnp.split(x @ w_gu, 2, axis=-1)
    return (jax.nn.silu(gate) * up) @ w_down


def axial_rope(x):
    T, D = x.shape[1], x.shape[3]
    t = jnp.arange(T)
    half = D // 2
    freqs = ROPE_BASE ** (-jnp.arange(0, half, 2, dtype=jnp.float32) / half)

    def rotate(xa, pos):
        ang = pos.astype(jnp.float32)[:, None] * freqs[None, :]
        cos = jnp.cos(ang)[None, :, None, :]
        sin = jnp.sin(ang)[None, :, None, :]
        x1, x2 = jnp.split(xa.astype(jnp.float32), 2, axis=-1)
        return jnp.concatenate([x1 * cos - x2 * sin, x1 * sin + x2 * cos], axis=-1)

    out = jnp.concatenate([rotate(x[..., :half], t // GRID_W), rotate(x[..., half:], t % GRID_W)], axis=-1)
    return out.astype(x.dtype)


def attn_project(xn, w_in, na_q_g, na_k_g, gqa_q_g, gqa_k_g):
    B, S, _ = xn.shape
    na_w = NA_HEADS * HEAD_DIM
    q_w = GQA_HEADS * HEAD_DIM
    kv_w = GQA_KV_HEADS * HEAD_DIM
    splits = [na_w, 2 * na_w, 3 * na_w, 3 * na_w + q_w, 3 * na_w + q_w + kv_w]
    qa, ka, va, qb, kb, vb = jnp.split(xn @ w_in, splits, axis=-1)
    heads = lambda t: t.reshape(B, S, -1, HEAD_DIM)
    return (rms_norm(heads(qa), na_q_g), rms_norm(heads(ka), na_k_g), heads(va),
            rms_norm(heads(qb), gqa_q_g), rms_norm(heads(kb), gqa_k_g), heads(vb))


def dense_attention(q, k, v):
    B, S, Hq, D = q.shape
    Hk = k.shape[2]
    G = Hq // Hk
    nb = S // Q_BLOCK
    qb = (q * D ** -0.5).reshape(B, nb, Q_BLOCK, Hk, G, D).transpose(1, 0, 2, 3, 4, 5)

    def block(qblk):
        s = jnp.einsum('bqkgd,bskd->bkgqs', qblk, k).astype(jnp.float32)
        p = jax.nn.softmax(s, axis=-1).astype(v.dtype)
        return jnp.einsum('bkgqs,bskd->bqkgd', p, v)

    o = lax.map(block, qb)
    return o.transpose(1, 0, 2, 3, 4, 5).reshape(B, S, Hq, D)


def neighbourhood_attention(q, k, v, ctx_k, ctx_v, rpb):
    B, T, H, D = q.shape
    rows = T // GRID_W
    wr = min(NA_WIN_R, rows)
    ncb = GRID_W // NA_QC
    r = jnp.arange(rows)
    row_idx = jnp.clip(r - wr // 2, 0, rows - wr)[:, None] + jnp.arange(wr)[None, :]
    cb = jnp.arange(ncb)
    col_idx = jnp.clip(cb * NA_QC - NA_WIN_C // 2, 0, GRID_W - NA_KC)[:, None] + jnp.arange(NA_KC)[None, :]
    q_col = cb[:, None] * NA_QC + jnp.arange(NA_QC)[None, :]
    win_start = jnp.clip(q_col - NA_WIN_C // 2, 0, GRID_W - NA_WIN_C)
    kc = col_idx[:, None, :]
    ws = win_start[:, :, None]
    col_ok = (kc >= ws) & (kc < ws + NA_WIN_C)
    dr = row_idx - r[:, None] + NA_WIN_R - 1
    dc = jnp.clip(kc - q_col[:, :, None] + NA_WIN_C - 1, 0, 2 * NA_WIN_C - 2)
    bias = rpb[:, dr[:, None, None, :, None], dc[None, :, :, None, :]]
    bias = jnp.where(col_ok[None, None, :, :, None, :], bias.astype(jnp.float32), NEG_INF)
    n_loc = wr * NA_KC
    bias = bias.reshape(H, rows, ncb, NA_QC, n_loc).transpose(1, 2, 0, 3, 4)

    qg = (q * D ** -0.5).reshape(B, rows, ncb, NA_QC, H, D)
    kg = k.reshape(B, rows, GRID_W, H, D)
    vg = v.reshape(B, rows, GRID_W, H, D)
    ri = row_idx[:, None, :, None]
    ci = col_idx[None, :, None, :]
    kb = kg[:, ri, ci].reshape(B, rows, ncb, n_loc, H, D)
    vb = vg[:, ri, ci].reshape(B, rows, ncb, n_loc, H, D)
    s_loc = jnp.einsum('brnqhd,brnkhd->brnhqk', qg, kb).astype(jnp.float32) + bias[None]
    s_ctx = jnp.einsum('brnqhd,blhd->brnhql', qg, ctx_k).astype(jnp.float32)
    p = jax.nn.softmax(jnp.concatenate([s_loc, s_ctx], axis=-1), axis=-1).astype(v.dtype)
    o = (jnp.einsum('brnhqk,brnkhd->brnqhd', p[..., :n_loc], vb)
         + jnp.einsum('brnhql,blhd->brnqhd', p[..., n_loc:], ctx_v))
    return o.reshape(B, T, H, D)


def merge_heads(oa, ob, w_out):
    B, S = oa.shape[0], oa.shape[1]
    return jnp.concatenate([oa.reshape(B, S, -1), ob.reshape(B, S, -1)], axis=-1) @ w_out


def sgu_mixer(xn, w_in, v_g, w_s, b_s, w_out):
    B, S, _ = xn.shape
    u, v = jnp.split(jax.nn.gelu(xn @ w_in, approximate=False), 2, axis=-1)
    v = rms_norm(v, v_g)
    nc = S // SGU_CHUNK
    gw = SGU_WIDTH // SGU_GROUPS
    vg = v.reshape(B, nc, SGU_CHUNK, SGU_GROUPS, gw)
    sv = jnp.einsum('gts,bcsgd->bctgd', w_s, vg) + b_s.T[None, None, :, :, None]
    gated = u.reshape(B, nc, SGU_CHUNK, SGU_GROUPS, gw) * sv
    return gated.reshape(B, S, SGU_WIDTH) @ w_out


def setup_inputs(seed: int = 0) -> dict:
    key = jax.random.key(seed)
    ks = jax.random.split(key, 27)
    nrm = lambda k, shape, scale: jax.random.normal(k, shape, jnp.float32) * scale
    D = D_MODEL
    return {
        'x_prompt': nrm(ks[0], (BATCH, SEQ, D), 1.0),
        'x_sample': nrm(ks[1], (DEC_BATCH, DEC_SEQ, D), 1.0),
        'cache_na_k': nrm(ks[2], (DEC_BATCH, N_ATTN_LAYERS, PAST_LEN, NA_HEADS, HEAD_DIM), 1.0),
        'cache_na_v': nrm(ks[3], (DEC_BATCH, N_ATTN_LAYERS, PAST_LEN, NA_HEADS, HEAD_DIM), 1.0),
        'cache_gqa_k': nrm(ks[4], (DEC_BATCH, N_ATTN_LAYERS, PAST_LEN, GQA_KV_HEADS, HEAD_DIM), 1.0),
        'cache_gqa_v': nrm(ks[5], (DEC_BATCH, N_ATTN_LAYERS, PAST_LEN, GQA_KV_HEADS, HEAD_DIM), 1.0),
        'c': nrm(ks[6], (DEC_BATCH, D), 1.0),
        'c_ctx': nrm(ks[7], (D,), 1.0),
        'norm_g': 1.0 + nrm(ks[8], (DEPTH, 3, D), 0.1),
        'mod_w': nrm(ks[9], (DEPTH, D, N_MOD * D), 0.5 * D ** -0.5),
        'mod_b': nrm(ks[10], (DEPTH, N_MOD * D), 0.01),
        'ffn1_w_gu': nrm(ks[11], (DEPTH, D, 2 * FFN_HIDDEN), D ** -0.5),
        'ffn1_w_down': nrm(ks[12], (DEPTH, FFN_HIDDEN, D), FFN_HIDDEN ** -0.5),
        'ffn2_w_gu': nrm(ks[13], (DEPTH, D, 2 * FFN_HIDDEN), D ** -0.5),
        'ffn2_w_down': nrm(ks[14], (DEPTH, FFN_HIDDEN, D), FFN_HIDDEN ** -0.5),
        'attn_w_in': nrm(ks[15], (N_ATTN_LAYERS, D, QKV_WIDTH), D ** -0.5),
        'attn_w_out': nrm(ks[16], (N_ATTN_LAYERS, ATTN_OUT_WIDTH, D), ATTN_OUT_WIDTH ** -0.5),
        'na_q_g': 1.0 + nrm(ks[17], (N_ATTN_LAYERS, HEAD_DIM), 0.1),
        'na_k_g': 1.0 + nrm(ks[18], (N_ATTN_LAYERS, HEAD_DIM), 0.1),
        'na_rpb': nrm(ks[19], (N_ATTN_LAYERS, NA_HEADS, 2 * NA_WIN_R - 1, 2 * NA_WIN_C - 1), 0.1),
        'gqa_q_g': 1.0 + nrm(ks[20], (N_ATTN_LAYERS, HEAD_DIM), 0.1),
        'gqa_k_g': 1.0 + nrm(ks[21], (N_ATTN_LAYERS, HEAD_DIM), 0.1),
        'sgu_w_in': nrm(ks[22], (N_SGU_LAYERS, D, 2 * SGU_WIDTH), D ** -0.5),
        'sgu_v_g': 1.0 + nrm(ks[23], (N_SGU_LAYERS, SGU_WIDTH), 0.1),
        'sgu_w_s': nrm(ks[24], (N_SGU_LAYERS, SGU_GROUPS, SGU_CHUNK, SGU_CHUNK), 0.5 * SGU_CHUNK ** -0.5),
        'sgu_b_s': 1.0 + nrm(ks[25], (N_SGU_LAYERS, SGU_GROUPS, SGU_CHUNK), 0.1),
        'sgu_w_out': nrm(ks[26], (N_SGU_LAYERS, SGU_WIDTH, D), SGU_WIDTH ** -0.5),
    }


def reference(x_prompt, x_sample, cache_na_k, cache_na_v, cache_gqa_k, cache_gqa_v, c, c_ctx,
              norm_g, mod_w, mod_b, ffn1_w_gu, ffn1_w_down, ffn2_w_gu, ffn2_w_down,
              attn_w_in, attn_w_out, na_q_g, na_k_g, na_rpb, gqa_q_g, gqa_k_g,
              sgu_w_in, sgu_v_g, sgu_w_s, sgu_b_s, sgu_w_out):
    hp = x_prompt
    hs = x_sample
    na_k_list, na_v_list, gqa_k_list, gqa_v_list = [], [], [], []
    for i in range(DEPTH):
        m_ctx = modulation(c_ctx[None, :], mod_w[i], mod_b[i])
        m_lat = modulation(c, mod_w[i], mod_b[i])

        hp = hp + 0.5 * adaln_gate(m_ctx, 0) * swiglu(adaln_input(hp, m_ctx, 0, norm_g[i, 0]), ffn1_w_gu[i], ffn1_w_down[i])
        hs = hs + 0.5 * adaln_gate(m_lat, 0) * swiglu(adaln_input(hs, m_lat, 0, norm_g[i, 0]), ffn1_w_gu[i], ffn1_w_down[i])

        xp = adaln_input(hp, m_ctx, 1, norm_g[i, 1])
        xs = adaln_input(hs, m_lat, 1, norm_g[i, 1])
        if i % 2 == 0:
            e = i // 2
            qa, ka, va, qb, kb, vb = attn_project(xp, attn_w_in[e], na_q_g[e], na_k_g[e], gqa_q_g[e], gqa_k_g[e])
            out_p = merge_heads(dense_attention(qa, ka, va), dense_attention(qb, kb, vb), attn_w_out[e])
            na_k_list.append(ka)
            na_v_list.append(va)
            gqa_k_list.append(kb)
            gqa_v_list.append(vb)
            qa, ka, va, qb, kb, vb = attn_project(xs, attn_w_in[e], na_q_g[e], na_k_g[e], gqa_q_g[e], gqa_k_g[e])
            oa = neighbourhood_attention(qa, ka, va, cache_na_k[:, e], cache_na_v[:, e], na_rpb[e])
            ob = dense_attention(axial_rope(qb),
                                 jnp.concatenate([axial_rope(kb), cache_gqa_k[:, e]], axis=1),
                                 jnp.concatenate([vb, cache_gqa_v[:, e]], axis=1))
            out_s = merge_heads(oa, ob, attn_w_out[e])
        else:
            o = i // 2
            out_p = sgu_mixer(xp, sgu_w_in[o], sgu_v_g[o], sgu_w_s[o], sgu_b_s[o], sgu_w_out[o])
            out_s = sgu_mixer(xs, sgu_w_in[o], sgu_v_g[o], sgu_w_s[o], sgu_b_s[o], sgu_w_out[o])
        hp = hp + adaln_gate(m_ctx, 1) * out_p
        hs = hs + adaln_gate(m_lat, 1) * out_s

        hp = hp + 0.5 * adaln_gate(m_ctx, 2) * swiglu(adaln_input(hp, m_ctx, 2, norm_g[i, 2]), ffn2_w_gu[i], ffn2_w_down[i])
        hs = hs + 0.5 * adaln_gate(m_lat, 2) * swiglu(adaln_input(hs, m_lat, 2, norm_g[i, 2]), ffn2_w_gu[i], ffn2_w_down[i])

    new_na_k = jnp.stack(na_k_list, axis=1)
    new_na_v = jnp.stack(na_v_list, axis=1)
    new_gqa_k = jnp.stack(gqa_k_list, axis=1)
    new_gqa_v = jnp.stack(gqa_v_list, axis=1)
    return (hp, hs, new_na_k, new_na_v, new_gqa_k, new_gqa_v)
```

```python
import functools
import math

import jax
import jax.numpy as jnp
from jax import lax
from jax.experimental import pallas as pl
from jax.experimental.pallas import tpu as pltpu

D_MODEL = 1024
DEPTH = 2
GRID_W = 64
HEAD_DIM = 64
NA_HEADS = 8
NA_WIN_R = 8
NA_WIN_C = 16
GQA_HEADS = 8
GQA_KV_HEADS = 2
GQA_GROUP = GQA_HEADS // GQA_KV_HEADS
NA_W = NA_HEADS * HEAD_DIM
GQ_W = GQA_HEADS * HEAD_DIM
GKV_W = GQA_KV_HEADS * HEAD_DIM
QKV_WIDTH = 3 * NA_W + GQ_W + 2 * GKV_W
ROPE_BASE = 10000.0
SGU_CHUNK = 128
SGU_GROUPS = 8
SGU_WIDTH = 2 * D_MODEL
SGU_GW = SGU_WIDTH // SGU_GROUPS
FFN_HIDDEN = 2816
N_MOD = 9
EPS = 1e-6
NEG_INF = -1e30
Q_SCALE = HEAD_DIM ** -0.5

NA_BAND_ROWS = 4
NA_BAND_KEY_ROWS = 12
NA_BAND_Q = NA_BAND_ROWS * GRID_W
NA_BAND_K = NA_BAND_KEY_ROWS * GRID_W

MXU_COLS = 256
VMEM_LIMIT = 56 * 1024 * 1024

BF16 = jnp.bfloat16
F32 = jnp.float32


def _dot(a, b):
    return jnp.dot(a, b, preferred_element_type=F32)


def _dot_nt(a, b):
    return lax.dot_general(a, b, (((1,), (1,)), ((), ())), preferred_element_type=F32)


def _params(*sem):
    return pltpu.CompilerParams(dimension_semantics=sem, vmem_limit_bytes=VMEM_LIMIT)


def _adaln(x, g, mod_ref, j):
    shift = mod_ref[0, 3 * j:3 * j + 1, :]
    scale = mod_ref[0, 3 * j + 1:3 * j + 2, :]
    y = x * lax.rsqrt(jnp.mean(x * x, axis=-1, keepdims=True) + EPS)
    return (y * g) * (1.0 + scale) + shift


def _gate(mod_ref, j):
    return mod_ref[0, 3 * j + 2:3 * j + 3, :]


def _mod_kernel(c_ref, w_ref, b_ref, o_ref):
    c = c_ref[...]
    a = (c * jax.nn.sigmoid(c)).astype(BF16)
    o_ref[...] = _dot(a, w_ref[...].astype(BF16)) + b_ref[...]


def _modulation(cond8, mod_w, mod_b):
    tn = 1024
    n = N_MOD * D_MODEL
    return pl.pallas_call(
        _mod_kernel,
        grid=(DEPTH, n // tn),
        in_specs=[
            pl.BlockSpec((8, D_MODEL), lambda l, k: (0, 0)),
            pl.BlockSpec((None, D_MODEL, tn), lambda l, k: (l, 0, k)),
            pl.BlockSpec((None, 1, tn), lambda l, k: (l, 0, k)),
        ],
        out_specs=pl.BlockSpec((None, 8, tn), lambda l, k: (l, 0, k)),
        out_shape=jax.ShapeDtypeStruct((DEPTH, 8, n), F32),
        compiler_params=_params("arbitrary", "arbitrary"),
        name="modulation",
    )(cond8, mod_w, mod_b.reshape(DEPTH, 1, n))


def _ffn_kernel(h_ref, mod_ref, g_ref, wg_ref, wu_ref, wd_ref, o_ref, xn_ref, acc_ref, *, j, nk):
    k = pl.program_id(1)

    @pl.when(k == 0)
    def _():
        xn_ref[...] = _adaln(h_ref[...], g_ref[...], mod_ref, j).astype(BF16)
        acc_ref[...] = jnp.zeros_like(acc_ref)

    xn = xn_ref[...]
    gate = _dot(xn, wg_ref[...].astype(BF16))
    up = _dot(xn, wu_ref[...].astype(BF16))
    a = ((gate * jax.nn.sigmoid(gate)) * up).astype(BF16)
    acc_ref[...] += _dot(a, wd_ref[...].astype(BF16))

    @pl.when(k == nk - 1)
    def _():
        o_ref[...] = h_ref[...] + (0.5 * _gate(mod_ref, j)) * acc_ref[...]


def _ffn(h, mod, g, w_gu, w_down, layer, j, cond_div):
    t = h.shape[0]
    tm, th = 1024, 256
    nk = FFN_HIDDEN // th
    return pl.pallas_call(
        functools.partial(_ffn_kernel, j=j, nk=nk),
        grid=(t // tm, nk),
        in_specs=[
            pl.BlockSpec((tm, D_MODEL), lambda i, k: (i, 0)),
            pl.BlockSpec((1, N_MOD, D_MODEL), lambda i, k: (i // cond_div, 0, 0)),
            pl.BlockSpec((1, D_MODEL), lambda i, k: (0, 0)),
            pl.BlockSpec((None, D_MODEL, th), lambda i, k: (layer, 0, k)),
            pl.BlockSpec((None, D_MODEL, th), lambda i, k: (layer, 0, k + nk)),
            pl.BlockSpec((None, th, D_MODEL), lambda i, k: (layer, k, 0)),
        ],
        out_specs=pl.BlockSpec((tm, D_MODEL), lambda i, k: (i, 0)),
        out_shape=jax.ShapeDtypeStruct((t, D_MODEL), F32),
        scratch_shapes=[pltpu.VMEM((tm, D_MODEL), BF16), pltpu.VMEM((tm, D_MODEL), F32)],
        compiler_params=_params("arbitrary", "arbitrary"),
        name=f"ffn{j}_l{layer}",
    )(h, mod, g, w_gu, w_gu, w_down)


def _head_mean_sq(x):
    w = x.shape[-1]
    r = lax.broadcasted_iota(jnp.int32, (MXU_COLS, MXU_COLS), 0) // HEAD_DIM
    c = lax.broadcasted_iota(jnp.int32, (MXU_COLS, MXU_COLS), 1) // HEAD_DIM
    bd = jnp.where(r == c, 1.0 / HEAD_DIM, 0.0).astype(BF16)
    sq = (x * x).astype(BF16)
    parts = []
    for s in range(0, w, MXU_COLS):
        e = min(s + MXU_COLS, w)
        parts.append(_dot(sq[:, s:e], bd[:e - s, :e - s]))
    return parts[0] if len(parts) == 1 else jnp.concatenate(parts, axis=-1)


def _head_rms(x, gain):
    return (x * lax.rsqrt(_head_mean_sq(x) + EPS)) * gain


def _rope(x, cos, sin_signed):
    lanes = cos.shape[-1]
    outs = []
    for s in range(0, x.shape[-1], lanes):
        xs = x[:, s:s + lanes]
        blk = lax.broadcasted_iota(jnp.int32, xs.shape, 1) // (HEAD_DIM // 4)
        partner = jnp.where(blk % 2 == 0,
                            pltpu.roll(xs, lanes - HEAD_DIM // 4, 1),
                            pltpu.roll(xs, HEAD_DIM // 4, 1))
        outs.append(xs * cos + partner * sin_signed)
    return outs[0] if len(outs) == 1 else jnp.concatenate(outs, axis=-1)


def _qkv_kernel(*refs, rope, cache_out):
    h_ref, mod_ref, g_ref, w_ref, gqa_ref, gka_ref, gqb_ref, gkb_ref = refs[:8]
    pos = 8
    if rope:
        cos_ref, sin_ref = refs[pos:pos + 2]
        pos += 2
    q_ref, kva_ref, kvb_ref = refs[pos:pos + 3]
    pos += 3

    xn = _adaln(h_ref[...], g_ref[...], mod_ref, 1).astype(BF16)
    y = _dot(xn, w_ref[...].astype(BF16))
    o = 0
    qa = _head_rms(y[:, o:o + NA_W], gqa_ref[...]); o += NA_W
    ka = _head_rms(y[:, o:o + NA_W], gka_ref[...]); o += NA_W
    va = y[:, o:o + NA_W]; o += NA_W
    qb = _head_rms(y[:, o:o + GQ_W], gqb_ref[...]); o += GQ_W
    kb = _head_rms(y[:, o:o + GKV_W], gkb_ref[...]); o += GKV_W
    vb = y[:, o:o + GKV_W]
    if cache_out:
        ka_ref, va_ref, kb_ref, vb_ref = refs[pos:pos + 4]
        ka_ref[...] = ka
        va_ref[...] = va
        kb_ref[...] = kb
        vb_ref[...] = vb
    if rope:
        cos, sin = cos_ref[...], sin_ref[...]
        qb = _rope(qb, cos, sin)
        kb = _rope(kb, cos, sin)
    q_ref[:, :NA_W] = (qa * Q_SCALE).astype(BF16)
    q_ref[:, NA_W:] = (qb * Q_SCALE).astype(BF16)
    kva_ref[:, :NA_W] = ka.astype(BF16)
    kva_ref[:, NA_W:] = va.astype(BF16)
    kvb_ref[:, :GKV_W] = kb.astype(BF16)
    kvb_ref[:, GKV_W:] = vb.astype(BF16)


def _qkv(h, mod, g, w_in, gains, cond_div, rope_tabs=None, cache_out=False):
    t = h.shape[0]
    tm = 512
    rope = rope_tabs is not None
    row = lambda i: (i, 0)
    const = lambda i: (0, 0)
    in_specs = [
        pl.BlockSpec((tm, D_MODEL), row),
        pl.BlockSpec((1, N_MOD, D_MODEL), lambda i: (i // cond_div, 0, 0)),
        pl.BlockSpec((1, D_MODEL), const),
        pl.BlockSpec((None, D_MODEL, QKV_WIDTH), lambda i: (0, 0, 0)),
        pl.BlockSpec((1, NA_W), const),
        pl.BlockSpec((1, NA_W), const),
        pl.BlockSpec((1, GQ_W), const),
        pl.BlockSpec((1, GKV_W), const),
    ]
    args = [h, mod, g, w_in, *gains]
    if rope:
        seq_tiles = rope_tabs[0].shape[0] // tm
        in_specs += [pl.BlockSpec((tm, 2 * HEAD_DIM), lambda i: (i % seq_tiles, 0))] * 2
        args += list(rope_tabs)
    out_specs = [pl.BlockSpec((tm, NA_W + GQ_W), row), pl.BlockSpec((tm, 2 * NA_W), row),
                 pl.BlockSpec((tm, 2 * GKV_W), row)]
    out_shape = [jax.ShapeDtypeStruct((t, NA_W + GQ_W), BF16), jax.ShapeDtypeStruct((t, 2 * NA_W), BF16),
                 jax.ShapeDtypeStruct((t, 2 * GKV_W), BF16)]
    if cache_out:
        out_specs += [pl.BlockSpec((tm, NA_W), row), pl.BlockSpec((tm, NA_W), row),
                      pl.BlockSpec((tm, GKV_W), row), pl.BlockSpec((tm, GKV_W), row)]
        out_shape += [jax.ShapeDtypeStruct((t, NA_W), F32), jax.ShapeDtypeStruct((t, NA_W), F32),
                      jax.ShapeDtypeStruct((t, GKV_W), F32), jax.ShapeDtypeStruct((t, GKV_W), F32)]
    return pl.pallas_call(
        functools.partial(_qkv_kernel, rope=rope, cache_out=cache_out),
        grid=(t // tm,),
        in_specs=in_specs,
        out_specs=out_specs,
        out_shape=out_shape,
        compiler_params=_params("arbitrary"),
        name="qkv_latent" if rope else "qkv_context",
    )(*args)


def _softmax_pv(scores, values):
    m = functools.reduce(jnp.maximum, [jnp.max(s, axis=-1, keepdims=True) for s in scores])
    ps = [jnp.exp(s - m) for s in scores]
    l = functools.reduce(jnp.add, [jnp.sum(p, axis=-1, keepdims=True) for p in ps])
    o = functools.reduce(jnp.add, [_dot(p.astype(BF16), v) for p, v in zip(ps, values)])
    return o * (1.0 / l)


def _ctx_attn_kernel(q_ref, kva_ref, kvb_ref, o_ref):
    tq = q_ref.shape[0]
    for h in range(NA_HEADS):
        sl = slice(h * HEAD_DIM, (h + 1) * HEAD_DIM)
        q = q_ref[:, sl]
        k = kva_ref[:, sl]
        v = kva_ref[:, NA_W + h * HEAD_DIM:NA_W + (h + 1) * HEAD_DIM]
        o_ref[:, sl] = _softmax_pv([_dot_nt(q, k)], [v]).astype(o_ref.dtype)
    for g in range(GQA_KV_HEADS):
        heads = [NA_W + (g * GQA_GROUP + i) * HEAD_DIM for i in range(GQA_GROUP)]
        q = jnp.concatenate([q_ref[:, o:o + HEAD_DIM] for o in heads], axis=0)
        k = kvb_ref[:, g * HEAD_DIM:(g + 1) * HEAD_DIM]
        v = kvb_ref[:, GKV_W + g * HEAD_DIM:GKV_W + (g + 1) * HEAD_DIM]
        o = _softmax_pv([_dot_nt(q, k)], [v]).astype(o_ref.dtype)
        for i, off in enumerate(heads):
            o_ref[:, off:off + HEAD_DIM] = o[i * tq:(i + 1) * tq]


def _ctx_attention(q, kva, kvb, seq):
    t = q.shape[0]
    row = lambda b: (b, 0)
    return pl.pallas_call(
        _ctx_attn_kernel,
        grid=(t // seq,),
        in_specs=[pl.BlockSpec((seq, NA_W + GQ_W), row), pl.BlockSpec((seq, 2 * NA_W), row),
                  pl.BlockSpec((seq, 2 * GKV_W), row)],
        out_specs=pl.BlockSpec((seq, NA_W + GQ_W), row),
        out_shape=jax.ShapeDtypeStruct((t, NA_W + GQ_W), BF16),
        compiler_params=_params("arbitrary"),
        name="ctx_attention",
    )(q, kva, kvb)


def _na_band_key_row0(r0, rows):
    return jnp.clip(r0 - NA_WIN_R // 2, 0, rows - NA_BAND_KEY_ROWS)


def _na_kernel(q_ref, kva_ref, ck_ref, cv_ref, bias_ref, o_ref, *, rows):
    j = pl.program_id(1)
    k0 = pl.multiple_of(_na_band_key_row0(j * NA_BAND_ROWS, rows) * GRID_W, GRID_W)
    band = pl.ds(k0, NA_BAND_K)
    for h in range(NA_HEADS):
        sl = slice(h * HEAD_DIM, (h + 1) * HEAD_DIM)
        vsl = slice(NA_W + h * HEAD_DIM, NA_W + (h + 1) * HEAD_DIM)
        q = q_ref[:, sl]
        s_loc = _dot_nt(q, kva_ref[band, sl]) + bias_ref[h]
        s_ctx = _dot_nt(q, ck_ref[:, sl])
        o_ref[:, sl] = _softmax_pv([s_loc, s_ctx], [kva_ref[band, vsl], cv_ref[:, sl]]).astype(o_ref.dtype)


def _na_bias(rpb, rows):
    out = []
    nb = rows // NA_BAND_ROWS
    for band in (0, 1, nb - 1):
        r = band * NA_BAND_ROWS + jnp.arange(NA_BAND_ROWS)
        start = jnp.clip(r - NA_WIN_R // 2, 0, rows - NA_WIN_R)
        kr = _na_band_key_row0(band * NA_BAND_ROWS, rows) + jnp.arange(NA_BAND_KEY_ROWS)
        row_ok = (kr[None, :] >= start[:, None]) & (kr[None, :] < start[:, None] + NA_WIN_R)
        dr = jnp.clip(kr[None, :] - r[:, None] + NA_WIN_R - 1, 0, 2 * NA_WIN_R - 2)
        qc = jnp.arange(GRID_W)
        kc = jnp.arange(GRID_W)
        ws = jnp.clip(qc - NA_WIN_C // 2, 0, GRID_W - NA_WIN_C)
        col_ok = (kc[None, :] >= ws[:, None]) & (kc[None, :] < ws[:, None] + NA_WIN_C)
        dc = jnp.clip(kc[None, :] - qc[:, None] + NA_WIN_C - 1, 0, 2 * NA_WIN_C - 2)
        b = rpb[:, dr[:, None, :, None], dc[None, :, None, :]].astype(F32)
        ok = row_ok[:, None, :, None] & col_ok[None, :, None, :]
        out.append(jnp.where(ok[None], b, NEG_INF).reshape(NA_HEADS, NA_BAND_Q, NA_BAND_K))
    return jnp.stack(out)


def _na_attention(q, kva, ctx_k, ctx_v, bias, batch, seq):
    rows = seq // GRID_W
    nb = rows // NA_BAND_ROWS
    past = ctx_k.shape[1]
    band_type = lambda j: jnp.where(j == 0, 0, jnp.where(j == nb - 1, 2, 1))
    return pl.pallas_call(
        functools.partial(_na_kernel, rows=rows),
        grid=(batch, nb),
        in_specs=[
            pl.BlockSpec((NA_BAND_Q, NA_W), lambda b, j: (b * nb + j, 0)),
            pl.BlockSpec((seq, 2 * NA_W), lambda b, j: (b, 0)),
            pl.BlockSpec((None, past, NA_W), lambda b, j: (b, 0, 0)),
            pl.BlockSpec((None, past, NA_W), lambda b, j: (b, 0, 0)),
            pl.BlockSpec((None, NA_HEADS, NA_BAND_Q, NA_BAND_K), lambda b, j: (band_type(j), 0, 0, 0)),
        ],
        out_specs=pl.BlockSpec((NA_BAND_Q, NA_W), lambda b, j: (b * nb + j, 0)),
        out_shape=jax.ShapeDtypeStruct((batch * seq, NA_W), BF16),
        compiler_params=_params("arbitrary", "arbitrary"),
        name="na_attention",
    )(q, kva, ctx_k, ctx_v, bias)


def _gqa_kernel(q_ref, kvb_ref, ck_ref, cv_ref, o_ref):
    tq = q_ref.shape[0]
    for g in range(GQA_KV_HEADS):
        heads = [(g * GQA_GROUP + i) * HEAD_DIM for i in range(GQA_GROUP)]
        q = jnp.concatenate([q_ref[:, o:o + HEAD_DIM] for o in heads], axis=0)
        ksl = slice(g * HEAD_DIM, (g + 1) * HEAD_DIM)
        vsl = slice(GKV_W + g * HEAD_DIM, GKV_W + (g + 1) * HEAD_DIM)
        o = _softmax_pv([_dot_nt(q, kvb_ref[:, ksl]), _dot_nt(q, ck_ref[:, ksl])],
                        [kvb_ref[:, vsl], cv_ref[:, ksl]]).astype(o_ref.dtype)
        for i, off in enumerate(heads):
            o_ref[:, off:off + HEAD_DIM] = o[i * tq:(i + 1) * tq]


def _gqa_attention(q, kvb, ctx_k, ctx_v, batch, seq):
    tq = 128
    nt = seq // tq
    past = ctx_k.shape[1]
    return pl.pallas_call(
        _gqa_kernel,
        grid=(batch, nt),
        in_specs=[
            pl.BlockSpec((tq, GQ_W), lambda b, i: (b * nt + i, 1)),
            pl.BlockSpec((seq, 2 * GKV_W), lambda b, i: (b, 0)),
            pl.BlockSpec((None, past, GKV_W), lambda b, i: (b, 0, 0)),
            pl.BlockSpec((None, past, GKV_W), lambda b, i: (b, 0, 0)),
        ],
        out_specs=pl.BlockSpec((tq, GQ_W), lambda b, i: (b * nt + i, 0)),
        out_shape=jax.ShapeDtypeStruct((batch * seq, GQ_W), BF16),
        compiler_params=_params("arbitrary", "arbitrary"),
        name="gqa_attention",
    )(q, kvb, ctx_k, ctx_v)


def _outproj_kernel(*refs, n_in):
    h_ref, mod_ref = refs[:2]
    o_refs = refs[2:2 + n_in]
    w_ref, out_ref = refs[2 + n_in:]
    acc = None
    row = 0
    for o_ref in o_refs:
        width = o_ref.shape[1]
        part = _dot(o_ref[...], w_ref[row:row + width, :].astype(BF16))
        acc = part if acc is None else acc + part
        row += width
    out_ref[...] = h_ref[...] + _gate(mod_ref, 1) * acc


def _outproj(h, mod, heads_out, w_out, cond_div):
    t = h.shape[0]
    tm = 512
    row = lambda i: (i, 0)
    n_in = len(heads_out)
    return pl.pallas_call(
        functools.partial(_outproj_kernel, n_in=n_in),
        grid=(t // tm,),
        in_specs=[pl.BlockSpec((tm, D_MODEL), row),
                  pl.BlockSpec((1, N_MOD, D_MODEL), lambda i: (i // cond_div, 0, 0))]
                 + [pl.BlockSpec((tm, o.shape[1]), row) for o in heads_out]
                 + [pl.BlockSpec((None, D_MODEL, D_MODEL), lambda i: (0, 0, 0))],
        out_specs=pl.BlockSpec((tm, D_MODEL), row),
        out_shape=jax.ShapeDtypeStruct((t, D_MODEL), F32),
        compiler_params=_params("arbitrary"),
        name="attn_outproj",
    )(h, mod, *heads_out, w_out)


def _sgu_kernel(h_ref, mod_ref, g_ref, win_ref, vg_ref, ws_ref, bs_ref, wout_ref, o_ref, gated_ref):
    tm = h_ref.shape[0]
    xn = _adaln(h_ref[...], g_ref[...], mod_ref, 1).astype(BF16)
    y = _dot(xn, win_ref[...].astype(BF16))
    y = 0.5 * y * (1.0 + lax.erf(y * math.sqrt(0.5)))
    u = y[:, :SGU_WIDTH]
    v = y[:, SGU_WIDTH:]
    v = (v * lax.rsqrt(jnp.mean(v * v, axis=-1, keepdims=True) + EPS)) * vg_ref[...]
    vb = v.astype(BF16)
    for c in range(tm // SGU_CHUNK):
        rs = slice(c * SGU_CHUNK, (c + 1) * SGU_CHUNK)
        for g in range(SGU_GROUPS):
            cs = slice(g * SGU_GW, (g + 1) * SGU_GW)
            sv = _dot(ws_ref[g].astype(BF16), vb[rs, cs]) + bs_ref[:, g:g + 1]
            gated_ref[rs, cs] = (u[rs, cs] * sv).astype(BF16)
    out = _dot(gated_ref[...], wout_ref[...].astype(BF16))
    o_ref[...] = h_ref[...] + _gate(mod_ref, 1) * out


def _sgu(h, mod, g, w_in, v_g, w_s, b_s_t, w_out, cond_div):
    t = h.shape[0]
    tm = 256
    row = lambda i: (i, 0)
    const = lambda i: (0, 0)
    once = pl.Buffered(1)
    return pl.pallas_call(
        _sgu_kernel,
        grid=(t // tm,),
        in_specs=[
            pl.BlockSpec((tm, D_MODEL), row),
            pl.BlockSpec((1, N_MOD, D_MODEL), lambda i: (i // cond_div, 0, 0)),
            pl.BlockSpec((1, D_MODEL), const),
            pl.BlockSpec((None, D_MODEL, 2 * SGU_WIDTH), lambda i: (0, 0, 0), pipeline_mode=once),
            pl.BlockSpec((1, SGU_WIDTH), const),
            pl.BlockSpec((None, SGU_GROUPS, SGU_CHUNK, SGU_CHUNK), lambda i: (0, 0, 0, 0)),
            pl.BlockSpec((SGU_CHUNK, SGU_GROUPS), const),
            pl.BlockSpec((None, SGU_WIDTH, D_MODEL), lambda i: (0, 0, 0), pipeline_mode=once),
        ],
        out_specs=pl.BlockSpec((tm, D_MODEL), row),
        out_shape=jax.ShapeDtypeStruct((t, D_MODEL), F32),
        scratch_shapes=[pltpu.VMEM((tm, SGU_WIDTH), BF16)],
        compiler_params=_params("arbitrary"),
        name="sgu",
    )(h, mod, g, w_in, v_g, w_s, b_s_t, w_out)


def _rope_tables(seq):
    half = HEAD_DIM // 2
    t = jnp.arange(seq)
    freqs = ROPE_BASE ** (-jnp.arange(0, half, 2, dtype=F32) / half)
    def tab(pos):
        ang = pos.astype(F32)[:, None] * freqs[None, :]
        cos, sin = jnp.cos(ang), jnp.sin(ang)
        return jnp.concatenate([cos, cos], -1), jnp.concatenate([-sin, sin], -1)
    cr, sr = tab(t // GRID_W)
    cc, sc = tab(t % GRID_W)
    cos = jnp.concatenate([cr, cc], -1)
    sin = jnp.concatenate([sr, sc], -1)
    return jnp.tile(cos, (1, 2)), jnp.tile(sin, (1, 2))


def kernel(x_prompt, x_sample, cache_na_k, cache_na_v, cache_gqa_k, cache_gqa_v, c, c_ctx, norm_g, mod_w, mod_b, ffn1_w_gu, ffn1_w_down, ffn2_w_gu, ffn2_w_down, attn_w_in, attn_w_out, na_q_g, na_k_g, na_rpb, gqa_q_g, gqa_k_g, sgu_w_in, sgu_v_g, sgu_w_s, sgu_b_s, sgu_w_out):
    batch, seq, _ = x_prompt.shape
    dec_batch, dec_seq, _ = x_sample.shape
    past = cache_na_k.shape[2]
    hp = x_prompt.reshape(batch * seq, D_MODEL)
    hs = x_sample.reshape(dec_batch * dec_seq, D_MODEL)

    cond8 = jnp.zeros((8, D_MODEL), F32).at[0].set(c_ctx).at[1:1 + dec_batch].set(c)
    mod_all = _modulation(cond8, mod_w, mod_b).reshape(DEPTH, 8, N_MOD, D_MODEL)

    new_cache = None
    for layer in range(DEPTH):
        mod_l = mod_all[layer]
        mods = (mod_l[0:1], mod_l[1:1 + dec_batch])
        g0, g1, g2 = (norm_g[layer, s][None, :] for s in range(3))

        hp = _ffn(hp, mods[0], g0, ffn1_w_gu, ffn1_w_down, layer, 0, hp.shape[0] // 1024)
        hs = _ffn(hs, mods[1], g0, ffn1_w_gu, ffn1_w_down, layer, 0, dec_seq // 1024)

        if layer % 2 == 0:
            e = layer // 2
            gains = (jnp.tile(na_q_g[e], NA_HEADS)[None], jnp.tile(na_k_g[e], NA_HEADS)[None],
                     jnp.tile(gqa_q_g[e], GQA_HEADS)[None], jnp.tile(gqa_k_g[e], GQA_KV_HEADS)[None])
            w_in = attn_w_in[e][None]
            w_out = attn_w_out[e][None]
            q, kva, kvb, ka, va, kb, vb = _qkv(hp, mods[0], g1, w_in, gains, hp.shape[0] // 512,
                                               cache_out=True)
            new_cache = (ka, va, kb, vb)
            o = _ctx_attention(q, kva, kvb, seq)
            hp = _outproj(hp, mods[0], [o], w_out, hp.shape[0] // 512)
            q, kva, kvb = _qkv(hs, mods[1], g1, w_in, gains, dec_seq // 512,
                               rope_tabs=_rope_tables(dec_seq))
            ck_a = cache_na_k[:, e].reshape(dec_batch, past, NA_W).astype(BF16)
            cv_a = cache_na_v[:, e].reshape(dec_batch, past, NA_W).astype(BF16)
            ck_b = cache_gqa_k[:, e].reshape(dec_batch, past, GKV_W).astype(BF16)
            cv_b = cache_gqa_v[:, e].reshape(dec_batch, past, GKV_W).astype(BF16)
            oa = _na_attention(q, kva, ck_a, cv_a, _na_bias(na_rpb[e], dec_seq // GRID_W), dec_batch, dec_seq)
            ob = _gqa_attention(q, kvb, ck_b, cv_b, dec_batch, dec_seq)
            hs = _outproj(hs, mods[1], [oa, ob], w_out, dec_seq // 512)
        else:
            o = layer // 2
            sgu_args = (g1, sgu_w_in[o][None], sgu_v_g[o][None], sgu_w_s[o][None], sgu_b_s[o].T,
                        sgu_w_out[o][None])
            hp = _sgu(hp, mods[0], *sgu_args, hp.shape[0] // 256)
            hs = _sgu(hs, mods[1], *sgu_args, dec_seq // 256)

        hp = _ffn(hp, mods[0], g2, ffn2_w_gu, ffn2_w_down, layer, 2, hp.shape[0] // 1024)
        hs = _ffn(hs, mods[1], g2, ffn2_w_gu, ffn2_w_down, layer, 2, dec_seq // 1024)

    ka, va, kb, vb = new_cache
    n_attn = (DEPTH + 1) // 2
    assert n_attn == 1
    return (hp.reshape(batch, seq, D_MODEL), hs.reshape(dec_batch, dec_seq, D_MODEL),
            ka.reshape(batch, n_attn, seq, NA_HEADS, HEAD_DIM), va.reshape(batch, n_attn, seq, NA_HEADS, HEAD_DIM),
            kb.reshape(batch, n_attn, seq, GQA_KV_HEADS, HEAD_DIM), vb.reshape(batch, n_attn, seq, GQA_KV_HEADS, HEAD_DIM))
```

```python
import functools
import math

import jax
import jax.numpy as jnp
import numpy as np
from jax import lax
from jax.experimental import pallas as pl
from jax.experimental.pallas import tpu as pltpu

D_MODEL = 1024
DEPTH = 2
GRID_W = 64
HEAD_DIM = 64
NA_HEADS = 8
NA_WIN_R = 8
NA_WIN_C = 16
GQA_HEADS = 8
GQA_KV_HEADS = 2
GQA_GROUP = GQA_HEADS // GQA_KV_HEADS
NA_W = NA_HEADS * HEAD_DIM
GQ_W = GQA_HEADS * HEAD_DIM
GKV_W = GQA_KV_HEADS * HEAD_DIM
QKV_WIDTH = 3 * NA_W + GQ_W + 2 * GKV_W
ROPE_BASE = 10000.0
SGU_CHUNK = 128
SGU_GROUPS = 8
SGU_WIDTH = 2 * D_MODEL
SGU_GW = SGU_WIDTH // SGU_GROUPS
FFN_HIDDEN = 2816
N_MOD = 9
EPS = 1e-6
NEG_INF = -1e30
Q_SCALE = HEAD_DIM ** -0.5

NA_BAND_ROWS = 4
NA_BAND_KEY_ROWS = 12
NA_BAND_Q = NA_BAND_ROWS * GRID_W
NA_BAND_K = NA_BAND_KEY_ROWS * GRID_W

MXU_COLS = 256
VMEM_LIMIT = 56 * 1024 * 1024

BF16 = jnp.bfloat16
F32 = jnp.float32


def _dot(a, b):
    return jnp.dot(a, b, preferred_element_type=F32)


def _dot_nt(a, b):
    return lax.dot_general(a, b, (((1,), (1,)), ((), ())), preferred_element_type=F32)


def _params(*sem):
    return pltpu.CompilerParams(dimension_semantics=sem, vmem_limit_bytes=VMEM_LIMIT)


def _adaln(x, g, mod_ref, j):
    shift = mod_ref[0, 3 * j:3 * j + 1, :]
    scale = mod_ref[0, 3 * j + 1:3 * j + 2, :]
    y = x * lax.rsqrt(jnp.mean(x * x, axis=-1, keepdims=True) + EPS)
    return (y * g) * (1.0 + scale) + shift


def _gate(mod_ref, j):
    return mod_ref[0, 3 * j + 2:3 * j + 3, :]


def _mod_kernel(c_ref, w_ref, b_ref, o_ref):
    c = c_ref[...]
    a = (c * jax.nn.sigmoid(c)).astype(BF16)
    o_ref[...] = _dot(a, w_ref[...].astype(BF16)) + b_ref[...]


def _modulation(cond8, mod_w, mod_b):
    tn = 1024
    n = N_MOD * D_MODEL
    return pl.pallas_call(
        _mod_kernel,
        grid=(DEPTH, n // tn),
        in_specs=[
            pl.BlockSpec((8, D_MODEL), lambda l, k: (0, 0)),
            pl.BlockSpec((None, D_MODEL, tn), lambda l, k: (l, 0, k)),
            pl.BlockSpec((None, 1, tn), lambda l, k: (l, 0, k)),
        ],
        out_specs=pl.BlockSpec((None, 8, tn), lambda l, k: (l, 0, k)),
        out_shape=jax.ShapeDtypeStruct((DEPTH, 8, n), F32),
        compiler_params=_params("arbitrary", "arbitrary"),
        name="modulation",
    )(cond8, mod_w, mod_b.reshape(DEPTH, 1, n))


def _ffn_kernel(h_ref, mod_ref, g_ref, wg_ref, wu_ref, wd_ref, o_ref, xn_ref, acc_ref, *, j, nk):
    k = pl.program_id(1)

    @pl.when(k == 0)
    def _():
        xn_ref[...] = _adaln(h_ref[...], g_ref[...], mod_ref, j).astype(BF16)
        acc_ref[...] = jnp.zeros_like(acc_ref)

    xn = xn_ref[...]
    gate = _dot(xn, wg_ref[...].astype(BF16))
    up = _dot(xn, wu_ref[...].astype(BF16))
    a = ((gate * jax.nn.sigmoid(gate)) * up).astype(BF16)
    acc_ref[...] += _dot(a, wd_ref[...].astype(BF16))

    @pl.when(k == nk - 1)
    def _():
        o_ref[...] = h_ref[...] + (0.5 * _gate(mod_ref, j)) * acc_ref[...]


def _ffn(h, mod, g, w_gu, w_down, layer, j, cond_div):
    t = h.shape[0]
    tm, th = 1024, 256
    nk = FFN_HIDDEN // th
    return pl.pallas_call(
        functools.partial(_ffn_kernel, j=j, nk=nk),
        grid=(t // tm, nk),
        in_specs=[
            pl.BlockSpec((tm, D_MODEL), lambda i, k: (i, 0)),
            pl.BlockSpec((1, N_MOD, D_MODEL), lambda i, k: (i // cond_div, 0, 0)),
            pl.BlockSpec((1, D_MODEL), lambda i, k: (0, 0)),
            pl.BlockSpec((None, D_MODEL, th), lambda i, k: (layer, 0, k)),
            pl.BlockSpec((None, D_MODEL, th), lambda i, k: (layer, 0, k + nk)),
            pl.BlockSpec((None, th, D_MODEL), lambda i, k: (layer, k, 0)),
        ],
        out_specs=pl.BlockSpec((tm, D_MODEL), lambda i, k: (i, 0)),
        out_shape=jax.ShapeDtypeStruct((t, D_MODEL), F32),
        scratch_shapes=[pltpu.VMEM((tm, D_MODEL), BF16), pltpu.VMEM((tm, D_MODEL), F32)],
        compiler_params=_params("arbitrary", "arbitrary"),
        name=f"ffn{j}_l{layer}",
    )(h, mod, g, w_gu, w_gu, w_down)


def _head_mean_sq(x):
    w = x.shape[-1]
    r = lax.broadcasted_iota(jnp.int32, (MXU_COLS, MXU_COLS), 0) // HEAD_DIM
    c = lax.broadcasted_iota(jnp.int32, (MXU_COLS, MXU_COLS), 1) // HEAD_DIM
    bd = jnp.where(r == c, 1.0 / HEAD_DIM, 0.0).astype(BF16)
    sq = (x * x).astype(BF16)
    parts = []
    for s in range(0, w, MXU_COLS):
        e = min(s + MXU_COLS, w)
        parts.append(_dot(sq[:, s:e], bd[:e - s, :e - s]))
    return parts[0] if len(parts) == 1 else jnp.concatenate(parts, axis=-1)


def _head_rms(x, gain):
    return (x * lax.rsqrt(_head_mean_sq(x) + EPS)) * gain


def _rope(x, cos, sin_signed):
    lanes = cos.shape[-1]
    outs = []
    for s in range(0, x.shape[-1], lanes):
        xs = x[:, s:s + lanes]
        blk = lax.broadcasted_iota(jnp.int32, xs.shape, 1) // (HEAD_DIM // 4)
        partner = jnp.where(blk % 2 == 0,
                            pltpu.roll(xs, lanes - HEAD_DIM // 4, 1),
                            pltpu.roll(xs, HEAD_DIM // 4, 1))
        outs.append(xs * cos + partner * sin_signed)
    return outs[0] if len(outs) == 1 else jnp.concatenate(outs, axis=-1)


def _qkv_kernel(*refs, rope, cache_out):
    h_ref, mod_ref, g_ref, w_ref, gqa_ref, gka_ref, gqb_ref, gkb_ref = refs[:8]
    pos = 8
    if rope:
        cos_ref, sin_ref = refs[pos:pos + 2]
        pos += 2
    q_ref, kva_ref, kvb_ref = refs[pos:pos + 3]
    pos += 3

    xn = _adaln(h_ref[...], g_ref[...], mod_ref, 1).astype(BF16)
    y = _dot(xn, w_ref[...].astype(BF16))
    o = 0
    qa = _head_rms(y[:, o:o + NA_W], gqa_ref[...]); o += NA_W
    ka = _head_rms(y[:, o:o + NA_W], gka_ref[...]); o += NA_W
    va = y[:, o:o + NA_W]; o += NA_W
    qb = _head_rms(y[:, o:o + GQ_W], gqb_ref[...]); o += GQ_W
    kb = _head_rms(y[:, o:o + GKV_W], gkb_ref[...]); o += GKV_W
    vb = y[:, o:o + GKV_W]
    if cache_out:
        ka_ref, va_ref, kb_ref, vb_ref = refs[pos:pos + 4]
        ka_ref[...] = ka
        va_ref[...] = va
        kb_ref[...] = kb
        vb_ref[...] = vb
    if rope:
        cos, sin = cos_ref[...], sin_ref[...]
        qb = _rope(qb, cos, sin)
        kb = _rope(kb, cos, sin)
    q_ref[:, :NA_W] = (qa * Q_SCALE).astype(BF16)
    q_ref[:, NA_W:] = (qb * Q_SCALE).astype(BF16)
    kva_ref[:, :NA_W] = ka.astype(BF16)
    kva_ref[:, NA_W:] = va.astype(BF16)
    kvb_ref[:, :GKV_W] = kb.astype(BF16)
    kvb_ref[:, GKV_W:] = vb.astype(BF16)


def _qkv(h, mod, g, w_in, gains, cond_div, rope_tabs=None, cache_out=False):
    t = h.shape[0]
    tm = 512
    rope = rope_tabs is not None
    row = lambda i: (i, 0)
    const = lambda i: (0, 0)
    in_specs = [
        pl.BlockSpec((tm, D_MODEL), row),
        pl.BlockSpec((1, N_MOD, D_MODEL), lambda i: (i // cond_div, 0, 0)),
        pl.BlockSpec((1, D_MODEL), const),
        pl.BlockSpec((None, D_MODEL, QKV_WIDTH), lambda i: (0, 0, 0)),
        pl.BlockSpec((1, NA_W), const),
        pl.BlockSpec((1, NA_W), const),
        pl.BlockSpec((1, GQ_W), const),
        pl.BlockSpec((1, GKV_W), const),
    ]
    args = [h, mod, g, w_in, *gains]
    if rope:
        seq_tiles = rope_tabs[0].shape[0] // tm
        in_specs += [pl.BlockSpec((tm, 2 * HEAD_DIM), lambda i: (i % seq_tiles, 0))] * 2
        args += list(rope_tabs)
    out_specs = [pl.BlockSpec((tm, NA_W + GQ_W), row), pl.BlockSpec((tm, 2 * NA_W), row),
                 pl.BlockSpec((tm, 2 * GKV_W), row)]
    out_shape = [jax.ShapeDtypeStruct((t, NA_W + GQ_W), BF16), jax.ShapeDtypeStruct((t, 2 * NA_W), BF16),
                 jax.ShapeDtypeStruct((t, 2 * GKV_W), BF16)]
    if cache_out:
        out_specs += [pl.BlockSpec((tm, NA_W), row), pl.BlockSpec((tm, NA_W), row),
                      pl.BlockSpec((tm, GKV_W), row), pl.BlockSpec((tm, GKV_W), row)]
        out_shape += [jax.ShapeDtypeStruct((t, NA_W), F32), jax.ShapeDtypeStruct((t, NA_W), F32),
                      jax.ShapeDtypeStruct((t, GKV_W), F32), jax.ShapeDtypeStruct((t, GKV_W), F32)]
    return pl.pallas_call(
        functools.partial(_qkv_kernel, rope=rope, cache_out=cache_out),
        grid=(t // tm,),
        in_specs=in_specs,
        out_specs=out_specs,
        out_shape=out_shape,
        compiler_params=_params("arbitrary"),
        name="qkv_latent" if rope else "qkv_context",
    )(*args)


def _softmax_pv(scores, values):
    m = functools.reduce(jnp.maximum, [jnp.max(s, axis=-1, keepdims=True) for s in scores])
    ps = [jnp.exp(s - m) for s in scores]
    l = functools.reduce(jnp.add, [jnp.sum(p, axis=-1, keepdims=True) for p in ps])
    o = functools.reduce(jnp.add, [_dot(p.astype(BF16), v) for p, v in zip(ps, values)])
    return o * (1.0 / l)


def _ctx_attn_kernel(q_ref, kva_ref, kvb_ref, o_ref):
    tq = q_ref.shape[0]
    for h in range(NA_HEADS):
        sl = slice(h * HEAD_DIM, (h + 1) * HEAD_DIM)
        q = q_ref[:, sl]
        k = kva_ref[:, sl]
        v = kva_ref[:, NA_W + h * HEAD_DIM:NA_W + (h + 1) * HEAD_DIM]
        o_ref[:, sl] = _softmax_pv([_dot_nt(q, k)], [v]).astype(o_ref.dtype)
    for g in range(GQA_KV_HEADS):
        heads = [NA_W + (g * GQA_GROUP + i) * HEAD_DIM for i in range(GQA_GROUP)]
        q = jnp.concatenate([q_ref[:, o:o + HEAD_DIM] for o in heads], axis=0)
        k = kvb_ref[:, g * HEAD_DIM:(g + 1) * HEAD_DIM]
        v = kvb_ref[:, GKV_W + g * HEAD_DIM:GKV_W + (g + 1) * HEAD_DIM]
        o = _softmax_pv([_dot_nt(q, k)], [v]).astype(o_ref.dtype)
        for i, off in enumerate(heads):
            o_ref[:, off:off + HEAD_DIM] = o[i * tq:(i + 1) * tq]


def _ctx_attention(q, kva, kvb, seq):
    t = q.shape[0]
    row = lambda b: (b, 0)
    return pl.pallas_call(
        _ctx_attn_kernel,
        grid=(t // seq,),
        in_specs=[pl.BlockSpec((seq, NA_W + GQ_W), row), pl.BlockSpec((seq, 2 * NA_W), row),
                  pl.BlockSpec((seq, 2 * GKV_W), row)],
        out_specs=pl.BlockSpec((seq, NA_W + GQ_W), row),
        out_shape=jax.ShapeDtypeStruct((t, NA_W + GQ_W), BF16),
        compiler_params=_params("arbitrary"),
        name="ctx_attention",
    )(q, kva, kvb)


def _na_band_key_row0(r0, rows):
    return jnp.clip(r0 - NA_WIN_R // 2, 0, rows - NA_BAND_KEY_ROWS)


def _na_kernel(q_ref, kva_ref, ck_ref, cv_ref, bias_ref, o_ref, *, rows):
    j = pl.program_id(1)
    k0 = pl.multiple_of(_na_band_key_row0(j * NA_BAND_ROWS, rows) * GRID_W, GRID_W)
    band = pl.ds(k0, NA_BAND_K)
    for h in range(NA_HEADS):
        sl = slice(h * HEAD_DIM, (h + 1) * HEAD_DIM)
        vsl = slice(NA_W + h * HEAD_DIM, NA_W + (h + 1) * HEAD_DIM)
        q = q_ref[:, sl]
        s_loc = _dot_nt(q, kva_ref[band, sl]) + bias_ref[h]
        s_ctx = _dot_nt(q, ck_ref[:, sl])
        o_ref[:, sl] = _softmax_pv([s_loc, s_ctx], [kva_ref[band, vsl], cv_ref[:, sl]]).astype(o_ref.dtype)


def _na_bias(rpb, rows):
    qc = np.arange(GRID_W)
    kc = np.arange(GRID_W)
    ws = np.clip(qc - NA_WIN_C // 2, 0, GRID_W - NA_WIN_C)
    col_ok = (kc[None, :] >= ws[:, None]) & (kc[None, :] < ws[:, None] + NA_WIN_C)
    dc = np.clip(kc[None, :] - qc[:, None] + NA_WIN_C - 1, 0, 2 * NA_WIN_C - 2)
    pick = ((dc[:, :, None] == np.arange(2 * NA_WIN_C - 1)) & col_ok[:, :, None]).astype(np.float32)
    tiles = jnp.einsum("hrd,qkd->hrqk", rpb.astype(F32), pick, precision=lax.Precision.HIGHEST)
    tiles = jnp.where(col_ok, tiles, NEG_INF)
    masked = jnp.full((NA_HEADS, GRID_W, GRID_W), NEG_INF, F32)
    out = []
    nb = rows // NA_BAND_ROWS
    for band in (0, 1, nb - 1):
        r0 = band * NA_BAND_ROWS
        k0 = int(np.clip(r0 - NA_WIN_R // 2, 0, rows - NA_BAND_KEY_ROWS))
        q_rows = []
        for r in range(r0, r0 + NA_BAND_ROWS):
            start = int(np.clip(r - NA_WIN_R // 2, 0, rows - NA_WIN_R))
            q_rows.append(jnp.concatenate(
                [tiles[:, kr - r + NA_WIN_R - 1] if start <= kr < start + NA_WIN_R else masked
                 for kr in range(k0, k0 + NA_BAND_KEY_ROWS)], axis=-1))
        out.append(jnp.concatenate(q_rows, axis=1))
    return jnp.stack(out)


def _na_attention(q, kva, ctx_k, ctx_v, bias, batch, seq):
    rows = seq // GRID_W
    nb = rows // NA_BAND_ROWS
    past = ctx_k.shape[1]
    band_type = lambda j: jnp.where(j == 0, 0, jnp.where(j == nb - 1, 2, 1))
    return pl.pallas_call(
        functools.partial(_na_kernel, rows=rows),
        grid=(batch, nb),
        in_specs=[
            pl.BlockSpec((NA_BAND_Q, NA_W), lambda b, j: (b * nb + j, 0)),
            pl.BlockSpec((seq, 2 * NA_W), lambda b, j: (b, 0)),
            pl.BlockSpec((None, past, NA_W), lambda b, j: (b, 0, 0)),
            pl.BlockSpec((None, past, NA_W), lambda b, j: (b, 0, 0)),
            pl.BlockSpec((None, NA_HEADS, NA_BAND_Q, NA_BAND_K), lambda b, j: (band_type(j), 0, 0, 0)),
        ],
        out_specs=pl.BlockSpec((NA_BAND_Q, NA_W), lambda b, j: (b * nb + j, 0)),
        out_shape=jax.ShapeDtypeStruct((batch * seq, NA_W), BF16),
        compiler_params=_params("arbitrary", "arbitrary"),
        name="na_attention",
    )(q, kva, ctx_k, ctx_v, bias)


def _gqa_kernel(q_ref, kvb_ref, ck_ref, cv_ref, o_ref):
    tq = q_ref.shape[0]
    for g in range(GQA_KV_HEADS):
        heads = [(g * GQA_GROUP + i) * HEAD_DIM for i in range(GQA_GROUP)]
        q = jnp.concatenate([q_ref[:, o:o + HEAD_DIM] for o in heads], axis=0)
        ksl = slice(g * HEAD_DIM, (g + 1) * HEAD_DIM)
        vsl = slice(GKV_W + g * HEAD_DIM, GKV_W + (g + 1) * HEAD_DIM)
        o = _softmax_pv([_dot_nt(q, kvb_ref[:, ksl]), _dot_nt(q, ck_ref[:, ksl])],
                        [kvb_ref[:, vsl], cv_ref[:, ksl]]).astype(o_ref.dtype)
        for i, off in enumerate(heads):
            o_ref[:, off:off + HEAD_DIM] = o[i * tq:(i + 1) * tq]


def _gqa_attention(q, kvb, ctx_k, ctx_v, batch, seq):
    tq = 128
    nt = seq // tq
    past = ctx_k.shape[1]
    return pl.pallas_call(
        _gqa_kernel,
        grid=(batch, nt),
        in_specs=[
            pl.BlockSpec((tq, GQ_W), lambda b, i: (b * nt + i, 1)),
            pl.BlockSpec((seq, 2 * GKV_W), lambda b, i: (b, 0)),
            pl.BlockSpec((None, past, GKV_W), lambda b, i: (b, 0, 0)),
            pl.BlockSpec((None, past, GKV_W), lambda b, i: (b, 0, 0)),
        ],
        out_specs=pl.BlockSpec((tq, GQ_W), lambda b, i: (b * nt + i, 0)),
        out_shape=jax.ShapeDtypeStruct((batch * seq, GQ_W), BF16),
        compiler_params=_params("arbitrary", "arbitrary"),
        name="gqa_attention",
    )(q, kvb, ctx_k, ctx_v)


def _outproj_kernel(*refs, n_in):
    h_ref, mod_ref = refs[:2]
    o_refs = refs[2:2 + n_in]
    w_ref, out_ref = refs[2 + n_in:]
    acc = None
    row = 0
    for o_ref in o_refs:
        width = o_ref.shape[1]
        part = _dot(o_ref[...], w_ref[row:row + width, :].astype(BF16))
        acc = part if acc is None else acc + part
        row += width
    out_ref[...] = h_ref[...] + _gate(mod_ref, 1) * acc


def _outproj(h, mod, heads_out, w_out, cond_div):
    t = h.shape[0]
    tm = 512
    row = lambda i: (i, 0)
    n_in = len(heads_out)
    return pl.pallas_call(
        functools.partial(_outproj_kernel, n_in=n_in),
        grid=(t // tm,),
        in_specs=[pl.BlockSpec((tm, D_MODEL), row),
                  pl.BlockSpec((1, N_MOD, D_MODEL), lambda i: (i // cond_div, 0, 0))]
                 + [pl.BlockSpec((tm, o.shape[1]), row) for o in heads_out]
                 + [pl.BlockSpec((None, D_MODEL, D_MODEL), lambda i: (0, 0, 0))],
        out_specs=pl.BlockSpec((tm, D_MODEL), row),
        out_shape=jax.ShapeDtypeStruct((t, D_MODEL), F32),
        compiler_params=_params("arbitrary"),
        name="attn_outproj",
    )(h, mod, *heads_out, w_out)


def _sgu_kernel(h_ref, mod_ref, g_ref, win_ref, vg_ref, ws_ref, bs_ref, wout_ref, o_ref, gated_ref):
    tm = h_ref.shape[0]
    xn = _adaln(h_ref[...], g_ref[...], mod_ref, 1).astype(BF16)
    y = _dot(xn, win_ref[...].astype(BF16))
    y = 0.5 * y * (1.0 + lax.erf(y * math.sqrt(0.5)))
    u = y[:, :SGU_WIDTH]
    v = y[:, SGU_WIDTH:]
    v = (v * lax.rsqrt(jnp.mean(v * v, axis=-1, keepdims=True) + EPS)) * vg_ref[...]
    vb = v.astype(BF16)
    for c in range(tm // SGU_CHUNK):
        rs = slice(c * SGU_CHUNK, (c + 1) * SGU_CHUNK)
        for g in range(SGU_GROUPS):
            cs = slice(g * SGU_GW, (g + 1) * SGU_GW)
            sv = _dot(ws_ref[g].astype(BF16), vb[rs, cs]) + bs_ref[:, g:g + 1]
            gated_ref[rs, cs] = (u[rs, cs] * sv).astype(BF16)
    out = _dot(gated_ref[...], wout_ref[...].astype(BF16))
    o_ref[...] = h_ref[...] + _gate(mod_ref, 1) * out


def _sgu(h, mod, g, w_in, v_g, w_s, b_s_t, w_out, cond_div):
    t = h.shape[0]
    tm = 256
    row = lambda i: (i, 0)
    const = lambda i: (0, 0)
    once = pl.Buffered(1)
    return pl.pallas_call(
        _sgu_kernel,
        grid=(t // tm,),
        in_specs=[
            pl.BlockSpec((tm, D_MODEL), row),
            pl.BlockSpec((1, N_MOD, D_MODEL), lambda i: (i // cond_div, 0, 0)),
            pl.BlockSpec((1, D_MODEL), const),
            pl.BlockSpec((None, D_MODEL, 2 * SGU_WIDTH), lambda i: (0, 0, 0), pipeline_mode=once),
            pl.BlockSpec((1, SGU_WIDTH), const),
            pl.BlockSpec((None, SGU_GROUPS, SGU_CHUNK, SGU_CHUNK), lambda i: (0, 0, 0, 0)),
            pl.BlockSpec((SGU_CHUNK, SGU_GROUPS), const),
            pl.BlockSpec((None, SGU_WIDTH, D_MODEL), lambda i: (0, 0, 0), pipeline_mode=once),
        ],
        out_specs=pl.BlockSpec((tm, D_MODEL), row),
        out_shape=jax.ShapeDtypeStruct((t, D_MODEL), F32),
        scratch_shapes=[pltpu.VMEM((tm, SGU_WIDTH), BF16)],
        compiler_params=_params("arbitrary"),
        name="sgu",
    )(h, mod, g, w_in, v_g, w_s, b_s_t, w_out)


def _rope_tables(seq):
    half = HEAD_DIM // 2
    t = jnp.arange(seq)
    freqs = ROPE_BASE ** (-jnp.arange(0, half, 2, dtype=F32) / half)
    def tab(pos):
        ang = pos.astype(F32)[:, None] * freqs[None, :]
        cos, sin = jnp.cos(ang), jnp.sin(ang)
        return jnp.concatenate([cos, cos], -1), jnp.concatenate([-sin, sin], -1)
    cr, sr = tab(t // GRID_W)
    cc, sc = tab(t % GRID_W)
    cos = jnp.concatenate([cr, cc], -1)
    sin = jnp.concatenate([sr, sc], -1)
    return jnp.tile(cos, (1, 2)), jnp.tile(sin, (1, 2))


def kernel(x_prompt, x_sample, cache_na_k, cache_na_v, cache_gqa_k, cache_gqa_v, c, c_ctx, norm_g, mod_w, mod_b, ffn1_w_gu, ffn1_w_down, ffn2_w_gu, ffn2_w_down, attn_w_in, attn_w_out, na_q_g, na_k_g, na_rpb, gqa_q_g, gqa_k_g, sgu_w_in, sgu_v_g, sgu_w_s, sgu_b_s, sgu_w_out):
    batch, seq, _ = x_prompt.shape
    dec_batch, dec_seq, _ = x_sample.shape
    past = cache_na_k.shape[2]
    hp = x_prompt.reshape(batch * seq, D_MODEL)
    hs = x_sample.reshape(dec_batch * dec_seq, D_MODEL)

    cond8 = jnp.zeros((8, D_MODEL), F32).at[0].set(c_ctx).at[1:1 + dec_batch].set(c)
    mod_all = _modulation(cond8, mod_w, mod_b).reshape(DEPTH, 8, N_MOD, D_MODEL)

    new_cache = None
    for layer in range(DEPTH):
        mod_l = mod_all[layer]
        mods = (mod_l[0:1], mod_l[1:1 + dec_batch])
        g0, g1, g2 = (norm_g[layer, s][None, :] for s in range(3))

        hp = _ffn(hp, mods[0], g0, ffn1_w_gu, ffn1_w_down, layer, 0, hp.shape[0] // 1024)
        hs = _ffn(hs, mods[1], g0, ffn1_w_gu, ffn1_w_down, layer, 0, dec_seq // 1024)

        if layer % 2 == 0:
            e = layer // 2
            gains = (jnp.tile(na_q_g[e], NA_HEADS)[None], jnp.tile(na_k_g[e], NA_HEADS)[None],
                     jnp.tile(gqa_q_g[e], GQA_HEADS)[None], jnp.tile(gqa_k_g[e], GQA_KV_HEADS)[None])
            w_in = attn_w_in[e][None]
            w_out = attn_w_out[e][None]
            q, kva, kvb, ka, va, kb, vb = _qkv(hp, mods[0], g1, w_in, gains, hp.shape[0] // 512,
                                               cache_out=True)
            new_cache = (ka, va, kb, vb)
            o = _ctx_attention(q, kva, kvb, seq)
            hp = _outproj(hp, mods[0], [o], w_out, hp.shape[0] // 512)
            q, kva, kvb = _qkv(hs, mods[1], g1, w_in, gains, dec_seq // 512,
                               rope_tabs=_rope_tables(dec_seq))
            ck_a = cache_na_k[:, e].reshape(dec_batch, past, NA_W).astype(BF16)
            cv_a = cache_na_v[:, e].reshape(dec_batch, past, NA_W).astype(BF16)
            ck_b = cache_gqa_k[:, e].reshape(dec_batch, past, GKV_W).astype(BF16)
            cv_b = cache_gqa_v[:, e].reshape(dec_batch, past, GKV_W).astype(BF16)
            oa = _na_attention(q, kva, ck_a, cv_a, _na_bias(na_rpb[e], dec_seq // GRID_W), dec_batch, dec_seq)
            ob = _gqa_attention(q, kvb, ck_b, cv_b, dec_batch, dec_seq)
            hs = _outproj(hs, mods[1], [oa, ob], w_out, dec_seq // 512)
        else:
            o = layer // 2
            sgu_args = (g1, sgu_w_in[o][None], sgu_v_g[o][None], sgu_w_s[o][None], sgu_b_s[o].T,
                        sgu_w_out[o][None])
            hp = _sgu(hp, mods[0], *sgu_args, hp.shape[0] // 256)
            hs = _sgu(hs, mods[1], *sgu_args, dec_seq // 256)

        hp = _ffn(hp, mods[0], g2, ffn2_w_gu, ffn2_w_down, layer, 2, hp.shape[0] // 1024)
        hs = _ffn(hs, mods[1], g2, ffn2_w_gu, ffn2_w_down, layer, 2, dec_seq // 1024)

    ka, va, kb, vb = new_cache
    n_attn = (DEPTH + 1) // 2
    assert n_attn == 1
    return (hp.reshape(batch, seq, D_MODEL), hs.reshape(dec_batch, dec_seq, D_MODEL),
            ka.reshape(batch, n_attn, seq, NA_HEADS, HEAD_DIM), va.reshape(batch, n_attn, seq, NA_HEADS, HEAD_DIM),
            kb.reshape(batch, n_attn, seq, GQA_KV_HEADS, HEAD_DIM), vb.reshape(batch, n_attn, seq, GQA_KV_HEADS, HEAD_DIM))
```

```python
import functools
import math

import jax
import jax.numpy as jnp
import numpy as np
from jax import lax
from jax.experimental import pallas as pl
from jax.experimental.pallas import tpu as pltpu

D_MODEL = 1024
DEPTH = 2
GRID_W = 64
HEAD_DIM = 64
NA_HEADS = 8
NA_WIN_R = 8
NA_WIN_C = 16
GQA_HEADS = 8
GQA_KV_HEADS = 2
GQA_GROUP = GQA_HEADS // GQA_KV_HEADS
NA_W = NA_HEADS * HEAD_DIM
GQ_W = GQA_HEADS * HEAD_DIM
GKV_W = GQA_KV_HEADS * HEAD_DIM
QKV_WIDTH = 3 * NA_W + GQ_W + 2 * GKV_W
ROPE_BASE = 10000.0
SGU_CHUNK = 128
SGU_GROUPS = 8
SGU_WIDTH = 2 * D_MODEL
SGU_GW = SGU_WIDTH // SGU_GROUPS
FFN_HIDDEN = 2816
N_MOD = 9
EPS = 1e-6
NEG_INF = -1e30
Q_SCALE = HEAD_DIM ** -0.5

NA_BAND_ROWS = 4
NA_BAND_KEY_ROWS = 12
NA_BAND_Q = NA_BAND_ROWS * GRID_W
NA_BAND_K = NA_BAND_KEY_ROWS * GRID_W

FFN_TM = 512
FFN_ACT_CHUNK = 512
FFN_STAGE = 256

MXU_COLS = 256
VMEM_LIMIT = 56 * 1024 * 1024

BF16 = jnp.bfloat16
F32 = jnp.float32


def _dot(a, b):
    return jnp.dot(a, b, preferred_element_type=F32)


def _dot_nt(a, b):
    return lax.dot_general(a, b, (((1,), (1,)), ((), ())), preferred_element_type=F32)


def _params(*sem):
    return pltpu.CompilerParams(dimension_semantics=sem, vmem_limit_bytes=VMEM_LIMIT)


def _adaln(x, g, mod_ref, j):
    shift = mod_ref[0, 3 * j:3 * j + 1, :]
    scale = mod_ref[0, 3 * j + 1:3 * j + 2, :]
    y = x * lax.rsqrt(jnp.mean(x * x, axis=-1, keepdims=True) + EPS)
    return (y * g) * (1.0 + scale) + shift


def _gate(mod_ref, j):
    return mod_ref[0, 3 * j + 2:3 * j + 3, :]


def _mod_kernel(c_ref, w_ref, b_ref, o_ref):
    c = c_ref[...]
    a = (c * jax.nn.sigmoid(c)).astype(BF16)
    o_ref[...] = _dot(a, w_ref[...].astype(BF16)) + b_ref[...]


def _modulation(cond8, mod_w, mod_b):
    tn = 1024
    n = N_MOD * D_MODEL
    return pl.pallas_call(
        _mod_kernel,
        grid=(DEPTH, n // tn),
        in_specs=[
            pl.BlockSpec((8, D_MODEL), lambda l, k: (0, 0)),
            pl.BlockSpec((None, D_MODEL, tn), lambda l, k: (l, 0, k)),
            pl.BlockSpec((None, 1, tn), lambda l, k: (l, 0, k)),
        ],
        out_specs=pl.BlockSpec((None, 8, tn), lambda l, k: (l, 0, k)),
        out_shape=jax.ShapeDtypeStruct((DEPTH, 8, n), F32),
        compiler_params=_params("arbitrary", "arbitrary"),
        name="modulation",
    )(cond8, mod_w, mod_b.reshape(DEPTH, 1, n))


def _two_stream_maps(tm, n_ctx_rows, dec_seq):
    n_ctx = n_ctx_rows // tm
    per_seq = dec_seq // tm
    ctx_map = lambda i: (jnp.minimum(i, n_ctx - 1), 0)
    lat_map = lambda i: (jnp.maximum(i - n_ctx, 0), 0)
    mod_map = lambda i: (jnp.where(i < n_ctx, 0, 1 + (i - n_ctx) // per_seq), 0, 0)
    return n_ctx, ctx_map, lat_map, mod_map


def _stream_cast(pairs, stage_ref, sem_ref):
    def copy(k):
        return pltpu.make_async_copy(pairs[k][0], stage_ref.at[k % 2], sem_ref.at[k % 2])
    copy(0).start()
    for k in range(len(pairs)):
        if k + 1 < len(pairs):
            copy(k + 1).start()
        copy(k).wait()
        dst, idx = pairs[k][1]
        dst[idx] = stage_ref[k % 2].astype(BF16)


def _ffn_kernel(hp_ref, hs_ref, mod_ref, g_ref, wgu_hbm, wd_hbm, op_ref, os_ref,
                wgu_ref, wd_ref, stage_gu, stage_d, sem_gu, sem_d, act_ref, *, layer, j, n_ctx):
    i = pl.program_id(0)

    @pl.when(i == 0)
    def _load_weights():
        cw = stage_gu.shape[-1]
        _stream_cast([(wgu_hbm.at[layer, :, pl.ds(c, cw)], (wgu_ref, (slice(None), slice(c, c + cw))))
                      for c in range(0, 2 * FFN_HIDDEN, cw)], stage_gu, sem_gu)
        rw = stage_d.shape[1]
        _stream_cast([(wd_hbm.at[layer, pl.ds(r, rw), :], (wd_ref, (slice(r, r + rw), slice(None))))
                      for r in range(0, FFN_HIDDEN, rw)], stage_d, sem_d)

    is_ctx = i < n_ctx
    h = jnp.where(is_ctx, hp_ref[...], hs_ref[...])
    xn = _adaln(h, g_ref[...], mod_ref, j).astype(BF16)
    for c in range(0, FFN_HIDDEN, FFN_ACT_CHUNK):
        w = min(FFN_ACT_CHUNK, FFN_HIDDEN - c)
        gate = _dot(xn, wgu_ref[:, c:c + w])
        up = _dot(xn, wgu_ref[:, FFN_HIDDEN + c:FFN_HIDDEN + c + w])
        act_ref[:, c:c + w] = ((gate * jax.nn.sigmoid(gate)) * up).astype(BF16)
    res = h + (0.5 * _gate(mod_ref, j)) * _dot(act_ref[...], wd_ref[...])

    @pl.when(is_ctx)
    def _():
        op_ref[...] = res

    @pl.when(jnp.logical_not(is_ctx))
    def _():
        os_ref[...] = res


def _ffn(hp, hs, mod, g, w_gu, w_down, layer, j, dec_seq):
    tm = FFN_TM
    n_ctx, ctx_map, lat_map, mod_map = _two_stream_maps(tm, hp.shape[0], dec_seq)
    row_spec = lambda m: pl.BlockSpec((tm, D_MODEL), m)
    return pl.pallas_call(
        functools.partial(_ffn_kernel, layer=layer, j=j, n_ctx=n_ctx),
        grid=((hp.shape[0] + hs.shape[0]) // tm,),
        in_specs=[
            row_spec(ctx_map), row_spec(lat_map),
            pl.BlockSpec((1, N_MOD, D_MODEL), mod_map),
            pl.BlockSpec((1, D_MODEL), lambda i: (0, 0)),
            pl.BlockSpec(memory_space=pl.ANY),
            pl.BlockSpec(memory_space=pl.ANY),
        ],
        out_specs=[row_spec(ctx_map), row_spec(lat_map)],
        out_shape=[jax.ShapeDtypeStruct(hp.shape, F32), jax.ShapeDtypeStruct(hs.shape, F32)],
        scratch_shapes=[
            pltpu.VMEM((D_MODEL, 2 * FFN_HIDDEN), BF16),
            pltpu.VMEM((FFN_HIDDEN, D_MODEL), BF16),
            pltpu.VMEM((2, D_MODEL, FFN_STAGE), F32),
            pltpu.VMEM((2, FFN_STAGE, D_MODEL), F32),
            pltpu.SemaphoreType.DMA((2,)),
            pltpu.SemaphoreType.DMA((2,)),
            pltpu.VMEM((tm, FFN_HIDDEN), BF16),
        ],
        compiler_params=_params("arbitrary"),
        name=f"ffn{j}_l{layer}",
    )(hp, hs, mod, g, w_gu, w_down)


def _head_mean_sq(x):
    w = x.shape[-1]
    r = lax.broadcasted_iota(jnp.int32, (MXU_COLS, MXU_COLS), 0) // HEAD_DIM
    c = lax.broadcasted_iota(jnp.int32, (MXU_COLS, MXU_COLS), 1) // HEAD_DIM
    bd = jnp.where(r == c, 1.0 / HEAD_DIM, 0.0).astype(BF16)
    sq = (x * x).astype(BF16)
    parts = []
    for s in range(0, w, MXU_COLS):
        e = min(s + MXU_COLS, w)
        parts.append(_dot(sq[:, s:e], bd[:e - s, :e - s]))
    return parts[0] if len(parts) == 1 else jnp.concatenate(parts, axis=-1)


def _head_rms(x, gain):
    return (x * lax.rsqrt(_head_mean_sq(x) + EPS)) * gain


def _rope(x, cos, sin_signed):
    lanes = cos.shape[-1]
    outs = []
    for s in range(0, x.shape[-1], lanes):
        xs = x[:, s:s + lanes]
        blk = lax.broadcasted_iota(jnp.int32, xs.shape, 1) // (HEAD_DIM // 4)
        partner = jnp.where(blk % 2 == 0,
                            pltpu.roll(xs, lanes - HEAD_DIM // 4, 1),
                            pltpu.roll(xs, HEAD_DIM // 4, 1))
        outs.append(xs * cos + partner * sin_signed)
    return outs[0] if len(outs) == 1 else jnp.concatenate(outs, axis=-1)


def _qkv_kernel(*refs, rope, cache_out):
    h_ref, mod_ref, g_ref, w_ref, gqa_ref, gka_ref, gqb_ref, gkb_ref = refs[:8]
    pos = 8
    if rope:
        cos_ref, sin_ref = refs[pos:pos + 2]
        pos += 2
    q_ref, kva_ref, kvb_ref = refs[pos:pos + 3]
    pos += 3

    xn = _adaln(h_ref[...], g_ref[...], mod_ref, 1).astype(BF16)
    y = _dot(xn, w_ref[...].astype(BF16))
    o = 0
    qa = _head_rms(y[:, o:o + NA_W], gqa_ref[...]); o += NA_W
    ka = _head_rms(y[:, o:o + NA_W], gka_ref[...]); o += NA_W
    va = y[:, o:o + NA_W]; o += NA_W
    qb = _head_rms(y[:, o:o + GQ_W], gqb_ref[...]); o += GQ_W
    kb = _head_rms(y[:, o:o + GKV_W], gkb_ref[...]); o += GKV_W
    vb = y[:, o:o + GKV_W]
    if cache_out:
        ka_ref, va_ref, kb_ref, vb_ref = refs[pos:pos + 4]
        ka_ref[...] = ka
        va_ref[...] = va
        kb_ref[...] = kb
        vb_ref[...] = vb
    if rope:
        cos, sin = cos_ref[...], sin_ref[...]
        qb = _rope(qb, cos, sin)
        kb = _rope(kb, cos, sin)
    q_ref[:, :NA_W] = (qa * Q_SCALE).astype(BF16)
    q_ref[:, NA_W:] = (qb * Q_SCALE).astype(BF16)
    kva_ref[:, :NA_W] = ka.astype(BF16)
    kva_ref[:, NA_W:] = va.astype(BF16)
    kvb_ref[:, :GKV_W] = kb.astype(BF16)
    kvb_ref[:, GKV_W:] = vb.astype(BF16)


def _qkv(h, mod, g, w_in, gains, cond_div, rope_tabs=None, cache_out=False):
    t = h.shape[0]
    tm = 512
    rope = rope_tabs is not None
    row = lambda i: (i, 0)
    const = lambda i: (0, 0)
    in_specs = [
        pl.BlockSpec((tm, D_MODEL), row),
        pl.BlockSpec((1, N_MOD, D_MODEL), lambda i: (i // cond_div, 0, 0)),
        pl.BlockSpec((1, D_MODEL), const),
        pl.BlockSpec((None, D_MODEL, QKV_WIDTH), lambda i: (0, 0, 0)),
        pl.BlockSpec((1, NA_W), const),
        pl.BlockSpec((1, NA_W), const),
        pl.BlockSpec((1, GQ_W), const),
        pl.BlockSpec((1, GKV_W), const),
    ]
    args = [h, mod, g, w_in, *gains]
    if rope:
        seq_tiles = rope_tabs[0].shape[0] // tm
        in_specs += [pl.BlockSpec((tm, 2 * HEAD_DIM), lambda i: (i % seq_tiles, 0))] * 2
        args += list(rope_tabs)
    out_specs = [pl.BlockSpec((tm, NA_W + GQ_W), row), pl.BlockSpec((tm, 2 * NA_W), row),
                 pl.BlockSpec((tm, 2 * GKV_W), row)]
    out_shape = [jax.ShapeDtypeStruct((t, NA_W + GQ_W), BF16), jax.ShapeDtypeStruct((t, 2 * NA_W), BF16),
                 jax.ShapeDtypeStruct((t, 2 * GKV_W), BF16)]
    if cache_out:
        out_specs += [pl.BlockSpec((tm, NA_W), row), pl.BlockSpec((tm, NA_W), row),
                      pl.BlockSpec((tm, GKV_W), row), pl.BlockSpec((tm, GKV_W), row)]
        out_shape += [jax.ShapeDtypeStruct((t, NA_W), F32), jax.ShapeDtypeStruct((t, NA_W), F32),
                      jax.ShapeDtypeStruct((t, GKV_W), F32), jax.ShapeDtypeStruct((t, GKV_W), F32)]
    return pl.pallas_call(
        functools.partial(_qkv_kernel, rope=rope, cache_out=cache_out),
        grid=(t // tm,),
        in_specs=in_specs,
        out_specs=out_specs,
        out_shape=out_shape,
        compiler_params=_params("arbitrary"),
        name="qkv_latent" if rope else "qkv_context",
    )(*args)


def _softmax_pv(scores, values):
    m = functools.reduce(jnp.maximum, [jnp.max(s, axis=-1, keepdims=True) for s in scores])
    ps = [jnp.exp(s - m) for s in scores]
    l = functools.reduce(jnp.add, [jnp.sum(p, axis=-1, keepdims=True) for p in ps])
    o = functools.reduce(jnp.add, [_dot(p.astype(BF16), v) for p, v in zip(ps, values)])
    return o * (1.0 / l)


def _ctx_attn_kernel(q_ref, kva_ref, kvb_ref, o_ref):
    tq = q_ref.shape[0]
    for h in range(NA_HEADS):
        sl = slice(h * HEAD_DIM, (h + 1) * HEAD_DIM)
        q = q_ref[:, sl]
        k = kva_ref[:, sl]
        v = kva_ref[:, NA_W + h * HEAD_DIM:NA_W + (h + 1) * HEAD_DIM]
        o_ref[:, sl] = _softmax_pv([_dot_nt(q, k)], [v]).astype(o_ref.dtype)
    for g in range(GQA_KV_HEADS):
        heads = [NA_W + (g * GQA_GROUP + i) * HEAD_DIM for i in range(GQA_GROUP)]
        q = jnp.concatenate([q_ref[:, o:o + HEAD_DIM] for o in heads], axis=0)
        k = kvb_ref[:, g * HEAD_DIM:(g + 1) * HEAD_DIM]
        v = kvb_ref[:, GKV_W + g * HEAD_DIM:GKV_W + (g + 1) * HEAD_DIM]
        o = _softmax_pv([_dot_nt(q, k)], [v]).astype(o_ref.dtype)
        for i, off in enumerate(heads):
            o_ref[:, off:off + HEAD_DIM] = o[i * tq:(i + 1) * tq]


def _ctx_attention(q, kva, kvb, seq):
    t = q.shape[0]
    row = lambda b: (b, 0)
    return pl.pallas_call(
        _ctx_attn_kernel,
        grid=(t // seq,),
        in_specs=[pl.BlockSpec((seq, NA_W + GQ_W), row), pl.BlockSpec((seq, 2 * NA_W), row),
                  pl.BlockSpec((seq, 2 * GKV_W), row)],
        out_specs=pl.BlockSpec((seq, NA_W + GQ_W), row),
        out_shape=jax.ShapeDtypeStruct((t, NA_W + GQ_W), BF16),
        compiler_params=_params("arbitrary"),
        name="ctx_attention",
    )(q, kva, kvb)


def _na_band_key_row0(r0, rows):
    return jnp.clip(r0 - NA_WIN_R // 2, 0, rows - NA_BAND_KEY_ROWS)


def _na_kernel(q_ref, kva_ref, ck_ref, cv_ref, bias_ref, o_ref, *, rows):
    j = pl.program_id(1)
    k0 = pl.multiple_of(_na_band_key_row0(j * NA_BAND_ROWS, rows) * GRID_W, GRID_W)
    band = pl.ds(k0, NA_BAND_K)
    for h in range(NA_HEADS):
        sl = slice(h * HEAD_DIM, (h + 1) * HEAD_DIM)
        vsl = slice(NA_W + h * HEAD_DIM, NA_W + (h + 1) * HEAD_DIM)
        q = q_ref[:, sl]
        s_loc = _dot_nt(q, kva_ref[band, sl]) + bias_ref[h]
        s_ctx = _dot_nt(q, ck_ref[:, sl])
        o_ref[:, sl] = _softmax_pv([s_loc, s_ctx], [kva_ref[band, vsl], cv_ref[:, sl]]).astype(o_ref.dtype)


def _na_bias(rpb, rows):
    qc = np.arange(GRID_W)
    kc = np.arange(GRID_W)
    ws = np.clip(qc - NA_WIN_C // 2, 0, GRID_W - NA_WIN_C)
    col_ok = (kc[None, :] >= ws[:, None]) & (kc[None, :] < ws[:, None] + NA_WIN_C)
    dc = np.clip(kc[None, :] - qc[:, None] + NA_WIN_C - 1, 0, 2 * NA_WIN_C - 2)
    pick = ((dc[:, :, None] == np.arange(2 * NA_WIN_C - 1)) & col_ok[:, :, None]).astype(np.float32)
    tiles = jnp.einsum("hrd,qkd->hrqk", rpb.astype(F32), pick, precision=lax.Precision.HIGHEST)
    tiles = jnp.where(col_ok, tiles, NEG_INF)
    masked = jnp.full((NA_HEADS, GRID_W, GRID_W), NEG_INF, F32)
    out = []
    nb = rows // NA_BAND_ROWS
    for band in (0, 1, nb - 1):
        r0 = band * NA_BAND_ROWS
        k0 = int(np.clip(r0 - NA_WIN_R // 2, 0, rows - NA_BAND_KEY_ROWS))
        q_rows = []
        for r in range(r0, r0 + NA_BAND_ROWS):
            start = int(np.clip(r - NA_WIN_R // 2, 0, rows - NA_WIN_R))
            q_rows.append(jnp.concatenate(
                [tiles[:, kr - r + NA_WIN_R - 1] if start <= kr < start + NA_WIN_R else masked
                 for kr in range(k0, k0 + NA_BAND_KEY_ROWS)], axis=-1))
        out.append(jnp.concatenate(q_rows, axis=1))
    return jnp.stack(out)


def _na_attention(q, kva, ctx_k, ctx_v, bias, batch, seq):
    rows = seq // GRID_W
    nb = rows // NA_BAND_ROWS
    past = ctx_k.shape[1]
    band_type = lambda j: jnp.where(j == 0, 0, jnp.where(j == nb - 1, 2, 1))
    return pl.pallas_call(
        functools.partial(_na_kernel, rows=rows),
        grid=(batch, nb),
        in_specs=[
            pl.BlockSpec((NA_BAND_Q, NA_W), lambda b, j: (b * nb + j, 0)),
            pl.BlockSpec((seq, 2 * NA_W), lambda b, j: (b, 0)),
            pl.BlockSpec((None, past, NA_W), lambda b, j: (b, 0, 0)),
            pl.BlockSpec((None, past, NA_W), lambda b, j: (b, 0, 0)),
            pl.BlockSpec((None, NA_HEADS, NA_BAND_Q, NA_BAND_K), lambda b, j: (band_type(j), 0, 0, 0)),
        ],
        out_specs=pl.BlockSpec((NA_BAND_Q, NA_W), lambda b, j: (b * nb + j, 0)),
        out_shape=jax.ShapeDtypeStruct((batch * seq, NA_W), BF16),
        compiler_params=_params("arbitrary", "arbitrary"),
        name="na_attention",
    )(q, kva, ctx_k, ctx_v, bias)


def _gqa_kernel(q_ref, kvb_ref, ck_ref, cv_ref, o_ref):
    tq = q_ref.shape[0]
    for g in range(GQA_KV_HEADS):
        heads = [(g * GQA_GROUP + i) * HEAD_DIM for i in range(GQA_GROUP)]
        q = jnp.concatenate([q_ref[:, o:o + HEAD_DIM] for o in heads], axis=0)
        ksl = slice(g * HEAD_DIM, (g + 1) * HEAD_DIM)
        vsl = slice(GKV_W + g * HEAD_DIM, GKV_W + (g + 1) * HEAD_DIM)
        o = _softmax_pv([_dot_nt(q, kvb_ref[:, ksl]), _dot_nt(q, ck_ref[:, ksl])],
                        [kvb_ref[:, vsl], cv_ref[:, ksl]]).astype(o_ref.dtype)
        for i, off in enumerate(heads):
            o_ref[:, off:off + HEAD_DIM] = o[i * tq:(i + 1) * tq]


def _gqa_attention(q, kvb, ctx_k, ctx_v, batch, seq):
    tq = 128
    nt = seq // tq
    past = ctx_k.shape[1]
    return pl.pallas_call(
        _gqa_kernel,
        grid=(batch, nt),
        in_specs=[
            pl.BlockSpec((tq, GQ_W), lambda b, i: (b * nt + i, 1)),
            pl.BlockSpec((seq, 2 * GKV_W), lambda b, i: (b, 0)),
            pl.BlockSpec((None, past, GKV_W), lambda b, i: (b, 0, 0)),
            pl.BlockSpec((None, past, GKV_W), lambda b, i: (b, 0, 0)),
        ],
        out_specs=pl.BlockSpec((tq, GQ_W), lambda b, i: (b * nt + i, 0)),
        out_shape=jax.ShapeDtypeStruct((batch * seq, GQ_W), BF16),
        compiler_params=_params("arbitrary", "arbitrary"),
        name="gqa_attention",
    )(q, kvb, ctx_k, ctx_v)


def _outproj_kernel(*refs, n_in):
    h_ref, mod_ref = refs[:2]
    o_refs = refs[2:2 + n_in]
    w_ref, out_ref = refs[2 + n_in:]
    acc = None
    row = 0
    for o_ref in o_refs:
        width = o_ref.shape[1]
        part = _dot(o_ref[...], w_ref[row:row + width, :].astype(BF16))
        acc = part if acc is None else acc + part
        row += width
    out_ref[...] = h_ref[...] + _gate(mod_ref, 1) * acc


def _outproj(h, mod, heads_out, w_out, cond_div):
    t = h.shape[0]
    tm = 512
    row = lambda i: (i, 0)
    n_in = len(heads_out)
    return pl.pallas_call(
        functools.partial(_outproj_kernel, n_in=n_in),
        grid=(t // tm,),
        in_specs=[pl.BlockSpec((tm, D_MODEL), row),
                  pl.BlockSpec((1, N_MOD, D_MODEL), lambda i: (i // cond_div, 0, 0))]
                 + [pl.BlockSpec((tm, o.shape[1]), row) for o in heads_out]
                 + [pl.BlockSpec((None, D_MODEL, D_MODEL), lambda i: (0, 0, 0))],
        out_specs=pl.BlockSpec((tm, D_MODEL), row),
        out_shape=jax.ShapeDtypeStruct((t, D_MODEL), F32),
        compiler_params=_params("arbitrary"),
        name="attn_outproj",
    )(h, mod, *heads_out, w_out)


def _sgu_kernel(h_ref, mod_ref, g_ref, win_ref, vg_ref, ws_ref, bs_ref, wout_ref, o_ref, gated_ref):
    tm = h_ref.shape[0]
    xn = _adaln(h_ref[...], g_ref[...], mod_ref, 1).astype(BF16)
    y = _dot(xn, win_ref[...].astype(BF16))
    y = 0.5 * y * (1.0 + lax.erf(y * math.sqrt(0.5)))
    u = y[:, :SGU_WIDTH]
    v = y[:, SGU_WIDTH:]
    v = (v * lax.rsqrt(jnp.mean(v * v, axis=-1, keepdims=True) + EPS)) * vg_ref[...]
    vb = v.astype(BF16)
    for c in range(tm // SGU_CHUNK):
        rs = slice(c * SGU_CHUNK, (c + 1) * SGU_CHUNK)
        for g in range(SGU_GROUPS):
            cs = slice(g * SGU_GW, (g + 1) * SGU_GW)
            sv = _dot(ws_ref[g].astype(BF16), vb[rs, cs]) + bs_ref[:, g:g + 1]
            gated_ref[rs, cs] = (u[rs, cs] * sv).astype(BF16)
    out = _dot(gated_ref[...], wout_ref[...].astype(BF16))
    o_ref[...] = h_ref[...] + _gate(mod_ref, 1) * out


def _sgu(h, mod, g, w_in, v_g, w_s, b_s_t, w_out, cond_div):
    t = h.shape[0]
    tm = 256
    row = lambda i: (i, 0)
    const = lambda i: (0, 0)
    once = pl.Buffered(1)
    return pl.pallas_call(
        _sgu_kernel,
        grid=(t // tm,),
        in_specs=[
            pl.BlockSpec((tm, D_MODEL), row),
            pl.BlockSpec((1, N_MOD, D_MODEL), lambda i: (i // cond_div, 0, 0)),
            pl.BlockSpec((1, D_MODEL), const),
            pl.BlockSpec((None, D_MODEL, 2 * SGU_WIDTH), lambda i: (0, 0, 0), pipeline_mode=once),
            pl.BlockSpec((1, SGU_WIDTH), const),
            pl.BlockSpec((None, SGU_GROUPS, SGU_CHUNK, SGU_CHUNK), lambda i: (0, 0, 0, 0)),
            pl.BlockSpec((SGU_CHUNK, SGU_GROUPS), const),
            pl.BlockSpec((None, SGU_WIDTH, D_MODEL), lambda i: (0, 0, 0), pipeline_mode=once),
        ],
        out_specs=pl.BlockSpec((tm, D_MODEL), row),
        out_shape=jax.ShapeDtypeStruct((t, D_MODEL), F32),
        scratch_shapes=[pltpu.VMEM((tm, SGU_WIDTH), BF16)],
        compiler_params=_params("arbitrary"),
        name="sgu",
    )(h, mod, g, w_in, v_g, w_s, b_s_t, w_out)


def _rope_tables(seq):
    half = HEAD_DIM // 2
    t = jnp.arange(seq)
    freqs = ROPE_BASE ** (-jnp.arange(0, half, 2, dtype=F32) / half)
    def tab(pos):
        ang = pos.astype(F32)[:, None] * freqs[None, :]
        cos, sin = jnp.cos(ang), jnp.sin(ang)
        return jnp.concatenate([cos, cos], -1), jnp.concatenate([-sin, sin], -1)
    cr, sr = tab(t // GRID_W)
    cc, sc = tab(t % GRID_W)
    cos = jnp.concatenate([cr, cc], -1)
    sin = jnp.concatenate([sr, sc], -1)
    return jnp.tile(cos, (1, 2)), jnp.tile(sin, (1, 2))


def kernel(x_prompt, x_sample, cache_na_k, cache_na_v, cache_gqa_k, cache_gqa_v, c, c_ctx, norm_g, mod_w, mod_b, ffn1_w_gu, ffn1_w_down, ffn2_w_gu, ffn2_w_down, attn_w_in, attn_w_out, na_q_g, na_k_g, na_rpb, gqa_q_g, gqa_k_g, sgu_w_in, sgu_v_g, sgu_w_s, sgu_b_s, sgu_w_out):
    batch, seq, _ = x_prompt.shape
    dec_batch, dec_seq, _ = x_sample.shape
    past = cache_na_k.shape[2]
    hp = x_prompt.reshape(batch * seq, D_MODEL)
    hs = x_sample.reshape(dec_batch * dec_seq, D_MODEL)

    cond8 = jnp.zeros((8, D_MODEL), F32).at[0].set(c_ctx).at[1:1 + dec_batch].set(c)
    mod_all = _modulation(cond8, mod_w, mod_b).reshape(DEPTH, 8, N_MOD, D_MODEL)

    new_cache = None
    for layer in range(DEPTH):
        mod_l = mod_all[layer]
        mods = (mod_l[0:1], mod_l[1:1 + dec_batch])
        g0, g1, g2 = (norm_g[layer, s][None, :] for s in range(3))

        mod3 = mod_l[0:1 + dec_batch]
        hp, hs = _ffn(hp, hs, mod3, g0, ffn1_w_gu, ffn1_w_down, layer, 0, dec_seq)

        if layer % 2 == 0:
            e = layer // 2
            gains = (jnp.tile(na_q_g[e], NA_HEADS)[None], jnp.tile(na_k_g[e], NA_HEADS)[None],
                     jnp.tile(gqa_q_g[e], GQA_HEADS)[None], jnp.tile(gqa_k_g[e], GQA_KV_HEADS)[None])
            w_in = attn_w_in[e][None]
            w_out = attn_w_out[e][None]
            q, kva, kvb, ka, va, kb, vb = _qkv(hp, mods[0], g1, w_in, gains, hp.shape[0] // 512,
                                               cache_out=True)
            new_cache = (ka, va, kb, vb)
            o = _ctx_attention(q, kva, kvb, seq)
            hp = _outproj(hp, mods[0], [o], w_out, hp.shape[0] // 512)
            q, kva, kvb = _qkv(hs, mods[1], g1, w_in, gains, dec_seq // 512,
                               rope_tabs=_rope_tables(dec_seq))
            ck_a = cache_na_k[:, e].reshape(dec_batch, past, NA_W).astype(BF16)
            cv_a = cache_na_v[:, e].reshape(dec_batch, past, NA_W).astype(BF16)
            ck_b = cache_gqa_k[:, e].reshape(dec_batch, past, GKV_W).astype(BF16)
            cv_b = cache_gqa_v[:, e].reshape(dec_batch, past, GKV_W).astype(BF16)
            oa = _na_attention(q, kva, ck_a, cv_a, _na_bias(na_rpb[e], dec_seq // GRID_W), dec_batch, dec_seq)
            ob = _gqa_attention(q, kvb, ck_b, cv_b, dec_batch, dec_seq)
            hs = _outproj(hs, mods[1], [oa, ob], w_out, dec_seq // 512)
        else:
            o = layer // 2
            sgu_args = (g1, sgu_w_in[o][None], sgu_v_g[o][None], sgu_w_s[o][None], sgu_b_s[o].T,
                        sgu_w_out[o][None])
            hp = _sgu(hp, mods[0], *sgu_args, hp.shape[0] // 256)
            hs = _sgu(hs, mods[1], *sgu_args, dec_seq // 256)

        hp, hs = _ffn(hp, hs, mod3, g2, ffn2_w_gu, ffn2_w_down, layer, 2, dec_seq)

    ka, va, kb, vb = new_cache
    n_attn = (DEPTH + 1) // 2
    assert n_attn == 1
    return (hp.reshape(batch, seq, D_MODEL), hs.reshape(dec_batch, dec_seq, D_MODEL),
            ka.reshape(batch, n_attn, seq, NA_HEADS, HEAD_DIM), va.reshape(batch, n_attn, seq, NA_HEADS, HEAD_DIM),
            kb.reshape(batch, n_attn, seq, GQA_KV_HEADS, HEAD_DIM), vb.reshape(batch, n_attn, seq, GQA_KV_HEADS, HEAD_DIM))
```

```python
import functools
import math

import jax
import jax.numpy as jnp
import numpy as np
from jax import lax
from jax.experimental import pallas as pl
from jax.experimental.pallas import tpu as pltpu

D_MODEL = 1024
DEPTH = 2
GRID_W = 64
HEAD_DIM = 64
NA_HEADS = 8
NA_WIN_R = 8
NA_WIN_C = 16
GQA_HEADS = 8
GQA_KV_HEADS = 2
GQA_GROUP = GQA_HEADS // GQA_KV_HEADS
NA_W = NA_HEADS * HEAD_DIM
GQ_W = GQA_HEADS * HEAD_DIM
GKV_W = GQA_KV_HEADS * HEAD_DIM
QKV_WIDTH = 3 * NA_W + GQ_W + 2 * GKV_W
ROPE_BASE = 10000.0
SGU_CHUNK = 128
SGU_GROUPS = 8
SGU_WIDTH = 2 * D_MODEL
SGU_GW = SGU_WIDTH // SGU_GROUPS
FFN_HIDDEN = 2816
N_MOD = 9
EPS = 1e-6
NEG_INF = -1e30
Q_SCALE = HEAD_DIM ** -0.5

NA_BAND_ROWS = 4
NA_BAND_KEY_ROWS = 12
NA_BAND_Q = NA_BAND_ROWS * GRID_W
NA_BAND_K = NA_BAND_KEY_ROWS * GRID_W

FFN_TM = 512
FFN_ACT_CHUNK = 512
FFN_STAGE = 256

LANES = 128
KT_ROWS = NA_W + 2 * GKV_W
MXU_COLS = 256
VMEM_LIMIT = 56 * 1024 * 1024

BF16 = jnp.bfloat16
F32 = jnp.float32


def _dot(a, b):
    return jnp.dot(a, b, preferred_element_type=F32)


def _params(*sem):
    return pltpu.CompilerParams(dimension_semantics=sem, vmem_limit_bytes=VMEM_LIMIT)


def _adaln(x, g, mod_ref, j):
    shift = mod_ref[0, 3 * j:3 * j + 1, :]
    scale = mod_ref[0, 3 * j + 1:3 * j + 2, :]
    y = x * lax.rsqrt(jnp.mean(x * x, axis=-1, keepdims=True) + EPS)
    return (y * g) * (1.0 + scale) + shift


def _gate(mod_ref, j):
    return mod_ref[0, 3 * j + 2:3 * j + 3, :]


def _mod_kernel(c_ref, w_ref, b_ref, o_ref):
    c = c_ref[...]
    a = (c * jax.nn.sigmoid(c)).astype(BF16)
    o_ref[...] = _dot(a, w_ref[...].astype(BF16)) + b_ref[...]


def _modulation(cond8, mod_w, mod_b):
    tn = 1024
    n = N_MOD * D_MODEL
    return pl.pallas_call(
        _mod_kernel,
        grid=(DEPTH, n // tn),
        in_specs=[
            pl.BlockSpec((8, D_MODEL), lambda l, k: (0, 0)),
            pl.BlockSpec((None, D_MODEL, tn), lambda l, k: (l, 0, k)),
            pl.BlockSpec((None, 1, tn), lambda l, k: (l, 0, k)),
        ],
        out_specs=pl.BlockSpec((None, 8, tn), lambda l, k: (l, 0, k)),
        out_shape=jax.ShapeDtypeStruct((DEPTH, 8, n), F32),
        compiler_params=_params("arbitrary", "arbitrary"),
        name="modulation",
    )(cond8, mod_w, mod_b.reshape(DEPTH, 1, n))


def _two_stream_maps(tm, n_ctx_rows, dec_seq):
    n_ctx = n_ctx_rows // tm
    per_seq = dec_seq // tm
    ctx_map = lambda i: (jnp.minimum(i, n_ctx - 1), 0)
    lat_map = lambda i: (jnp.maximum(i - n_ctx, 0), 0)
    mod_map = lambda i: (jnp.where(i < n_ctx, 0, 1 + (i - n_ctx) // per_seq), 0, 0)
    return n_ctx, ctx_map, lat_map, mod_map


def _stream_cast(pairs, stage_ref, sem_ref):
    def copy(k):
        return pltpu.make_async_copy(pairs[k][0], stage_ref.at[k % 2], sem_ref.at[k % 2])
    copy(0).start()
    for k in range(len(pairs)):
        if k + 1 < len(pairs):
            copy(k + 1).start()
        copy(k).wait()
        dst, idx = pairs[k][1]
        dst[idx] = stage_ref[k % 2].astype(BF16)


def _ffn_kernel(hp_ref, hs_ref, mod_ref, g_ref, wgu_hbm, wd_hbm, op_ref, os_ref,
                wgu_ref, wd_ref, stage_gu, stage_d, sem_gu, sem_d, act_ref, *, layer, j, n_ctx):
    i = pl.program_id(0)

    @pl.when(i == 0)
    def _load_weights():
        cw = stage_gu.shape[-1]
        _stream_cast([(wgu_hbm.at[layer, :, pl.ds(c, cw)], (wgu_ref, (slice(None), slice(c, c + cw))))
                      for c in range(0, 2 * FFN_HIDDEN, cw)], stage_gu, sem_gu)
        rw = stage_d.shape[1]
        _stream_cast([(wd_hbm.at[layer, pl.ds(r, rw), :], (wd_ref, (slice(r, r + rw), slice(None))))
                      for r in range(0, FFN_HIDDEN, rw)], stage_d, sem_d)

    is_ctx = i < n_ctx
    h = jnp.where(is_ctx, hp_ref[...], hs_ref[...])
    xn = _adaln(h, g_ref[...], mod_ref, j).astype(BF16)
    for c in range(0, FFN_HIDDEN, FFN_ACT_CHUNK):
        w = min(FFN_ACT_CHUNK, FFN_HIDDEN - c)
        gate = _dot(xn, wgu_ref[:, c:c + w])
        up = _dot(xn, wgu_ref[:, FFN_HIDDEN + c:FFN_HIDDEN + c + w])
        act_ref[:, c:c + w] = ((gate * jax.nn.sigmoid(gate)) * up).astype(BF16)
    res = h + (0.5 * _gate(mod_ref, j)) * _dot(act_ref[...], wd_ref[...])

    @pl.when(is_ctx)
    def _():
        op_ref[...] = res

    @pl.when(jnp.logical_not(is_ctx))
    def _():
        os_ref[...] = res


def _ffn(hp, hs, mod, g, w_gu, w_down, layer, j, dec_seq):
    tm = FFN_TM
    n_ctx, ctx_map, lat_map, mod_map = _two_stream_maps(tm, hp.shape[0], dec_seq)
    row_spec = lambda m: pl.BlockSpec((tm, D_MODEL), m)
    return pl.pallas_call(
        functools.partial(_ffn_kernel, layer=layer, j=j, n_ctx=n_ctx),
        grid=((hp.shape[0] + hs.shape[0]) // tm,),
        in_specs=[
            row_spec(ctx_map), row_spec(lat_map),
            pl.BlockSpec((1, N_MOD, D_MODEL), mod_map),
            pl.BlockSpec((1, D_MODEL), lambda i: (0, 0)),
            pl.BlockSpec(memory_space=pl.ANY),
            pl.BlockSpec(memory_space=pl.ANY),
        ],
        out_specs=[row_spec(ctx_map), row_spec(lat_map)],
        out_shape=[jax.ShapeDtypeStruct(hp.shape, F32), jax.ShapeDtypeStruct(hs.shape, F32)],
        scratch_shapes=[
            pltpu.VMEM((D_MODEL, 2 * FFN_HIDDEN), BF16),
            pltpu.VMEM((FFN_HIDDEN, D_MODEL), BF16),
            pltpu.VMEM((2, D_MODEL, FFN_STAGE), F32),
            pltpu.VMEM((2, FFN_STAGE, D_MODEL), F32),
            pltpu.SemaphoreType.DMA((2,)),
            pltpu.SemaphoreType.DMA((2,)),
            pltpu.VMEM((tm, FFN_HIDDEN), BF16),
        ],
        compiler_params=_params("arbitrary"),
        name=f"ffn{j}_l{layer}",
    )(hp, hs, mod, g, w_gu, w_down)


def _head_mean_sq(x):
    w = x.shape[-1]
    r = lax.broadcasted_iota(jnp.int32, (MXU_COLS, MXU_COLS), 0) // HEAD_DIM
    c = lax.broadcasted_iota(jnp.int32, (MXU_COLS, MXU_COLS), 1) // HEAD_DIM
    bd = jnp.where(r == c, 1.0 / HEAD_DIM, 0.0).astype(BF16)
    sq = (x * x).astype(BF16)
    parts = []
    for s in range(0, w, MXU_COLS):
        e = min(s + MXU_COLS, w)
        parts.append(_dot(sq[:, s:e], bd[:e - s, :e - s]))
    return parts[0] if len(parts) == 1 else jnp.concatenate(parts, axis=-1)


def _low_half():
    return lax.broadcasted_iota(jnp.int32, (1, LANES), 1) < HEAD_DIM


def _head_rms(x, gain):
    return (x * lax.rsqrt(_head_mean_sq(x) + EPS)) * gain


def _rope(x, cos, sin_signed):
    lanes = cos.shape[-1]
    outs = []
    for s in range(0, x.shape[-1], lanes):
        xs = x[:, s:s + lanes]
        blk = lax.broadcasted_iota(jnp.int32, xs.shape, 1) // (HEAD_DIM // 4)
        partner = jnp.where(blk % 2 == 0,
                            pltpu.roll(xs, lanes - HEAD_DIM // 4, 1),
                            pltpu.roll(xs, HEAD_DIM // 4, 1))
        outs.append(xs * cos + partner * sin_signed)
    return outs[0] if len(outs) == 1 else jnp.concatenate(outs, axis=-1)


def _qkv_kernel(*refs, rope, cache_out):
    h_ref, mod_ref, g_ref, w_ref, gqa_ref, gka_ref, gqb_ref, gkb_ref = refs[:8]
    pos = 8
    if rope:
        cos_ref, sin_ref = refs[pos:pos + 2]
        pos += 2
    q_ref, kt_ref, v_ref = refs[pos:pos + 3]
    pos += 3

    xn = _adaln(h_ref[...], g_ref[...], mod_ref, 1).astype(BF16)
    y = _dot(xn, w_ref[...].astype(BF16))
    o = 0
    qa = _head_rms(y[:, o:o + NA_W], gqa_ref[...]); o += NA_W
    ka = _head_rms(y[:, o:o + NA_W], gka_ref[...]); o += NA_W
    va = y[:, o:o + NA_W]; o += NA_W
    qb = _head_rms(y[:, o:o + GQ_W], gqb_ref[...]); o += GQ_W
    kb = _head_rms(y[:, o:o + GKV_W], gkb_ref[...]); o += GKV_W
    vb = y[:, o:o + GKV_W]
    if rope:
        cos, sin = cos_ref[...], sin_ref[...]
        qb = _rope(qb, cos, sin)
        kb = _rope(kb, cos, sin)
    q_ref[:, :NA_W] = (qa * Q_SCALE).astype(BF16)
    q_ref[:, NA_W:] = (qb * Q_SCALE).astype(BF16)
    low = _low_half()
    vb_swapped = pltpu.roll(vb, HEAD_DIM, 1)
    v_ref[:, :NA_W] = va.astype(BF16)
    v_ref[:, NA_W:NA_W + LANES] = jnp.where(low, vb, vb_swapped).astype(BF16)
    v_ref[:, NA_W + LANES:] = jnp.where(low, vb_swapped, vb).astype(BF16)
    n_seq, _, s = kt_ref.shape
    for i in range(n_seq):
        rows = slice(i * s, (i + 1) * s)
        ka_t = ka[rows].T
        kb_t = kb[rows].T
        k0, k1 = kb_t[:HEAD_DIM], kb_t[HEAD_DIM:]
        kt_ref[i, :NA_W, :] = ka_t.astype(BF16)
        kt_ref[i, NA_W:, :] = jnp.concatenate([k0, k0, k1, k1], axis=0).astype(BF16)
        if cache_out:
            kat_ref, vat_ref, kbt_ref, vbt_ref = refs[pos:pos + 4]
            kat_ref[i] = ka_t
            vat_ref[i] = va[rows].T
            kbt_ref[i] = kb_t
            vbt_ref[i] = vb[rows].T


def _qkv(h, mod, g, w_in, gains, cond_div, seq, rope_tabs=None, cache_out=False):
    t = h.shape[0]
    tm = 512
    rope = rope_tabs is not None
    assert not (rope and cache_out)
    row = lambda i: (i, 0)
    const = lambda i: (0, 0)
    in_specs = [
        pl.BlockSpec((tm, D_MODEL), row),
        pl.BlockSpec((1, N_MOD, D_MODEL), lambda i: (i // cond_div, 0, 0)),
        pl.BlockSpec((1, D_MODEL), const),
        pl.BlockSpec((None, D_MODEL, QKV_WIDTH), lambda i: (0, 0, 0)),
        pl.BlockSpec((1, NA_W), const),
        pl.BlockSpec((1, NA_W), const),
        pl.BlockSpec((1, GQ_W), const),
        pl.BlockSpec((1, GKV_W), const),
    ]
    args = [h, mod, g, w_in, *gains]
    if rope:
        seq_tiles = rope_tabs[0].shape[0] // tm
        in_specs += [pl.BlockSpec((tm, 2 * HEAD_DIM), lambda i: (i % seq_tiles, 0))] * 2
        args += list(rope_tabs)
    if seq >= tm:
        per_seq = seq // tm
        t_block = lambda rows: pl.BlockSpec((1, rows, tm), lambda i: (i // per_seq, 0, i % per_seq))
    else:
        t_block = lambda rows: pl.BlockSpec((tm // seq, rows, seq), lambda i: (i, 0, 0))
    t_shape = lambda rows, dtype: jax.ShapeDtypeStruct((t // seq, rows, seq), dtype)
    out_specs = [pl.BlockSpec((tm, NA_W + GQ_W), row), t_block(KT_ROWS), pl.BlockSpec((tm, KT_ROWS), row)]
    out_shape = [jax.ShapeDtypeStruct((t, NA_W + GQ_W), BF16), t_shape(KT_ROWS, BF16),
                 jax.ShapeDtypeStruct((t, KT_ROWS), BF16)]
    if cache_out:
        out_specs += [t_block(NA_W), t_block(NA_W), t_block(GKV_W), t_block(GKV_W)]
        out_shape += [t_shape(NA_W, F32), t_shape(NA_W, F32), t_shape(GKV_W, F32), t_shape(GKV_W, F32)]
    return pl.pallas_call(
        functools.partial(_qkv_kernel, rope=rope, cache_out=cache_out),
        grid=(t // tm,),
        in_specs=in_specs,
        out_specs=out_specs,
        out_shape=out_shape,
        compiler_params=_params("arbitrary"),
        name="qkv_latent" if rope else "qkv_context",
    )(*args)


def _split_pair(q2):
    low = _low_half()
    zero = jnp.zeros((), q2.dtype)
    return jnp.where(low, q2, zero), jnp.where(low, zero, q2)


def _with_ones(v2):
    return jnp.concatenate([v2, jnp.ones(v2.shape, v2.dtype)], axis=1)


def _attend(qm, kts, biases, vexts):
    scores = [_dot(qm, kt) if b is None else _dot(qm, kt) + b for kt, b in zip(kts, biases)]
    m = functools.reduce(jnp.maximum, [jnp.max(s, axis=-1, keepdims=True) for s in scores])
    return functools.reduce(jnp.add, [_dot(jnp.exp(s - m).astype(BF16), v) for s, v in zip(scores, vexts)])


def _merge_pair(oe_even, oe_odd):
    even = oe_even[:, :LANES] * (1.0 / oe_even[:, LANES:])
    odd = oe_odd[:, :LANES] * (1.0 / oe_odd[:, LANES:])
    return jnp.where(_low_half(), even, odd)


def _ctx_attn_kernel(q_ref, kt_ref, v_ref, o_ref):
    s = q_ref.shape[0]
    for p in range(NA_HEADS // 2):
        sl = slice(p * LANES, (p + 1) * LANES)
        vext = _with_ones(v_ref[:, sl])
        oe = [_attend(qm, [kt_ref[sl, :]], [None], [vext]) for qm in _split_pair(q_ref[:, sl])]
        o_ref[:, sl] = _merge_pair(*oe).astype(o_ref.dtype)
    for g in range(GQA_KV_HEADS):
        kv = slice(NA_W + g * LANES, NA_W + (g + 1) * LANES)
        slabs = [slice(NA_W + (2 * g + j) * LANES, NA_W + (2 * g + j + 1) * LANES) for j in range(2)]
        qm = jnp.concatenate([part for sl in slabs for part in _split_pair(q_ref[:, sl])], axis=0)
        oe = _attend(qm, [kt_ref[kv, :]], [None], [_with_ones(v_ref[:, kv])])
        for j, sl in enumerate(slabs):
            o_ref[:, sl] = _merge_pair(oe[2 * j * s:(2 * j + 1) * s],
                                       oe[(2 * j + 1) * s:(2 * j + 2) * s]).astype(o_ref.dtype)


def _ctx_attention(q, kt, v, seq):
    t = q.shape[0]
    row = lambda b: (b, 0)
    return pl.pallas_call(
        _ctx_attn_kernel,
        grid=(t // seq,),
        in_specs=[pl.BlockSpec((seq, NA_W + GQ_W), row),
                  pl.BlockSpec((None, KT_ROWS, seq), lambda b: (b, 0, 0)),
                  pl.BlockSpec((seq, KT_ROWS), row)],
        out_specs=pl.BlockSpec((seq, NA_W + GQ_W), row),
        out_shape=jax.ShapeDtypeStruct((t, NA_W + GQ_W), BF16),
        compiler_params=_params("arbitrary"),
        name="ctx_attention",
    )(q, kt, v)


def _na_kernel(q_ref, kt0_ref, kt1_ref, kt2_ref, v0_ref, v1_ref, v2_ref, ckt_ref, cv_ref, bias_ref, o_ref):
    for p in range(NA_HEADS // 2):
        sl = slice(p * LANES, (p + 1) * LANES)
        kt_loc = jnp.concatenate([kt0_ref[sl, :], kt1_ref[sl, :], kt2_ref[sl, :]], axis=1)
        v_loc = _with_ones(jnp.concatenate([v0_ref[:, sl], v1_ref[:, sl], v2_ref[:, sl]], axis=0))
        v_ctx = _with_ones(cv_ref[:, sl])
        oe = [_attend(qm, [kt_loc, ckt_ref[sl, :]], [bias_ref[2 * p + half], None], [v_loc, v_ctx])
              for half, qm in enumerate(_split_pair(q_ref[:, sl]))]
        o_ref[:, sl] = _merge_pair(*oe).astype(o_ref.dtype)


def _na_bias(rpb, rows):
    qc = np.arange(GRID_W)
    kc = np.arange(GRID_W)
    ws = np.clip(qc - NA_WIN_C // 2, 0, GRID_W - NA_WIN_C)
    col_ok = (kc[None, :] >= ws[:, None]) & (kc[None, :] < ws[:, None] + NA_WIN_C)
    dc = np.clip(kc[None, :] - qc[:, None] + NA_WIN_C - 1, 0, 2 * NA_WIN_C - 2)
    pick = ((dc[:, :, None] == np.arange(2 * NA_WIN_C - 1)) & col_ok[:, :, None]).astype(np.float32)
    tiles = jnp.einsum("hrd,qkd->hrqk", rpb.astype(F32), pick, precision=lax.Precision.HIGHEST)
    tiles = jnp.where(col_ok, tiles, NEG_INF)
    masked = jnp.full((NA_HEADS, GRID_W, GRID_W), NEG_INF, F32)
    out = []
    nb = rows // NA_BAND_ROWS
    for band in (0, 1, nb - 1):
        r0 = band * NA_BAND_ROWS
        k0 = int(np.clip(r0 - NA_WIN_R // 2, 0, rows - NA_BAND_KEY_ROWS))
        q_rows = []
        for r in range(r0, r0 + NA_BAND_ROWS):
            start = int(np.clip(r - NA_WIN_R // 2, 0, rows - NA_WIN_R))
            q_rows.append(jnp.concatenate(
                [tiles[:, kr - r + NA_WIN_R - 1] if start <= kr < start + NA_WIN_R else masked
                 for kr in range(k0, k0 + NA_BAND_KEY_ROWS)], axis=-1))
        out.append(jnp.concatenate(q_rows, axis=1))
    return jnp.stack(out)


def _na_attention(q, kt, v, ctx_kt, ctx_v, bias, batch, seq):
    rows = seq // GRID_W
    nb = rows // NA_BAND_ROWS
    past = ctx_v.shape[1]
    assert (rows - NA_BAND_KEY_ROWS) % NA_BAND_ROWS == 0
    n_chunks = NA_BAND_KEY_ROWS // NA_BAND_ROWS
    band_type = lambda j: jnp.where(j == 0, 0, jnp.where(j == nb - 1, 2, 1))
    chunk0 = lambda j: jnp.clip(j - NA_WIN_R // 2 // NA_BAND_ROWS, 0, nb - n_chunks)
    kt_spec = lambda c: pl.BlockSpec((None, NA_W, NA_BAND_Q), lambda b, j: (b, 0, chunk0(j) + c))
    v_spec = lambda c: pl.BlockSpec((NA_BAND_Q, NA_W), lambda b, j: (b * nb + chunk0(j) + c, 0))
    return pl.pallas_call(
        _na_kernel,
        grid=(batch, nb),
        in_specs=[pl.BlockSpec((NA_BAND_Q, NA_W), lambda b, j: (b * nb + j, 0))]
                 + [kt_spec(c) for c in range(n_chunks)] + [v_spec(c) for c in range(n_chunks)]
                 + [pl.BlockSpec((None, NA_W, past), lambda b, j: (b, 0, 0)),
                    pl.BlockSpec((None, past, NA_W), lambda b, j: (b, 0, 0)),
                    pl.BlockSpec((None, NA_HEADS, NA_BAND_Q, NA_BAND_K), lambda b, j: (band_type(j), 0, 0, 0))],
        out_specs=pl.BlockSpec((NA_BAND_Q, NA_W), lambda b, j: (b * nb + j, 0)),
        out_shape=jax.ShapeDtypeStruct((batch * seq, NA_W), BF16),
        compiler_params=_params("arbitrary", "arbitrary"),
        name="na_attention",
    )(q, *([kt] * n_chunks), *([v] * n_chunks), ctx_kt, ctx_v, bias)


def _gqa_kernel(q_ref, kt_ref, v_ref, ckt_ref, cv_ref, o_ref):
    tq = q_ref.shape[0]
    for g in range(GQA_KV_HEADS):
        kv = slice(g * LANES, (g + 1) * LANES)
        slabs = [slice((2 * g + j) * LANES, (2 * g + j + 1) * LANES) for j in range(2)]
        qm = jnp.concatenate([part for sl in slabs for part in _split_pair(q_ref[:, sl])], axis=0)
        oe = _attend(qm, [kt_ref[kv, :], ckt_ref[kv, :]], [None, None],
                     [_with_ones(v_ref[:, kv]), _with_ones(cv_ref[:, kv])])
        for j, sl in enumerate(slabs):
            o_ref[:, sl] = _merge_pair(oe[2 * j * tq:(2 * j + 1) * tq],
                                       oe[(2 * j + 1) * tq:(2 * j + 2) * tq]).astype(o_ref.dtype)


def _gqa_attention(q, kt, v, ctx_kt, ctx_v, batch, seq):
    tq = 128
    nt = seq // tq
    past = ctx_v.shape[1]
    dup = 2 * GKV_W
    return pl.pallas_call(
        _gqa_kernel,
        grid=(batch, nt),
        in_specs=[
            pl.BlockSpec((tq, GQ_W), lambda b, i: (b * nt + i, 1)),
            pl.BlockSpec((None, dup, seq), lambda b, i: (b, NA_W // dup, 0)),
            pl.BlockSpec((seq, dup), lambda b, i: (b, NA_W // dup)),
            pl.BlockSpec((None, dup, past), lambda b, i: (b, 0, 0)),
            pl.BlockSpec((None, past, dup), lambda b, i: (b, 0, 0)),
        ],
        out_specs=pl.BlockSpec((tq, GQ_W), lambda b, i: (b * nt + i, 0)),
        out_shape=jax.ShapeDtypeStruct((batch * seq, GQ_W), BF16),
        compiler_params=_params("arbitrary", "arbitrary"),
        name="gqa_attention",
    )(q, kt, v, ctx_kt, ctx_v)


def _outproj_kernel(*refs, n_in):
    h_ref, mod_ref = refs[:2]
    o_refs = refs[2:2 + n_in]
    w_ref, out_ref = refs[2 + n_in:]
    acc = None
    row = 0
    for o_ref in o_refs:
        width = o_ref.shape[1]
        part = _dot(o_ref[...], w_ref[row:row + width, :].astype(BF16))
        acc = part if acc is None else acc + part
        row += width
    out_ref[...] = h_ref[...] + _gate(mod_ref, 1) * acc


def _outproj(h, mod, heads_out, w_out, cond_div):
    t = h.shape[0]
    tm = 512
    row = lambda i: (i, 0)
    n_in = len(heads_out)
    return pl.pallas_call(
        functools.partial(_outproj_kernel, n_in=n_in),
        grid=(t // tm,),
        in_specs=[pl.BlockSpec((tm, D_MODEL), row),
                  pl.BlockSpec((1, N_MOD, D_MODEL), lambda i: (i // cond_div, 0, 0))]
                 + [pl.BlockSpec((tm, o.shape[1]), row) for o in heads_out]
                 + [pl.BlockSpec((None, D_MODEL, D_MODEL), lambda i: (0, 0, 0))],
        out_specs=pl.BlockSpec((tm, D_MODEL), row),
        out_shape=jax.ShapeDtypeStruct((t, D_MODEL), F32),
        compiler_params=_params("arbitrary"),
        name="attn_outproj",
    )(h, mod, *heads_out, w_out)


def _sgu_kernel(h_ref, mod_ref, g_ref, win_ref, vg_ref, ws_ref, bs_ref, wout_ref, o_ref, gated_ref):
    tm = h_ref.shape[0]
    xn = _adaln(h_ref[...], g_ref[...], mod_ref, 1).astype(BF16)
    y = _dot(xn, win_ref[...].astype(BF16))
    y = 0.5 * y * (1.0 + lax.erf(y * math.sqrt(0.5)))
    u = y[:, :SGU_WIDTH]
    v = y[:, SGU_WIDTH:]
    v = (v * lax.rsqrt(jnp.mean(v * v, axis=-1, keepdims=True) + EPS)) * vg_ref[...]
    vb = v.astype(BF16)
    for c in range(tm // SGU_CHUNK):
        rs = slice(c * SGU_CHUNK, (c + 1) * SGU_CHUNK)
        for g in range(SGU_GROUPS):
            cs = slice(g * SGU_GW, (g + 1) * SGU_GW)
            sv = _dot(ws_ref[g].astype(BF16), vb[rs, cs]) + bs_ref[:, g:g + 1]
            gated_ref[rs, cs] = (u[rs, cs] * sv).astype(BF16)
    out = _dot(gated_ref[...], wout_ref[...].astype(BF16))
    o_ref[...] = h_ref[...] + _gate(mod_ref, 1) * out


def _sgu(h, mod, g, w_in, v_g, w_s, b_s_t, w_out, cond_div):
    t = h.shape[0]
    tm = 256
    row = lambda i: (i, 0)
    const = lambda i: (0, 0)
    once = pl.Buffered(1)
    return pl.pallas_call(
        _sgu_kernel,
        grid=(t // tm,),
        in_specs=[
            pl.BlockSpec((tm, D_MODEL), row),
            pl.BlockSpec((1, N_MOD, D_MODEL), lambda i: (i // cond_div, 0, 0)),
            pl.BlockSpec((1, D_MODEL), const),
            pl.BlockSpec((None, D_MODEL, 2 * SGU_WIDTH), lambda i: (0, 0, 0), pipeline_mode=once),
            pl.BlockSpec((1, SGU_WIDTH), const),
            pl.BlockSpec((None, SGU_GROUPS, SGU_CHUNK, SGU_CHUNK), lambda i: (0, 0, 0, 0)),
            pl.BlockSpec((SGU_CHUNK, SGU_GROUPS), const),
            pl.BlockSpec((None, SGU_WIDTH, D_MODEL), lambda i: (0, 0, 0), pipeline_mode=once),
        ],
        out_specs=pl.BlockSpec((tm, D_MODEL), row),
        out_shape=jax.ShapeDtypeStruct((t, D_MODEL), F32),
        scratch_shapes=[pltpu.VMEM((tm, SGU_WIDTH), BF16)],
        compiler_params=_params("arbitrary"),
        name="sgu",
    )(h, mod, g, w_in, v_g, w_s, b_s_t, w_out)


def _rope_tables(seq):
    half = HEAD_DIM // 2
    t = jnp.arange(seq)
    freqs = ROPE_BASE ** (-jnp.arange(0, half, 2, dtype=F32) / half)
    def tab(pos):
        ang = pos.astype(F32)[:, None] * freqs[None, :]
        cos, sin = jnp.cos(ang), jnp.sin(ang)
        return jnp.concatenate([cos, cos], -1), jnp.concatenate([-sin, sin], -1)
    cr, sr = tab(t // GRID_W)
    cc, sc = tab(t % GRID_W)
    cos = jnp.concatenate([cr, cc], -1)
    sin = jnp.concatenate([sr, sc], -1)
    return jnp.tile(cos, (1, 2)), jnp.tile(sin, (1, 2))


def kernel(x_prompt, x_sample, cache_na_k, cache_na_v, cache_gqa_k, cache_gqa_v, c, c_ctx, norm_g, mod_w, mod_b, ffn1_w_gu, ffn1_w_down, ffn2_w_gu, ffn2_w_down, attn_w_in, attn_w_out, na_q_g, na_k_g, na_rpb, gqa_q_g, gqa_k_g, sgu_w_in, sgu_v_g, sgu_w_s, sgu_b_s, sgu_w_out):
    batch, seq, _ = x_prompt.shape
    dec_batch, dec_seq, _ = x_sample.shape
    past = cache_na_k.shape[2]
    hp = x_prompt.reshape(batch * seq, D_MODEL)
    hs = x_sample.reshape(dec_batch * dec_seq, D_MODEL)

    cond8 = jnp.zeros((8, D_MODEL), F32).at[0].set(c_ctx).at[1:1 + dec_batch].set(c)
    mod_all = _modulation(cond8, mod_w, mod_b).reshape(DEPTH, 8, N_MOD, D_MODEL)

    new_cache = None
    for layer in range(DEPTH):
        mod_l = mod_all[layer]
        mods = (mod_l[0:1], mod_l[1:1 + dec_batch])
        g0, g1, g2 = (norm_g[layer, s][None, :] for s in range(3))

        mod3 = mod_l[0:1 + dec_batch]
        hp, hs = _ffn(hp, hs, mod3, g0, ffn1_w_gu, ffn1_w_down, layer, 0, dec_seq)

        if layer % 2 == 0:
            e = layer // 2
            gains = (jnp.tile(na_q_g[e], NA_HEADS)[None], jnp.tile(na_k_g[e], NA_HEADS)[None],
                     jnp.tile(gqa_q_g[e], GQA_HEADS)[None], jnp.tile(gqa_k_g[e], GQA_KV_HEADS)[None])
            w_in = attn_w_in[e][None]
            w_out = attn_w_out[e][None]
            q, kt, v, *new_cache = _qkv(hp, mods[0], g1, w_in, gains, hp.shape[0] // 512, seq,
                                        cache_out=True)
            o = _ctx_attention(q, kt, v, seq)
            hp = _outproj(hp, mods[0], [o], w_out, hp.shape[0] // 512)
            q, kt, v = _qkv(hs, mods[1], g1, w_in, gains, dec_seq // 512, dec_seq,
                            rope_tabs=_rope_tables(dec_seq))
            to_kt = lambda ck, rep: jnp.repeat(jnp.transpose(ck[:, e], (0, 2, 3, 1)), rep, axis=1).reshape(
                dec_batch, -1, past).astype(BF16)
            to_v = lambda cv, rep: jnp.repeat(cv[:, e], rep, axis=2).reshape(dec_batch, past, -1).astype(BF16)
            oa = _na_attention(q, kt, v, to_kt(cache_na_k, 1), to_v(cache_na_v, 1),
                               _na_bias(na_rpb[e], dec_seq // GRID_W), dec_batch, dec_seq)
            ob = _gqa_attention(q, kt, v, to_kt(cache_gqa_k, 2), to_v(cache_gqa_v, 2), dec_batch, dec_seq)
            hs = _outproj(hs, mods[1], [oa, ob], w_out, dec_seq // 512)
        else:
            o = layer // 2
            sgu_args = (g1, sgu_w_in[o][None], sgu_v_g[o][None], sgu_w_s[o][None], sgu_b_s[o].T,
                        sgu_w_out[o][None])
            hp = _sgu(hp, mods[0], *sgu_args, hp.shape[0] // 256)
            hs = _sgu(hs, mods[1], *sgu_args, dec_seq // 256)

        hp, hs = _ffn(hp, hs, mod3, g2, ffn2_w_gu, ffn2_w_down, layer, 2, dec_seq)

    n_attn = (DEPTH + 1) // 2
    assert n_attn == 1
    from_t = lambda x: jnp.transpose(x.reshape(batch, n_attn, -1, HEAD_DIM, seq), (0, 1, 4, 2, 3))
    return (hp.reshape(batch, seq, D_MODEL), hs.reshape(dec_batch, dec_seq, D_MODEL),
            *(from_t(x) for x in new_cache))
```

```python
import functools
import math

import jax
import jax.numpy as jnp
import numpy as np
from jax import lax
from jax.experimental import pallas as pl
from jax.experimental.pallas import tpu as pltpu

D_MODEL = 1024
DEPTH = 2
GRID_W = 64
HEAD_DIM = 64
NA_HEADS = 8
NA_WIN_R = 8
NA_WIN_C = 16
GQA_HEADS = 8
GQA_KV_HEADS = 2
GQA_GROUP = GQA_HEADS // GQA_KV_HEADS
NA_W = NA_HEADS * HEAD_DIM
GQ_W = GQA_HEADS * HEAD_DIM
GKV_W = GQA_KV_HEADS * HEAD_DIM
QKV_WIDTH = 3 * NA_W + GQ_W + 2 * GKV_W
ROPE_BASE = 10000.0
SGU_CHUNK = 128
SGU_GROUPS = 8
SGU_WIDTH = 2 * D_MODEL
SGU_GW = SGU_WIDTH // SGU_GROUPS
FFN_HIDDEN = 2816
N_MOD = 9
EPS = 1e-6
NEG_INF = -1e30
Q_SCALE = HEAD_DIM ** -0.5

NA_BAND_ROWS = 4
NA_BAND_KEY_ROWS = 12
NA_BAND_Q = NA_BAND_ROWS * GRID_W
NA_BAND_K = NA_BAND_KEY_ROWS * GRID_W

FFN_TM = 512
FFN_ACT_CHUNK = 512
SGU_ACT_CHUNK = 512
SGU_TM = 512
PROJ_TM = 512
W_STAGE = 256

LANES = 128
KT_ROWS = NA_W + 2 * GKV_W
MXU_COLS = 256
VMEM_LIMIT = 56 * 1024 * 1024

BF16 = jnp.bfloat16
F32 = jnp.float32


def _dot(a, b):
    return jnp.dot(a, b, preferred_element_type=F32)


def _params(*sem):
    return pltpu.CompilerParams(dimension_semantics=sem, vmem_limit_bytes=VMEM_LIMIT)


def _adaln(x, g, mod_ref, j):
    shift = mod_ref[0, 3 * j:3 * j + 1, :]
    scale = mod_ref[0, 3 * j + 1:3 * j + 2, :]
    y = x * lax.rsqrt(jnp.mean(x * x, axis=-1, keepdims=True) + EPS)
    return (y * g) * (1.0 + scale) + shift


def _gate(mod_ref, j):
    return mod_ref[0, 3 * j + 2:3 * j + 3, :]


def _mod_kernel(c_ref, w_ref, b_ref, o_ref):
    c = c_ref[...]
    a = (c * jax.nn.sigmoid(c)).astype(BF16)
    o_ref[...] = _dot(a, w_ref[...].astype(BF16)) + b_ref[...]


def _modulation(cond8, mod_w, mod_b):
    tn = 1024
    n = N_MOD * D_MODEL
    return pl.pallas_call(
        _mod_kernel,
        grid=(DEPTH, n // tn),
        in_specs=[
            pl.BlockSpec((8, D_MODEL), lambda l, k: (0, 0)),
            pl.BlockSpec((None, D_MODEL, tn), lambda l, k: (l, 0, k)),
            pl.BlockSpec((None, 1, tn), lambda l, k: (l, 0, k)),
        ],
        out_specs=pl.BlockSpec((None, 8, tn), lambda l, k: (l, 0, k)),
        out_shape=jax.ShapeDtypeStruct((DEPTH, 8, n), F32),
        compiler_params=_params("arbitrary", "arbitrary"),
        name="modulation",
    )(cond8, mod_w, mod_b.reshape(DEPTH, 1, n))


def _two_stream_maps(tm, n_ctx_rows, dec_seq):
    n_ctx = n_ctx_rows // tm
    per_seq = dec_seq // tm
    ctx_map = lambda i: (jnp.minimum(i, n_ctx - 1), 0)
    lat_map = lambda i: (jnp.maximum(i - n_ctx, 0), 0)
    mod_map = lambda i: (jnp.where(i < n_ctx, 0, 1 + (i - n_ctx) // per_seq), 0, 0)
    return n_ctx, ctx_map, lat_map, mod_map


def _stream_cast(pairs, stage_ref, sem_ref):
    def copy(k):
        return pltpu.make_async_copy(pairs[k][0], stage_ref.at[k % 2], sem_ref.at[k % 2])
    copy(0).start()
    for k in range(len(pairs)):
        if k + 1 < len(pairs):
            copy(k + 1).start()
        copy(k).wait()
        dst, idx = pairs[k][1]
        dst[idx] = stage_ref[k % 2].astype(BF16)


def _load_by_cols(w_hbm, layer, w_ref, stage_ref, sem_ref):
    if len(w_ref.shape) == 2:
        dst = lambda c: (slice(None), slice(c, c + W_STAGE))
        n = w_ref.shape[1]
    else:
        width = w_ref.shape[2]
        dst = lambda c: (c // width, slice(None), slice(c % width, c % width + W_STAGE))
        n = w_ref.shape[0] * width
    _stream_cast([(w_hbm.at[layer, :, pl.ds(c, W_STAGE)], (w_ref, dst(c))) for c in range(0, n, W_STAGE)],
                 stage_ref, sem_ref)


def _load_by_rows(w_hbm, layer, w_ref, stage_ref, sem_ref):
    _stream_cast([(w_hbm.at[layer, pl.ds(r, W_STAGE), :], (w_ref, (slice(r, r + W_STAGE), slice(None))))
                  for r in range(0, w_ref.shape[0], W_STAGE)], stage_ref, sem_ref)


COL_STAGE = (pltpu.VMEM((2, D_MODEL, W_STAGE), F32), pltpu.SemaphoreType.DMA((2,)))
ROW_STAGE = (pltpu.VMEM((2, W_STAGE, D_MODEL), F32), pltpu.SemaphoreType.DMA((2,)))
HBM = pl.BlockSpec(memory_space=pl.ANY)


def _ffn_kernel(hp_ref, hs_ref, mod_ref, g_ref, wgu_hbm, wd_hbm, op_ref, os_ref,
                wgu_ref, wd_ref, stage_gu, sem_gu, stage_d, sem_d, act_ref, *, layer, j, n_ctx):
    i = pl.program_id(0)

    @pl.when(i == 0)
    def _load_weights():
        _load_by_cols(wgu_hbm, layer, wgu_ref, stage_gu, sem_gu)
        _load_by_rows(wd_hbm, layer, wd_ref, stage_d, sem_d)

    is_ctx = i < n_ctx
    h = jnp.where(is_ctx, hp_ref[...], hs_ref[...])
    xn = _adaln(h, g_ref[...], mod_ref, j).astype(BF16)
    for c in range(0, FFN_HIDDEN, FFN_ACT_CHUNK):
        w = min(FFN_ACT_CHUNK, FFN_HIDDEN - c)
        gate = _dot(xn, wgu_ref[:, c:c + w])
        up = _dot(xn, wgu_ref[:, FFN_HIDDEN + c:FFN_HIDDEN + c + w])
        act_ref[:, c:c + w] = ((gate * jax.nn.sigmoid(gate)) * up).astype(BF16)
    res = h + (0.5 * _gate(mod_ref, j)) * _dot(act_ref[...], wd_ref[...])

    @pl.when(is_ctx)
    def _():
        op_ref[...] = res

    @pl.when(jnp.logical_not(is_ctx))
    def _():
        os_ref[...] = res


def _ffn(hp, hs, mod, g, w_gu, w_down, layer, j, dec_seq):
    tm = FFN_TM
    n_ctx, ctx_map, lat_map, mod_map = _two_stream_maps(tm, hp.shape[0], dec_seq)
    row_spec = lambda m: pl.BlockSpec((tm, D_MODEL), m)
    return pl.pallas_call(
        functools.partial(_ffn_kernel, layer=layer, j=j, n_ctx=n_ctx),
        grid=((hp.shape[0] + hs.shape[0]) // tm,),
        in_specs=[
            row_spec(ctx_map), row_spec(lat_map),
            pl.BlockSpec((1, N_MOD, D_MODEL), mod_map),
            pl.BlockSpec((1, D_MODEL), lambda i: (0, 0)),
            HBM, HBM,
        ],
        out_specs=[row_spec(ctx_map), row_spec(lat_map)],
        out_shape=[jax.ShapeDtypeStruct(hp.shape, F32), jax.ShapeDtypeStruct(hs.shape, F32)],
        scratch_shapes=[
            pltpu.VMEM((D_MODEL, 2 * FFN_HIDDEN), BF16),
            pltpu.VMEM((FFN_HIDDEN, D_MODEL), BF16),
            *COL_STAGE, *ROW_STAGE,
            pltpu.VMEM((tm, FFN_HIDDEN), BF16),
        ],
        compiler_params=_params("arbitrary"),
        name=f"ffn{j}_l{layer}",
    )(hp, hs, mod, g, w_gu, w_down)


def _head_mean_sq(x):
    w = x.shape[-1]
    r = lax.broadcasted_iota(jnp.int32, (MXU_COLS, MXU_COLS), 0) // HEAD_DIM
    c = lax.broadcasted_iota(jnp.int32, (MXU_COLS, MXU_COLS), 1) // HEAD_DIM
    bd = jnp.where(r == c, 1.0 / HEAD_DIM, 0.0).astype(BF16)
    sq = (x * x).astype(BF16)
    parts = []
    for s in range(0, w, MXU_COLS):
        e = min(s + MXU_COLS, w)
        parts.append(_dot(sq[:, s:e], bd[:e - s, :e - s]))
    return parts[0] if len(parts) == 1 else jnp.concatenate(parts, axis=-1)


def _low_half():
    return lax.broadcasted_iota(jnp.int32, (1, LANES), 1) < HEAD_DIM


def _head_rms(x, gain):
    return (x * lax.rsqrt(_head_mean_sq(x) + EPS)) * gain


def _rope(x, cos, sin_signed):
    lanes = cos.shape[-1]
    outs = []
    for s in range(0, x.shape[-1], lanes):
        xs = x[:, s:s + lanes]
        blk = lax.broadcasted_iota(jnp.int32, xs.shape, 1) // (HEAD_DIM // 4)
        partner = jnp.where(blk % 2 == 0,
                            pltpu.roll(xs, lanes - HEAD_DIM // 4, 1),
                            pltpu.roll(xs, HEAD_DIM // 4, 1))
        outs.append(xs * cos + partner * sin_signed)
    return outs[0] if len(outs) == 1 else jnp.concatenate(outs, axis=-1)


def _qkv_kernel(*refs, layer, rope, cache_out):
    h_ref, mod_ref, g_ref, w_hbm, gqa_ref, gka_ref, gqb_ref, gkb_ref = refs[:8]
    w_ref, stage_ref, sem_ref = refs[-3:]
    pos = 8
    if rope:
        cos_ref, sin_ref = refs[pos:pos + 2]
        pos += 2
    q_ref, kt_ref, v_ref = refs[pos:pos + 3]
    pos += 3

    @pl.when(pl.program_id(0) == 0)
    def _load_weights():
        _load_by_cols(w_hbm, layer, w_ref, stage_ref, sem_ref)

    xn = _adaln(h_ref[...], g_ref[...], mod_ref, 1).astype(BF16)
    y = _dot(xn, w_ref[...])
    o = 0
    qa = _head_rms(y[:, o:o + NA_W], gqa_ref[...]); o += NA_W
    ka = _head_rms(y[:, o:o + NA_W], gka_ref[...]); o += NA_W
    va = y[:, o:o + NA_W]; o += NA_W
    qb = _head_rms(y[:, o:o + GQ_W], gqb_ref[...]); o += GQ_W
    kb = _head_rms(y[:, o:o + GKV_W], gkb_ref[...]); o += GKV_W
    vb = y[:, o:o + GKV_W]
    if rope:
        cos, sin = cos_ref[...], sin_ref[...]
        qb = _rope(qb, cos, sin)
        kb = _rope(kb, cos, sin)
    q_ref[:, :NA_W] = (qa * Q_SCALE).astype(BF16)
    q_ref[:, NA_W:] = (qb * Q_SCALE).astype(BF16)
    low = _low_half()
    vb_swapped = pltpu.roll(vb, HEAD_DIM, 1)
    v_ref[:, :NA_W] = va.astype(BF16)
    v_ref[:, NA_W:NA_W + LANES] = jnp.where(low, vb, vb_swapped).astype(BF16)
    v_ref[:, NA_W + LANES:] = jnp.where(low, vb_swapped, vb).astype(BF16)
    n_seq, _, s = kt_ref.shape
    for i in range(n_seq):
        rows = slice(i * s, (i + 1) * s)
        ka_t = ka[rows].T
        kb_t = kb[rows].T
        k0, k1 = kb_t[:HEAD_DIM], kb_t[HEAD_DIM:]
        kt_ref[i, :NA_W, :] = ka_t.astype(BF16)
        kt_ref[i, NA_W:, :] = jnp.concatenate([k0, k0, k1, k1], axis=0).astype(BF16)
        if cache_out:
            kat_ref, vat_ref, kbt_ref, vbt_ref = refs[pos:pos + 4]
            kat_ref[i] = ka_t
            vat_ref[i] = va[rows].T
            kbt_ref[i] = kb_t
            vbt_ref[i] = vb[rows].T


def _qkv(h, mod, g, w_in, layer, gains, cond_div, seq, rope_tabs=None, cache_out=False):
    t = h.shape[0]
    tm = PROJ_TM
    rope = rope_tabs is not None
    assert not (rope and cache_out)
    row = lambda i: (i, 0)
    const = lambda i: (0, 0)
    in_specs = [
        pl.BlockSpec((tm, D_MODEL), row),
        pl.BlockSpec((1, N_MOD, D_MODEL), lambda i: (i // cond_div, 0, 0)),
        pl.BlockSpec((1, D_MODEL), const),
        HBM,
        pl.BlockSpec((1, NA_W), const),
        pl.BlockSpec((1, NA_W), const),
        pl.BlockSpec((1, GQ_W), const),
        pl.BlockSpec((1, GKV_W), const),
    ]
    args = [h, mod, g, w_in, *gains]
    if rope:
        seq_tiles = rope_tabs[0].shape[0] // tm
        in_specs += [pl.BlockSpec((tm, 2 * HEAD_DIM), lambda i: (i % seq_tiles, 0))] * 2
        args += list(rope_tabs)
    if seq >= tm:
        per_seq = seq // tm
        t_block = lambda rows: pl.BlockSpec((1, rows, tm), lambda i: (i // per_seq, 0, i % per_seq))
    else:
        t_block = lambda rows: pl.BlockSpec((tm // seq, rows, seq), lambda i: (i, 0, 0))
    t_shape = lambda rows, dtype: jax.ShapeDtypeStruct((t // seq, rows, seq), dtype)
    out_specs = [pl.BlockSpec((tm, NA_W + GQ_W), row), t_block(KT_ROWS), pl.BlockSpec((tm, KT_ROWS), row)]
    out_shape = [jax.ShapeDtypeStruct((t, NA_W + GQ_W), BF16), t_shape(KT_ROWS, BF16),
                 jax.ShapeDtypeStruct((t, KT_ROWS), BF16)]
    if cache_out:
        out_specs += [t_block(NA_W), t_block(NA_W), t_block(GKV_W), t_block(GKV_W)]
        out_shape += [t_shape(NA_W, F32), t_shape(NA_W, F32), t_shape(GKV_W, F32), t_shape(GKV_W, F32)]
    return pl.pallas_call(
        functools.partial(_qkv_kernel, layer=layer, rope=rope, cache_out=cache_out),
        grid=(t // tm,),
        in_specs=in_specs,
        out_specs=out_specs,
        out_shape=out_shape,
        scratch_shapes=[pltpu.VMEM((D_MODEL, QKV_WIDTH), BF16), *COL_STAGE],
        compiler_params=_params("arbitrary"),
        name="qkv_latent" if rope else "qkv_context",
    )(*args)


def _split_pair(q2):
    low = _low_half()
    zero = jnp.zeros((), q2.dtype)
    return jnp.where(low, q2, zero), jnp.where(low, zero, q2)


def _with_ones(v2):
    return jnp.concatenate([v2, jnp.ones(v2.shape, v2.dtype)], axis=1)


def _join(blocks, axis):
    return blocks[0] if len(blocks) == 1 else jnp.concatenate(blocks, axis=axis)


def _attend(problems):
    scores = [_dot(qm, _join(kts, 1)) for qm, kts, _, _ in problems]
    probs = []
    for s, (_, kts, biases, _) in zip(scores, problems):
        parts, start = [], 0
        for kt, b in zip(kts, biases):
            part = s[:, start:start + kt.shape[1]]
            parts.append(part if b is None else part + b)
            start += kt.shape[1]
        m = functools.reduce(jnp.maximum, [jnp.max(part, axis=-1, keepdims=True) for part in parts])
        probs.append(_join([jnp.exp(part - m).astype(BF16) for part in parts], 1))
    return [_dot(p, _join(vexts, 0)) for p, (_, _, _, vexts) in zip(probs, problems)]


def _merge_pair(oe_even, oe_odd):
    even = oe_even[:, :LANES] * (1.0 / oe_even[:, LANES:])
    odd = oe_odd[:, :LANES] * (1.0 / oe_odd[:, LANES:])
    return jnp.where(_low_half(), even, odd)


def _gqa_groups(q_ref, q_off, o_ref, o_off, keys_values):
    tq = q_ref.shape[0]
    slabs = lambda g, off: [slice(off + (2 * g + j) * LANES, off + (2 * g + j + 1) * LANES) for j in range(2)]
    problems = []
    for g in range(GQA_KV_HEADS):
        qm = jnp.concatenate([part for sl in slabs(g, q_off) for part in _split_pair(q_ref[:, sl])], axis=0)
        problems.append((qm, *keys_values(slice(g * LANES, (g + 1) * LANES))))
    for g, oe in enumerate(_attend(problems)):
        for j, sl in enumerate(slabs(g, o_off)):
            o_ref[:, sl] = _merge_pair(oe[2 * j * tq:(2 * j + 1) * tq],
                                       oe[(2 * j + 1) * tq:(2 * j + 2) * tq]).astype(o_ref.dtype)


def _ctx_attn_kernel(q_ref, kt_ref, v_ref, o_ref):
    for p in range(NA_HEADS // 2):
        sl = slice(p * LANES, (p + 1) * LANES)
        vext = _with_ones(v_ref[:, sl])
        oe = [_attend([(qm, [kt_ref[sl, :]], [None], [vext])])[0] for qm in _split_pair(q_ref[:, sl])]
        o_ref[:, sl] = _merge_pair(*oe).astype(o_ref.dtype)
    _gqa_groups(q_ref, NA_W, o_ref, NA_W,
                lambda kv: ([kt_ref[NA_W + kv.start:NA_W + kv.stop, :]], [None],
                            [_with_ones(v_ref[:, NA_W + kv.start:NA_W + kv.stop])]))


def _ctx_attention(q, kt, v, seq):
    t = q.shape[0]
    row = lambda b: (b, 0)
    return pl.pallas_call(
        _ctx_attn_kernel,
        grid=(t // seq,),
        in_specs=[pl.BlockSpec((seq, NA_W + GQ_W), row),
                  pl.BlockSpec((None, KT_ROWS, seq), lambda b: (b, 0, 0)),
                  pl.BlockSpec((seq, KT_ROWS), row)],
        out_specs=pl.BlockSpec((seq, NA_W + GQ_W), row),
        out_shape=jax.ShapeDtypeStruct((t, NA_W + GQ_W), BF16),
        compiler_params=_params("arbitrary"),
        name="ctx_attention",
    )(q, kt, v)


def _na_kernel(q_ref, kt0_ref, kt1_ref, kt2_ref, v0_ref, v1_ref, v2_ref, ckt_ref, cv_ref, bias_ref, o_ref):
    for p in range(NA_HEADS // 2):
        sl = slice(p * LANES, (p + 1) * LANES)
        kt_loc = jnp.concatenate([kt0_ref[sl, :], kt1_ref[sl, :], kt2_ref[sl, :]], axis=1)
        v_loc = _with_ones(jnp.concatenate([v0_ref[:, sl], v1_ref[:, sl], v2_ref[:, sl]], axis=0))
        v_ctx = _with_ones(cv_ref[:, sl])
        oe = _attend([(qm, [kt_loc, ckt_ref[sl, :]], [bias_ref[2 * p + half], None], [v_loc, v_ctx])
                      for half, qm in enumerate(_split_pair(q_ref[:, sl]))])
        o_ref[:, sl] = _merge_pair(*oe).astype(o_ref.dtype)


def _na_bias(rpb, rows):
    qc = np.arange(GRID_W)
    kc = np.arange(GRID_W)
    ws = np.clip(qc - NA_WIN_C // 2, 0, GRID_W - NA_WIN_C)
    col_ok = (kc[None, :] >= ws[:, None]) & (kc[None, :] < ws[:, None] + NA_WIN_C)
    dc = np.clip(kc[None, :] - qc[:, None] + NA_WIN_C - 1, 0, 2 * NA_WIN_C - 2)
    pick = ((dc[:, :, None] == np.arange(2 * NA_WIN_C - 1)) & col_ok[:, :, None]).astype(np.float32)
    tiles = jnp.einsum("hrd,qkd->hrqk", rpb.astype(F32), pick, precision=lax.Precision.HIGHEST)
    tiles = jnp.where(col_ok, tiles, NEG_INF)
    masked = jnp.full((NA_HEADS, GRID_W, GRID_W), NEG_INF, F32)
    out = []
    nb = rows // NA_BAND_ROWS
    for band in (0, 1, nb - 1):
        r0 = band * NA_BAND_ROWS
        k0 = int(np.clip(r0 - NA_WIN_R // 2, 0, rows - NA_BAND_KEY_ROWS))
        q_rows = []
        for r in range(r0, r0 + NA_BAND_ROWS):
            start = int(np.clip(r - NA_WIN_R // 2, 0, rows - NA_WIN_R))
            q_rows.append(jnp.concatenate(
                [tiles[:, kr - r + NA_WIN_R - 1] if start <= kr < start + NA_WIN_R else masked
                 for kr in range(k0, k0 + NA_BAND_KEY_ROWS)], axis=-1))
        out.append(jnp.concatenate(q_rows, axis=1))
    return jnp.stack(out)


def _na_attention(q, kt, v, ctx_kt, ctx_v, bias, batch, seq):
    rows = seq // GRID_W
    nb = rows // NA_BAND_ROWS
    past = ctx_v.shape[1]
    assert (rows - NA_BAND_KEY_ROWS) % NA_BAND_ROWS == 0
    n_chunks = NA_BAND_KEY_ROWS // NA_BAND_ROWS
    band_type = lambda j: jnp.where(j == 0, 0, jnp.where(j == nb - 1, 2, 1))
    chunk0 = lambda j: jnp.clip(j - NA_WIN_R // 2 // NA_BAND_ROWS, 0, nb - n_chunks)
    kt_spec = lambda c: pl.BlockSpec((None, NA_W, NA_BAND_Q), lambda b, j: (b, 0, chunk0(j) + c))
    v_spec = lambda c: pl.BlockSpec((NA_BAND_Q, NA_W), lambda b, j: (b * nb + chunk0(j) + c, 0))
    return pl.pallas_call(
        _na_kernel,
        grid=(batch, nb),
        in_specs=[pl.BlockSpec((NA_BAND_Q, NA_W), lambda b, j: (b * nb + j, 0))]
                 + [kt_spec(c) for c in range(n_chunks)] + [v_spec(c) for c in range(n_chunks)]
                 + [pl.BlockSpec((None, NA_W, past), lambda b, j: (b, 0, 0)),
                    pl.BlockSpec((None, past, NA_W), lambda b, j: (b, 0, 0)),
                    pl.BlockSpec((None, NA_HEADS, NA_BAND_Q, NA_BAND_K), lambda b, j: (band_type(j), 0, 0, 0))],
        out_specs=pl.BlockSpec((NA_BAND_Q, NA_W), lambda b, j: (b * nb + j, 0)),
        out_shape=jax.ShapeDtypeStruct((batch * seq, NA_W), BF16),
        compiler_params=_params("arbitrary", "arbitrary"),
        name="na_attention",
    )(q, *([kt] * n_chunks), *([v] * n_chunks), ctx_kt, ctx_v, bias)


def _gqa_kernel(q_ref, kt_ref, v_ref, ckt_ref, cv_ref, o_ref):
    _gqa_groups(q_ref, 0, o_ref, 0,
                lambda kv: ([kt_ref[kv, :], ckt_ref[kv, :]], [None, None],
                            [_with_ones(v_ref[:, kv]), _with_ones(cv_ref[:, kv])]))


def _gqa_attention(q, kt, v, ctx_kt, ctx_v, batch, seq):
    tq = 128
    nt = seq // tq
    past = ctx_v.shape[1]
    dup = 2 * GKV_W
    return pl.pallas_call(
        _gqa_kernel,
        grid=(batch, nt),
        in_specs=[
            pl.BlockSpec((tq, GQ_W), lambda b, i: (b * nt + i, 1)),
            pl.BlockSpec((None, dup, seq), lambda b, i: (b, NA_W // dup, 0)),
            pl.BlockSpec((seq, dup), lambda b, i: (b, NA_W // dup)),
            pl.BlockSpec((None, dup, past), lambda b, i: (b, 0, 0)),
            pl.BlockSpec((None, past, dup), lambda b, i: (b, 0, 0)),
        ],
        out_specs=pl.BlockSpec((tq, GQ_W), lambda b, i: (b * nt + i, 0)),
        out_shape=jax.ShapeDtypeStruct((batch * seq, GQ_W), BF16),
        compiler_params=_params("arbitrary", "arbitrary"),
        name="gqa_attention",
    )(q, kt, v, ctx_kt, ctx_v)


def _outproj_kernel(*refs, layer, n_in):
    h_ref, mod_ref = refs[:2]
    o_refs = refs[2:2 + n_in]
    w_hbm, out_ref, w_ref, stage_ref, sem_ref = refs[2 + n_in:]

    @pl.when(pl.program_id(0) == 0)
    def _load_weights():
        _load_by_rows(w_hbm, layer, w_ref, stage_ref, sem_ref)

    acc = None
    row = 0
    for o_ref in o_refs:
        width = o_ref.shape[1]
        part = _dot(o_ref[...], w_ref[row:row + width, :])
        acc = part if acc is None else acc + part
        row += width
    out_ref[...] = h_ref[...] + _gate(mod_ref, 1) * acc


def _outproj(h, mod, heads_out, w_out, layer, cond_div):
    t = h.shape[0]
    tm = PROJ_TM
    row = lambda i: (i, 0)
    n_in = len(heads_out)
    return pl.pallas_call(
        functools.partial(_outproj_kernel, layer=layer, n_in=n_in),
        grid=(t // tm,),
        in_specs=[pl.BlockSpec((tm, D_MODEL), row),
                  pl.BlockSpec((1, N_MOD, D_MODEL), lambda i: (i // cond_div, 0, 0))]
                 + [pl.BlockSpec((tm, o.shape[1]), row) for o in heads_out] + [HBM],
        out_specs=pl.BlockSpec((tm, D_MODEL), row),
        out_shape=jax.ShapeDtypeStruct((t, D_MODEL), F32),
        scratch_shapes=[pltpu.VMEM((D_MODEL, D_MODEL), BF16), *ROW_STAGE],
        compiler_params=_params("arbitrary"),
        name="attn_outproj",
    )(h, mod, *heads_out, w_out)


def _sgu_kernel(hp_ref, hs_ref, mod_ref, g_ref, win_hbm, vg_ref, ws_ref, bs_ref, wout_hbm, op_ref, os_ref,
                win_ref, wout_ref, stage_in, sem_in, stage_out, sem_out, gated_ref, y_ref,
                *, layer, n_ctx):
    i = pl.program_id(0)

    @pl.when(i == 0)
    def _load_weights():
        _load_by_cols(win_hbm, layer, win_ref, stage_in, sem_in)
        _load_by_rows(wout_hbm, layer, wout_ref, stage_out, sem_out)

    tm = hp_ref.shape[0]
    is_ctx = i < n_ctx
    h = jnp.where(is_ctx, hp_ref[...], hs_ref[...])
    xn = _adaln(h, g_ref[...], mod_ref, 1).astype(BF16)

    n_chunks = y_ref.shape[0]
    ssq = jnp.zeros((tm, 1), F32)
    for c in range(n_chunks):
        y = _dot(xn, win_ref[c])
        y = 0.5 * y * (1.0 + lax.erf(y * math.sqrt(0.5)))
        y_ref[c] = y
        if c >= n_chunks // 2:
            ssq = ssq + jnp.sum(y * y, axis=-1, keepdims=True)
    v_scale = lax.rsqrt(ssq * (1.0 / SGU_WIDTH) + EPS)

    per_chunk = SGU_ACT_CHUNK // SGU_GW
    for c in range(tm // SGU_CHUNK):
        rs = slice(c * SGU_CHUNK, (c + 1) * SGU_CHUNK)
        for g in range(SGU_GROUPS):
            cs = slice(g * SGU_GW, (g + 1) * SGU_GW)
            in_chunk = slice((g % per_chunk) * SGU_GW, (g % per_chunk + 1) * SGU_GW)
            u = y_ref[g // per_chunk, rs, in_chunk]
            v = y_ref[n_chunks // 2 + g // per_chunk, rs, in_chunk]
            vn = ((v * v_scale[rs]) * vg_ref[:, cs]).astype(BF16)
            sv = _dot(ws_ref[g].astype(BF16), vn) + bs_ref[:, g:g + 1]
            gated_ref[rs, cs] = (u * sv).astype(BF16)
    res = h + _gate(mod_ref, 1) * _dot(gated_ref[...], wout_ref[...])

    @pl.when(is_ctx)
    def _():
        op_ref[...] = res

    @pl.when(jnp.logical_not(is_ctx))
    def _():
        os_ref[...] = res


def _sgu(hp, hs, mod, g, w_in, v_g, w_s, b_s_t, w_out, layer, dec_seq):
    tm = SGU_TM
    n_act = 2 * SGU_WIDTH // SGU_ACT_CHUNK
    n_ctx, ctx_map, lat_map, mod_map = _two_stream_maps(tm, hp.shape[0], dec_seq)
    row_spec = lambda m: pl.BlockSpec((tm, D_MODEL), m)
    const = lambda i: (0, 0)
    return pl.pallas_call(
        functools.partial(_sgu_kernel, layer=layer, n_ctx=n_ctx),
        grid=((hp.shape[0] + hs.shape[0]) // tm,),
        in_specs=[
            row_spec(ctx_map), row_spec(lat_map),
            pl.BlockSpec((1, N_MOD, D_MODEL), mod_map),
            pl.BlockSpec((1, D_MODEL), const),
            HBM,
            pl.BlockSpec((1, SGU_WIDTH), const),
            pl.BlockSpec((None, SGU_GROUPS, SGU_CHUNK, SGU_CHUNK), lambda i: (layer, 0, 0, 0)),
            pl.BlockSpec((SGU_CHUNK, SGU_GROUPS), const),
            HBM,
        ],
        out_specs=[row_spec(ctx_map), row_spec(lat_map)],
        out_shape=[jax.ShapeDtypeStruct(hp.shape, F32), jax.ShapeDtypeStruct(hs.shape, F32)],
        scratch_shapes=[
            pltpu.VMEM((n_act, D_MODEL, SGU_ACT_CHUNK), BF16),
            pltpu.VMEM((SGU_WIDTH, D_MODEL), BF16),
            *COL_STAGE, *ROW_STAGE,
            pltpu.VMEM((tm, SGU_WIDTH), BF16),
            pltpu.VMEM((n_act, tm, SGU_ACT_CHUNK), F32),
        ],
        compiler_params=_params("arbitrary"),
        name="sgu",
    )(hp, hs, mod, g, w_in, v_g, w_s, b_s_t, w_out)


def _rope_tables(seq):
    half = HEAD_DIM // 2
    t = jnp.arange(seq)
    freqs = ROPE_BASE ** (-jnp.arange(0, half, 2, dtype=F32) / half)
    def tab(pos):
        ang = pos.astype(F32)[:, None] * freqs[None, :]
        cos, sin = jnp.cos(ang), jnp.sin(ang)
        return jnp.concatenate([cos, cos], -1), jnp.concatenate([-sin, sin], -1)
    cr, sr = tab(t // GRID_W)
    cc, sc = tab(t % GRID_W)
    cos = jnp.concatenate([cr, cc], -1)
    sin = jnp.concatenate([sr, sc], -1)
    return jnp.tile(cos, (1, 2)), jnp.tile(sin, (1, 2))


def kernel(x_prompt, x_sample, cache_na_k, cache_na_v, cache_gqa_k, cache_gqa_v, c, c_ctx, norm_g, mod_w, mod_b, ffn1_w_gu, ffn1_w_down, ffn2_w_gu, ffn2_w_down, attn_w_in, attn_w_out, na_q_g, na_k_g, na_rpb, gqa_q_g, gqa_k_g, sgu_w_in, sgu_v_g, sgu_w_s, sgu_b_s, sgu_w_out):
    batch, seq, _ = x_prompt.shape
    dec_batch, dec_seq, _ = x_sample.shape
    past = cache_na_k.shape[2]
    hp = x_prompt.reshape(batch * seq, D_MODEL)
    hs = x_sample.reshape(dec_batch * dec_seq, D_MODEL)

    cond8 = jnp.zeros((8, D_MODEL), F32).at[0].set(c_ctx).at[1:1 + dec_batch].set(c)
    mod_all = _modulation(cond8, mod_w, mod_b).reshape(DEPTH, 8, N_MOD, D_MODEL)

    new_cache = None
    for layer in range(DEPTH):
        mod_l = mod_all[layer]
        mods = (mod_l[0:1], mod_l[1:1 + dec_batch])
        g0, g1, g2 = (norm_g[layer, s][None, :] for s in range(3))

        mod3 = mod_l[0:1 + dec_batch]
        hp, hs = _ffn(hp, hs, mod3, g0, ffn1_w_gu, ffn1_w_down, layer, 0, dec_seq)

        if layer % 2 == 0:
            e = layer // 2
            gains = (jnp.tile(na_q_g[e], NA_HEADS)[None], jnp.tile(na_k_g[e], NA_HEADS)[None],
                     jnp.tile(gqa_q_g[e], GQA_HEADS)[None], jnp.tile(gqa_k_g[e], GQA_KV_HEADS)[None])
            q, kt, v, *new_cache = _qkv(hp, mods[0], g1, attn_w_in, e, gains, hp.shape[0] // PROJ_TM, seq,
                                        cache_out=True)
            o = _ctx_attention(q, kt, v, seq)
            hp = _outproj(hp, mods[0], [o], attn_w_out, e, hp.shape[0] // PROJ_TM)
            q, kt, v = _qkv(hs, mods[1], g1, attn_w_in, e, gains, dec_seq // PROJ_TM, dec_seq,
                            rope_tabs=_rope_tables(dec_seq))
            to_kt = lambda ck, rep: jnp.repeat(jnp.transpose(ck[:, e], (0, 2, 3, 1)), rep, axis=1).reshape(
                dec_batch, -1, past).astype(BF16)
            to_v = lambda cv, rep: jnp.repeat(cv[:, e], rep, axis=2).reshape(dec_batch, past, -1).astype(BF16)
            oa = _na_attention(q, kt, v, to_kt(cache_na_k, 1), to_v(cache_na_v, 1),
                               _na_bias(na_rpb[e], dec_seq // GRID_W), dec_batch, dec_seq)
            ob = _gqa_attention(q, kt, v, to_kt(cache_gqa_k, 2), to_v(cache_gqa_v, 2), dec_batch, dec_seq)
            hs = _outproj(hs, mods[1], [oa, ob], attn_w_out, e, dec_seq // PROJ_TM)
        else:
            o = layer // 2
            hp, hs = _sgu(hp, hs, mod3, g1, sgu_w_in, sgu_v_g[o][None], sgu_w_s, sgu_b_s[o].T, sgu_w_out,
                          o, dec_seq)

        hp, hs = _ffn(hp, hs, mod3, g2, ffn2_w_gu, ffn2_w_down, layer, 2, dec_seq)

    n_attn = (DEPTH + 1) // 2
    assert n_attn == 1
    from_t = lambda x: jnp.transpose(x.reshape(batch, n_attn, -1, HEAD_DIM, seq), (0, 1, 4, 2, 3))
    return (hp.reshape(batch, seq, D_MODEL), hs.reshape(dec_batch, dec_seq, D_MODEL),
            *(from_t(x) for x in new_cache))
```

```python
import functools
import math

import jax
import jax.numpy as jnp
import numpy as np
from jax import lax
from jax.experimental import pallas as pl
from jax.experimental.pallas import tpu as pltpu

D_MODEL = 1024
DEPTH = 2
GRID_W = 64
HEAD_DIM = 64
NA_HEADS = 8
NA_WIN_R = 8
NA_WIN_C = 16
GQA_HEADS = 8
GQA_KV_HEADS = 2
GQA_GROUP = GQA_HEADS // GQA_KV_HEADS
NA_W = NA_HEADS * HEAD_DIM
GQ_W = GQA_HEADS * HEAD_DIM
GKV_W = GQA_KV_HEADS * HEAD_DIM
QKV_WIDTH = 3 * NA_W + GQ_W + 2 * GKV_W
ROPE_BASE = 10000.0
SGU_CHUNK = 128
SGU_GROUPS = 8
SGU_WIDTH = 2 * D_MODEL
SGU_GW = SGU_WIDTH // SGU_GROUPS
FFN_HIDDEN = 2816
N_MOD = 9
EPS = 1e-6
NEG_INF = -1e30
Q_SCALE = HEAD_DIM ** -0.5

NA_BAND_ROWS = 4
NA_BAND_KEY_ROWS = 12
NA_BAND_Q = NA_BAND_ROWS * GRID_W
NA_BAND_K = NA_BAND_KEY_ROWS * GRID_W

FFN_TM = 512
FFN_ACT_CHUNK = 512
SGU_ACT_CHUNK = 512
SGU_TM = 512
PROJ_TM = 512
W_STAGE = 256

LANES = 128
KT_ROWS = NA_W + 2 * GKV_W
MXU_COLS = 256
VMEM_LIMIT = 56 * 1024 * 1024

BF16 = jnp.bfloat16
F32 = jnp.float32


def _dot(a, b):
    return jnp.dot(a, b, preferred_element_type=F32)


def _params(*sem):
    return pltpu.CompilerParams(dimension_semantics=sem, vmem_limit_bytes=VMEM_LIMIT)


def _adaln(x, g, mod_ref, j):
    shift = mod_ref[0, 3 * j:3 * j + 1, :]
    scale = mod_ref[0, 3 * j + 1:3 * j + 2, :]
    y = x * lax.rsqrt(jnp.mean(x * x, axis=-1, keepdims=True) + EPS)
    return (y * g) * (1.0 + scale) + shift


def _gate(mod_ref, j):
    return mod_ref[0, 3 * j + 2:3 * j + 3, :]


def _mod_kernel(c_ref, w_ref, b_ref, o_ref):
    c = c_ref[...]
    a = (c * jax.nn.sigmoid(c)).astype(BF16)
    o_ref[...] = _dot(a, w_ref[...].astype(BF16)) + b_ref[...]


def _modulation(cond8, mod_w, mod_b):
    tn = 1024
    n = N_MOD * D_MODEL
    return pl.pallas_call(
        _mod_kernel,
        grid=(DEPTH, n // tn),
        in_specs=[
            pl.BlockSpec((8, D_MODEL), lambda l, k: (0, 0)),
            pl.BlockSpec((None, D_MODEL, tn), lambda l, k: (l, 0, k)),
            pl.BlockSpec((None, 1, tn), lambda l, k: (l, 0, k)),
        ],
        out_specs=pl.BlockSpec((None, 8, tn), lambda l, k: (l, 0, k)),
        out_shape=jax.ShapeDtypeStruct((DEPTH, 8, n), F32),
        compiler_params=_params("arbitrary", "arbitrary"),
        name="modulation",
    )(cond8, mod_w, mod_b.reshape(DEPTH, 1, n))


def _two_stream_maps(tm, n_ctx_rows, dec_seq):
    n_ctx = n_ctx_rows // tm
    per_seq = dec_seq // tm
    ctx_map = lambda i: (jnp.minimum(i, n_ctx - 1), 0)
    lat_map = lambda i: (jnp.maximum(i - n_ctx, 0), 0)
    mod_map = lambda i: (jnp.where(i < n_ctx, 0, 1 + (i - n_ctx) // per_seq), 0, 0)
    return n_ctx, ctx_map, lat_map, mod_map


def _stream_cast(pairs, stage_ref, sem_ref):
    def copy(k):
        return pltpu.make_async_copy(pairs[k][0], stage_ref.at[k % 2], sem_ref.at[k % 2])
    copy(0).start()
    for k in range(len(pairs)):
        if k + 1 < len(pairs):
            copy(k + 1).start()
        copy(k).wait()
        dst, idx = pairs[k][1]
        dst[idx] = stage_ref[k % 2].astype(BF16)


def _load_by_cols(w_hbm, layer, w_ref, stage_ref, sem_ref):
    if len(w_ref.shape) == 2:
        dst = lambda c: (slice(None), slice(c, c + W_STAGE))
        n = w_ref.shape[1]
    else:
        width = w_ref.shape[2]
        dst = lambda c: (c // width, slice(None), slice(c % width, c % width + W_STAGE))
        n = w_ref.shape[0] * width
    _stream_cast([(w_hbm.at[layer, :, pl.ds(c, W_STAGE)], (w_ref, dst(c))) for c in range(0, n, W_STAGE)],
                 stage_ref, sem_ref)


def _load_by_rows(w_hbm, layer, w_ref, stage_ref, sem_ref):
    _stream_cast([(w_hbm.at[layer, pl.ds(r, W_STAGE), :], (w_ref, (slice(r, r + W_STAGE), slice(None))))
                  for r in range(0, w_ref.shape[0], W_STAGE)], stage_ref, sem_ref)


COL_STAGE = (pltpu.VMEM((2, D_MODEL, W_STAGE), F32), pltpu.SemaphoreType.DMA((2,)))
ROW_STAGE = (pltpu.VMEM((2, W_STAGE, D_MODEL), F32), pltpu.SemaphoreType.DMA((2,)))
HBM = pl.BlockSpec(memory_space=pl.ANY)


def _ffn_kernel(*refs, layer, j, n_ctx, proj_layer):
    hp_ref, hs_ref, mod_ref, g_ref, wgu_hbm, wd_hbm = refs[:6]
    pos = 6
    if proj_layer is not None:
        octx_ref, oa_ref, ob_ref, wo_hbm = refs[pos:pos + 4]
        pos += 4
    op_ref, os_ref, wgu_ref, wd_ref, stage_gu, sem_gu, stage_d, sem_d, act_ref = refs[pos:pos + 9]
    i = pl.program_id(0)

    @pl.when(i == 0)
    def _load_weights():
        _load_by_cols(wgu_hbm, layer, wgu_ref, stage_gu, sem_gu)
        _load_by_rows(wd_hbm, layer, wd_ref, stage_d, sem_d)
        if proj_layer is not None:
            _load_by_rows(wo_hbm, proj_layer, refs[pos + 9], stage_d, sem_d)

    is_ctx = i < n_ctx
    h = jnp.where(is_ctx, hp_ref[...], hs_ref[...])
    if proj_layer is not None:
        heads = jnp.where(is_ctx, octx_ref[...], jnp.concatenate([oa_ref[...], ob_ref[...]], axis=1))
        h = h + _gate(mod_ref, 1) * _dot(heads, refs[pos + 9][...])
    xn = _adaln(h, g_ref[...], mod_ref, j).astype(BF16)
    for c in range(0, FFN_HIDDEN, FFN_ACT_CHUNK):
        w = min(FFN_ACT_CHUNK, FFN_HIDDEN - c)
        gate = _dot(xn, wgu_ref[:, c:c + w])
        up = _dot(xn, wgu_ref[:, FFN_HIDDEN + c:FFN_HIDDEN + c + w])
        act_ref[:, c:c + w] = ((gate * jax.nn.sigmoid(gate)) * up).astype(BF16)
    res = h + (0.5 * _gate(mod_ref, j)) * _dot(act_ref[...], wd_ref[...])

    @pl.when(is_ctx)
    def _():
        op_ref[...] = res

    @pl.when(jnp.logical_not(is_ctx))
    def _():
        os_ref[...] = res


def _ffn(hp, hs, mod, g, w_gu, w_down, layer, j, dec_seq, attn=None):
    tm = FFN_TM
    n_ctx, ctx_map, lat_map, mod_map = _two_stream_maps(tm, hp.shape[0], dec_seq)
    row_spec = lambda m, width=D_MODEL: pl.BlockSpec((tm, width), m)
    in_specs = [row_spec(ctx_map), row_spec(lat_map),
                pl.BlockSpec((1, N_MOD, D_MODEL), mod_map),
                pl.BlockSpec((1, D_MODEL), lambda i: (0, 0)),
                HBM, HBM]
    args = [hp, hs, mod, g, w_gu, w_down]
    scratch = [pltpu.VMEM((D_MODEL, 2 * FFN_HIDDEN), BF16), pltpu.VMEM((FFN_HIDDEN, D_MODEL), BF16),
               *COL_STAGE, *ROW_STAGE, pltpu.VMEM((tm, FFN_HIDDEN), BF16)]
    proj_layer = None
    if attn is not None:
        o_ctx, oa, ob, w_out, proj_layer = attn
        in_specs += [row_spec(ctx_map, o_ctx.shape[1]), row_spec(lat_map, oa.shape[1]),
                     row_spec(lat_map, ob.shape[1]), HBM]
        args += [o_ctx, oa, ob, w_out]
        scratch.append(pltpu.VMEM((o_ctx.shape[1], D_MODEL), BF16))
    return pl.pallas_call(
        functools.partial(_ffn_kernel, layer=layer, j=j, n_ctx=n_ctx, proj_layer=proj_layer),
        grid=((hp.shape[0] + hs.shape[0]) // tm,),
        in_specs=in_specs,
        out_specs=[row_spec(ctx_map), row_spec(lat_map)],
        out_shape=[jax.ShapeDtypeStruct(hp.shape, F32), jax.ShapeDtypeStruct(hs.shape, F32)],
        scratch_shapes=scratch,
        compiler_params=_params("arbitrary"),
        name=f"ffn{j}_l{layer}",
    )(*args)


def _head_mean_sq(x):
    w = x.shape[-1]
    r = lax.broadcasted_iota(jnp.int32, (MXU_COLS, MXU_COLS), 0) // HEAD_DIM
    c = lax.broadcasted_iota(jnp.int32, (MXU_COLS, MXU_COLS), 1) // HEAD_DIM
    bd = jnp.where(r == c, 1.0 / HEAD_DIM, 0.0).astype(BF16)
    sq = (x * x).astype(BF16)
    parts = []
    for s in range(0, w, MXU_COLS):
        e = min(s + MXU_COLS, w)
        parts.append(_dot(sq[:, s:e], bd[:e - s, :e - s]))
    return parts[0] if len(parts) == 1 else jnp.concatenate(parts, axis=-1)


def _low_half():
    return lax.broadcasted_iota(jnp.int32, (1, LANES), 1) < HEAD_DIM


def _head_rms(x, gain):
    return (x * lax.rsqrt(_head_mean_sq(x) + EPS)) * gain


def _rope(x, cos, sin_signed):
    lanes = cos.shape[-1]
    outs = []
    for s in range(0, x.shape[-1], lanes):
        xs = x[:, s:s + lanes]
        blk = lax.broadcasted_iota(jnp.int32, xs.shape, 1) // (HEAD_DIM // 4)
        partner = jnp.where(blk % 2 == 0,
                            pltpu.roll(xs, lanes - HEAD_DIM // 4, 1),
                            pltpu.roll(xs, HEAD_DIM // 4, 1))
        outs.append(xs * cos + partner * sin_signed)
    return outs[0] if len(outs) == 1 else jnp.concatenate(outs, axis=-1)


def _qkv_kernel(*refs, layer, rope, cache_out):
    h_ref, mod_ref, g_ref, w_hbm, gqa_ref, gka_ref, gqb_ref, gkb_ref = refs[:8]
    w_ref, stage_ref, sem_ref = refs[-3:]
    pos = 8
    if rope:
        cos_ref, sin_ref = refs[pos:pos + 2]
        pos += 2
    q_ref, kt_ref, v_ref = refs[pos:pos + 3]
    pos += 3

    @pl.when(pl.program_id(0) == 0)
    def _load_weights():
        _load_by_cols(w_hbm, layer, w_ref, stage_ref, sem_ref)

    xn = _adaln(h_ref[...], g_ref[...], mod_ref, 1).astype(BF16)
    y = _dot(xn, w_ref[...])
    o = 0
    qa = _head_rms(y[:, o:o + NA_W], gqa_ref[...]); o += NA_W
    ka = _head_rms(y[:, o:o + NA_W], gka_ref[...]); o += NA_W
    va = y[:, o:o + NA_W]; o += NA_W
    qb = _head_rms(y[:, o:o + GQ_W], gqb_ref[...]); o += GQ_W
    kb = _head_rms(y[:, o:o + GKV_W], gkb_ref[...]); o += GKV_W
    vb = y[:, o:o + GKV_W]
    if rope:
        cos, sin = cos_ref[...], sin_ref[...]
        qb = _rope(qb, cos, sin)
        kb = _rope(kb, cos, sin)
    q_ref[:, :NA_W] = (qa * Q_SCALE).astype(BF16)
    q_ref[:, NA_W:] = (qb * Q_SCALE).astype(BF16)
    low = _low_half()
    vb_swapped = pltpu.roll(vb, HEAD_DIM, 1)
    v_ref[:, :NA_W] = va.astype(BF16)
    v_ref[:, NA_W:NA_W + LANES] = jnp.where(low, vb, vb_swapped).astype(BF16)
    v_ref[:, NA_W + LANES:] = jnp.where(low, vb_swapped, vb).astype(BF16)
    n_seq, _, s = kt_ref.shape
    for i in range(n_seq):
        rows = slice(i * s, (i + 1) * s)
        ka_t = ka[rows].T
        kb_t = kb[rows].T
        k0, k1 = kb_t[:HEAD_DIM], kb_t[HEAD_DIM:]
        kt_ref[i, :NA_W, :] = ka_t.astype(BF16)
        kt_ref[i, NA_W:, :] = jnp.concatenate([k0, k0, k1, k1], axis=0).astype(BF16)
        if cache_out:
            kat_ref, vat_ref, kbt_ref, vbt_ref = refs[pos:pos + 4]
            kat_ref[i] = ka_t
            vat_ref[i] = va[rows].T
            kbt_ref[i] = kb_t
            vbt_ref[i] = vb[rows].T


def _qkv(h, mod, g, w_in, layer, gains, cond_div, seq, rope_tabs=None, cache_out=False):
    t = h.shape[0]
    tm = PROJ_TM
    rope = rope_tabs is not None
    assert not (rope and cache_out)
    row = lambda i: (i, 0)
    const = lambda i: (0, 0)
    in_specs = [
        pl.BlockSpec((tm, D_MODEL), row),
        pl.BlockSpec((1, N_MOD, D_MODEL), lambda i: (i // cond_div, 0, 0)),
        pl.BlockSpec((1, D_MODEL), const),
        HBM,
        pl.BlockSpec((1, NA_W), const),
        pl.BlockSpec((1, NA_W), const),
        pl.BlockSpec((1, GQ_W), const),
        pl.BlockSpec((1, GKV_W), const),
    ]
    args = [h, mod, g, w_in, *gains]
    if rope:
        seq_tiles = rope_tabs[0].shape[0] // tm
        in_specs += [pl.BlockSpec((tm, 2 * HEAD_DIM), lambda i: (i % seq_tiles, 0))] * 2
        args += list(rope_tabs)
    if seq >= tm:
        per_seq = seq // tm
        t_block = lambda rows: pl.BlockSpec((1, rows, tm), lambda i: (i // per_seq, 0, i % per_seq))
    else:
        t_block = lambda rows: pl.BlockSpec((tm // seq, rows, seq), lambda i: (i, 0, 0))
    t_shape = lambda rows, dtype: jax.ShapeDtypeStruct((t // seq, rows, seq), dtype)
    out_specs = [pl.BlockSpec((tm, NA_W + GQ_W), row), t_block(KT_ROWS), pl.BlockSpec((tm, KT_ROWS), row)]
    out_shape = [jax.ShapeDtypeStruct((t, NA_W + GQ_W), BF16), t_shape(KT_ROWS, BF16),
                 jax.ShapeDtypeStruct((t, KT_ROWS), BF16)]
    if cache_out:
        out_specs += [t_block(NA_W), t_block(NA_W), t_block(GKV_W), t_block(GKV_W)]
        out_shape += [t_shape(NA_W, F32), t_shape(NA_W, F32), t_shape(GKV_W, F32), t_shape(GKV_W, F32)]
    return pl.pallas_call(
        functools.partial(_qkv_kernel, layer=layer, rope=rope, cache_out=cache_out),
        grid=(t // tm,),
        in_specs=in_specs,
        out_specs=out_specs,
        out_shape=out_shape,
        scratch_shapes=[pltpu.VMEM((D_MODEL, QKV_WIDTH), BF16), *COL_STAGE],
        compiler_params=_params("arbitrary"),
        name="qkv_latent" if rope else "qkv_context",
    )(*args)


def _split_pair(q2):
    low = _low_half()
    zero = jnp.zeros((), q2.dtype)
    return jnp.where(low, q2, zero), jnp.where(low, zero, q2)


def _with_ones(v2):
    return jnp.concatenate([v2, jnp.ones(v2.shape, v2.dtype)], axis=1)


def _join(blocks, axis):
    return blocks[0] if len(blocks) == 1 else jnp.concatenate(blocks, axis=axis)


def _attend(problems):
    scores = [_dot(qm, _join(kts, 1)) for qm, kts, _, _ in problems]
    probs = []
    for s, (_, kts, biases, _) in zip(scores, problems):
        parts, start = [], 0
        for kt, b in zip(kts, biases):
            part = s[:, start:start + kt.shape[1]]
            parts.append(part if b is None else part + b)
            start += kt.shape[1]
        m = functools.reduce(jnp.maximum, [jnp.max(part, axis=-1, keepdims=True) for part in parts])
        probs.append(_join([jnp.exp(part - m).astype(BF16) for part in parts], 1))
    return [_dot(p, _join(vexts, 0)) for p, (_, _, _, vexts) in zip(probs, problems)]


def _merge_pair(oe_even, oe_odd):
    even = oe_even[:, :LANES] * (1.0 / oe_even[:, LANES:])
    odd = oe_odd[:, :LANES] * (1.0 / oe_odd[:, LANES:])
    return jnp.where(_low_half(), even, odd)


def _gqa_groups(q_ref, q_off, o_ref, o_off, keys_values):
    tq = q_ref.shape[0]
    slabs = lambda g, off: [slice(off + (2 * g + j) * LANES, off + (2 * g + j + 1) * LANES) for j in range(2)]
    problems = []
    for g in range(GQA_KV_HEADS):
        qm = jnp.concatenate([part for sl in slabs(g, q_off) for part in _split_pair(q_ref[:, sl])], axis=0)
        problems.append((qm, *keys_values(slice(g * LANES, (g + 1) * LANES))))
    for g, oe in enumerate(_attend(problems)):
        for j, sl in enumerate(slabs(g, o_off)):
            o_ref[:, sl] = _merge_pair(oe[2 * j * tq:(2 * j + 1) * tq],
                                       oe[(2 * j + 1) * tq:(2 * j + 2) * tq]).astype(o_ref.dtype)


def _ctx_attn_kernel(q_ref, kt_ref, v_ref, o_ref):
    for p in range(NA_HEADS // 2):
        sl = slice(p * LANES, (p + 1) * LANES)
        vext = _with_ones(v_ref[:, sl])
        oe = [_attend([(qm, [kt_ref[sl, :]], [None], [vext])])[0] for qm in _split_pair(q_ref[:, sl])]
        o_ref[:, sl] = _merge_pair(*oe).astype(o_ref.dtype)
    _gqa_groups(q_ref, NA_W, o_ref, NA_W,
                lambda kv: ([kt_ref[NA_W + kv.start:NA_W + kv.stop, :]], [None],
                            [_with_ones(v_ref[:, NA_W + kv.start:NA_W + kv.stop])]))


def _ctx_attention(q, kt, v, seq):
    t = q.shape[0]
    row = lambda b: (b, 0)
    return pl.pallas_call(
        _ctx_attn_kernel,
        grid=(t // seq,),
        in_specs=[pl.BlockSpec((seq, NA_W + GQ_W), row),
                  pl.BlockSpec((None, KT_ROWS, seq), lambda b: (b, 0, 0)),
                  pl.BlockSpec((seq, KT_ROWS), row)],
        out_specs=pl.BlockSpec((seq, NA_W + GQ_W), row),
        out_shape=jax.ShapeDtypeStruct((t, NA_W + GQ_W), BF16),
        compiler_params=_params("arbitrary"),
        name="ctx_attention",
    )(q, kt, v)


def _na_kernel(q_ref, kt0_ref, kt1_ref, kt2_ref, v0_ref, v1_ref, v2_ref, ckt_ref, cv_ref, bias_ref, o_ref):
    for p in range(NA_HEADS // 2):
        sl = slice(p * LANES, (p + 1) * LANES)
        kt_loc = jnp.concatenate([kt0_ref[sl, :], kt1_ref[sl, :], kt2_ref[sl, :]], axis=1)
        v_loc = _with_ones(jnp.concatenate([v0_ref[:, sl], v1_ref[:, sl], v2_ref[:, sl]], axis=0))
        v_ctx = _with_ones(cv_ref[:, sl])
        oe = _attend([(qm, [kt_loc, ckt_ref[sl, :]], [bias_ref[2 * p + half], None], [v_loc, v_ctx])
                      for half, qm in enumerate(_split_pair(q_ref[:, sl]))])
        o_ref[:, sl] = _merge_pair(*oe).astype(o_ref.dtype)


def _na_bias(rpb, rows):
    qc = np.arange(GRID_W)
    kc = np.arange(GRID_W)
    ws = np.clip(qc - NA_WIN_C // 2, 0, GRID_W - NA_WIN_C)
    col_ok = (kc[None, :] >= ws[:, None]) & (kc[None, :] < ws[:, None] + NA_WIN_C)
    dc = np.clip(kc[None, :] - qc[:, None] + NA_WIN_C - 1, 0, 2 * NA_WIN_C - 2)
    pick = ((dc[:, :, None] == np.arange(2 * NA_WIN_C - 1)) & col_ok[:, :, None]).astype(np.float32)
    tiles = jnp.einsum("hrd,qkd->hrqk", rpb.astype(F32), pick, precision=lax.Precision.HIGHEST)
    tiles = jnp.where(col_ok, tiles, NEG_INF)
    tiles = jnp.concatenate([tiles, tiles], axis=-1)
    n_off = 2 * NA_WIN_R - 1
    return pl.pallas_call(
        functools.partial(_na_bias_kernel, rows=rows),
        grid=(NA_HEADS,),
        in_specs=[pl.BlockSpec((None, n_off, GRID_W, LANES), lambda h: (h, 0, 0, 0))],
        out_specs=pl.BlockSpec((3, None, NA_BAND_Q, NA_BAND_K), lambda h: (0, h, 0, 0)),
        out_shape=jax.ShapeDtypeStruct((3, NA_HEADS, NA_BAND_Q, NA_BAND_K), F32),
        compiler_params=_params("arbitrary"),
        name="na_bias",
    )(tiles)


def _na_bias_kernel(tiles_ref, o_ref, *, rows):
    low = _low_half()
    masked = jnp.full((GRID_W, LANES), NEG_INF, F32)
    nb = rows // NA_BAND_ROWS
    for t, band in enumerate((0, 1, nb - 1)):
        r0 = band * NA_BAND_ROWS
        k0 = int(np.clip(r0 - NA_WIN_R // 2, 0, rows - NA_BAND_KEY_ROWS))
        for ri in range(NA_BAND_ROWS):
            r = r0 + ri
            start = int(np.clip(r - NA_WIN_R // 2, 0, rows - NA_WIN_R))
            tile = lambda kr: tiles_ref[kr - r + NA_WIN_R - 1] if start <= kr < start + NA_WIN_R else masked
            for m in range(NA_BAND_KEY_ROWS // 2):
                kr = k0 + 2 * m
                o_ref[t, ri * GRID_W:(ri + 1) * GRID_W, m * LANES:(m + 1) * LANES] = jnp.where(
                    low, tile(kr), tile(kr + 1))


def _na_attention(q, kt, v, ctx_kt, ctx_v, bias, batch, seq):
    rows = seq // GRID_W
    nb = rows // NA_BAND_ROWS
    past = ctx_v.shape[1]
    assert (rows - NA_BAND_KEY_ROWS) % NA_BAND_ROWS == 0
    n_chunks = NA_BAND_KEY_ROWS // NA_BAND_ROWS
    band_type = lambda j: jnp.where(j == 0, 0, jnp.where(j == nb - 1, 2, 1))
    chunk0 = lambda j: jnp.clip(j - NA_WIN_R // 2 // NA_BAND_ROWS, 0, nb - n_chunks)
    kt_spec = lambda c: pl.BlockSpec((None, NA_W, NA_BAND_Q), lambda b, j: (b, 0, chunk0(j) + c))
    v_spec = lambda c: pl.BlockSpec((NA_BAND_Q, NA_W), lambda b, j: (b * nb + chunk0(j) + c, 0))
    return pl.pallas_call(
        _na_kernel,
        grid=(batch, nb),
        in_specs=[pl.BlockSpec((NA_BAND_Q, NA_W), lambda b, j: (b * nb + j, 0))]
                 + [kt_spec(c) for c in range(n_chunks)] + [v_spec(c) for c in range(n_chunks)]
                 + [pl.BlockSpec((None, NA_W, past), lambda b, j: (b, 0, 0)),
                    pl.BlockSpec((None, past, NA_W), lambda b, j: (b, 0, 0)),
                    pl.BlockSpec((None, NA_HEADS, NA_BAND_Q, NA_BAND_K), lambda b, j: (band_type(j), 0, 0, 0))],
        out_specs=pl.BlockSpec((NA_BAND_Q, NA_W), lambda b, j: (b * nb + j, 0)),
        out_shape=jax.ShapeDtypeStruct((batch * seq, NA_W), BF16),
        compiler_params=_params("arbitrary", "arbitrary"),
        name="na_attention",
    )(q, *([kt] * n_chunks), *([v] * n_chunks), ctx_kt, ctx_v, bias)


def _gqa_kernel(q_ref, kt_ref, v_ref, ckt_ref, cv_ref, o_ref):
    _gqa_groups(q_ref, 0, o_ref, 0,
                lambda kv: ([kt_ref[kv, :], ckt_ref[kv, :]], [None, None],
                            [_with_ones(v_ref[:, kv]), _with_ones(cv_ref[:, kv])]))


def _gqa_attention(q, kt, v, ctx_kt, ctx_v, batch, seq):
    tq = 128
    nt = seq // tq
    past = ctx_v.shape[1]
    dup = 2 * GKV_W
    return pl.pallas_call(
        _gqa_kernel,
        grid=(batch, nt),
        in_specs=[
            pl.BlockSpec((tq, GQ_W), lambda b, i: (b * nt + i, 1)),
            pl.BlockSpec((None, dup, seq), lambda b, i: (b, NA_W // dup, 0)),
            pl.BlockSpec((seq, dup), lambda b, i: (b, NA_W // dup)),
            pl.BlockSpec((None, dup, past), lambda b, i: (b, 0, 0)),
            pl.BlockSpec((None, past, dup), lambda b, i: (b, 0, 0)),
        ],
        out_specs=pl.BlockSpec((tq, GQ_W), lambda b, i: (b * nt + i, 0)),
        out_shape=jax.ShapeDtypeStruct((batch * seq, GQ_W), BF16),
        compiler_params=_params("arbitrary", "arbitrary"),
        name="gqa_attention",
    )(q, kt, v, ctx_kt, ctx_v)


def _sgu_kernel(hp_ref, hs_ref, mod_ref, g_ref, win_hbm, vg_ref, ws_ref, bs_ref, wout_hbm, op_ref, os_ref,
                win_ref, wout_ref, stage_in, sem_in, stage_out, sem_out, gated_ref, y_ref,
                *, layer, n_ctx):
    i = pl.program_id(0)

    @pl.when(i == 0)
    def _load_weights():
        _load_by_cols(win_hbm, layer, win_ref, stage_in, sem_in)
        _load_by_rows(wout_hbm, layer, wout_ref, stage_out, sem_out)

    tm = hp_ref.shape[0]
    is_ctx = i < n_ctx
    h = jnp.where(is_ctx, hp_ref[...], hs_ref[...])
    xn = _adaln(h, g_ref[...], mod_ref, 1).astype(BF16)

    n_chunks = y_ref.shape[0]
    ssq = jnp.zeros((tm, 1), F32)
    for c in range(n_chunks):
        y = _dot(xn, win_ref[c])
        y = 0.5 * y * (1.0 + lax.erf(y * math.sqrt(0.5)))
        y_ref[c] = y
        if c >= n_chunks // 2:
            ssq = ssq + jnp.sum(y * y, axis=-1, keepdims=True)
    v_scale = lax.rsqrt(ssq * (1.0 / SGU_WIDTH) + EPS)

    per_chunk = SGU_ACT_CHUNK // SGU_GW
    for c in range(tm // SGU_CHUNK):
        rs = slice(c * SGU_CHUNK, (c + 1) * SGU_CHUNK)
        for g in range(SGU_GROUPS):
            cs = slice(g * SGU_GW, (g + 1) * SGU_GW)
            in_chunk = slice((g % per_chunk) * SGU_GW, (g % per_chunk + 1) * SGU_GW)
            u = y_ref[g // per_chunk, rs, in_chunk]
            v = y_ref[n_chunks // 2 + g // per_chunk, rs, in_chunk]
            vn = ((v * v_scale[rs]) * vg_ref[:, cs]).astype(BF16)
            sv = _dot(ws_ref[g].astype(BF16), vn) + bs_ref[:, g:g + 1]
            gated_ref[rs, cs] = (u * sv).astype(BF16)
    res = h + _gate(mod_ref, 1) * _dot(gated_ref[...], wout_ref[...])

    @pl.when(is_ctx)
    def _():
        op_ref[...] = res

    @pl.when(jnp.logical_not(is_ctx))
    def _():
        os_ref[...] = res


def _sgu(hp, hs, mod, g, w_in, v_g, w_s, b_s_t, w_out, layer, dec_seq):
    tm = SGU_TM
    n_act = 2 * SGU_WIDTH // SGU_ACT_CHUNK
    n_ctx, ctx_map, lat_map, mod_map = _two_stream_maps(tm, hp.shape[0], dec_seq)
    row_spec = lambda m: pl.BlockSpec((tm, D_MODEL), m)
    const = lambda i: (0, 0)
    return pl.pallas_call(
        functools.partial(_sgu_kernel, layer=layer, n_ctx=n_ctx),
        grid=((hp.shape[0] + hs.shape[0]) // tm,),
        in_specs=[
            row_spec(ctx_map), row_spec(lat_map),
            pl.BlockSpec((1, N_MOD, D_MODEL), mod_map),
            pl.BlockSpec((1, D_MODEL), const),
            HBM,
            pl.BlockSpec((1, SGU_WIDTH), const),
            pl.BlockSpec((None, SGU_GROUPS, SGU_CHUNK, SGU_CHUNK), lambda i: (layer, 0, 0, 0)),
            pl.BlockSpec((SGU_CHUNK, SGU_GROUPS), const),
            HBM,
        ],
        out_specs=[row_spec(ctx_map), row_spec(lat_map)],
        out_shape=[jax.ShapeDtypeStruct(hp.shape, F32), jax.ShapeDtypeStruct(hs.shape, F32)],
        scratch_shapes=[
            pltpu.VMEM((n_act, D_MODEL, SGU_ACT_CHUNK), BF16),
            pltpu.VMEM((SGU_WIDTH, D_MODEL), BF16),
            *COL_STAGE, *ROW_STAGE,
            pltpu.VMEM((tm, SGU_WIDTH), BF16),
            pltpu.VMEM((n_act, tm, SGU_ACT_CHUNK), F32),
        ],
        compiler_params=_params("arbitrary"),
        name="sgu",
    )(hp, hs, mod, g, w_in, v_g, w_s, b_s_t, w_out)


def _rope_tables(seq):
    half = HEAD_DIM // 2
    t = jnp.arange(seq)
    freqs = ROPE_BASE ** (-jnp.arange(0, half, 2, dtype=F32) / half)
    def tab(pos):
        ang = pos.astype(F32)[:, None] * freqs[None, :]
        cos, sin = jnp.cos(ang), jnp.sin(ang)
        return jnp.concatenate([cos, cos], -1), jnp.concatenate([-sin, sin], -1)
    cr, sr = tab(t // GRID_W)
    cc, sc = tab(t % GRID_W)
    cos = jnp.concatenate([cr, cc], -1)
    sin = jnp.concatenate([sr, sc], -1)
    return jnp.tile(cos, (1, 2)), jnp.tile(sin, (1, 2))


def kernel(x_prompt, x_sample, cache_na_k, cache_na_v, cache_gqa_k, cache_gqa_v, c, c_ctx, norm_g, mod_w, mod_b, ffn1_w_gu, ffn1_w_down, ffn2_w_gu, ffn2_w_down, attn_w_in, attn_w_out, na_q_g, na_k_g, na_rpb, gqa_q_g, gqa_k_g, sgu_w_in, sgu_v_g, sgu_w_s, sgu_b_s, sgu_w_out):
    batch, seq, _ = x_prompt.shape
    dec_batch, dec_seq, _ = x_sample.shape
    past = cache_na_k.shape[2]
    hp = x_prompt.reshape(batch * seq, D_MODEL)
    hs = x_sample.reshape(dec_batch * dec_seq, D_MODEL)

    cond8 = jnp.zeros((8, D_MODEL), F32).at[0].set(c_ctx).at[1:1 + dec_batch].set(c)
    mod_all = _modulation(cond8, mod_w, mod_b).reshape(DEPTH, 8, N_MOD, D_MODEL)

    new_cache = None
    for layer in range(DEPTH):
        mod_l = mod_all[layer]
        mods = (mod_l[0:1], mod_l[1:1 + dec_batch])
        g0, g1, g2 = (norm_g[layer, s][None, :] for s in range(3))

        mod3 = mod_l[0:1 + dec_batch]
        hp, hs = _ffn(hp, hs, mod3, g0, ffn1_w_gu, ffn1_w_down, layer, 0, dec_seq)

        if layer % 2 == 0:
            e = layer // 2
            gains = (jnp.tile(na_q_g[e], NA_HEADS)[None], jnp.tile(na_k_g[e], NA_HEADS)[None],
                     jnp.tile(gqa_q_g[e], GQA_HEADS)[None], jnp.tile(gqa_k_g[e], GQA_KV_HEADS)[None])
            q, kt, v, *new_cache = _qkv(hp, mods[0], g1, attn_w_in, e, gains, hp.shape[0] // PROJ_TM, seq,
                                        cache_out=True)
            o_ctx = _ctx_attention(q, kt, v, seq)
            q, kt, v = _qkv(hs, mods[1], g1, attn_w_in, e, gains, dec_seq // PROJ_TM, dec_seq,
                            rope_tabs=_rope_tables(dec_seq))
            to_kt = lambda ck, rep: jnp.repeat(jnp.transpose(ck[:, e], (0, 2, 3, 1)), rep, axis=1).reshape(
                dec_batch, -1, past).astype(BF16)
            to_v = lambda cv, rep: jnp.repeat(cv[:, e], rep, axis=2).reshape(dec_batch, past, -1).astype(BF16)
            oa = _na_attention(q, kt, v, to_kt(cache_na_k, 1), to_v(cache_na_v, 1),
                               _na_bias(na_rpb[e], dec_seq // GRID_W), dec_batch, dec_seq)
            ob = _gqa_attention(q, kt, v, to_kt(cache_gqa_k, 2), to_v(cache_gqa_v, 2), dec_batch, dec_seq)
            attn = (o_ctx, oa, ob, attn_w_out, e)
        else:
            attn = None
            o = layer // 2
            hp, hs = _sgu(hp, hs, mod3, g1, sgu_w_in, sgu_v_g[o][None], sgu_w_s, sgu_b_s[o].T, sgu_w_out,
                          o, dec_seq)

        hp, hs = _ffn(hp, hs, mod3, g2, ffn2_w_gu, ffn2_w_down, layer, 2, dec_seq, attn=attn)

    n_attn = (DEPTH + 1) // 2
    assert n_attn == 1
    from_t = lambda x: jnp.transpose(x.reshape(batch, n_attn, -1, HEAD_DIM, seq), (0, 1, 4, 2, 3))
    return (hp.reshape(batch, seq, D_MODEL), hs.reshape(dec_batch, dec_seq, D_MODEL),
            *(from_t(x) for x in new_cache))
```

```python
import functools
import math

import jax
import jax.numpy as jnp
import numpy as np
from jax import lax
from jax.experimental import pallas as pl
from jax.experimental.pallas import tpu as pltpu

D_MODEL = 1024
DEPTH = 2
GRID_W = 64
HEAD_DIM = 64
NA_HEADS = 8
NA_WIN_R = 8
NA_WIN_C = 16
GQA_HEADS = 8
GQA_KV_HEADS = 2
GQA_GROUP = GQA_HEADS // GQA_KV_HEADS
NA_W = NA_HEADS * HEAD_DIM
GQ_W = GQA_HEADS * HEAD_DIM
GKV_W = GQA_KV_HEADS * HEAD_DIM
QKV_WIDTH = 3 * NA_W + GQ_W + 2 * GKV_W
ROPE_BASE = 10000.0
SGU_CHUNK = 128
SGU_GROUPS = 8
SGU_WIDTH = 2 * D_MODEL
SGU_GW = SGU_WIDTH // SGU_GROUPS
FFN_HIDDEN = 2816
N_MOD = 9
EPS = 1e-6
NEG_INF = -1e30
Q_SCALE = HEAD_DIM ** -0.5

NA_BAND_ROWS = 4
NA_BAND_KEY_ROWS = 12
NA_BAND_Q = NA_BAND_ROWS * GRID_W
NA_BAND_K = NA_BAND_KEY_ROWS * GRID_W

FFN_TM = 512
FFN_ACT_CHUNK = 512
SGU_ACT_CHUNK = 512
SGU_TM = 512
PROJ_TM = 512
W_STAGE = 256

LANES = 128
KT_ROWS = NA_W + 2 * GKV_W
MXU_COLS = 256
VMEM_LIMIT = 56 * 1024 * 1024

BF16 = jnp.bfloat16
F32 = jnp.float32


def _dot(a, b):
    return jnp.dot(a, b, preferred_element_type=F32)


def _params(*sem):
    return pltpu.CompilerParams(dimension_semantics=sem, vmem_limit_bytes=VMEM_LIMIT)


def _adaln(x, g, mod_ref, j):
    shift = mod_ref[0, 3 * j:3 * j + 1, :]
    scale = mod_ref[0, 3 * j + 1:3 * j + 2, :]
    y = x * lax.rsqrt(jnp.mean(x * x, axis=-1, keepdims=True) + EPS)
    return (y * g) * (1.0 + scale) + shift


def _gate(mod_ref, j):
    return mod_ref[0, 3 * j + 2:3 * j + 3, :]


def _mod_kernel(c_ref, w_ref, b_ref, o_ref):
    c = c_ref[...]
    a = (c * jax.nn.sigmoid(c)).astype(BF16)
    o_ref[...] = _dot(a, w_ref[...].astype(BF16)) + b_ref[...]


def _modulation(cond8, mod_w, mod_b):
    tn = 1024
    n = N_MOD * D_MODEL
    return pl.pallas_call(
        _mod_kernel,
        grid=(DEPTH, n // tn),
        in_specs=[
            pl.BlockSpec((8, D_MODEL), lambda l, k: (0, 0)),
            pl.BlockSpec((None, D_MODEL, tn), lambda l, k: (l, 0, k)),
            pl.BlockSpec((None, 1, tn), lambda l, k: (l, 0, k)),
        ],
        out_specs=pl.BlockSpec((None, 8, tn), lambda l, k: (l, 0, k)),
        out_shape=jax.ShapeDtypeStruct((DEPTH, 8, n), F32),
        compiler_params=_params("arbitrary", "arbitrary"),
        name="modulation",
    )(cond8, mod_w, mod_b.reshape(DEPTH, 1, n))


def _two_stream_maps(tm, n_ctx_rows, dec_seq):
    n_ctx = n_ctx_rows // tm
    per_seq = dec_seq // tm
    ctx_map = lambda i: (jnp.minimum(i, n_ctx - 1), 0)
    lat_map = lambda i: (jnp.maximum(i - n_ctx, 0), 0)
    mod_map = lambda i: (jnp.where(i < n_ctx, 0, 1 + (i - n_ctx) // per_seq), 0, 0)
    return n_ctx, ctx_map, lat_map, mod_map


class _WeightStream:
    def __init__(self, pairs, stage_ref, sem_ref):
        self.pairs, self.stage, self.sem, self.done = pairs, stage_ref, sem_ref, 0
        for k in range(min(2, len(pairs))):
            self._copy(k).start()

    def _copy(self, k):
        return pltpu.make_async_copy(self.pairs[k][0], self.stage.at[k % 2], self.sem.at[k % 2])

    def take(self, n=None):
        n = len(self.pairs) - self.done if n is None else min(n, len(self.pairs) - self.done)
        for _ in range(n):
            k = self.done
            self._copy(k).wait()
            dst, idx = self.pairs[k][1]
            dst[idx] = self.stage[k % 2].astype(BF16)
            if k + 2 < len(self.pairs):
                self._copy(k + 2).start()
            self.done += 1


def _col_pairs(w_hbm, layer, w_ref, cols=None):
    if len(w_ref.shape) == 2:
        dst = lambda c: (slice(None), slice(c, c + W_STAGE))
        n = w_ref.shape[1]
    else:
        width = w_ref.shape[2]
        dst = lambda c: (c // width, slice(None), slice(c % width, c % width + W_STAGE))
        n = w_ref.shape[0] * width
    cols = range(0, n, W_STAGE) if cols is None else cols
    return [(w_hbm.at[layer, :, pl.ds(c, W_STAGE)], (w_ref, dst(c))) for c in cols]


def _row_pairs(w_hbm, layer, w_ref):
    return [(w_hbm.at[layer, pl.ds(r, W_STAGE), :], (w_ref, (slice(r, r + W_STAGE), slice(None))))
            for r in range(0, w_ref.shape[0], W_STAGE)]


def _load_by_cols(w_hbm, layer, w_ref, stage_ref, sem_ref):
    _WeightStream(_col_pairs(w_hbm, layer, w_ref), stage_ref, sem_ref).take()


def _load_by_rows(w_hbm, layer, w_ref, stage_ref, sem_ref):
    _WeightStream(_row_pairs(w_hbm, layer, w_ref), stage_ref, sem_ref).take()


COL_STAGE = (pltpu.VMEM((2, D_MODEL, W_STAGE), F32), pltpu.SemaphoreType.DMA((2,)))
ROW_STAGE = (pltpu.VMEM((2, W_STAGE, D_MODEL), F32), pltpu.SemaphoreType.DMA((2,)))
HBM = pl.BlockSpec(memory_space=pl.ANY)


def _ffn_kernel(*refs, layer, j, n_ctx, proj_layer):
    hp_ref, hs_ref, mod_ref, g_ref, wgu_hbm, wd_hbm = refs[:6]
    pos = 6
    if proj_layer is not None:
        octx_ref, oa_ref, ob_ref, wo_hbm = refs[pos:pos + 4]
        pos += 4
    op_ref, os_ref, wgu_ref, wd_ref, stage_gu, sem_gu, stage_d, sem_d, act_ref = refs[pos:pos + 9]
    wo_ref = refs[pos + 9] if proj_layer is not None else None
    i = pl.program_id(0)
    is_ctx = i < n_ctx
    chunks = [(c, min(FFN_ACT_CHUNK, FFN_HIDDEN - c)) for c in range(0, FFN_HIDDEN, FFN_ACT_CHUNK)]

    def tile(first):
        if first:
            need = [col for c, w in chunks for base in (c, FFN_HIDDEN + c) for col in range(base, base + w, W_STAGE)]
            cols = _WeightStream(_col_pairs(wgu_hbm, layer, wgu_ref, need), stage_gu, sem_gu)
            proj = _row_pairs(wo_hbm, proj_layer, wo_ref) if proj_layer is not None else []
            rows = _WeightStream(proj + _row_pairs(wd_hbm, layer, wd_ref), stage_d, sem_d)
            rows.take(len(proj))
        h = hp_ref[...] if first else jnp.where(is_ctx, hp_ref[...], hs_ref[...])
        if proj_layer is not None:
            latent_heads = jnp.concatenate([oa_ref[...], ob_ref[...]], axis=1)
            heads = octx_ref[...] if first else jnp.where(is_ctx, octx_ref[...], latent_heads)
            h = h + _gate(mod_ref, 1) * _dot(heads, wo_ref[...])
        xn = _adaln(h, g_ref[...], mod_ref, j).astype(BF16)
        for c, w in chunks:
            if first:
                cols.take(2 * w // W_STAGE)
            gate = _dot(xn, wgu_ref[:, c:c + w])
            up = _dot(xn, wgu_ref[:, FFN_HIDDEN + c:FFN_HIDDEN + c + w])
            act_ref[:, c:c + w] = ((gate * jax.nn.sigmoid(gate)) * up).astype(BF16)
            if first:
                rows.take(2)
        if first:
            rows.take()
        res = h + (0.5 * _gate(mod_ref, j)) * _dot(act_ref[...], wd_ref[...])
        if first:
            op_ref[...] = res
        else:
            @pl.when(is_ctx)
            def _():
                op_ref[...] = res

            @pl.when(jnp.logical_not(is_ctx))
            def _():
                os_ref[...] = res

    pl.when(i == 0)(lambda: tile(True))
    pl.when(i > 0)(lambda: tile(False))


def _ffn(hp, hs, mod, g, w_gu, w_down, layer, j, dec_seq, attn=None):
    tm = FFN_TM
    n_ctx, ctx_map, lat_map, mod_map = _two_stream_maps(tm, hp.shape[0], dec_seq)
    row_spec = lambda m, width=D_MODEL: pl.BlockSpec((tm, width), m)
    in_specs = [row_spec(ctx_map), row_spec(lat_map),
                pl.BlockSpec((1, N_MOD, D_MODEL), mod_map),
                pl.BlockSpec((1, D_MODEL), lambda i: (0, 0)),
                HBM, HBM]
    args = [hp, hs, mod, g, w_gu, w_down]
    scratch = [pltpu.VMEM((D_MODEL, 2 * FFN_HIDDEN), BF16), pltpu.VMEM((FFN_HIDDEN, D_MODEL), BF16),
               *COL_STAGE, *ROW_STAGE, pltpu.VMEM((tm, FFN_HIDDEN), BF16)]
    proj_layer = None
    if attn is not None:
        o_ctx, oa, ob, w_out, proj_layer = attn
        in_specs += [row_spec(ctx_map, o_ctx.shape[1]), row_spec(lat_map, oa.shape[1]),
                     row_spec(lat_map, ob.shape[1]), HBM]
        args += [o_ctx, oa, ob, w_out]
        scratch.append(pltpu.VMEM((o_ctx.shape[1], D_MODEL), BF16))
    return pl.pallas_call(
        functools.partial(_ffn_kernel, layer=layer, j=j, n_ctx=n_ctx, proj_layer=proj_layer),
        grid=((hp.shape[0] + hs.shape[0]) // tm,),
        in_specs=in_specs,
        out_specs=[row_spec(ctx_map), row_spec(lat_map)],
        out_shape=[jax.ShapeDtypeStruct(hp.shape, F32), jax.ShapeDtypeStruct(hs.shape, F32)],
        scratch_shapes=scratch,
        compiler_params=_params("arbitrary"),
        name=f"ffn{j}_l{layer}",
    )(*args)


def _head_mean_sq(x):
    w = x.shape[-1]
    r = lax.broadcasted_iota(jnp.int32, (MXU_COLS, MXU_COLS), 0) // HEAD_DIM
    c = lax.broadcasted_iota(jnp.int32, (MXU_COLS, MXU_COLS), 1) // HEAD_DIM
    bd = jnp.where(r == c, 1.0 / HEAD_DIM, 0.0).astype(BF16)
    sq = (x * x).astype(BF16)
    parts = []
    for s in range(0, w, MXU_COLS):
        e = min(s + MXU_COLS, w)
        parts.append(_dot(sq[:, s:e], bd[:e - s, :e - s]))
    return parts[0] if len(parts) == 1 else jnp.concatenate(parts, axis=-1)


def _low_half():
    return lax.broadcasted_iota(jnp.int32, (1, LANES), 1) < HEAD_DIM


def _head_rms(x, gain):
    return (x * lax.rsqrt(_head_mean_sq(x) + EPS)) * gain


def _rope(x, cos, sin_signed):
    lanes = cos.shape[-1]
    outs = []
    for s in range(0, x.shape[-1], lanes):
        xs = x[:, s:s + lanes]
        blk = lax.broadcasted_iota(jnp.int32, xs.shape, 1) // (HEAD_DIM // 4)
        partner = jnp.where(blk % 2 == 0,
                            pltpu.roll(xs, lanes - HEAD_DIM // 4, 1),
                            pltpu.roll(xs, HEAD_DIM // 4, 1))
        outs.append(xs * cos + partner * sin_signed)
    return outs[0] if len(outs) == 1 else jnp.concatenate(outs, axis=-1)


def _qkv_kernel(*refs, layer, rope, cache_out):
    h_ref, mod_ref, g_ref, w_hbm, gqa_ref, gka_ref, gqb_ref, gkb_ref = refs[:8]
    w_ref, stage_ref, sem_ref = refs[-3:]
    pos = 8
    if rope:
        cos_ref, sin_ref = refs[pos:pos + 2]
        pos += 2
    q_ref, kt_ref, v_ref = refs[pos:pos + 3]
    pos += 3

    @pl.when(pl.program_id(0) == 0)
    def _load_weights():
        _load_by_cols(w_hbm, layer, w_ref, stage_ref, sem_ref)

    xn = _adaln(h_ref[...], g_ref[...], mod_ref, 1).astype(BF16)
    y = _dot(xn, w_ref[...])
    o = 0
    qa = _head_rms(y[:, o:o + NA_W], gqa_ref[...]); o += NA_W
    ka = _head_rms(y[:, o:o + NA_W], gka_ref[...]); o += NA_W
    va = y[:, o:o + NA_W]; o += NA_W
    qb = _head_rms(y[:, o:o + GQ_W], gqb_ref[...]); o += GQ_W
    kb = _head_rms(y[:, o:o + GKV_W], gkb_ref[...]); o += GKV_W
    vb = y[:, o:o + GKV_W]
    if rope:
        cos, sin = cos_ref[...], sin_ref[...]
        qb = _rope(qb, cos, sin)
        kb = _rope(kb, cos, sin)
    q_ref[:, :NA_W] = (qa * Q_SCALE).astype(BF16)
    q_ref[:, NA_W:] = (qb * Q_SCALE).astype(BF16)
    low = _low_half()
    vb_swapped = pltpu.roll(vb, HEAD_DIM, 1)
    v_ref[:, :NA_W] = va.astype(BF16)
    v_ref[:, NA_W:NA_W + LANES] = jnp.where(low, vb, vb_swapped).astype(BF16)
    v_ref[:, NA_W + LANES:] = jnp.where(low, vb_swapped, vb).astype(BF16)
    n_seq, _, s = kt_ref.shape
    for i in range(n_seq):
        rows = slice(i * s, (i + 1) * s)
        ka_t = ka[rows].T
        kb_t = kb[rows].T
        k0, k1 = kb_t[:HEAD_DIM], kb_t[HEAD_DIM:]
        kt_ref[i, :NA_W, :] = ka_t.astype(BF16)
        kt_ref[i, NA_W:, :] = jnp.concatenate([k0, k0, k1, k1], axis=0).astype(BF16)
        if cache_out:
            kat_ref, vat_ref, kbt_ref, vbt_ref = refs[pos:pos + 4]
            kat_ref[i] = ka_t
            vat_ref[i] = va[rows].T
            kbt_ref[i] = kb_t
            vbt_ref[i] = vb[rows].T


def _qkv(h, mod, g, w_in, layer, gains, cond_div, seq, rope_tabs=None, cache_out=False):
    t = h.shape[0]
    tm = PROJ_TM
    rope = rope_tabs is not None
    assert not (rope and cache_out)
    row = lambda i: (i, 0)
    const = lambda i: (0, 0)
    in_specs = [
        pl.BlockSpec((tm, D_MODEL), row),
        pl.BlockSpec((1, N_MOD, D_MODEL), lambda i: (i // cond_div, 0, 0)),
        pl.BlockSpec((1, D_MODEL), const),
        HBM,
        pl.BlockSpec((1, NA_W), const),
        pl.BlockSpec((1, NA_W), const),
        pl.BlockSpec((1, GQ_W), const),
        pl.BlockSpec((1, GKV_W), const),
    ]
    args = [h, mod, g, w_in, *gains]
    if rope:
        seq_tiles = rope_tabs[0].shape[0] // tm
        in_specs += [pl.BlockSpec((tm, 2 * HEAD_DIM), lambda i: (i % seq_tiles, 0))] * 2
        args += list(rope_tabs)
    if seq >= tm:
        per_seq = seq // tm
        t_block = lambda rows: pl.BlockSpec((1, rows, tm), lambda i: (i // per_seq, 0, i % per_seq))
    else:
        t_block = lambda rows: pl.BlockSpec((tm // seq, rows, seq), lambda i: (i, 0, 0))
    t_shape = lambda rows, dtype: jax.ShapeDtypeStruct((t // seq, rows, seq), dtype)
    out_specs = [pl.BlockSpec((tm, NA_W + GQ_W), row), t_block(KT_ROWS), pl.BlockSpec((tm, KT_ROWS), row)]
    out_shape = [jax.ShapeDtypeStruct((t, NA_W + GQ_W), BF16), t_shape(KT_ROWS, BF16),
                 jax.ShapeDtypeStruct((t, KT_ROWS), BF16)]
    if cache_out:
        out_specs += [t_block(NA_W), t_block(NA_W), t_block(GKV_W), t_block(GKV_W)]
        out_shape += [t_shape(NA_W, F32), t_shape(NA_W, F32), t_shape(GKV_W, F32), t_shape(GKV_W, F32)]
    return pl.pallas_call(
        functools.partial(_qkv_kernel, layer=layer, rope=rope, cache_out=cache_out),
        grid=(t // tm,),
        in_specs=in_specs,
        out_specs=out_specs,
        out_shape=out_shape,
        scratch_shapes=[pltpu.VMEM((D_MODEL, QKV_WIDTH), BF16), *COL_STAGE],
        compiler_params=_params("arbitrary"),
        name="qkv_latent" if rope else "qkv_context",
    )(*args)


def _split_pair(q2):
    low = _low_half()
    zero = jnp.zeros((), q2.dtype)
    return jnp.where(low, q2, zero), jnp.where(low, zero, q2)


def _with_ones(v2):
    return jnp.concatenate([v2, jnp.ones(v2.shape, v2.dtype)], axis=1)


def _join(blocks, axis):
    return blocks[0] if len(blocks) == 1 else jnp.concatenate(blocks, axis=axis)


def _attend(problems):
    scores = [_dot(qm, _join(kts, 1)) for qm, kts, _, _ in problems]
    probs = []
    for s, (_, kts, biases, _) in zip(scores, problems):
        parts, start = [], 0
        for kt, b in zip(kts, biases):
            part = s[:, start:start + kt.shape[1]]
            parts.append(part if b is None else part + b)
            start += kt.shape[1]
        m = functools.reduce(jnp.maximum, [jnp.max(part, axis=-1, keepdims=True) for part in parts])
        probs.append(_join([jnp.exp(part - m).astype(BF16) for part in parts], 1))
    return [_dot(p, _join(vexts, 0)) for p, (_, _, _, vexts) in zip(probs, problems)]


def _merge_pair(oe_even, oe_odd):
    even = oe_even[:, :LANES] * (1.0 / oe_even[:, LANES:])
    odd = oe_odd[:, :LANES] * (1.0 / oe_odd[:, LANES:])
    return jnp.where(_low_half(), even, odd)


def _gqa_groups(q_ref, q_off, o_ref, o_off, keys_values):
    tq = q_ref.shape[0]
    slabs = lambda g, off: [slice(off + (2 * g + j) * LANES, off + (2 * g + j + 1) * LANES) for j in range(2)]
    problems = []
    for g in range(GQA_KV_HEADS):
        qm = jnp.concatenate([part for sl in slabs(g, q_off) for part in _split_pair(q_ref[:, sl])], axis=0)
        problems.append((qm, *keys_values(slice(g * LANES, (g + 1) * LANES))))
    for g, oe in enumerate(_attend(problems)):
        for j, sl in enumerate(slabs(g, o_off)):
            o_ref[:, sl] = _merge_pair(oe[2 * j * tq:(2 * j + 1) * tq],
                                       oe[(2 * j + 1) * tq:(2 * j + 2) * tq]).astype(o_ref.dtype)


def _ctx_attn_kernel(q_ref, kt_ref, v_ref, o_ref):
    for p in range(NA_HEADS // 2):
        sl = slice(p * LANES, (p + 1) * LANES)
        vext = _with_ones(v_ref[:, sl])
        oe = [_attend([(qm, [kt_ref[sl, :]], [None], [vext])])[0] for qm in _split_pair(q_ref[:, sl])]
        o_ref[:, sl] = _merge_pair(*oe).astype(o_ref.dtype)
    _gqa_groups(q_ref, NA_W, o_ref, NA_W,
                lambda kv: ([kt_ref[NA_W + kv.start:NA_W + kv.stop, :]], [None],
                            [_with_ones(v_ref[:, NA_W + kv.start:NA_W + kv.stop])]))


def _ctx_attention(q, kt, v, seq):
    t = q.shape[0]
    row = lambda b: (b, 0)
    return pl.pallas_call(
        _ctx_attn_kernel,
        grid=(t // seq,),
        in_specs=[pl.BlockSpec((seq, NA_W + GQ_W), row),
                  pl.BlockSpec((None, KT_ROWS, seq), lambda b: (b, 0, 0)),
                  pl.BlockSpec((seq, KT_ROWS), row)],
        out_specs=pl.BlockSpec((seq, NA_W + GQ_W), row),
        out_shape=jax.ShapeDtypeStruct((t, NA_W + GQ_W), BF16),
        compiler_params=_params("arbitrary"),
        name="ctx_attention",
    )(q, kt, v)


def _na_kernel(q_ref, kt0_ref, kt1_ref, kt2_ref, v0_ref, v1_ref, v2_ref, ckt_ref, cv_ref, bias_ref, o_ref):
    for p in range(NA_HEADS // 2):
        sl = slice(p * LANES, (p + 1) * LANES)
        kt_loc = jnp.concatenate([kt0_ref[sl, :], kt1_ref[sl, :], kt2_ref[sl, :]], axis=1)
        v_loc = _with_ones(jnp.concatenate([v0_ref[:, sl], v1_ref[:, sl], v2_ref[:, sl]], axis=0))
        v_ctx = _with_ones(cv_ref[:, sl])
        oe = _attend([(qm, [kt_loc, ckt_ref[sl, :]], [bias_ref[2 * p + half], None], [v_loc, v_ctx])
                      for half, qm in enumerate(_split_pair(q_ref[:, sl]))])
        o_ref[:, sl] = _merge_pair(*oe).astype(o_ref.dtype)


def _na_bias(rpb, rows):
    qc = np.arange(GRID_W)
    kc = np.arange(GRID_W)
    ws = np.clip(qc - NA_WIN_C // 2, 0, GRID_W - NA_WIN_C)
    col_ok = (kc[None, :] >= ws[:, None]) & (kc[None, :] < ws[:, None] + NA_WIN_C)
    dc = np.clip(kc[None, :] - qc[:, None] + NA_WIN_C - 1, 0, 2 * NA_WIN_C - 2)
    pick = ((dc[:, :, None] == np.arange(2 * NA_WIN_C - 1)) & col_ok[:, :, None]).astype(np.float32)
    tiles = jnp.einsum("hrd,qkd->hrqk", rpb.astype(F32), pick, precision=lax.Precision.HIGHEST)
    tiles = jnp.where(col_ok, tiles, NEG_INF)
    tiles = jnp.concatenate([tiles, tiles], axis=-1)
    n_off = 2 * NA_WIN_R - 1
    return pl.pallas_call(
        functools.partial(_na_bias_kernel, rows=rows),
        grid=(NA_HEADS,),
        in_specs=[pl.BlockSpec((None, n_off, GRID_W, LANES), lambda h: (h, 0, 0, 0))],
        out_specs=pl.BlockSpec((3, None, NA_BAND_Q, NA_BAND_K), lambda h: (0, h, 0, 0)),
        out_shape=jax.ShapeDtypeStruct((3, NA_HEADS, NA_BAND_Q, NA_BAND_K), F32),
        compiler_params=_params("arbitrary"),
        name="na_bias",
    )(tiles)


def _na_bias_kernel(tiles_ref, o_ref, *, rows):
    low = _low_half()
    masked = jnp.full((GRID_W, LANES), NEG_INF, F32)
    nb = rows // NA_BAND_ROWS
    for t, band in enumerate((0, 1, nb - 1)):
        r0 = band * NA_BAND_ROWS
        k0 = int(np.clip(r0 - NA_WIN_R // 2, 0, rows - NA_BAND_KEY_ROWS))
        for ri in range(NA_BAND_ROWS):
            r = r0 + ri
            start = int(np.clip(r - NA_WIN_R // 2, 0, rows - NA_WIN_R))
            tile = lambda kr: tiles_ref[kr - r + NA_WIN_R - 1] if start <= kr < start + NA_WIN_R else masked
            for m in range(NA_BAND_KEY_ROWS // 2):
                kr = k0 + 2 * m
                o_ref[t, ri * GRID_W:(ri + 1) * GRID_W, m * LANES:(m + 1) * LANES] = jnp.where(
                    low, tile(kr), tile(kr + 1))


def _na_attention(q, kt, v, ctx_kt, ctx_v, bias, batch, seq):
    rows = seq // GRID_W
    nb = rows // NA_BAND_ROWS
    past = ctx_v.shape[1]
    assert (rows - NA_BAND_KEY_ROWS) % NA_BAND_ROWS == 0
    n_chunks = NA_BAND_KEY_ROWS // NA_BAND_ROWS
    band_type = lambda j: jnp.where(j == 0, 0, jnp.where(j == nb - 1, 2, 1))
    chunk0 = lambda j: jnp.clip(j - NA_WIN_R // 2 // NA_BAND_ROWS, 0, nb - n_chunks)
    kt_spec = lambda c: pl.BlockSpec((None, NA_W, NA_BAND_Q), lambda b, j: (b, 0, chunk0(j) + c))
    v_spec = lambda c: pl.BlockSpec((NA_BAND_Q, NA_W), lambda b, j: (b * nb + chunk0(j) + c, 0))
    return pl.pallas_call(
        _na_kernel,
        grid=(batch, nb),
        in_specs=[pl.BlockSpec((NA_BAND_Q, NA_W), lambda b, j: (b * nb + j, 0))]
                 + [kt_spec(c) for c in range(n_chunks)] + [v_spec(c) for c in range(n_chunks)]
                 + [pl.BlockSpec((None, NA_W, past), lambda b, j: (b, 0, 0)),
                    pl.BlockSpec((None, past, NA_W), lambda b, j: (b, 0, 0)),
                    pl.BlockSpec((None, NA_HEADS, NA_BAND_Q, NA_BAND_K), lambda b, j: (band_type(j), 0, 0, 0))],
        out_specs=pl.BlockSpec((NA_BAND_Q, NA_W), lambda b, j: (b * nb + j, 0)),
        out_shape=jax.ShapeDtypeStruct((batch * seq, NA_W), BF16),
        compiler_params=_params("arbitrary", "arbitrary"),
        name="na_attention",
    )(q, *([kt] * n_chunks), *([v] * n_chunks), ctx_kt, ctx_v, bias)


def _gqa_kernel(q_ref, kt_ref, v_ref, ckt_ref, cv_ref, o_ref):
    _gqa_groups(q_ref, 0, o_ref, 0,
                lambda kv: ([kt_ref[kv, :], ckt_ref[kv, :]], [None, None],
                            [_with_ones(v_ref[:, kv]), _with_ones(cv_ref[:, kv])]))


def _gqa_attention(q, kt, v, ctx_kt, ctx_v, batch, seq):
    tq = 128
    nt = seq // tq
    past = ctx_v.shape[1]
    dup = 2 * GKV_W
    return pl.pallas_call(
        _gqa_kernel,
        grid=(batch, nt),
        in_specs=[
            pl.BlockSpec((tq, GQ_W), lambda b, i: (b * nt + i, 1)),
            pl.BlockSpec((None, dup, seq), lambda b, i: (b, NA_W // dup, 0)),
            pl.BlockSpec((seq, dup), lambda b, i: (b, NA_W // dup)),
            pl.BlockSpec((None, dup, past), lambda b, i: (b, 0, 0)),
            pl.BlockSpec((None, past, dup), lambda b, i: (b, 0, 0)),
        ],
        out_specs=pl.BlockSpec((tq, GQ_W), lambda b, i: (b * nt + i, 0)),
        out_shape=jax.ShapeDtypeStruct((batch * seq, GQ_W), BF16),
        compiler_params=_params("arbitrary", "arbitrary"),
        name="gqa_attention",
    )(q, kt, v, ctx_kt, ctx_v)


def _sgu_kernel(hp_ref, hs_ref, mod_ref, g_ref, win_hbm, vg_ref, ws_ref, bs_ref, wout_hbm, op_ref, os_ref,
                win_ref, wout_ref, stage_in, sem_in, stage_out, sem_out, gated_ref, y_ref,
                *, layer, n_ctx):
    i = pl.program_id(0)

    @pl.when(i == 0)
    def _load_weights():
        _load_by_cols(win_hbm, layer, win_ref, stage_in, sem_in)
        _load_by_rows(wout_hbm, layer, wout_ref, stage_out, sem_out)

    tm = hp_ref.shape[0]
    is_ctx = i < n_ctx
    h = jnp.where(is_ctx, hp_ref[...], hs_ref[...])
    xn = _adaln(h, g_ref[...], mod_ref, 1).astype(BF16)

    n_chunks = y_ref.shape[0]
    ssq = jnp.zeros((tm, 1), F32)
    for c in range(n_chunks):
        y = _dot(xn, win_ref[c])
        y = 0.5 * y * (1.0 + lax.erf(y * math.sqrt(0.5)))
        y_ref[c] = y
        if c >= n_chunks // 2:
            ssq = ssq + jnp.sum(y * y, axis=-1, keepdims=True)
    v_scale = lax.rsqrt(ssq * (1.0 / SGU_WIDTH) + EPS)

    per_chunk = SGU_ACT_CHUNK // SGU_GW
    for c in range(tm // SGU_CHUNK):
        rs = slice(c * SGU_CHUNK, (c + 1) * SGU_CHUNK)
        for g in range(SGU_GROUPS):
            cs = slice(g * SGU_GW, (g + 1) * SGU_GW)
            in_chunk = slice((g % per_chunk) * SGU_GW, (g % per_chunk + 1) * SGU_GW)
            u = y_ref[g // per_chunk, rs, in_chunk]
            v = y_ref[n_chunks // 2 + g // per_chunk, rs, in_chunk]
            vn = ((v * v_scale[rs]) * vg_ref[:, cs]).astype(BF16)
            sv = _dot(ws_ref[g].astype(BF16), vn) + bs_ref[:, g:g + 1]
            gated_ref[rs, cs] = (u * sv).astype(BF16)
    res = h + _gate(mod_ref, 1) * _dot(gated_ref[...], wout_ref[...])

    @pl.when(is_ctx)
    def _():
        op_ref[...] = res

    @pl.when(jnp.logical_not(is_ctx))
    def _():
        os_ref[...] = res


def _sgu(hp, hs, mod, g, w_in, v_g, w_s, b_s_t, w_out, layer, dec_seq):
    tm = SGU_TM
    n_act = 2 * SGU_WIDTH // SGU_ACT_CHUNK
    n_ctx, ctx_map, lat_map, mod_map = _two_stream_maps(tm, hp.shape[0], dec_seq)
    row_spec = lambda m: pl.BlockSpec((tm, D_MODEL), m)
    const = lambda i: (0, 0)
    return pl.pallas_call(
        functools.partial(_sgu_kernel, layer=layer, n_ctx=n_ctx),
        grid=((hp.shape[0] + hs.shape[0]) // tm,),
        in_specs=[
            row_spec(ctx_map), row_spec(lat_map),
            pl.BlockSpec((1, N_MOD, D_MODEL), mod_map),
            pl.BlockSpec((1, D_MODEL), const),
            HBM,
            pl.BlockSpec((1, SGU_WIDTH), const),
            pl.BlockSpec((None, SGU_GROUPS, SGU_CHUNK, SGU_CHUNK), lambda i: (layer, 0, 0, 0)),
            pl.BlockSpec((SGU_CHUNK, SGU_GROUPS), const),
            HBM,
        ],
        out_specs=[row_spec(ctx_map), row_spec(lat_map)],
        out_shape=[jax.ShapeDtypeStruct(hp.shape, F32), jax.ShapeDtypeStruct(hs.shape, F32)],
        scratch_shapes=[
            pltpu.VMEM((n_act, D_MODEL, SGU_ACT_CHUNK), BF16),
            pltpu.VMEM((SGU_WIDTH, D_MODEL), BF16),
            *COL_STAGE, *ROW_STAGE,
            pltpu.VMEM((tm, SGU_WIDTH), BF16),
            pltpu.VMEM((n_act, tm, SGU_ACT_CHUNK), F32),
        ],
        compiler_params=_params("arbitrary"),
        name="sgu",
    )(hp, hs, mod, g, w_in, v_g, w_s, b_s_t, w_out)


def _rope_tables(seq):
    half = HEAD_DIM // 2
    t = jnp.arange(seq)
    freqs = ROPE_BASE ** (-jnp.arange(0, half, 2, dtype=F32) / half)
    def tab(pos):
        ang = pos.astype(F32)[:, None] * freqs[None, :]
        cos, sin = jnp.cos(ang), jnp.sin(ang)
        return jnp.concatenate([cos, cos], -1), jnp.concatenate([-sin, sin], -1)
    cr, sr = tab(t // GRID_W)
    cc, sc = tab(t % GRID_W)
    cos = jnp.concatenate([cr, cc], -1)
    sin = jnp.concatenate([sr, sc], -1)
    return jnp.tile(cos, (1, 2)), jnp.tile(sin, (1, 2))


def kernel(x_prompt, x_sample, cache_na_k, cache_na_v, cache_gqa_k, cache_gqa_v, c, c_ctx, norm_g, mod_w, mod_b, ffn1_w_gu, ffn1_w_down, ffn2_w_gu, ffn2_w_down, attn_w_in, attn_w_out, na_q_g, na_k_g, na_rpb, gqa_q_g, gqa_k_g, sgu_w_in, sgu_v_g, sgu_w_s, sgu_b_s, sgu_w_out):
    batch, seq, _ = x_prompt.shape
    dec_batch, dec_seq, _ = x_sample.shape
    past = cache_na_k.shape[2]
    hp = x_prompt.reshape(batch * seq, D_MODEL)
    hs = x_sample.reshape(dec_batch * dec_seq, D_MODEL)

    cond8 = jnp.zeros((8, D_MODEL), F32).at[0].set(c_ctx).at[1:1 + dec_batch].set(c)
    mod_all = _modulation(cond8, mod_w, mod_b).reshape(DEPTH, 8, N_MOD, D_MODEL)

    new_cache = None
    for layer in range(DEPTH):
        mod_l = mod_all[layer]
        mods = (mod_l[0:1], mod_l[1:1 + dec_batch])
        g0, g1, g2 = (norm_g[layer, s][None, :] for s in range(3))

        mod3 = mod_l[0:1 + dec_batch]
        hp, hs = _ffn(hp, hs, mod3, g0, ffn1_w_gu, ffn1_w_down, layer, 0, dec_seq)

        if layer % 2 == 0:
            e = layer // 2
            gains = (jnp.tile(na_q_g[e], NA_HEADS)[None], jnp.tile(na_k_g[e], NA_HEADS)[None],
                     jnp.tile(gqa_q_g[e], GQA_HEADS)[None], jnp.tile(gqa_k_g[e], GQA_KV_HEADS)[None])
            q, kt, v, *new_cache = _qkv(hp, mods[0], g1, attn_w_in, e, gains, hp.shape[0] // PROJ_TM, seq,
                                        cache_out=True)
            o_ctx = _ctx_attention(q, kt, v, seq)
            q, kt, v = _qkv(hs, mods[1], g1, attn_w_in, e, gains, dec_seq // PROJ_TM, dec_seq,
                            rope_tabs=_rope_tables(dec_seq))
            to_kt = lambda ck, rep: jnp.repeat(jnp.transpose(ck[:, e], (0, 2, 3, 1)), rep, axis=1).reshape(
                dec_batch, -1, past).astype(BF16)
            to_v = lambda cv, rep: jnp.repeat(cv[:, e], rep, axis=2).reshape(dec_batch, past, -1).astype(BF16)
            oa = _na_attention(q, kt, v, to_kt(cache_na_k, 1), to_v(cache_na_v, 1),
                               _na_bias(na_rpb[e], dec_seq // GRID_W), dec_batch, dec_seq)
            ob = _gqa_attention(q, kt, v, to_kt(cache_gqa_k, 2), to_v(cache_gqa_v, 2), dec_batch, dec_seq)
            attn = (o_ctx, oa, ob, attn_w_out, e)
        else:
            attn = None
            o = layer // 2
            hp, hs = _sgu(hp, hs, mod3, g1, sgu_w_in, sgu_v_g[o][None], sgu_w_s, sgu_b_s[o].T, sgu_w_out,
                          o, dec_seq)

        hp, hs = _ffn(hp, hs, mod3, g2, ffn2_w_gu, ffn2_w_down, layer, 2, dec_seq, attn=attn)

    n_attn = (DEPTH + 1) // 2
    assert n_attn == 1
    from_t = lambda x: jnp.transpose(x.reshape(batch, n_attn, -1, HEAD_DIM, seq), (0, 1, 4, 2, 3))
    return (hp.reshape(batch, seq, D_MODEL), hs.reshape(dec_batch, dec_seq, D_MODEL),
            *(from_t(x) for x in new_cache))
```

```python
import functools
import math

import jax
import jax.numpy as jnp
import numpy as np
from jax import lax
from jax.experimental import pallas as pl
from jax.experimental.pallas import tpu as pltpu

D_MODEL = 1024
DEPTH = 2
GRID_W = 64
HEAD_DIM = 64
NA_HEADS = 8
NA_WIN_R = 8
NA_WIN_C = 16
GQA_HEADS = 8
GQA_KV_HEADS = 2
GQA_GROUP = GQA_HEADS // GQA_KV_HEADS
NA_W = NA_HEADS * HEAD_DIM
GQ_W = GQA_HEADS * HEAD_DIM
GKV_W = GQA_KV_HEADS * HEAD_DIM
QKV_WIDTH = 3 * NA_W + GQ_W + 2 * GKV_W
ROPE_BASE = 10000.0
SGU_CHUNK = 128
SGU_GROUPS = 8
SGU_WIDTH = 2 * D_MODEL
SGU_GW = SGU_WIDTH // SGU_GROUPS
FFN_HIDDEN = 2816
N_MOD = 9
EPS = 1e-6
NEG_INF = -1e30
Q_SCALE = HEAD_DIM ** -0.5

NA_BAND_ROWS = 4
NA_BAND_KEY_ROWS = 12
NA_BAND_Q = NA_BAND_ROWS * GRID_W
NA_BAND_K = NA_BAND_KEY_ROWS * GRID_W

FFN_TM = 512
FFN_ACT_CHUNK = 512
SGU_ACT_CHUNK = 512
SGU_TM = 512
PROJ_TM = 512
W_STAGE = 256

LANES = 128
KT_ROWS = NA_W + 2 * GKV_W
MXU_COLS = 256
VMEM_LIMIT = 56 * 1024 * 1024

BF16 = jnp.bfloat16
F32 = jnp.float32


def _dot(a, b):
    return jnp.dot(a, b, preferred_element_type=F32)


def _params(*sem):
    return pltpu.CompilerParams(dimension_semantics=sem, vmem_limit_bytes=VMEM_LIMIT)


def _adaln(x, g, mod_ref, j):
    shift = mod_ref[0, 3 * j:3 * j + 1, :]
    scale = mod_ref[0, 3 * j + 1:3 * j + 2, :]
    y = x * lax.rsqrt(jnp.mean(x * x, axis=-1, keepdims=True) + EPS)
    return (y * g) * (1.0 + scale) + shift


def _gate(mod_ref, j):
    return mod_ref[0, 3 * j + 2:3 * j + 3, :]


def _mod_kernel(c_ref, w_ref, b_ref, o_ref):
    c = c_ref[...]
    a = (c * jax.nn.sigmoid(c)).astype(BF16)
    o_ref[...] = _dot(a, w_ref[...].astype(BF16)) + b_ref[...]


def _modulation(cond8, mod_w, mod_b):
    tn = 1024
    n = N_MOD * D_MODEL
    return pl.pallas_call(
        _mod_kernel,
        grid=(DEPTH, n // tn),
        in_specs=[
            pl.BlockSpec((8, D_MODEL), lambda l, k: (0, 0)),
            pl.BlockSpec((None, D_MODEL, tn), lambda l, k: (l, 0, k)),
            pl.BlockSpec((None, 1, tn), lambda l, k: (l, 0, k)),
        ],
        out_specs=pl.BlockSpec((None, 8, tn), lambda l, k: (l, 0, k)),
        out_shape=jax.ShapeDtypeStruct((DEPTH, 8, n), F32),
        compiler_params=_params("arbitrary", "arbitrary"),
        name="modulation",
    )(cond8, mod_w, mod_b.reshape(DEPTH, 1, n))


def _two_stream_maps(tm, n_ctx_rows, dec_seq):
    n_ctx = n_ctx_rows // tm
    per_seq = dec_seq // tm
    ctx_map = lambda i: (jnp.minimum(i, n_ctx - 1), 0)
    lat_map = lambda i: (jnp.maximum(i - n_ctx, 0), 0)
    mod_map = lambda i: (jnp.where(i < n_ctx, 0, 1 + (i - n_ctx) // per_seq), 0, 0)
    return n_ctx, ctx_map, lat_map, mod_map


class _WeightStream:
    def __init__(self, pairs, stage_ref, sem_ref):
        self.pairs, self.stage, self.sem, self.done = pairs, stage_ref, sem_ref, 0
        for k in range(min(2, len(pairs))):
            self._copy(k).start()

    def _copy(self, k):
        return pltpu.make_async_copy(self.pairs[k][0], self.stage.at[k % 2], self.sem.at[k % 2])

    def take(self, n=None):
        n = len(self.pairs) - self.done if n is None else min(n, len(self.pairs) - self.done)
        for _ in range(n):
            k = self.done
            self._copy(k).wait()
            dst, idx = self.pairs[k][1]
            dst[idx] = self.stage[k % 2].astype(BF16)
            if k + 2 < len(self.pairs):
                self._copy(k + 2).start()
            self.done += 1


def _col_pairs(w_hbm, layer, w_ref, cols=None):
    if len(w_ref.shape) == 2:
        dst = lambda c: (slice(None), slice(c, c + W_STAGE))
        n = w_ref.shape[1]
    else:
        width = w_ref.shape[2]
        dst = lambda c: (c // width, slice(None), slice(c % width, c % width + W_STAGE))
        n = w_ref.shape[0] * width
    cols = range(0, n, W_STAGE) if cols is None else cols
    return [(w_hbm.at[layer, :, pl.ds(c, W_STAGE)], (w_ref, dst(c))) for c in cols]


def _row_pairs(w_hbm, layer, w_ref):
    return [(w_hbm.at[layer, pl.ds(r, W_STAGE), :], (w_ref, (slice(r, r + W_STAGE), slice(None))))
            for r in range(0, w_ref.shape[0], W_STAGE)]


def _load_by_cols(w_hbm, layer, w_ref, stage_ref, sem_ref):
    _WeightStream(_col_pairs(w_hbm, layer, w_ref), stage_ref, sem_ref).take()


def _load_by_rows(w_hbm, layer, w_ref, stage_ref, sem_ref):
    _WeightStream(_row_pairs(w_hbm, layer, w_ref), stage_ref, sem_ref).take()


COL_STAGE = (pltpu.VMEM((2, D_MODEL, W_STAGE), F32), pltpu.SemaphoreType.DMA((2,)))
ROW_STAGE = (pltpu.VMEM((2, W_STAGE, D_MODEL), F32), pltpu.SemaphoreType.DMA((2,)))
HBM = pl.BlockSpec(memory_space=pl.ANY)


def _ffn_kernel(*refs, layer, j, n_ctx, proj_layer, split_in, split_out):
    h_refs = refs[:2 if split_in else 1]
    pos = len(h_refs)
    mod_ref, g_ref, wgu_hbm, wd_hbm = refs[pos:pos + 4]
    pos += 4
    if proj_layer is not None:
        octx_ref, oa_ref, ob_ref, wo_hbm = refs[pos:pos + 4]
        pos += 4
    out_refs = refs[pos:pos + (2 if split_out else 1)]
    pos += len(out_refs)
    wgu_ref, wd_ref, stage_gu, sem_gu, stage_d, sem_d, act_ref = refs[pos:pos + 7]
    wo_ref = refs[pos + 7] if proj_layer is not None else None
    i = pl.program_id(0)
    is_ctx = i < n_ctx
    chunks = [(c, min(FFN_ACT_CHUNK, FFN_HIDDEN - c)) for c in range(0, FFN_HIDDEN, FFN_ACT_CHUNK)]

    def tile(first):
        if first:
            need = [col for c, w in chunks for base in (c, FFN_HIDDEN + c) for col in range(base, base + w, W_STAGE)]
            cols = _WeightStream(_col_pairs(wgu_hbm, layer, wgu_ref, need), stage_gu, sem_gu)
            proj = _row_pairs(wo_hbm, proj_layer, wo_ref) if proj_layer is not None else []
            rows = _WeightStream(proj + _row_pairs(wd_hbm, layer, wd_ref), stage_d, sem_d)
            rows.take(len(proj))
        if first or not split_in:
            h = h_refs[0][...]
        else:
            h = jnp.where(is_ctx, h_refs[0][...], h_refs[1][...])
        if proj_layer is not None:
            latent_heads = jnp.concatenate([oa_ref[...], ob_ref[...]], axis=1)
            heads = octx_ref[...] if first else jnp.where(is_ctx, octx_ref[...], latent_heads)
            h = h + _gate(mod_ref, 1) * _dot(heads, wo_ref[...])
        xn = _adaln(h, g_ref[...], mod_ref, j).astype(BF16)
        for c, w in chunks:
            if first:
                cols.take(2 * w // W_STAGE)
            gate = _dot(xn, wgu_ref[:, c:c + w])
            up = _dot(xn, wgu_ref[:, FFN_HIDDEN + c:FFN_HIDDEN + c + w])
            act_ref[:, c:c + w] = ((gate * jax.nn.sigmoid(gate)) * up).astype(BF16)
            if first:
                rows.take(2)
        if first:
            rows.take()
        res = h + (0.5 * _gate(mod_ref, j)) * _dot(act_ref[...], wd_ref[...])
        if first or not split_out:
            out_refs[0][...] = res
        else:
            @pl.when(is_ctx)
            def _():
                out_refs[0][...] = res

            @pl.when(jnp.logical_not(is_ctx))
            def _():
                out_refs[1][...] = res

    pl.when(i == 0)(lambda: tile(True))
    pl.when(i > 0)(lambda: tile(False))


def _ffn(h, mod, g, w_gu, w_down, layer, j, ctx_rows, dec_seq, attn=None, split_out=False):
    tm = FFN_TM
    split_in = isinstance(h, tuple)
    h_arrays = list(h) if split_in else [h]
    total_rows = sum(a.shape[0] for a in h_arrays)
    n_ctx, ctx_map, lat_map, mod_map = _two_stream_maps(tm, ctx_rows, dec_seq)
    assert n_ctx >= 1
    row_map = lambda i: (i, 0)
    row_spec = lambda m, width=D_MODEL: pl.BlockSpec((tm, width), m)
    in_specs = ([row_spec(ctx_map), row_spec(lat_map)] if split_in else [row_spec(row_map)]) + [
        pl.BlockSpec((1, N_MOD, D_MODEL), mod_map),
        pl.BlockSpec((1, D_MODEL), lambda i: (0, 0)),
        HBM, HBM]
    args = [*h_arrays, mod, g, w_gu, w_down]
    scratch = [pltpu.VMEM((D_MODEL, 2 * FFN_HIDDEN), BF16), pltpu.VMEM((FFN_HIDDEN, D_MODEL), BF16),
               *COL_STAGE, *ROW_STAGE, pltpu.VMEM((tm, FFN_HIDDEN), BF16)]
    proj_layer = None
    if attn is not None:
        o_ctx, oa, ob, w_out, proj_layer = attn
        in_specs += [row_spec(ctx_map, o_ctx.shape[1]), row_spec(lat_map, oa.shape[1]),
                     row_spec(lat_map, ob.shape[1]), HBM]
        args += [o_ctx, oa, ob, w_out]
        scratch.append(pltpu.VMEM((o_ctx.shape[1], D_MODEL), BF16))
    if split_out:
        out_specs = [row_spec(ctx_map), row_spec(lat_map)]
        out_shape = [jax.ShapeDtypeStruct((ctx_rows, D_MODEL), F32),
                     jax.ShapeDtypeStruct((total_rows - ctx_rows, D_MODEL), F32)]
    else:
        out_specs = row_spec(row_map)
        out_shape = jax.ShapeDtypeStruct((total_rows, D_MODEL), F32)
    return pl.pallas_call(
        functools.partial(_ffn_kernel, layer=layer, j=j, n_ctx=n_ctx, proj_layer=proj_layer,
                          split_in=split_in, split_out=split_out),
        grid=(total_rows // tm,),
        in_specs=in_specs,
        out_specs=out_specs,
        out_shape=out_shape,
        scratch_shapes=scratch,
        compiler_params=_params("arbitrary"),
        name=f"ffn{j}_l{layer}",
    )(*args)


def _head_mean_sq(x):
    w = x.shape[-1]
    r = lax.broadcasted_iota(jnp.int32, (MXU_COLS, MXU_COLS), 0) // HEAD_DIM
    c = lax.broadcasted_iota(jnp.int32, (MXU_COLS, MXU_COLS), 1) // HEAD_DIM
    bd = jnp.where(r == c, 1.0 / HEAD_DIM, 0.0).astype(BF16)
    sq = (x * x).astype(BF16)
    parts = []
    for s in range(0, w, MXU_COLS):
        e = min(s + MXU_COLS, w)
        parts.append(_dot(sq[:, s:e], bd[:e - s, :e - s]))
    return parts[0] if len(parts) == 1 else jnp.concatenate(parts, axis=-1)


def _low_half():
    return lax.broadcasted_iota(jnp.int32, (1, LANES), 1) < HEAD_DIM


def _head_rms(x, gain):
    return (x * lax.rsqrt(_head_mean_sq(x) + EPS)) * gain


def _rope(x, cos, sin_signed):
    lanes = cos.shape[-1]
    outs = []
    for s in range(0, x.shape[-1], lanes):
        xs = x[:, s:s + lanes]
        blk = lax.broadcasted_iota(jnp.int32, xs.shape, 1) // (HEAD_DIM // 4)
        partner = jnp.where(blk % 2 == 0,
                            pltpu.roll(xs, lanes - HEAD_DIM // 4, 1),
                            pltpu.roll(xs, HEAD_DIM // 4, 1))
        outs.append(xs * cos + partner * sin_signed)
    return outs[0] if len(outs) == 1 else jnp.concatenate(outs, axis=-1)


def _qkv_kernel(*refs, layer, rope, cache_out):
    h_ref, mod_ref, g_ref, w_hbm, gqa_ref, gka_ref, gqb_ref, gkb_ref = refs[:8]
    w_ref, stage_ref, sem_ref = refs[-3:]
    pos = 8
    if rope:
        cos_ref, sin_ref = refs[pos:pos + 2]
        pos += 2
    q_ref, kt_ref, v_ref = refs[pos:pos + 3]
    pos += 3

    @pl.when(pl.program_id(0) == 0)
    def _load_weights():
        _load_by_cols(w_hbm, layer, w_ref, stage_ref, sem_ref)

    xn = _adaln(h_ref[...], g_ref[...], mod_ref, 1).astype(BF16)
    y = _dot(xn, w_ref[...])
    o = 0
    qa = _head_rms(y[:, o:o + NA_W], gqa_ref[...]); o += NA_W
    ka = _head_rms(y[:, o:o + NA_W], gka_ref[...]); o += NA_W
    va = y[:, o:o + NA_W]; o += NA_W
    qb = _head_rms(y[:, o:o + GQ_W], gqb_ref[...]); o += GQ_W
    kb = _head_rms(y[:, o:o + GKV_W], gkb_ref[...]); o += GKV_W
    vb = y[:, o:o + GKV_W]
    if rope:
        cos, sin = cos_ref[...], sin_ref[...]
        qb = _rope(qb, cos, sin)
        kb = _rope(kb, cos, sin)
    q_ref[:, :NA_W] = (qa * Q_SCALE).astype(BF16)
    q_ref[:, NA_W:] = (qb * Q_SCALE).astype(BF16)
    low = _low_half()
    vb_swapped = pltpu.roll(vb, HEAD_DIM, 1)
    v_ref[:, :NA_W] = va.astype(BF16)
    v_ref[:, NA_W:NA_W + LANES] = jnp.where(low, vb, vb_swapped).astype(BF16)
    v_ref[:, NA_W + LANES:] = jnp.where(low, vb_swapped, vb).astype(BF16)
    n_seq, _, s = kt_ref.shape
    for i in range(n_seq):
        rows = slice(i * s, (i + 1) * s)
        ka_t = ka[rows].T
        kb_t = kb[rows].T
        k0, k1 = kb_t[:HEAD_DIM], kb_t[HEAD_DIM:]
        kt_ref[i, :NA_W, :] = ka_t.astype(BF16)
        kt_ref[i, NA_W:, :] = jnp.concatenate([k0, k0, k1, k1], axis=0).astype(BF16)
        if cache_out:
            kat_ref, vat_ref, kbt_ref, vbt_ref = refs[pos:pos + 4]
            kat_ref[i] = ka_t
            vat_ref[i] = va[rows].T
            kbt_ref[i] = kb_t
            vbt_ref[i] = vb[rows].T


def _qkv(h, row0, t, mod, g, w_in, layer, gains, cond_div, seq, rope_tabs=None, cache_out=False):
    tm = PROJ_TM
    rope = rope_tabs is not None
    assert not (rope and cache_out)
    row = lambda i: (i, 0)
    const = lambda i: (0, 0)
    in_specs = [
        pl.BlockSpec((tm, D_MODEL), lambda i: (i + row0 // tm, 0)),
        pl.BlockSpec((1, N_MOD, D_MODEL), lambda i: (i // cond_div, 0, 0)),
        pl.BlockSpec((1, D_MODEL), const),
        HBM,
        pl.BlockSpec((1, NA_W), const),
        pl.BlockSpec((1, NA_W), const),
        pl.BlockSpec((1, GQ_W), const),
        pl.BlockSpec((1, GKV_W), const),
    ]
    args = [h, mod, g, w_in, *gains]
    if rope:
        seq_tiles = rope_tabs[0].shape[0] // tm
        in_specs += [pl.BlockSpec((tm, 2 * HEAD_DIM), lambda i: (i % seq_tiles, 0))] * 2
        args += list(rope_tabs)
    if seq >= tm:
        per_seq = seq // tm
        t_block = lambda rows: pl.BlockSpec((1, rows, tm), lambda i: (i // per_seq, 0, i % per_seq))
    else:
        t_block = lambda rows: pl.BlockSpec((tm // seq, rows, seq), lambda i: (i, 0, 0))
    t_shape = lambda rows, dtype: jax.ShapeDtypeStruct((t // seq, rows, seq), dtype)
    out_specs = [pl.BlockSpec((tm, NA_W + GQ_W), row), t_block(KT_ROWS), pl.BlockSpec((tm, KT_ROWS), row)]
    out_shape = [jax.ShapeDtypeStruct((t, NA_W + GQ_W), BF16), t_shape(KT_ROWS, BF16),
                 jax.ShapeDtypeStruct((t, KT_ROWS), BF16)]
    if cache_out:
        out_specs += [t_block(NA_W), t_block(NA_W), t_block(GKV_W), t_block(GKV_W)]
        out_shape += [t_shape(NA_W, F32), t_shape(NA_W, F32), t_shape(GKV_W, F32), t_shape(GKV_W, F32)]
    return pl.pallas_call(
        functools.partial(_qkv_kernel, layer=layer, rope=rope, cache_out=cache_out),
        grid=(t // tm,),
        in_specs=in_specs,
        out_specs=out_specs,
        out_shape=out_shape,
        scratch_shapes=[pltpu.VMEM((D_MODEL, QKV_WIDTH), BF16), *COL_STAGE],
        compiler_params=_params("arbitrary"),
        name="qkv_latent" if rope else "qkv_context",
    )(*args)


def _split_pair(q2):
    low = _low_half()
    zero = jnp.zeros((), q2.dtype)
    return jnp.where(low, q2, zero), jnp.where(low, zero, q2)


def _with_ones(v2):
    return jnp.concatenate([v2, jnp.ones(v2.shape, v2.dtype)], axis=1)


def _join(blocks, axis):
    return blocks[0] if len(blocks) == 1 else jnp.concatenate(blocks, axis=axis)


def _attend(problems):
    scores = [_dot(qm, _join(kts, 1)) for qm, kts, _, _ in problems]
    probs = []
    for s, (_, kts, biases, _) in zip(scores, problems):
        parts, start = [], 0
        for kt, b in zip(kts, biases):
            part = s[:, start:start + kt.shape[1]]
            parts.append(part if b is None else part + b)
            start += kt.shape[1]
        m = functools.reduce(jnp.maximum, [jnp.max(part, axis=-1, keepdims=True) for part in parts])
        probs.append(_join([jnp.exp(part - m).astype(BF16) for part in parts], 1))
    return [_dot(p, _join(vexts, 0)) for p, (_, _, _, vexts) in zip(probs, problems)]


def _merge_pair(oe_even, oe_odd):
    even = oe_even[:, :LANES] * (1.0 / oe_even[:, LANES:])
    odd = oe_odd[:, :LANES] * (1.0 / oe_odd[:, LANES:])
    return jnp.where(_low_half(), even, odd)


def _gqa_groups(q_ref, q_off, o_ref, o_off, keys_values):
    tq = q_ref.shape[0]
    slabs = lambda g, off: [slice(off + (2 * g + j) * LANES, off + (2 * g + j + 1) * LANES) for j in range(2)]
    problems = []
    for g in range(GQA_KV_HEADS):
        qm = jnp.concatenate([part for sl in slabs(g, q_off) for part in _split_pair(q_ref[:, sl])], axis=0)
        problems.append((qm, *keys_values(slice(g * LANES, (g + 1) * LANES))))
    for g, oe in enumerate(_attend(problems)):
        for j, sl in enumerate(slabs(g, o_off)):
            o_ref[:, sl] = _merge_pair(oe[2 * j * tq:(2 * j + 1) * tq],
                                       oe[(2 * j + 1) * tq:(2 * j + 2) * tq]).astype(o_ref.dtype)


def _ctx_attn_kernel(q_ref, kt_ref, v_ref, o_ref):
    for p in range(NA_HEADS // 2):
        sl = slice(p * LANES, (p + 1) * LANES)
        vext = _with_ones(v_ref[:, sl])
        oe = [_attend([(qm, [kt_ref[sl, :]], [None], [vext])])[0] for qm in _split_pair(q_ref[:, sl])]
        o_ref[:, sl] = _merge_pair(*oe).astype(o_ref.dtype)
    _gqa_groups(q_ref, NA_W, o_ref, NA_W,
                lambda kv: ([kt_ref[NA_W + kv.start:NA_W + kv.stop, :]], [None],
                            [_with_ones(v_ref[:, NA_W + kv.start:NA_W + kv.stop])]))


def _ctx_attention(q, kt, v, seq):
    t = q.shape[0]
    row = lambda b: (b, 0)
    return pl.pallas_call(
        _ctx_attn_kernel,
        grid=(t // seq,),
        in_specs=[pl.BlockSpec((seq, NA_W + GQ_W), row),
                  pl.BlockSpec((None, KT_ROWS, seq), lambda b: (b, 0, 0)),
                  pl.BlockSpec((seq, KT_ROWS), row)],
        out_specs=pl.BlockSpec((seq, NA_W + GQ_W), row),
        out_shape=jax.ShapeDtypeStruct((t, NA_W + GQ_W), BF16),
        compiler_params=_params("arbitrary"),
        name="ctx_attention",
    )(q, kt, v)


def _na_kernel(q_ref, kt0_ref, kt1_ref, kt2_ref, v0_ref, v1_ref, v2_ref, ckt_ref, cv_ref, bias_ref, o_ref):
    for p in range(NA_HEADS // 2):
        sl = slice(p * LANES, (p + 1) * LANES)
        kt_loc = jnp.concatenate([kt0_ref[sl, :], kt1_ref[sl, :], kt2_ref[sl, :]], axis=1)
        v_loc = _with_ones(jnp.concatenate([v0_ref[:, sl], v1_ref[:, sl], v2_ref[:, sl]], axis=0))
        v_ctx = _with_ones(cv_ref[:, sl])
        oe = _attend([(qm, [kt_loc, ckt_ref[sl, :]], [bias_ref[2 * p + half], None], [v_loc, v_ctx])
                      for half, qm in enumerate(_split_pair(q_ref[:, sl]))])
        o_ref[:, sl] = _merge_pair(*oe).astype(o_ref.dtype)


def _na_bias(rpb, rows):
    qc = np.arange(GRID_W)
    kc = np.arange(GRID_W)
    ws = np.clip(qc - NA_WIN_C // 2, 0, GRID_W - NA_WIN_C)
    col_ok = (kc[None, :] >= ws[:, None]) & (kc[None, :] < ws[:, None] + NA_WIN_C)
    dc = np.clip(kc[None, :] - qc[:, None] + NA_WIN_C - 1, 0, 2 * NA_WIN_C - 2)
    pick = ((dc[:, :, None] == np.arange(2 * NA_WIN_C - 1)) & col_ok[:, :, None]).astype(np.float32)
    tiles = jnp.einsum("hrd,qkd->hrqk", rpb.astype(F32), pick, precision=lax.Precision.HIGHEST)
    tiles = jnp.where(col_ok, tiles, NEG_INF)
    tiles = jnp.concatenate([tiles, tiles], axis=-1)
    n_off = 2 * NA_WIN_R - 1
    return pl.pallas_call(
        functools.partial(_na_bias_kernel, rows=rows),
        grid=(NA_HEADS,),
        in_specs=[pl.BlockSpec((None, n_off, GRID_W, LANES), lambda h: (h, 0, 0, 0))],
        out_specs=pl.BlockSpec((3, None, NA_BAND_Q, NA_BAND_K), lambda h: (0, h, 0, 0)),
        out_shape=jax.ShapeDtypeStruct((3, NA_HEADS, NA_BAND_Q, NA_BAND_K), F32),
        compiler_params=_params("arbitrary"),
        name="na_bias",
    )(tiles)


def _na_bias_kernel(tiles_ref, o_ref, *, rows):
    low = _low_half()
    masked = jnp.full((GRID_W, LANES), NEG_INF, F32)
    nb = rows // NA_BAND_ROWS
    for t, band in enumerate((0, 1, nb - 1)):
        r0 = band * NA_BAND_ROWS
        k0 = int(np.clip(r0 - NA_WIN_R // 2, 0, rows - NA_BAND_KEY_ROWS))
        for ri in range(NA_BAND_ROWS):
            r = r0 + ri
            start = int(np.clip(r - NA_WIN_R // 2, 0, rows - NA_WIN_R))
            tile = lambda kr: tiles_ref[kr - r + NA_WIN_R - 1] if start <= kr < start + NA_WIN_R else masked
            for m in range(NA_BAND_KEY_ROWS // 2):
                kr = k0 + 2 * m
                o_ref[t, ri * GRID_W:(ri + 1) * GRID_W, m * LANES:(m + 1) * LANES] = jnp.where(
                    low, tile(kr), tile(kr + 1))


def _na_attention(q, kt, v, ctx_kt, ctx_v, bias, batch, seq):
    rows = seq // GRID_W
    nb = rows // NA_BAND_ROWS
    past = ctx_v.shape[1]
    assert (rows - NA_BAND_KEY_ROWS) % NA_BAND_ROWS == 0
    n_chunks = NA_BAND_KEY_ROWS // NA_BAND_ROWS
    band_type = lambda j: jnp.where(j == 0, 0, jnp.where(j == nb - 1, 2, 1))
    chunk0 = lambda j: jnp.clip(j - NA_WIN_R // 2 // NA_BAND_ROWS, 0, nb - n_chunks)
    kt_spec = lambda c: pl.BlockSpec((None, NA_W, NA_BAND_Q), lambda b, j: (b, 0, chunk0(j) + c))
    v_spec = lambda c: pl.BlockSpec((NA_BAND_Q, NA_W), lambda b, j: (b * nb + chunk0(j) + c, 0))
    return pl.pallas_call(
        _na_kernel,
        grid=(batch, nb),
        in_specs=[pl.BlockSpec((NA_BAND_Q, NA_W), lambda b, j: (b * nb + j, 0))]
                 + [kt_spec(c) for c in range(n_chunks)] + [v_spec(c) for c in range(n_chunks)]
                 + [pl.BlockSpec((None, NA_W, past), lambda b, j: (b, 0, 0)),
                    pl.BlockSpec((None, past, NA_W), lambda b, j: (b, 0, 0)),
                    pl.BlockSpec((None, NA_HEADS, NA_BAND_Q, NA_BAND_K), lambda b, j: (band_type(j), 0, 0, 0))],
        out_specs=pl.BlockSpec((NA_BAND_Q, NA_W), lambda b, j: (b * nb + j, 0)),
        out_shape=jax.ShapeDtypeStruct((batch * seq, NA_W), BF16),
        compiler_params=_params("arbitrary", "arbitrary"),
        name="na_attention",
    )(q, *([kt] * n_chunks), *([v] * n_chunks), ctx_kt, ctx_v, bias)


def _gqa_kernel(q_ref, kt_ref, v_ref, ckt_ref, cv_ref, o_ref):
    _gqa_groups(q_ref, 0, o_ref, 0,
                lambda kv: ([kt_ref[kv, :], ckt_ref[kv, :]], [None, None],
                            [_with_ones(v_ref[:, kv]), _with_ones(cv_ref[:, kv])]))


def _gqa_attention(q, kt, v, ctx_kt, ctx_v, batch, seq):
    tq = 128
    nt = seq // tq
    past = ctx_v.shape[1]
    dup = 2 * GKV_W
    return pl.pallas_call(
        _gqa_kernel,
        grid=(batch, nt),
        in_specs=[
            pl.BlockSpec((tq, GQ_W), lambda b, i: (b * nt + i, 1)),
            pl.BlockSpec((None, dup, seq), lambda b, i: (b, NA_W // dup, 0)),
            pl.BlockSpec((seq, dup), lambda b, i: (b, NA_W // dup)),
            pl.BlockSpec((None, dup, past), lambda b, i: (b, 0, 0)),
            pl.BlockSpec((None, past, dup), lambda b, i: (b, 0, 0)),
        ],
        out_specs=pl.BlockSpec((tq, GQ_W), lambda b, i: (b * nt + i, 0)),
        out_shape=jax.ShapeDtypeStruct((batch * seq, GQ_W), BF16),
        compiler_params=_params("arbitrary", "arbitrary"),
        name="gqa_attention",
    )(q, kt, v, ctx_kt, ctx_v)


def _sgu_kernel(h_ref, mod_ref, g_ref, win_hbm, vg_ref, ws_ref, bs_ref, wout_hbm, o_ref,
                win_ref, wout_ref, stage_in, sem_in, stage_out, sem_out, gated_ref, y_ref, *, layer):
    tm = h_ref.shape[0]
    n_chunks = y_ref.shape[0]
    per_chunk = SGU_ACT_CHUNK // SGU_GW

    def tile(first):
        if first:
            cols = _WeightStream(_col_pairs(win_hbm, layer, win_ref), stage_in, sem_in)
            rows = _WeightStream(_row_pairs(wout_hbm, layer, wout_ref), stage_out, sem_out)
        h = h_ref[...]
        xn = _adaln(h, g_ref[...], mod_ref, 1).astype(BF16)
        ssq = jnp.zeros((tm, 1), F32)
        for c in range(n_chunks):
            if first:
                cols.take(SGU_ACT_CHUNK // W_STAGE)
            y = _dot(xn, win_ref[c])
            y = 0.5 * y * (1.0 + lax.erf(y * math.sqrt(0.5)))
            y_ref[c] = y
            if c >= n_chunks // 2:
                ssq = ssq + jnp.sum(y * y, axis=-1, keepdims=True)
            if first:
                rows.take(1)
        if first:
            rows.take()
        v_scale = lax.rsqrt(ssq * (1.0 / SGU_WIDTH) + EPS)
        for c in range(tm // SGU_CHUNK):
            rs = slice(c * SGU_CHUNK, (c + 1) * SGU_CHUNK)
            for g in range(SGU_GROUPS):
                cs = slice(g * SGU_GW, (g + 1) * SGU_GW)
                in_chunk = slice((g % per_chunk) * SGU_GW, (g % per_chunk + 1) * SGU_GW)
                u = y_ref[g // per_chunk, rs, in_chunk]
                v = y_ref[n_chunks // 2 + g // per_chunk, rs, in_chunk]
                vn = ((v * v_scale[rs]) * vg_ref[:, cs]).astype(BF16)
                sv = _dot(ws_ref[g].astype(BF16), vn) + bs_ref[:, g:g + 1]
                gated_ref[rs, cs] = (u * sv).astype(BF16)
        o_ref[...] = h + _gate(mod_ref, 1) * _dot(gated_ref[...], wout_ref[...])

    pl.when(pl.program_id(0) == 0)(lambda: tile(True))
    pl.when(pl.program_id(0) > 0)(lambda: tile(False))


def _sgu(h, mod, g, w_in, v_g, w_s, b_s_t, w_out, layer, ctx_rows, dec_seq):
    tm = SGU_TM
    n_act = 2 * SGU_WIDTH // SGU_ACT_CHUNK
    _, _, _, mod_map = _two_stream_maps(tm, ctx_rows, dec_seq)
    row_spec = pl.BlockSpec((tm, D_MODEL), lambda i: (i, 0))
    const = lambda i: (0, 0)
    return pl.pallas_call(
        functools.partial(_sgu_kernel, layer=layer),
        grid=(h.shape[0] // tm,),
        in_specs=[
            row_spec,
            pl.BlockSpec((1, N_MOD, D_MODEL), mod_map),
            pl.BlockSpec((1, D_MODEL), const),
            HBM,
            pl.BlockSpec((1, SGU_WIDTH), const),
            pl.BlockSpec((None, SGU_GROUPS, SGU_CHUNK, SGU_CHUNK), lambda i: (layer, 0, 0, 0)),
            pl.BlockSpec((SGU_CHUNK, SGU_GROUPS), const),
            HBM,
        ],
        out_specs=row_spec,
        out_shape=jax.ShapeDtypeStruct(h.shape, F32),
        scratch_shapes=[
            pltpu.VMEM((n_act, D_MODEL, SGU_ACT_CHUNK), BF16),
            pltpu.VMEM((SGU_WIDTH, D_MODEL), BF16),
            *COL_STAGE, *ROW_STAGE,
            pltpu.VMEM((tm, SGU_WIDTH), BF16),
            pltpu.VMEM((n_act, tm, SGU_ACT_CHUNK), F32),
        ],
        compiler_params=_params("arbitrary"),
        name="sgu",
    )(h, mod, g, w_in, v_g, w_s, b_s_t, w_out)


def _rope_tables(seq):
    half = HEAD_DIM // 2
    t = jnp.arange(seq)
    freqs = ROPE_BASE ** (-jnp.arange(0, half, 2, dtype=F32) / half)
    def tab(pos):
        ang = pos.astype(F32)[:, None] * freqs[None, :]
        cos, sin = jnp.cos(ang), jnp.sin(ang)
        return jnp.concatenate([cos, cos], -1), jnp.concatenate([-sin, sin], -1)
    cr, sr = tab(t // GRID_W)
    cc, sc = tab(t % GRID_W)
    cos = jnp.concatenate([cr, cc], -1)
    sin = jnp.concatenate([sr, sc], -1)
    return jnp.tile(cos, (1, 2)), jnp.tile(sin, (1, 2))


def kernel(x_prompt, x_sample, cache_na_k, cache_na_v, cache_gqa_k, cache_gqa_v, c, c_ctx, norm_g, mod_w, mod_b, ffn1_w_gu, ffn1_w_down, ffn2_w_gu, ffn2_w_down, attn_w_in, attn_w_out, na_q_g, na_k_g, na_rpb, gqa_q_g, gqa_k_g, sgu_w_in, sgu_v_g, sgu_w_s, sgu_b_s, sgu_w_out):
    batch, seq, _ = x_prompt.shape
    dec_batch, dec_seq, _ = x_sample.shape
    past = cache_na_k.shape[2]
    ctx_rows, lat_rows = batch * seq, dec_batch * dec_seq
    h = (x_prompt.reshape(ctx_rows, D_MODEL), x_sample.reshape(lat_rows, D_MODEL))

    cond8 = jnp.zeros((8, D_MODEL), F32).at[0].set(c_ctx).at[1:1 + dec_batch].set(c)
    mod_all = _modulation(cond8, mod_w, mod_b).reshape(DEPTH, 8, N_MOD, D_MODEL)

    new_cache = None
    for layer in range(DEPTH):
        mod_l = mod_all[layer]
        mods = (mod_l[0:1], mod_l[1:1 + dec_batch])
        g0, g1, g2 = (norm_g[layer, s][None, :] for s in range(3))

        mod3 = mod_l[0:1 + dec_batch]
        h = _ffn(h, mod3, g0, ffn1_w_gu, ffn1_w_down, layer, 0, ctx_rows, dec_seq)

        if layer % 2 == 0:
            e = layer // 2
            gains = (jnp.tile(na_q_g[e], NA_HEADS)[None], jnp.tile(na_k_g[e], NA_HEADS)[None],
                     jnp.tile(gqa_q_g[e], GQA_HEADS)[None], jnp.tile(gqa_k_g[e], GQA_KV_HEADS)[None])
            q, kt, v, *new_cache = _qkv(h, 0, ctx_rows, mods[0], g1, attn_w_in, e, gains, ctx_rows // PROJ_TM,
                                        seq, cache_out=True)
            o_ctx = _ctx_attention(q, kt, v, seq)
            q, kt, v = _qkv(h, ctx_rows, lat_rows, mods[1], g1, attn_w_in, e, gains, dec_seq // PROJ_TM,
                            dec_seq, rope_tabs=_rope_tables(dec_seq))
            to_kt = lambda ck, rep: jnp.repeat(jnp.transpose(ck[:, e], (0, 2, 3, 1)), rep, axis=1).reshape(
                dec_batch, -1, past).astype(BF16)
            to_v = lambda cv, rep: jnp.repeat(cv[:, e], rep, axis=2).reshape(dec_batch, past, -1).astype(BF16)
            oa = _na_attention(q, kt, v, to_kt(cache_na_k, 1), to_v(cache_na_v, 1),
                               _na_bias(na_rpb[e], dec_seq // GRID_W), dec_batch, dec_seq)
            ob = _gqa_attention(q, kt, v, to_kt(cache_gqa_k, 2), to_v(cache_gqa_v, 2), dec_batch, dec_seq)
            attn = (o_ctx, oa, ob, attn_w_out, e)
        else:
            attn = None
            o = layer // 2
            h = _sgu(h, mod3, g1, sgu_w_in, sgu_v_g[o][None], sgu_w_s, sgu_b_s[o].T, sgu_w_out,
                     o, ctx_rows, dec_seq)

        h = _ffn(h, mod3, g2, ffn2_w_gu, ffn2_w_down, layer, 2, ctx_rows, dec_seq, attn=attn,
                 split_out=layer == DEPTH - 1)
    hp, hs = h

    n_attn = (DEPTH + 1) // 2
    assert n_attn == 1
    from_t = lambda x: jnp.transpose(x.reshape(batch, n_attn, -1, HEAD_DIM, seq), (0, 1, 4, 2, 3))
    return (hp.reshape(batch, seq, D_MODEL), hs.reshape(dec_batch, dec_seq, D_MODEL),
            *(from_t(x) for x in new_cache))
```

```python
import functools
import math

import jax
import jax.numpy as jnp
import numpy as np
from jax import lax
from jax.experimental import pallas as pl
from jax.experimental.pallas import tpu as pltpu

D_MODEL = 1024
DEPTH = 2
GRID_W = 64
HEAD_DIM = 64
NA_HEADS = 8
NA_WIN_R = 8
NA_WIN_C = 16
GQA_HEADS = 8
GQA_KV_HEADS = 2
GQA_GROUP = GQA_HEADS // GQA_KV_HEADS
NA_W = NA_HEADS * HEAD_DIM
GQ_W = GQA_HEADS * HEAD_DIM
GKV_W = GQA_KV_HEADS * HEAD_DIM
QKV_WIDTH = 3 * NA_W + GQ_W + 2 * GKV_W
ROPE_BASE = 10000.0
SGU_CHUNK = 128
SGU_GROUPS = 8
SGU_WIDTH = 2 * D_MODEL
SGU_GW = SGU_WIDTH // SGU_GROUPS
FFN_HIDDEN = 2816
N_MOD = 9
EPS = 1e-6
NEG_INF = -1e30
Q_SCALE = HEAD_DIM ** -0.5

NA_BAND_ROWS = 4
NA_BAND_KEY_ROWS = 12
NA_BAND_Q = NA_BAND_ROWS * GRID_W
NA_BAND_K = NA_BAND_KEY_ROWS * GRID_W

FFN_TM = 512
FFN_ACT_CHUNK = 512
SGU_ACT_CHUNK = 512
SGU_TM = 512
CTX_SEQ_PER_STEP = 4
PROJ_TM = 512
W_STAGE = 256

LANES = 128
KT_ROWS = NA_W + 2 * GKV_W
MXU_COLS = 256
VMEM_LIMIT = 56 * 1024 * 1024

BF16 = jnp.bfloat16
F32 = jnp.float32


def _dot(a, b):
    return jnp.dot(a, b, preferred_element_type=F32)


def _params(*sem):
    return pltpu.CompilerParams(dimension_semantics=sem, vmem_limit_bytes=VMEM_LIMIT)


def _adaln(x, g, mod_ref, j):
    shift = mod_ref[0, 3 * j:3 * j + 1, :]
    scale = mod_ref[0, 3 * j + 1:3 * j + 2, :]
    y = x * lax.rsqrt(jnp.mean(x * x, axis=-1, keepdims=True) + EPS)
    return (y * g) * (1.0 + scale) + shift


def _gate(mod_ref, j):
    return mod_ref[0, 3 * j + 2:3 * j + 3, :]


def _mod_kernel(c_ref, w_ref, b_ref, o_ref):
    c = c_ref[...]
    a = (c * jax.nn.sigmoid(c)).astype(BF16)
    o_ref[...] = _dot(a, w_ref[...].astype(BF16)) + b_ref[...]


def _modulation(cond8, mod_w, mod_b):
    tn = 1024
    n = N_MOD * D_MODEL
    return pl.pallas_call(
        _mod_kernel,
        grid=(DEPTH, n // tn),
        in_specs=[
            pl.BlockSpec((8, D_MODEL), lambda l, k: (0, 0)),
            pl.BlockSpec((None, D_MODEL, tn), lambda l, k: (l, 0, k)),
            pl.BlockSpec((None, 1, tn), lambda l, k: (l, 0, k)),
        ],
        out_specs=pl.BlockSpec((None, 8, tn), lambda l, k: (l, 0, k)),
        out_shape=jax.ShapeDtypeStruct((DEPTH, 8, n), F32),
        compiler_params=_params("arbitrary", "arbitrary"),
        name="modulation",
    )(cond8, mod_w, mod_b.reshape(DEPTH, 1, n))


def _two_stream_maps(tm, n_ctx_rows, dec_seq):
    n_ctx = n_ctx_rows // tm
    per_seq = dec_seq // tm
    ctx_map = lambda i: (jnp.minimum(i, n_ctx - 1), 0)
    lat_map = lambda i: (jnp.maximum(i - n_ctx, 0), 0)
    cond = lambda i: jnp.where(i < n_ctx, 0, 1 + (i - n_ctx) // per_seq)
    return n_ctx, ctx_map, lat_map, cond


def _norm_spec(layer, j):
    return pl.BlockSpec((None, 1, D_MODEL), lambda i: (3 * layer + j, 0, 0))


def _mod_spec(layer, cond):
    return pl.BlockSpec((None, 1, N_MOD, D_MODEL), lambda i: (layer, cond(i), 0, 0))


class _WeightStream:
    def __init__(self, pairs, stage_ref, sem_ref):
        self.pairs, self.stage, self.sem, self.done = pairs, stage_ref, sem_ref, 0
        for k in range(min(2, len(pairs))):
            self._copy(k).start()

    def _copy(self, k):
        return pltpu.make_async_copy(self.pairs[k][0], self.stage.at[k % 2], self.sem.at[k % 2])

    def take(self, n=None):
        n = len(self.pairs) - self.done if n is None else min(n, len(self.pairs) - self.done)
        for _ in range(n):
            k = self.done
            self._copy(k).wait()
            dst, idx = self.pairs[k][1]
            dst[idx] = self.stage[k % 2].astype(BF16)
            if k + 2 < len(self.pairs):
                self._copy(k + 2).start()
            self.done += 1


def _col_pairs(w_hbm, layer, w_ref, cols=None):
    if len(w_ref.shape) == 2:
        dst = lambda c: (slice(None), slice(c, c + W_STAGE))
        n = w_ref.shape[1]
    else:
        width = w_ref.shape[2]
        dst = lambda c: (c // width, slice(None), slice(c % width, c % width + W_STAGE))
        n = w_ref.shape[0] * width
    cols = range(0, n, W_STAGE) if cols is None else cols
    return [(w_hbm.at[layer, :, pl.ds(c, W_STAGE)], (w_ref, dst(c))) for c in cols]


def _row_pairs(w_hbm, layer, w_ref):
    return [(w_hbm.at[layer, pl.ds(r, W_STAGE), :], (w_ref, (slice(r, r + W_STAGE), slice(None))))
            for r in range(0, w_ref.shape[0], W_STAGE)]


def _load_by_cols(w_hbm, layer, w_ref, stage_ref, sem_ref):
    _WeightStream(_col_pairs(w_hbm, layer, w_ref), stage_ref, sem_ref).take()


def _load_by_rows(w_hbm, layer, w_ref, stage_ref, sem_ref):
    _WeightStream(_row_pairs(w_hbm, layer, w_ref), stage_ref, sem_ref).take()


COL_STAGE = (pltpu.VMEM((2, D_MODEL, W_STAGE), F32), pltpu.SemaphoreType.DMA((2,)))
ROW_STAGE = (pltpu.VMEM((2, W_STAGE, D_MODEL), F32), pltpu.SemaphoreType.DMA((2,)))
HBM = pl.BlockSpec(memory_space=pl.ANY)


def _ffn_kernel(*refs, layer, j, n_ctx, proj_layer, split_in, split_out):
    h_refs = refs[:2 if split_in else 1]
    pos = len(h_refs)
    mod_ref, g_ref, wgu_hbm, wd_hbm = refs[pos:pos + 4]
    pos += 4
    if proj_layer is not None:
        octx_ref, oa_ref, ob_ref, wo_hbm = refs[pos:pos + 4]
        pos += 4
    out_refs = refs[pos:pos + (2 if split_out else 1)]
    pos += len(out_refs)
    wgu_ref, wd_ref, stage_gu, sem_gu, stage_d, sem_d, act_ref = refs[pos:pos + 7]
    wo_ref = refs[pos + 7] if proj_layer is not None else None
    i = pl.program_id(0)
    is_ctx = i < n_ctx
    chunks = [(c, min(FFN_ACT_CHUNK, FFN_HIDDEN - c)) for c in range(0, FFN_HIDDEN, FFN_ACT_CHUNK)]

    def tile(first):
        if first:
            need = [col for c, w in chunks for base in (c, FFN_HIDDEN + c) for col in range(base, base + w, W_STAGE)]
            cols = _WeightStream(_col_pairs(wgu_hbm, layer, wgu_ref, need), stage_gu, sem_gu)
            proj = _row_pairs(wo_hbm, proj_layer, wo_ref) if proj_layer is not None else []
            rows = _WeightStream(proj + _row_pairs(wd_hbm, layer, wd_ref), stage_d, sem_d)
            rows.take(len(proj))
        if first or not split_in:
            h = h_refs[0][...]
        else:
            h = jnp.where(is_ctx, h_refs[0][...], h_refs[1][...])
        if proj_layer is not None:
            latent_heads = jnp.concatenate([oa_ref[...], ob_ref[...]], axis=1)
            heads = octx_ref[...] if first else jnp.where(is_ctx, octx_ref[...], latent_heads)
            h = h + _gate(mod_ref, 1) * _dot(heads, wo_ref[...])
        xn = _adaln(h, g_ref[...], mod_ref, j).astype(BF16)
        for c, w in chunks:
            if first:
                cols.take(2 * w // W_STAGE)
            gate = _dot(xn, wgu_ref[:, c:c + w])
            up = _dot(xn, wgu_ref[:, FFN_HIDDEN + c:FFN_HIDDEN + c + w])
            act_ref[:, c:c + w] = ((gate * jax.nn.sigmoid(gate)) * up).astype(BF16)
            if first:
                rows.take(2)
        if first:
            rows.take()
        res = h + (0.5 * _gate(mod_ref, j)) * _dot(act_ref[...], wd_ref[...])
        if first or not split_out:
            out_refs[0][...] = res
        else:
            @pl.when(is_ctx)
            def _():
                out_refs[0][...] = res

            @pl.when(jnp.logical_not(is_ctx))
            def _():
                out_refs[1][...] = res

    pl.when(i == 0)(lambda: tile(True))
    pl.when(i > 0)(lambda: tile(False))


def _ffn(h, mod, g, w_gu, w_down, layer, j, ctx_rows, dec_seq, attn=None, split_out=False):
    tm = FFN_TM
    split_in = isinstance(h, tuple)
    h_arrays = list(h) if split_in else [h]
    total_rows = sum(a.shape[0] for a in h_arrays)
    n_ctx, ctx_map, lat_map, cond = _two_stream_maps(tm, ctx_rows, dec_seq)
    assert n_ctx >= 1
    row_map = lambda i: (i, 0)
    row_spec = lambda m, width=D_MODEL: pl.BlockSpec((tm, width), m)
    in_specs = ([row_spec(ctx_map), row_spec(lat_map)] if split_in else [row_spec(row_map)]) + [
        _mod_spec(layer, cond),
        _norm_spec(layer, j),
        HBM, HBM]
    args = [*h_arrays, mod, g, w_gu, w_down]
    scratch = [pltpu.VMEM((D_MODEL, 2 * FFN_HIDDEN), BF16), pltpu.VMEM((FFN_HIDDEN, D_MODEL), BF16),
               *COL_STAGE, *ROW_STAGE, pltpu.VMEM((tm, FFN_HIDDEN), BF16)]
    proj_layer = None
    if attn is not None:
        o_ctx, oa, ob, w_out, proj_layer = attn
        in_specs += [row_spec(ctx_map, o_ctx.shape[1]), row_spec(lat_map, oa.shape[1]),
                     row_spec(lat_map, ob.shape[1]), HBM]
        args += [o_ctx, oa, ob, w_out]
        scratch.append(pltpu.VMEM((o_ctx.shape[1], D_MODEL), BF16))
    if split_out:
        out_specs = [row_spec(ctx_map), row_spec(lat_map)]
        out_shape = [jax.ShapeDtypeStruct((ctx_rows, D_MODEL), F32),
                     jax.ShapeDtypeStruct((total_rows - ctx_rows, D_MODEL), F32)]
    else:
        out_specs = row_spec(row_map)
        out_shape = jax.ShapeDtypeStruct((total_rows, D_MODEL), F32)
    return pl.pallas_call(
        functools.partial(_ffn_kernel, layer=layer, j=j, n_ctx=n_ctx, proj_layer=proj_layer,
                          split_in=split_in, split_out=split_out),
        grid=(total_rows // tm,),
        in_specs=in_specs,
        out_specs=out_specs,
        out_shape=out_shape,
        scratch_shapes=scratch,
        compiler_params=_params("arbitrary"),
        name=f"ffn{j}_l{layer}",
    )(*args)


def _head_mean_sq(x):
    w = x.shape[-1]
    r = lax.broadcasted_iota(jnp.int32, (MXU_COLS, MXU_COLS), 0) // HEAD_DIM
    c = lax.broadcasted_iota(jnp.int32, (MXU_COLS, MXU_COLS), 1) // HEAD_DIM
    bd = jnp.where(r == c, 1.0 / HEAD_DIM, 0.0).astype(BF16)
    sq = (x * x).astype(BF16)
    parts = []
    for s in range(0, w, MXU_COLS):
        e = min(s + MXU_COLS, w)
        parts.append(_dot(sq[:, s:e], bd[:e - s, :e - s]))
    return parts[0] if len(parts) == 1 else jnp.concatenate(parts, axis=-1)


def _low_half():
    return lax.broadcasted_iota(jnp.int32, (1, LANES), 1) < HEAD_DIM


def _head_rms(x, gain):
    return (x * lax.rsqrt(_head_mean_sq(x) + EPS)) * gain


def _rope(x, cos, sin_signed):
    lanes = cos.shape[-1]
    outs = []
    for s in range(0, x.shape[-1], lanes):
        xs = x[:, s:s + lanes]
        blk = lax.broadcasted_iota(jnp.int32, xs.shape, 1) // (HEAD_DIM // 4)
        partner = jnp.where(blk % 2 == 0,
                            pltpu.roll(xs, lanes - HEAD_DIM // 4, 1),
                            pltpu.roll(xs, HEAD_DIM // 4, 1))
        outs.append(xs * cos + partner * sin_signed)
    return outs[0] if len(outs) == 1 else jnp.concatenate(outs, axis=-1)


def _qkv_kernel(*refs, layer, rope, cache_out):
    h_ref, mod_ref, g_ref, w_hbm, gain_ref = refs[:5]
    w_ref, stage_ref, sem_ref = refs[-3:]
    pos = 5
    if rope:
        cos_ref, sin_ref = refs[pos:pos + 2]
        pos += 2
    q_ref, kt_ref, v_ref = refs[pos:pos + 3]
    pos += 3

    @pl.when(pl.program_id(0) == 0)
    def _load_weights():
        _load_by_cols(w_hbm, layer, w_ref, stage_ref, sem_ref)

    xn = _adaln(h_ref[...], g_ref[...], mod_ref, 1).astype(BF16)
    y = _dot(xn, w_ref[...])
    o = 0
    qa = _head_rms(y[:, o:o + NA_W], gain_ref[:, :NA_W]); o += NA_W
    ka = _head_rms(y[:, o:o + NA_W], gain_ref[:, NA_W:2 * NA_W]); o += NA_W
    va = y[:, o:o + NA_W]; o += NA_W
    qb = _head_rms(y[:, o:o + GQ_W], gain_ref[:, 2 * NA_W:2 * NA_W + GQ_W]); o += GQ_W
    kb = _head_rms(y[:, o:o + GKV_W], gain_ref[:, 2 * NA_W + GQ_W:]); o += GKV_W
    vb = y[:, o:o + GKV_W]
    if rope:
        cos, sin = cos_ref[...], sin_ref[...]
        qb = _rope(qb, cos, sin)
        kb = _rope(kb, cos, sin)
    q_ref[:, :NA_W] = (qa * Q_SCALE).astype(BF16)
    q_ref[:, NA_W:] = (qb * Q_SCALE).astype(BF16)
    low = _low_half()
    vb_swapped = pltpu.roll(vb, HEAD_DIM, 1)
    v_ref[:, :NA_W] = va.astype(BF16)
    v_ref[:, NA_W:NA_W + LANES] = jnp.where(low, vb, vb_swapped).astype(BF16)
    v_ref[:, NA_W + LANES:] = jnp.where(low, vb_swapped, vb).astype(BF16)
    n_seq, _, s = kt_ref.shape
    for i in range(n_seq):
        rows = slice(i * s, (i + 1) * s)
        ka_t = ka[rows].T
        kb_t = kb[rows].T
        k0, k1 = kb_t[:HEAD_DIM], kb_t[HEAD_DIM:]
        kt_ref[i, :NA_W, :] = ka_t.astype(BF16)
        kt_ref[i, NA_W:, :] = jnp.concatenate([k0, k0, k1, k1], axis=0).astype(BF16)
        if cache_out:
            kat_ref, vat_ref, kbt_ref, vbt_ref = refs[pos:pos + 4]
            kat_ref[i] = ka_t
            vat_ref[i] = va[rows].T
            kbt_ref[i] = kb_t
            vbt_ref[i] = vb[rows].T


def _qkv(h, row0, t, mod, mod_layer, cond0, g, w_in, layer, gains, cond_div, seq, rope_tabs=None,
         cache_out=False):
    tm = PROJ_TM
    rope = rope_tabs is not None
    assert not (rope and cache_out)
    row = lambda i: (i, 0)
    const = lambda i: (0, 0)
    in_specs = [
        pl.BlockSpec((tm, D_MODEL), lambda i: (i + row0 // tm, 0)),
        _mod_spec(mod_layer, lambda i: cond0 + i // cond_div),
        _norm_spec(mod_layer, 1),
        HBM,
        pl.BlockSpec((1, 2 * NA_W + GQ_W + GKV_W), const),
    ]
    args = [h, mod, g, w_in, gains]
    if rope:
        seq_tiles = rope_tabs[0].shape[0] // tm
        in_specs += [pl.BlockSpec((tm, 2 * HEAD_DIM), lambda i: (i % seq_tiles, 0))] * 2
        args += list(rope_tabs)
    if seq >= tm:
        per_seq = seq // tm
        t_block = lambda rows: pl.BlockSpec((1, rows, tm), lambda i: (i // per_seq, 0, i % per_seq))
    else:
        t_block = lambda rows: pl.BlockSpec((tm // seq, rows, seq), lambda i: (i, 0, 0))
    t_shape = lambda rows, dtype: jax.ShapeDtypeStruct((t // seq, rows, seq), dtype)
    out_specs = [pl.BlockSpec((tm, NA_W + GQ_W), row), t_block(KT_ROWS), pl.BlockSpec((tm, KT_ROWS), row)]
    out_shape = [jax.ShapeDtypeStruct((t, NA_W + GQ_W), BF16), t_shape(KT_ROWS, BF16),
                 jax.ShapeDtypeStruct((t, KT_ROWS), BF16)]
    if cache_out:
        out_specs += [t_block(NA_W), t_block(NA_W), t_block(GKV_W), t_block(GKV_W)]
        out_shape += [t_shape(NA_W, F32), t_shape(NA_W, F32), t_shape(GKV_W, F32), t_shape(GKV_W, F32)]
    return pl.pallas_call(
        functools.partial(_qkv_kernel, layer=layer, rope=rope, cache_out=cache_out),
        grid=(t // tm,),
        in_specs=in_specs,
        out_specs=out_specs,
        out_shape=out_shape,
        scratch_shapes=[pltpu.VMEM((D_MODEL, QKV_WIDTH), BF16), *COL_STAGE],
        compiler_params=_params("arbitrary"),
        name="qkv_latent" if rope else "qkv_context",
    )(*args)


def _split_pair(q2):
    low = _low_half()
    zero = jnp.zeros((), q2.dtype)
    return jnp.where(low, q2, zero), jnp.where(low, zero, q2)


def _with_ones(v2):
    return jnp.concatenate([v2, jnp.ones(v2.shape, v2.dtype)], axis=1)


def _join(blocks, axis):
    return blocks[0] if len(blocks) == 1 else jnp.concatenate(blocks, axis=axis)


def _attend(problems):
    scores = [_dot(qm, _join(kts, 1)) for qm, kts, _, _ in problems]
    probs = []
    for s, (_, kts, biases, _) in zip(scores, problems):
        parts, start = [], 0
        for kt, b in zip(kts, biases):
            part = s[:, start:start + kt.shape[1]]
            parts.append(part if b is None else part + b)
            start += kt.shape[1]
        m = functools.reduce(jnp.maximum, [jnp.max(part, axis=-1, keepdims=True) for part in parts])
        probs.append(_join([jnp.exp(part - m).astype(BF16) for part in parts], 1))
    return [_dot(p, _join(vexts, 0)) for p, (_, _, _, vexts) in zip(probs, problems)]


def _merge_pair(oe_even, oe_odd):
    even = oe_even[:, :LANES] * (1.0 / oe_even[:, LANES:])
    odd = oe_odd[:, :LANES] * (1.0 / oe_odd[:, LANES:])
    return jnp.where(_low_half(), even, odd)


def _gqa_groups(q_ref, q_off, o_ref, o_off, keys_values):
    tq = q_ref.shape[0]
    slabs = lambda g, off: [slice(off + (2 * g + j) * LANES, off + (2 * g + j + 1) * LANES) for j in range(2)]
    problems = []
    for g in range(GQA_KV_HEADS):
        qm = jnp.concatenate([part for sl in slabs(g, q_off) for part in _split_pair(q_ref[:, sl])], axis=0)
        problems.append((qm, *keys_values(slice(g * LANES, (g + 1) * LANES))))
    for g, oe in enumerate(_attend(problems)):
        for j, sl in enumerate(slabs(g, o_off)):
            o_ref[:, sl] = _merge_pair(oe[2 * j * tq:(2 * j + 1) * tq],
                                       oe[(2 * j + 1) * tq:(2 * j + 2) * tq]).astype(o_ref.dtype)


def _ctx_attn_kernel(q_blk, kt_blk, v_blk, o_blk):
    n_seq, _, seq = kt_blk.shape
    for b in range(n_seq):
        rows = pl.ds(b * seq, seq)
        q_ref, v_ref, o_ref, kt_ref = q_blk.at[rows], v_blk.at[rows], o_blk.at[rows], kt_blk.at[b]
        for p in range(NA_HEADS // 2):
            sl = slice(p * LANES, (p + 1) * LANES)
            vext = _with_ones(v_ref[:, sl])
            oe = [_attend([(qm, [kt_ref[sl, :]], [None], [vext])])[0] for qm in _split_pair(q_ref[:, sl])]
            o_ref[:, sl] = _merge_pair(*oe).astype(o_ref.dtype)
        _gqa_groups(q_ref, NA_W, o_ref, NA_W,
                    lambda kv: ([kt_ref[NA_W + kv.start:NA_W + kv.stop, :]], [None],
                                [_with_ones(v_ref[:, NA_W + kv.start:NA_W + kv.stop])]))


def _ctx_attention(q, kt, v, seq):
    t = q.shape[0]
    n_seq = CTX_SEQ_PER_STEP
    row = lambda b: (b, 0)
    return pl.pallas_call(
        _ctx_attn_kernel,
        grid=(t // (n_seq * seq),),
        in_specs=[pl.BlockSpec((n_seq * seq, NA_W + GQ_W), row),
                  pl.BlockSpec((n_seq, KT_ROWS, seq), lambda b: (b, 0, 0)),
                  pl.BlockSpec((n_seq * seq, KT_ROWS), row)],
        out_specs=pl.BlockSpec((n_seq * seq, NA_W + GQ_W), row),
        out_shape=jax.ShapeDtypeStruct((t, NA_W + GQ_W), BF16),
        compiler_params=_params("arbitrary"),
        name="ctx_attention",
    )(q, kt, v)


def _na_kernel(q_ref, kt0_ref, kt1_ref, kt2_ref, v0_ref, v1_ref, v2_ref, ckt_ref, cv_ref, bias_ref, o_ref):
    for p in range(NA_HEADS // 2):
        sl = slice(p * LANES, (p + 1) * LANES)
        kt_loc = jnp.concatenate([kt0_ref[sl, :], kt1_ref[sl, :], kt2_ref[sl, :]], axis=1)
        v_loc = _with_ones(jnp.concatenate([v0_ref[:, sl], v1_ref[:, sl], v2_ref[:, sl]], axis=0))
        v_ctx = _with_ones(cv_ref[:, sl])
        oe = _attend([(qm, [kt_loc, ckt_ref[sl, :]], [bias_ref[2 * p + half], None], [v_loc, v_ctx])
                      for half, qm in enumerate(_split_pair(q_ref[:, sl]))])
        o_ref[:, sl] = _merge_pair(*oe).astype(o_ref.dtype)


def _na_bias(rpb, rows):
    qc = np.arange(GRID_W)
    kc = np.arange(GRID_W)
    ws = np.clip(qc - NA_WIN_C // 2, 0, GRID_W - NA_WIN_C)
    col_ok = (kc[None, :] >= ws[:, None]) & (kc[None, :] < ws[:, None] + NA_WIN_C)
    dc = np.clip(kc[None, :] - qc[:, None] + NA_WIN_C - 1, 0, 2 * NA_WIN_C - 2)
    pick = ((dc[:, :, None] == np.arange(2 * NA_WIN_C - 1)) & col_ok[:, :, None]).astype(np.float32)
    tiles = jnp.einsum("hrd,qkd->hrqk", rpb.astype(F32), pick, precision=lax.Precision.HIGHEST)
    tiles = jnp.where(col_ok, tiles, NEG_INF)
    tiles = jnp.concatenate([tiles, tiles], axis=-1)
    n_off = 2 * NA_WIN_R - 1
    return pl.pallas_call(
        functools.partial(_na_bias_kernel, rows=rows),
        grid=(NA_HEADS,),
        in_specs=[pl.BlockSpec((None, n_off, GRID_W, LANES), lambda h: (h, 0, 0, 0))],
        out_specs=pl.BlockSpec((3, None, NA_BAND_Q, NA_BAND_K), lambda h: (0, h, 0, 0)),
        out_shape=jax.ShapeDtypeStruct((3, NA_HEADS, NA_BAND_Q, NA_BAND_K), F32),
        compiler_params=_params("arbitrary"),
        name="na_bias",
    )(tiles)


def _na_bias_kernel(tiles_ref, o_ref, *, rows):
    low = _low_half()
    masked = jnp.full((GRID_W, LANES), NEG_INF, F32)
    nb = rows // NA_BAND_ROWS
    for t, band in enumerate((0, 1, nb - 1)):
        r0 = band * NA_BAND_ROWS
        k0 = int(np.clip(r0 - NA_WIN_R // 2, 0, rows - NA_BAND_KEY_ROWS))
        for ri in range(NA_BAND_ROWS):
            r = r0 + ri
            start = int(np.clip(r - NA_WIN_R // 2, 0, rows - NA_WIN_R))
            tile = lambda kr: tiles_ref[kr - r + NA_WIN_R - 1] if start <= kr < start + NA_WIN_R else masked
            for m in range(NA_BAND_KEY_ROWS // 2):
                kr = k0 + 2 * m
                o_ref[t, ri * GRID_W:(ri + 1) * GRID_W, m * LANES:(m + 1) * LANES] = jnp.where(
                    low, tile(kr), tile(kr + 1))


def _na_attention(q, kt, v, ctx_kt, ctx_v, bias, batch, seq):
    rows = seq // GRID_W
    nb = rows // NA_BAND_ROWS
    past = ctx_v.shape[1]
    assert (rows - NA_BAND_KEY_ROWS) % NA_BAND_ROWS == 0
    n_chunks = NA_BAND_KEY_ROWS // NA_BAND_ROWS
    band_type = lambda j: jnp.where(j == 0, 0, jnp.where(j == nb - 1, 2, 1))
    chunk0 = lambda j: jnp.clip(j - NA_WIN_R // 2 // NA_BAND_ROWS, 0, nb - n_chunks)
    kt_spec = lambda c: pl.BlockSpec((None, NA_W, NA_BAND_Q), lambda b, j: (b, 0, chunk0(j) + c))
    v_spec = lambda c: pl.BlockSpec((NA_BAND_Q, NA_W), lambda b, j: (b * nb + chunk0(j) + c, 0))
    return pl.pallas_call(
        _na_kernel,
        grid=(batch, nb),
        in_specs=[pl.BlockSpec((NA_BAND_Q, NA_W), lambda b, j: (b * nb + j, 0))]
                 + [kt_spec(c) for c in range(n_chunks)] + [v_spec(c) for c in range(n_chunks)]
                 + [pl.BlockSpec((None, NA_W, past), lambda b, j: (b, 0, 0)),
                    pl.BlockSpec((None, past, NA_W), lambda b, j: (b, 0, 0)),
                    pl.BlockSpec((None, NA_HEADS, NA_BAND_Q, NA_BAND_K), lambda b, j: (band_type(j), 0, 0, 0))],
        out_specs=pl.BlockSpec((NA_BAND_Q, NA_W), lambda b, j: (b * nb + j, 0)),
        out_shape=jax.ShapeDtypeStruct((batch * seq, NA_W), BF16),
        compiler_params=_params("arbitrary", "arbitrary"),
        name="na_attention",
    )(q, *([kt] * n_chunks), *([v] * n_chunks), ctx_kt, ctx_v, bias)


def _gqa_kernel(q_ref, kt_ref, v_ref, ckt_ref, cv_ref, o_ref):
    _gqa_groups(q_ref, 0, o_ref, 0,
                lambda kv: ([kt_ref[kv, :], ckt_ref[kv, :]], [None, None],
                            [_with_ones(v_ref[:, kv]), _with_ones(cv_ref[:, kv])]))


def _gqa_attention(q, kt, v, ctx_kt, ctx_v, batch, seq):
    tq = 128
    nt = seq // tq
    past = ctx_v.shape[1]
    dup = 2 * GKV_W
    return pl.pallas_call(
        _gqa_kernel,
        grid=(batch, nt),
        in_specs=[
            pl.BlockSpec((tq, GQ_W), lambda b, i: (b * nt + i, 1)),
            pl.BlockSpec((None, dup, seq), lambda b, i: (b, NA_W // dup, 0)),
            pl.BlockSpec((seq, dup), lambda b, i: (b, NA_W // dup)),
            pl.BlockSpec((None, dup, past), lambda b, i: (b, 0, 0)),
            pl.BlockSpec((None, past, dup), lambda b, i: (b, 0, 0)),
        ],
        out_specs=pl.BlockSpec((tq, GQ_W), lambda b, i: (b * nt + i, 0)),
        out_shape=jax.ShapeDtypeStruct((batch * seq, GQ_W), BF16),
        compiler_params=_params("arbitrary", "arbitrary"),
        name="gqa_attention",
    )(q, kt, v, ctx_kt, ctx_v)


def _sgu_kernel(h_ref, mod_ref, g_ref, win_hbm, vg_ref, ws_ref, bs_ref, wout_hbm, o_ref,
                win_ref, wout_ref, stage_in, sem_in, stage_out, sem_out, gated_ref, y_ref, *, layer):
    tm = h_ref.shape[0]
    n_chunks = y_ref.shape[0]
    per_chunk = SGU_ACT_CHUNK // SGU_GW

    def tile(first):
        if first:
            cols = _WeightStream(_col_pairs(win_hbm, layer, win_ref), stage_in, sem_in)
            rows = _WeightStream(_row_pairs(wout_hbm, layer, wout_ref), stage_out, sem_out)
        h = h_ref[...]
        xn = _adaln(h, g_ref[...], mod_ref, 1).astype(BF16)
        ssq = jnp.zeros((tm, 1), F32)
        for c in range(n_chunks):
            if first:
                cols.take(SGU_ACT_CHUNK // W_STAGE)
            y = _dot(xn, win_ref[c])
            y = 0.5 * y * (1.0 + lax.erf(y * math.sqrt(0.5)))
            y_ref[c] = y
            if c >= n_chunks // 2:
                ssq = ssq + jnp.sum(y * y, axis=-1, keepdims=True)
            if first:
                rows.take(1)
        if first:
            rows.take()
        v_scale = lax.rsqrt(ssq * (1.0 / SGU_WIDTH) + EPS)
        for c in range(tm // SGU_CHUNK):
            rs = slice(c * SGU_CHUNK, (c + 1) * SGU_CHUNK)
            for g in range(SGU_GROUPS):
                cs = slice(g * SGU_GW, (g + 1) * SGU_GW)
                in_chunk = slice((g % per_chunk) * SGU_GW, (g % per_chunk + 1) * SGU_GW)
                u = y_ref[g // per_chunk, rs, in_chunk]
                v = y_ref[n_chunks // 2 + g // per_chunk, rs, in_chunk]
                vn = ((v * v_scale[rs]) * vg_ref[:, cs]).astype(BF16)
                sv = _dot(ws_ref[g].astype(BF16), vn) + bs_ref[:, g:g + 1]
                gated_ref[rs, cs] = (u * sv).astype(BF16)
        o_ref[...] = h + _gate(mod_ref, 1) * _dot(gated_ref[...], wout_ref[...])

    pl.when(pl.program_id(0) == 0)(lambda: tile(True))
    pl.when(pl.program_id(0) > 0)(lambda: tile(False))


def _sgu(h, mod, mod_layer, g, w_in, v_g, w_s, b_s_t, w_out, layer, ctx_rows, dec_seq):
    tm = SGU_TM
    n_act = 2 * SGU_WIDTH // SGU_ACT_CHUNK
    _, _, _, cond = _two_stream_maps(tm, ctx_rows, dec_seq)
    row_spec = pl.BlockSpec((tm, D_MODEL), lambda i: (i, 0))
    const = lambda i: (0, 0)
    return pl.pallas_call(
        functools.partial(_sgu_kernel, layer=layer),
        grid=(h.shape[0] // tm,),
        in_specs=[
            row_spec,
            _mod_spec(mod_layer, cond),
            _norm_spec(mod_layer, 1),
            HBM,
            pl.BlockSpec((1, SGU_WIDTH), const),
            pl.BlockSpec((None, SGU_GROUPS, SGU_CHUNK, SGU_CHUNK), lambda i: (layer, 0, 0, 0)),
            pl.BlockSpec((SGU_CHUNK, SGU_GROUPS), const),
            HBM,
        ],
        out_specs=row_spec,
        out_shape=jax.ShapeDtypeStruct(h.shape, F32),
        scratch_shapes=[
            pltpu.VMEM((n_act, D_MODEL, SGU_ACT_CHUNK), BF16),
            pltpu.VMEM((SGU_WIDTH, D_MODEL), BF16),
            *COL_STAGE, *ROW_STAGE,
            pltpu.VMEM((tm, SGU_WIDTH), BF16),
            pltpu.VMEM((n_act, tm, SGU_ACT_CHUNK), F32),
        ],
        compiler_params=_params("arbitrary"),
        name="sgu",
    )(h, mod, g, w_in, v_g, w_s, b_s_t, w_out)


def _rope_tables(seq):
    half = HEAD_DIM // 2
    t = jnp.arange(seq)
    freqs = ROPE_BASE ** (-jnp.arange(0, half, 2, dtype=F32) / half)
    def tab(pos):
        ang = pos.astype(F32)[:, None] * freqs[None, :]
        cos, sin = jnp.cos(ang), jnp.sin(ang)
        return jnp.concatenate([cos, cos], -1), jnp.concatenate([-sin, sin], -1)
    cr, sr = tab(t // GRID_W)
    cc, sc = tab(t % GRID_W)
    cos = jnp.concatenate([cr, cc], -1)
    sin = jnp.concatenate([sr, sc], -1)
    return jnp.tile(cos, (1, 2)), jnp.tile(sin, (1, 2))


def kernel(x_prompt, x_sample, cache_na_k, cache_na_v, cache_gqa_k, cache_gqa_v, c, c_ctx, norm_g, mod_w, mod_b, ffn1_w_gu, ffn1_w_down, ffn2_w_gu, ffn2_w_down, attn_w_in, attn_w_out, na_q_g, na_k_g, na_rpb, gqa_q_g, gqa_k_g, sgu_w_in, sgu_v_g, sgu_w_s, sgu_b_s, sgu_w_out):
    batch, seq, _ = x_prompt.shape
    dec_batch, dec_seq, _ = x_sample.shape
    past = cache_na_k.shape[2]
    ctx_rows, lat_rows = batch * seq, dec_batch * dec_seq
    h = (x_prompt.reshape(ctx_rows, D_MODEL), x_sample.reshape(lat_rows, D_MODEL))

    assert 1 + dec_batch <= 8
    cond8 = jnp.concatenate([c_ctx[None], c, jnp.zeros((8 - 1 - dec_batch, D_MODEL), F32)], axis=0)
    mod = _modulation(cond8, mod_w, mod_b).reshape(DEPTH, 8, N_MOD, D_MODEL)

    g = norm_g.reshape(DEPTH * 3, 1, D_MODEL)
    new_cache = None
    for layer in range(DEPTH):
        h = _ffn(h, mod, g, ffn1_w_gu, ffn1_w_down, layer, 0, ctx_rows, dec_seq)

        if layer % 2 == 0:
            e = layer // 2
            gains = jnp.concatenate([jnp.tile(na_q_g[e], NA_HEADS), jnp.tile(na_k_g[e], NA_HEADS),
                                     jnp.tile(gqa_q_g[e], GQA_HEADS), jnp.tile(gqa_k_g[e], GQA_KV_HEADS)])[None]
            q, kt, v, *new_cache = _qkv(h, 0, ctx_rows, mod, layer, 0, g, attn_w_in, e, gains,
                                        ctx_rows // PROJ_TM, seq, cache_out=True)
            o_ctx = _ctx_attention(q, kt, v, seq)
            q, kt, v = _qkv(h, ctx_rows, lat_rows, mod, layer, 1, g, attn_w_in, e, gains,
                            dec_seq // PROJ_TM, dec_seq, rope_tabs=_rope_tables(dec_seq))
            to_kt = lambda ck, rep: jnp.repeat(jnp.transpose(ck[:, e], (0, 2, 3, 1)), rep, axis=1).reshape(
                dec_batch, -1, past).astype(BF16)
            to_v = lambda cv, rep: jnp.repeat(cv[:, e], rep, axis=2).reshape(dec_batch, past, -1).astype(BF16)
            oa = _na_attention(q, kt, v, to_kt(cache_na_k, 1), to_v(cache_na_v, 1),
                               _na_bias(na_rpb[e], dec_seq // GRID_W), dec_batch, dec_seq)
            ob = _gqa_attention(q, kt, v, to_kt(cache_gqa_k, 2), to_v(cache_gqa_v, 2), dec_batch, dec_seq)
            attn = (o_ctx, oa, ob, attn_w_out, e)
        else:
            attn = None
            o = layer // 2
            h = _sgu(h, mod, layer, g, sgu_w_in, sgu_v_g[o][None], sgu_w_s, sgu_b_s[o].T, sgu_w_out,
                     o, ctx_rows, dec_seq)

        h = _ffn(h, mod, g, ffn2_w_gu, ffn2_w_down, layer, 2, ctx_rows, dec_seq, attn=attn,
                 split_out=layer == DEPTH - 1)
    hp, hs = h

    n_attn = (DEPTH + 1) // 2
    assert n_attn == 1
    from_t = lambda x: jnp.transpose(x.reshape(batch, n_attn, -1, HEAD_DIM, seq), (0, 1, 4, 2, 3))
    return (hp.reshape(batch, seq, D_MODEL), hs.reshape(dec_batch, dec_seq, D_MODEL),
            *(from_t(x) for x in new_cache))
```

```python
import functools
import math

import jax
import jax.numpy as jnp
import numpy as np
from jax import lax
from jax.experimental import pallas as pl
from jax.experimental.pallas import tpu as pltpu

D_MODEL = 1024
DEPTH = 2
GRID_W = 64
HEAD_DIM = 64
NA_HEADS = 8
NA_WIN_R = 8
NA_WIN_C = 16
GQA_HEADS = 8
GQA_KV_HEADS = 2
GQA_GROUP = GQA_HEADS // GQA_KV_HEADS
NA_W = NA_HEADS * HEAD_DIM
GQ_W = GQA_HEADS * HEAD_DIM
GKV_W = GQA_KV_HEADS * HEAD_DIM
QKV_WIDTH = 3 * NA_W + GQ_W + 2 * GKV_W
ROPE_BASE = 10000.0
SGU_CHUNK = 128
SGU_GROUPS = 8
SGU_WIDTH = 2 * D_MODEL
SGU_GW = SGU_WIDTH // SGU_GROUPS
FFN_HIDDEN = 2816
N_MOD = 9
EPS = 1e-6
NEG_INF = -1e30
Q_SCALE = HEAD_DIM ** -0.5

NA_BAND_ROWS = 4
NA_BAND_KEY_ROWS = 12
NA_BAND_Q = NA_BAND_ROWS * GRID_W
NA_BAND_K = NA_BAND_KEY_ROWS * GRID_W

FFN_TM = 512
FFN_ACT_CHUNK = 512
SGU_ACT_CHUNK = 512
SGU_TM = 512
MOD_TN = 3072
GQA_TQ = 256
CTX_SEQ_PER_STEP = 4
PROJ_TM = 512
W_STAGE = 256

LANES = 128
KT_ROWS = NA_W + 2 * GKV_W
MXU_COLS = 256
VMEM_LIMIT = 56 * 1024 * 1024

BF16 = jnp.bfloat16
F32 = jnp.float32


def _dot(a, b):
    return jnp.dot(a, b, preferred_element_type=F32)


def _params(*sem):
    return pltpu.CompilerParams(dimension_semantics=sem, vmem_limit_bytes=VMEM_LIMIT)


def _adaln(x, g, mod_ref, j):
    shift = mod_ref[0, 3 * j:3 * j + 1, :]
    scale = mod_ref[0, 3 * j + 1:3 * j + 2, :]
    y = x * lax.rsqrt(jnp.mean(x * x, axis=-1, keepdims=True) + EPS)
    return (y * g) * (1.0 + scale) + shift


def _gate(mod_ref, j):
    return mod_ref[0, 3 * j + 2:3 * j + 3, :]


def _mod_kernel(c_ref, w_ref, b_ref, o_ref):
    c = c_ref[...]
    a = (c * jax.nn.sigmoid(c)).astype(BF16)
    o_ref[...] = _dot(a, w_ref[...].astype(BF16)) + b_ref[...]


def _modulation(cond8, mod_w, mod_b):
    tn = MOD_TN
    n = N_MOD * D_MODEL
    return pl.pallas_call(
        _mod_kernel,
        grid=(DEPTH, n // tn),
        in_specs=[
            pl.BlockSpec((8, D_MODEL), lambda l, k: (0, 0)),
            pl.BlockSpec((None, D_MODEL, tn), lambda l, k: (l, 0, k)),
            pl.BlockSpec((None, 1, tn), lambda l, k: (l, 0, k)),
        ],
        out_specs=pl.BlockSpec((None, 8, tn), lambda l, k: (l, 0, k)),
        out_shape=jax.ShapeDtypeStruct((DEPTH, 8, n), F32),
        compiler_params=_params("arbitrary", "arbitrary"),
        name="modulation",
    )(cond8, mod_w, mod_b.reshape(DEPTH, 1, n))


def _two_stream_maps(tm, n_ctx_rows, dec_seq):
    n_ctx = n_ctx_rows // tm
    per_seq = dec_seq // tm
    ctx_map = lambda i: (jnp.minimum(i, n_ctx - 1), 0)
    lat_map = lambda i: (jnp.maximum(i - n_ctx, 0), 0)
    cond = lambda i: jnp.where(i < n_ctx, 0, 1 + (i - n_ctx) // per_seq)
    return n_ctx, ctx_map, lat_map, cond


def _norm_spec(layer, j):
    return pl.BlockSpec((None, 1, D_MODEL), lambda i: (3 * layer + j, 0, 0))


def _mod_spec(layer, cond):
    return pl.BlockSpec((None, 1, N_MOD, D_MODEL), lambda i: (layer, cond(i), 0, 0))


class _WeightStream:
    def __init__(self, pairs, stage_ref, sem_ref):
        self.pairs, self.stage, self.sem, self.done = pairs, stage_ref, sem_ref, 0
        for k in range(min(2, len(pairs))):
            self._copy(k).start()

    def _copy(self, k):
        return pltpu.make_async_copy(self.pairs[k][0], self.stage.at[k % 2], self.sem.at[k % 2])

    def take(self, n=None):
        n = len(self.pairs) - self.done if n is None else min(n, len(self.pairs) - self.done)
        for _ in range(n):
            k = self.done
            self._copy(k).wait()
            dst, idx = self.pairs[k][1]
            dst[idx] = self.stage[k % 2].astype(BF16)
            if k + 2 < len(self.pairs):
                self._copy(k + 2).start()
            self.done += 1


def _col_pairs(w_hbm, layer, w_ref, cols=None):
    if len(w_ref.shape) == 2:
        dst = lambda c: (slice(None), slice(c, c + W_STAGE))
        n = w_ref.shape[1]
    else:
        width = w_ref.shape[2]
        dst = lambda c: (c // width, slice(None), slice(c % width, c % width + W_STAGE))
        n = w_ref.shape[0] * width
    cols = range(0, n, W_STAGE) if cols is None else cols
    return [(w_hbm.at[layer, :, pl.ds(c, W_STAGE)], (w_ref, dst(c))) for c in cols]


def _row_pairs(w_hbm, layer, w_ref):
    return [(w_hbm.at[layer, pl.ds(r, W_STAGE), :], (w_ref, (slice(r, r + W_STAGE), slice(None))))
            for r in range(0, w_ref.shape[0], W_STAGE)]


def _load_by_cols(w_hbm, layer, w_ref, stage_ref, sem_ref):
    _WeightStream(_col_pairs(w_hbm, layer, w_ref), stage_ref, sem_ref).take()


def _load_by_rows(w_hbm, layer, w_ref, stage_ref, sem_ref):
    _WeightStream(_row_pairs(w_hbm, layer, w_ref), stage_ref, sem_ref).take()


COL_STAGE = (pltpu.VMEM((2, D_MODEL, W_STAGE), F32), pltpu.SemaphoreType.DMA((2,)))
ROW_STAGE = (pltpu.VMEM((2, W_STAGE, D_MODEL), F32), pltpu.SemaphoreType.DMA((2,)))
HBM = pl.BlockSpec(memory_space=pl.ANY)


def _ffn_kernel(*refs, layer, j, n_ctx, proj_layer, split_in, split_out):
    h_refs = refs[:2 if split_in else 1]
    pos = len(h_refs)
    mod_ref, g_ref, wgu_hbm, wd_hbm = refs[pos:pos + 4]
    pos += 4
    if proj_layer is not None:
        octx_ref, oa_ref, ob_ref, wo_hbm = refs[pos:pos + 4]
        pos += 4
    out_refs = refs[pos:pos + (2 if split_out else 1)]
    pos += len(out_refs)
    wgu_ref, wd_ref, stage_gu, sem_gu, stage_d, sem_d, act_ref = refs[pos:pos + 7]
    wo_ref = refs[pos + 7] if proj_layer is not None else None
    i = pl.program_id(0)
    is_ctx = i < n_ctx
    chunks = [(c, min(FFN_ACT_CHUNK, FFN_HIDDEN - c)) for c in range(0, FFN_HIDDEN, FFN_ACT_CHUNK)]

    def tile(first):
        if first:
            need = [col for c, w in chunks for base in (c, FFN_HIDDEN + c) for col in range(base, base + w, W_STAGE)]
            cols = _WeightStream(_col_pairs(wgu_hbm, layer, wgu_ref, need), stage_gu, sem_gu)
            proj = _row_pairs(wo_hbm, proj_layer, wo_ref) if proj_layer is not None else []
            rows = _WeightStream(proj + _row_pairs(wd_hbm, layer, wd_ref), stage_d, sem_d)
            rows.take(len(proj))
        if first or not split_in:
            h = h_refs[0][...]
        else:
            h = jnp.where(is_ctx, h_refs[0][...], h_refs[1][...])
        if proj_layer is not None:
            latent_heads = jnp.concatenate([oa_ref[...], ob_ref[...]], axis=1)
            heads = octx_ref[...] if first else jnp.where(is_ctx, octx_ref[...], latent_heads)
            h = h + _gate(mod_ref, 1) * _dot(heads, wo_ref[...])
        xn = _adaln(h, g_ref[...], mod_ref, j).astype(BF16)
        for c, w in chunks:
            if first:
                cols.take(2 * w // W_STAGE)
            gate = _dot(xn, wgu_ref[:, c:c + w])
            up = _dot(xn, wgu_ref[:, FFN_HIDDEN + c:FFN_HIDDEN + c + w])
            act_ref[:, c:c + w] = ((gate * jax.nn.sigmoid(gate)) * up).astype(BF16)
            if first:
                rows.take(2)
        if first:
            rows.take()
        res = h + (0.5 * _gate(mod_ref, j)) * _dot(act_ref[...], wd_ref[...])
        if first or not split_out:
            out_refs[0][...] = res
        else:
            @pl.when(is_ctx)
            def _():
                out_refs[0][...] = res

            @pl.when(jnp.logical_not(is_ctx))
            def _():
                out_refs[1][...] = res

    pl.when(i == 0)(lambda: tile(True))
    pl.when(i > 0)(lambda: tile(False))


def _ffn(h, mod, g, w_gu, w_down, layer, j, ctx_rows, dec_seq, attn=None, split_out=False):
    tm = FFN_TM
    split_in = isinstance(h, tuple)
    h_arrays = list(h) if split_in else [h]
    total_rows = sum(a.shape[0] for a in h_arrays)
    n_ctx, ctx_map, lat_map, cond = _two_stream_maps(tm, ctx_rows, dec_seq)
    assert n_ctx >= 1
    row_map = lambda i: (i, 0)
    row_spec = lambda m, width=D_MODEL: pl.BlockSpec((tm, width), m)
    in_specs = ([row_spec(ctx_map), row_spec(lat_map)] if split_in else [row_spec(row_map)]) + [
        _mod_spec(layer, cond),
        _norm_spec(layer, j),
        HBM, HBM]
    args = [*h_arrays, mod, g, w_gu, w_down]
    scratch = [pltpu.VMEM((D_MODEL, 2 * FFN_HIDDEN), BF16), pltpu.VMEM((FFN_HIDDEN, D_MODEL), BF16),
               *COL_STAGE, *ROW_STAGE, pltpu.VMEM((tm, FFN_HIDDEN), BF16)]
    proj_layer = None
    if attn is not None:
        o_ctx, oa, ob, w_out, proj_layer = attn
        in_specs += [row_spec(ctx_map, o_ctx.shape[1]), row_spec(lat_map, oa.shape[1]),
                     row_spec(lat_map, ob.shape[1]), HBM]
        args += [o_ctx, oa, ob, w_out]
        scratch.append(pltpu.VMEM((o_ctx.shape[1], D_MODEL), BF16))
    if split_out:
        out_specs = [row_spec(ctx_map), row_spec(lat_map)]
        out_shape = [jax.ShapeDtypeStruct((ctx_rows, D_MODEL), F32),
                     jax.ShapeDtypeStruct((total_rows - ctx_rows, D_MODEL), F32)]
    else:
        out_specs = row_spec(row_map)
        out_shape = jax.ShapeDtypeStruct((total_rows, D_MODEL), F32)
    return pl.pallas_call(
        functools.partial(_ffn_kernel, layer=layer, j=j, n_ctx=n_ctx, proj_layer=proj_layer,
                          split_in=split_in, split_out=split_out),
        grid=(total_rows // tm,),
        in_specs=in_specs,
        out_specs=out_specs,
        out_shape=out_shape,
        scratch_shapes=scratch,
        compiler_params=_params("arbitrary"),
        name=f"ffn{j}_l{layer}",
    )(*args)


def _head_mean_sq(x):
    w = x.shape[-1]
    r = lax.broadcasted_iota(jnp.int32, (MXU_COLS, MXU_COLS), 0) // HEAD_DIM
    c = lax.broadcasted_iota(jnp.int32, (MXU_COLS, MXU_COLS), 1) // HEAD_DIM
    bd = jnp.where(r == c, 1.0 / HEAD_DIM, 0.0).astype(BF16)
    sq = (x * x).astype(BF16)
    parts = []
    for s in range(0, w, MXU_COLS):
        e = min(s + MXU_COLS, w)
        parts.append(_dot(sq[:, s:e], bd[:e - s, :e - s]))
    return parts[0] if len(parts) == 1 else jnp.concatenate(parts, axis=-1)


def _low_half():
    return lax.broadcasted_iota(jnp.int32, (1, LANES), 1) < HEAD_DIM


def _head_rms(x, gain):
    return (x * lax.rsqrt(_head_mean_sq(x) + EPS)) * gain


def _rope(x, cos, sin_signed):
    lanes = cos.shape[-1]
    outs = []
    for s in range(0, x.shape[-1], lanes):
        xs = x[:, s:s + lanes]
        blk = lax.broadcasted_iota(jnp.int32, xs.shape, 1) // (HEAD_DIM // 4)
        partner = jnp.where(blk % 2 == 0,
                            pltpu.roll(xs, lanes - HEAD_DIM // 4, 1),
                            pltpu.roll(xs, HEAD_DIM // 4, 1))
        outs.append(xs * cos + partner * sin_signed)
    return outs[0] if len(outs) == 1 else jnp.concatenate(outs, axis=-1)


def _qkv_kernel(*refs, layer, rope, cache_out):
    h_ref, mod_ref, g_ref, w_hbm, gain_ref = refs[:5]
    w_ref, stage_ref, sem_ref = refs[-3:]
    pos = 5
    if rope:
        cos_ref, sin_ref = refs[pos:pos + 2]
        pos += 2
    q_ref, kt_ref, v_ref = refs[pos:pos + 3]
    pos += 3

    @pl.when(pl.program_id(0) == 0)
    def _load_weights():
        _load_by_cols(w_hbm, layer, w_ref, stage_ref, sem_ref)

    xn = _adaln(h_ref[...], g_ref[...], mod_ref, 1).astype(BF16)
    y = _dot(xn, w_ref[...])
    o = 0
    qa = _head_rms(y[:, o:o + NA_W], gain_ref[:, :NA_W]); o += NA_W
    ka = _head_rms(y[:, o:o + NA_W], gain_ref[:, NA_W:2 * NA_W]); o += NA_W
    va = y[:, o:o + NA_W]; o += NA_W
    qb = _head_rms(y[:, o:o + GQ_W], gain_ref[:, 2 * NA_W:2 * NA_W + GQ_W]); o += GQ_W
    kb = _head_rms(y[:, o:o + GKV_W], gain_ref[:, 2 * NA_W + GQ_W:]); o += GKV_W
    vb = y[:, o:o + GKV_W]
    if rope:
        cos, sin = cos_ref[...], sin_ref[...]
        qb = _rope(qb, cos, sin)
        kb = _rope(kb, cos, sin)
    q_ref[:, :NA_W] = (qa * Q_SCALE).astype(BF16)
    q_ref[:, NA_W:] = (qb * Q_SCALE).astype(BF16)
    low = _low_half()
    vb_swapped = pltpu.roll(vb, HEAD_DIM, 1)
    v_ref[:, :NA_W] = va.astype(BF16)
    v_ref[:, NA_W:NA_W + LANES] = jnp.where(low, vb, vb_swapped).astype(BF16)
    v_ref[:, NA_W + LANES:] = jnp.where(low, vb_swapped, vb).astype(BF16)
    n_seq, _, s = kt_ref.shape
    for i in range(n_seq):
        rows = slice(i * s, (i + 1) * s)
        ka_t = ka[rows].T
        kb_t = kb[rows].T
        k0, k1 = kb_t[:HEAD_DIM], kb_t[HEAD_DIM:]
        kt_ref[i, :NA_W, :] = ka_t.astype(BF16)
        kt_ref[i, NA_W:, :] = jnp.concatenate([k0, k0, k1, k1], axis=0).astype(BF16)
        if cache_out:
            kat_ref, vat_ref, kbt_ref, vbt_ref = refs[pos:pos + 4]
            kat_ref[i] = ka_t
            vat_ref[i] = va[rows].T
            kbt_ref[i] = kb_t
            vbt_ref[i] = vb[rows].T


def _qkv(h, row0, t, mod, mod_layer, cond0, g, w_in, layer, gains, cond_div, seq, rope_tabs=None,
         cache_out=False):
    tm = PROJ_TM
    rope = rope_tabs is not None
    assert not (rope and cache_out)
    row = lambda i: (i, 0)
    const = lambda i: (0, 0)
    in_specs = [
        pl.BlockSpec((tm, D_MODEL), lambda i: (i + row0 // tm, 0)),
        _mod_spec(mod_layer, lambda i: cond0 + i // cond_div),
        _norm_spec(mod_layer, 1),
        HBM,
        pl.BlockSpec((1, 2 * NA_W + GQ_W + GKV_W), const),
    ]
    args = [h, mod, g, w_in, gains]
    if rope:
        seq_tiles = rope_tabs[0].shape[0] // tm
        in_specs += [pl.BlockSpec((tm, 2 * HEAD_DIM), lambda i: (i % seq_tiles, 0))] * 2
        args += list(rope_tabs)
    if seq >= tm:
        per_seq = seq // tm
        t_block = lambda rows: pl.BlockSpec((1, rows, tm), lambda i: (i // per_seq, 0, i % per_seq))
    else:
        t_block = lambda rows: pl.BlockSpec((tm // seq, rows, seq), lambda i: (i, 0, 0))
    t_shape = lambda rows, dtype: jax.ShapeDtypeStruct((t // seq, rows, seq), dtype)
    out_specs = [pl.BlockSpec((tm, NA_W + GQ_W), row), t_block(KT_ROWS), pl.BlockSpec((tm, KT_ROWS), row)]
    out_shape = [jax.ShapeDtypeStruct((t, NA_W + GQ_W), BF16), t_shape(KT_ROWS, BF16),
                 jax.ShapeDtypeStruct((t, KT_ROWS), BF16)]
    if cache_out:
        out_specs += [t_block(NA_W), t_block(NA_W), t_block(GKV_W), t_block(GKV_W)]
        out_shape += [t_shape(NA_W, F32), t_shape(NA_W, F32), t_shape(GKV_W, F32), t_shape(GKV_W, F32)]
    return pl.pallas_call(
        functools.partial(_qkv_kernel, layer=layer, rope=rope, cache_out=cache_out),
        grid=(t // tm,),
        in_specs=in_specs,
        out_specs=out_specs,
        out_shape=out_shape,
        scratch_shapes=[pltpu.VMEM((D_MODEL, QKV_WIDTH), BF16), *COL_STAGE],
        compiler_params=_params("arbitrary"),
        name="qkv_latent" if rope else "qkv_context",
    )(*args)


def _split_pair(q2):
    low = _low_half()
    zero = jnp.zeros((), q2.dtype)
    return jnp.where(low, q2, zero), jnp.where(low, zero, q2)


def _with_ones(v2):
    return jnp.concatenate([v2, jnp.ones(v2.shape, v2.dtype)], axis=1)


def _join(blocks, axis):
    return blocks[0] if len(blocks) == 1 else jnp.concatenate(blocks, axis=axis)


def _attend(problems):
    scores = [_dot(qm, _join(kts, 1)) for qm, kts, _, _ in problems]
    probs = []
    for s, (_, kts, biases, _) in zip(scores, problems):
        parts, start = [], 0
        for kt, b in zip(kts, biases):
            part = s[:, start:start + kt.shape[1]]
            parts.append(part if b is None else part + b)
            start += kt.shape[1]
        m = functools.reduce(jnp.maximum, [jnp.max(part, axis=-1, keepdims=True) for part in parts])
        probs.append(_join([jnp.exp(part - m).astype(BF16) for part in parts], 1))
    return [_dot(p, _join(vexts, 0)) for p, (_, _, _, vexts) in zip(probs, problems)]


def _merge_pair(oe_even, oe_odd):
    even = oe_even[:, :LANES] * (1.0 / oe_even[:, LANES:])
    odd = oe_odd[:, :LANES] * (1.0 / oe_odd[:, LANES:])
    return jnp.where(_low_half(), even, odd)


def _gqa_groups(q_ref, q_off, o_ref, o_off, keys_values):
    tq = q_ref.shape[0]
    slabs = lambda g, off: [slice(off + (2 * g + j) * LANES, off + (2 * g + j + 1) * LANES) for j in range(2)]
    problems = []
    for g in range(GQA_KV_HEADS):
        qm = jnp.concatenate([part for sl in slabs(g, q_off) for part in _split_pair(q_ref[:, sl])], axis=0)
        problems.append((qm, *keys_values(slice(g * LANES, (g + 1) * LANES))))
    for g, oe in enumerate(_attend(problems)):
        for j, sl in enumerate(slabs(g, o_off)):
            o_ref[:, sl] = _merge_pair(oe[2 * j * tq:(2 * j + 1) * tq],
                                       oe[(2 * j + 1) * tq:(2 * j + 2) * tq]).astype(o_ref.dtype)


def _ctx_attn_kernel(q_blk, kt_blk, v_blk, o_blk):
    n_seq, _, seq = kt_blk.shape
    for b in range(n_seq):
        rows = pl.ds(b * seq, seq)
        q_ref, v_ref, o_ref, kt_ref = q_blk.at[rows], v_blk.at[rows], o_blk.at[rows], kt_blk.at[b]
        for p in range(NA_HEADS // 2):
            sl = slice(p * LANES, (p + 1) * LANES)
            vext = _with_ones(v_ref[:, sl])
            oe = [_attend([(qm, [kt_ref[sl, :]], [None], [vext])])[0] for qm in _split_pair(q_ref[:, sl])]
            o_ref[:, sl] = _merge_pair(*oe).astype(o_ref.dtype)
        _gqa_groups(q_ref, NA_W, o_ref, NA_W,
                    lambda kv: ([kt_ref[NA_W + kv.start:NA_W + kv.stop, :]], [None],
                                [_with_ones(v_ref[:, NA_W + kv.start:NA_W + kv.stop])]))


def _ctx_attention(q, kt, v, seq):
    t = q.shape[0]
    n_seq = CTX_SEQ_PER_STEP
    row = lambda b: (b, 0)
    return pl.pallas_call(
        _ctx_attn_kernel,
        grid=(t // (n_seq * seq),),
        in_specs=[pl.BlockSpec((n_seq * seq, NA_W + GQ_W), row),
                  pl.BlockSpec((n_seq, KT_ROWS, seq), lambda b: (b, 0, 0)),
                  pl.BlockSpec((n_seq * seq, KT_ROWS), row)],
        out_specs=pl.BlockSpec((n_seq * seq, NA_W + GQ_W), row),
        out_shape=jax.ShapeDtypeStruct((t, NA_W + GQ_W), BF16),
        compiler_params=_params("arbitrary"),
        name="ctx_attention",
    )(q, kt, v)


def _na_kernel(q_ref, kt0_ref, kt1_ref, kt2_ref, v0_ref, v1_ref, v2_ref, ckt_ref, cv_ref, bias_ref, o_ref):
    for p in range(NA_HEADS // 2):
        sl = slice(p * LANES, (p + 1) * LANES)
        kt_loc = jnp.concatenate([kt0_ref[sl, :], kt1_ref[sl, :], kt2_ref[sl, :]], axis=1)
        v_loc = _with_ones(jnp.concatenate([v0_ref[:, sl], v1_ref[:, sl], v2_ref[:, sl]], axis=0))
        v_ctx = _with_ones(cv_ref[:, sl])
        oe = _attend([(qm, [kt_loc, ckt_ref[sl, :]], [bias_ref[2 * p + half], None], [v_loc, v_ctx])
                      for half, qm in enumerate(_split_pair(q_ref[:, sl]))])
        o_ref[:, sl] = _merge_pair(*oe).astype(o_ref.dtype)


def _na_bias(rpb, rows):
    qc = np.arange(GRID_W)
    kc = np.arange(GRID_W)
    ws = np.clip(qc - NA_WIN_C // 2, 0, GRID_W - NA_WIN_C)
    col_ok = (kc[None, :] >= ws[:, None]) & (kc[None, :] < ws[:, None] + NA_WIN_C)
    dc = np.clip(kc[None, :] - qc[:, None] + NA_WIN_C - 1, 0, 2 * NA_WIN_C - 2)
    pick = ((dc[:, :, None] == np.arange(2 * NA_WIN_C - 1)) & col_ok[:, :, None]).astype(np.float32)
    tiles = jnp.einsum("hrd,qkd->hrqk", rpb.astype(F32), pick, precision=lax.Precision.HIGHEST)
    tiles = jnp.where(col_ok, tiles, NEG_INF)
    tiles = jnp.concatenate([tiles, tiles], axis=-1)
    n_off = 2 * NA_WIN_R - 1
    return pl.pallas_call(
        functools.partial(_na_bias_kernel, rows=rows),
        grid=(NA_HEADS,),
        in_specs=[pl.BlockSpec((None, n_off, GRID_W, LANES), lambda h: (h, 0, 0, 0))],
        out_specs=pl.BlockSpec((3, None, NA_BAND_Q, NA_BAND_K), lambda h: (0, h, 0, 0)),
        out_shape=jax.ShapeDtypeStruct((3, NA_HEADS, NA_BAND_Q, NA_BAND_K), F32),
        compiler_params=_params("arbitrary"),
        name="na_bias",
    )(tiles)


def _na_bias_kernel(tiles_ref, o_ref, *, rows):
    low = _low_half()
    masked = jnp.full((GRID_W, LANES), NEG_INF, F32)
    nb = rows // NA_BAND_ROWS
    for t, band in enumerate((0, 1, nb - 1)):
        r0 = band * NA_BAND_ROWS
        k0 = int(np.clip(r0 - NA_WIN_R // 2, 0, rows - NA_BAND_KEY_ROWS))
        for ri in range(NA_BAND_ROWS):
            r = r0 + ri
            start = int(np.clip(r - NA_WIN_R // 2, 0, rows - NA_WIN_R))
            tile = lambda kr: tiles_ref[kr - r + NA_WIN_R - 1] if start <= kr < start + NA_WIN_R else masked
            for m in range(NA_BAND_KEY_ROWS // 2):
                kr = k0 + 2 * m
                o_ref[t, ri * GRID_W:(ri + 1) * GRID_W, m * LANES:(m + 1) * LANES] = jnp.where(
                    low, tile(kr), tile(kr + 1))


def _na_attention(q, kt, v, ctx_kt, ctx_v, bias, batch, seq):
    rows = seq // GRID_W
    nb = rows // NA_BAND_ROWS
    past = ctx_v.shape[1]
    assert (rows - NA_BAND_KEY_ROWS) % NA_BAND_ROWS == 0
    n_chunks = NA_BAND_KEY_ROWS // NA_BAND_ROWS
    band_type = lambda j: jnp.where(j == 0, 0, jnp.where(j == nb - 1, 2, 1))
    chunk0 = lambda j: jnp.clip(j - NA_WIN_R // 2 // NA_BAND_ROWS, 0, nb - n_chunks)
    kt_spec = lambda c: pl.BlockSpec((None, NA_W, NA_BAND_Q), lambda b, j: (b, 0, chunk0(j) + c))
    v_spec = lambda c: pl.BlockSpec((NA_BAND_Q, NA_W), lambda b, j: (b * nb + chunk0(j) + c, 0))
    return pl.pallas_call(
        _na_kernel,
        grid=(batch, nb),
        in_specs=[pl.BlockSpec((NA_BAND_Q, NA_W), lambda b, j: (b * nb + j, 0))]
                 + [kt_spec(c) for c in range(n_chunks)] + [v_spec(c) for c in range(n_chunks)]
                 + [pl.BlockSpec((None, NA_W, past), lambda b, j: (b, 0, 0)),
                    pl.BlockSpec((None, past, NA_W), lambda b, j: (b, 0, 0)),
                    pl.BlockSpec((None, NA_HEADS, NA_BAND_Q, NA_BAND_K), lambda b, j: (band_type(j), 0, 0, 0))],
        out_specs=pl.BlockSpec((NA_BAND_Q, NA_W), lambda b, j: (b * nb + j, 0)),
        out_shape=jax.ShapeDtypeStruct((batch * seq, NA_W), BF16),
        compiler_params=_params("arbitrary", "arbitrary"),
        name="na_attention",
    )(q, *([kt] * n_chunks), *([v] * n_chunks), ctx_kt, ctx_v, bias)


def _gqa_kernel(q_ref, kt_ref, v_ref, ckt_ref, cv_ref, o_ref):
    _gqa_groups(q_ref, 0, o_ref, 0,
                lambda kv: ([kt_ref[kv, :], ckt_ref[kv, :]], [None, None],
                            [_with_ones(v_ref[:, kv]), _with_ones(cv_ref[:, kv])]))


def _gqa_attention(q, kt, v, ctx_kt, ctx_v, batch, seq):
    tq = GQA_TQ
    nt = seq // tq
    past = ctx_v.shape[1]
    dup = 2 * GKV_W
    return pl.pallas_call(
        _gqa_kernel,
        grid=(batch, nt),
        in_specs=[
            pl.BlockSpec((tq, GQ_W), lambda b, i: (b * nt + i, 1)),
            pl.BlockSpec((None, dup, seq), lambda b, i: (b, NA_W // dup, 0)),
            pl.BlockSpec((seq, dup), lambda b, i: (b, NA_W // dup)),
            pl.BlockSpec((None, dup, past), lambda b, i: (b, 0, 0)),
            pl.BlockSpec((None, past, dup), lambda b, i: (b, 0, 0)),
        ],
        out_specs=pl.BlockSpec((tq, GQ_W), lambda b, i: (b * nt + i, 0)),
        out_shape=jax.ShapeDtypeStruct((batch * seq, GQ_W), BF16),
        compiler_params=_params("arbitrary", "arbitrary"),
        name="gqa_attention",
    )(q, kt, v, ctx_kt, ctx_v)


def _sgu_kernel(h_ref, mod_ref, g_ref, win_hbm, vg_ref, ws_ref, bs_ref, wout_hbm, o_ref,
                win_ref, wout_ref, stage_in, sem_in, stage_out, sem_out, gated_ref, y_ref, *, layer):
    tm = h_ref.shape[0]
    n_chunks = y_ref.shape[0]
    per_chunk = SGU_ACT_CHUNK // SGU_GW

    def tile(first):
        if first:
            cols = _WeightStream(_col_pairs(win_hbm, layer, win_ref), stage_in, sem_in)
            rows = _WeightStream(_row_pairs(wout_hbm, layer, wout_ref), stage_out, sem_out)
        h = h_ref[...]
        xn = _adaln(h, g_ref[...], mod_ref, 1).astype(BF16)
        ssq = jnp.zeros((tm, 1), F32)
        for c in range(n_chunks):
            if first:
                cols.take(SGU_ACT_CHUNK // W_STAGE)
            y = _dot(xn, win_ref[c])
            y = 0.5 * y * (1.0 + lax.erf(y * math.sqrt(0.5)))
            y_ref[c] = y
            if c >= n_chunks // 2:
                ssq = ssq + jnp.sum(y * y, axis=-1, keepdims=True)
            if first:
                rows.take(1)
        if first:
            rows.take()
        v_scale = lax.rsqrt(ssq * (1.0 / SGU_WIDTH) + EPS)
        for c in range(tm // SGU_CHUNK):
            rs = slice(c * SGU_CHUNK, (c + 1) * SGU_CHUNK)
            for g in range(SGU_GROUPS):
                cs = slice(g * SGU_GW, (g + 1) * SGU_GW)
                in_chunk = slice((g % per_chunk) * SGU_GW, (g % per_chunk + 1) * SGU_GW)
                u = y_ref[g // per_chunk, rs, in_chunk]
                v = y_ref[n_chunks // 2 + g // per_chunk, rs, in_chunk]
                vn = ((v * v_scale[rs]) * vg_ref[:, cs]).astype(BF16)
                sv = _dot(ws_ref[g].astype(BF16), vn) + bs_ref[:, g:g + 1]
                gated_ref[rs, cs] = (u * sv).astype(BF16)
        o_ref[...] = h + _gate(mod_ref, 1) * _dot(gated_ref[...], wout_ref[...])

    pl.when(pl.program_id(0) == 0)(lambda: tile(True))
    pl.when(pl.program_id(0) > 0)(lambda: tile(False))


def _sgu(h, mod, mod_layer, g, w_in, v_g, w_s, b_s_t, w_out, layer, ctx_rows, dec_seq):
    tm = SGU_TM
    n_act = 2 * SGU_WIDTH // SGU_ACT_CHUNK
    _, _, _, cond = _two_stream_maps(tm, ctx_rows, dec_seq)
    row_spec = pl.BlockSpec((tm, D_MODEL), lambda i: (i, 0))
    const = lambda i: (0, 0)
    return pl.pallas_call(
        functools.partial(_sgu_kernel, layer=layer),
        grid=(h.shape[0] // tm,),
        in_specs=[
            row_spec,
            _mod_spec(mod_layer, cond),
            _norm_spec(mod_layer, 1),
            HBM,
            pl.BlockSpec((1, SGU_WIDTH), const),
            pl.BlockSpec((None, SGU_GROUPS, SGU_CHUNK, SGU_CHUNK), lambda i: (layer, 0, 0, 0)),
            pl.BlockSpec((SGU_CHUNK, SGU_GROUPS), const),
            HBM,
        ],
        out_specs=row_spec,
        out_shape=jax.ShapeDtypeStruct(h.shape, F32),
        scratch_shapes=[
            pltpu.VMEM((n_act, D_MODEL, SGU_ACT_CHUNK), BF16),
            pltpu.VMEM((SGU_WIDTH, D_MODEL), BF16),
            *COL_STAGE, *ROW_STAGE,
            pltpu.VMEM((tm, SGU_WIDTH), BF16),
            pltpu.VMEM((n_act, tm, SGU_ACT_CHUNK), F32),
        ],
        compiler_params=_params("arbitrary"),
        name="sgu",
    )(h, mod, g, w_in, v_g, w_s, b_s_t, w_out)


def _rope_tables(seq):
    half = HEAD_DIM // 2
    t = jnp.arange(seq)
    freqs = ROPE_BASE ** (-jnp.arange(0, half, 2, dtype=F32) / half)
    def tab(pos):
        ang = pos.astype(F32)[:, None] * freqs[None, :]
        cos, sin = jnp.cos(ang), jnp.sin(ang)
        return jnp.concatenate([cos, cos], -1), jnp.concatenate([-sin, sin], -1)
    cr, sr = tab(t // GRID_W)
    cc, sc = tab(t % GRID_W)
    cos = jnp.concatenate([cr, cc], -1)
    sin = jnp.concatenate([sr, sc], -1)
    return jnp.tile(cos, (1, 2)), jnp.tile(sin, (1, 2))


def kernel(x_prompt, x_sample, cache_na_k, cache_na_v, cache_gqa_k, cache_gqa_v, c, c_ctx, norm_g, mod_w, mod_b, ffn1_w_gu, ffn1_w_down, ffn2_w_gu, ffn2_w_down, attn_w_in, attn_w_out, na_q_g, na_k_g, na_rpb, gqa_q_g, gqa_k_g, sgu_w_in, sgu_v_g, sgu_w_s, sgu_b_s, sgu_w_out):
    batch, seq, _ = x_prompt.shape
    dec_batch, dec_seq, _ = x_sample.shape
    past = cache_na_k.shape[2]
    ctx_rows, lat_rows = batch * seq, dec_batch * dec_seq
    h = (x_prompt.reshape(ctx_rows, D_MODEL), x_sample.reshape(lat_rows, D_MODEL))

    assert 1 + dec_batch <= 8
    cond8 = jnp.concatenate([c_ctx[None], c, jnp.zeros((8 - 1 - dec_batch, D_MODEL), F32)], axis=0)
    mod = _modulation(cond8, mod_w, mod_b).reshape(DEPTH, 8, N_MOD, D_MODEL)

    g = norm_g.reshape(DEPTH * 3, 1, D_MODEL)
    new_cache = None
    for layer in range(DEPTH):
        h = _ffn(h, mod, g, ffn1_w_gu, ffn1_w_down, layer, 0, ctx_rows, dec_seq)

        if layer % 2 == 0:
            e = layer // 2
            gains = jnp.concatenate([jnp.tile(na_q_g[e], NA_HEADS), jnp.tile(na_k_g[e], NA_HEADS),
                                     jnp.tile(gqa_q_g[e], GQA_HEADS), jnp.tile(gqa_k_g[e], GQA_KV_HEADS)])[None]
            q, kt, v, *new_cache = _qkv(h, 0, ctx_rows, mod, layer, 0, g, attn_w_in, e, gains,
                                        ctx_rows // PROJ_TM, seq, cache_out=True)
            o_ctx = _ctx_attention(q, kt, v, seq)
            q, kt, v = _qkv(h, ctx_rows, lat_rows, mod, layer, 1, g, attn_w_in, e, gains,
                            dec_seq // PROJ_TM, dec_seq, rope_tabs=_rope_tables(dec_seq))
            to_kt = lambda ck, rep: jnp.repeat(jnp.transpose(ck[:, e], (0, 2, 3, 1)), rep, axis=1).reshape(
                dec_batch, -1, past).astype(BF16)
            to_v = lambda cv, rep: jnp.repeat(cv[:, e], rep, axis=2).reshape(dec_batch, past, -1).astype(BF16)
            oa = _na_attention(q, kt, v, to_kt(cache_na_k, 1), to_v(cache_na_v, 1),
                               _na_bias(na_rpb[e], dec_seq // GRID_W), dec_batch, dec_seq)
            ob = _gqa_attention(q, kt, v, to_kt(cache_gqa_k, 2), to_v(cache_gqa_v, 2), dec_batch, dec_seq)
            attn = (o_ctx, oa, ob, attn_w_out, e)
        else:
            attn = None
            o = layer // 2
            h = _sgu(h, mod, layer, g, sgu_w_in, sgu_v_g[o][None], sgu_w_s, sgu_b_s[o].T, sgu_w_out,
                     o, ctx_rows, dec_seq)

        h = _ffn(h, mod, g, ffn2_w_gu, ffn2_w_down, layer, 2, ctx_rows, dec_seq, attn=attn,
                 split_out=layer == DEPTH - 1)
    hp, hs = h

    n_attn = (DEPTH + 1) // 2
    assert n_attn == 1
    from_t = lambda x: jnp.transpose(x.reshape(batch, n_attn, -1, HEAD_DIM, seq), (0, 1, 4, 2, 3))
    return (hp.reshape(batch, seq, D_MODEL), hs.reshape(dec_batch, dec_seq, D_MODEL),
            *(from_t(x) for x in new_cache))
```

```python
import functools
import math

import jax
import jax.numpy as jnp
import numpy as np
from jax import lax
from jax.experimental import pallas as pl
from jax.experimental.pallas import tpu as pltpu

D_MODEL = 1024
DEPTH = 2
GRID_W = 64
HEAD_DIM = 64
NA_HEADS = 8
NA_WIN_R = 8
NA_WIN_C = 16
GQA_HEADS = 8
GQA_KV_HEADS = 2
GQA_GROUP = GQA_HEADS // GQA_KV_HEADS
NA_W = NA_HEADS * HEAD_DIM
GQ_W = GQA_HEADS * HEAD_DIM
GKV_W = GQA_KV_HEADS * HEAD_DIM
QKV_WIDTH = 3 * NA_W + GQ_W + 2 * GKV_W
ROPE_BASE = 10000.0
SGU_CHUNK = 128
SGU_GROUPS = 8
SGU_WIDTH = 2 * D_MODEL
SGU_GW = SGU_WIDTH // SGU_GROUPS
FFN_HIDDEN = 2816
N_MOD = 9
EPS = 1e-6
NEG_INF = -1e30
Q_SCALE = HEAD_DIM ** -0.5

NA_BAND_ROWS = 4
NA_BAND_KEY_ROWS = 12
NA_BAND_Q = NA_BAND_ROWS * GRID_W
NA_BAND_K = NA_BAND_KEY_ROWS * GRID_W

FFN_TM = 512
FFN_ACT_CHUNK = 512
SGU_ACT_CHUNK = 512
SGU_TM = 512
MOD_TN = 3072
GQA_TQ = 256
CTX_SEQ_PER_STEP = 4
PROJ_TM = 1024
W_STAGE = 256

LANES = 128
KT_ROWS = NA_W + 2 * GKV_W
MXU_COLS = 256
VMEM_LIMIT = 56 * 1024 * 1024

BF16 = jnp.bfloat16
F32 = jnp.float32


def _dot(a, b):
    return jnp.dot(a, b, preferred_element_type=F32)


def _params(*sem):
    return pltpu.CompilerParams(dimension_semantics=sem, vmem_limit_bytes=VMEM_LIMIT)


def _adaln(x, g, mod_ref, j):
    shift = mod_ref[0, 3 * j:3 * j + 1, :]
    scale = mod_ref[0, 3 * j + 1:3 * j + 2, :]
    y = x * lax.rsqrt(jnp.mean(x * x, axis=-1, keepdims=True) + EPS)
    return (y * g) * (1.0 + scale) + shift


def _gate(mod_ref, j):
    return mod_ref[0, 3 * j + 2:3 * j + 3, :]


def _mod_kernel(c_ref, w_ref, b_ref, o_ref):
    c = c_ref[...]
    a = (c * jax.nn.sigmoid(c)).astype(BF16)
    o_ref[...] = _dot(a, w_ref[...].astype(BF16)) + b_ref[...]


def _modulation(cond8, mod_w, mod_b):
    tn = MOD_TN
    n = N_MOD * D_MODEL
    return pl.pallas_call(
        _mod_kernel,
        grid=(DEPTH, n // tn),
        in_specs=[
            pl.BlockSpec((8, D_MODEL), lambda l, k: (0, 0)),
            pl.BlockSpec((None, D_MODEL, tn), lambda l, k: (l, 0, k)),
            pl.BlockSpec((None, 1, tn), lambda l, k: (l, 0, k)),
        ],
        out_specs=pl.BlockSpec((None, 8, tn), lambda l, k: (l, 0, k)),
        out_shape=jax.ShapeDtypeStruct((DEPTH, 8, n), F32),
        compiler_params=_params("arbitrary", "arbitrary"),
        name="modulation",
    )(cond8, mod_w, mod_b.reshape(DEPTH, 1, n))


def _two_stream_maps(tm, n_ctx_rows, dec_seq):
    n_ctx = n_ctx_rows // tm
    per_seq = dec_seq // tm
    ctx_map = lambda i: (jnp.minimum(i, n_ctx - 1), 0)
    lat_map = lambda i: (jnp.maximum(i - n_ctx, 0), 0)
    cond = lambda i: jnp.where(i < n_ctx, 0, 1 + (i - n_ctx) // per_seq)
    return n_ctx, ctx_map, lat_map, cond


def _norm_spec(layer, j):
    return pl.BlockSpec((None, 1, D_MODEL), lambda i: (3 * layer + j, 0, 0))


def _mod_spec(layer, cond):
    return pl.BlockSpec((None, 1, N_MOD, D_MODEL), lambda i: (layer, cond(i), 0, 0))


class _WeightStream:
    def __init__(self, pairs, stage_ref, sem_ref):
        self.pairs, self.stage, self.sem, self.done = pairs, stage_ref, sem_ref, 0
        for k in range(min(2, len(pairs))):
            self._copy(k).start()

    def _copy(self, k):
        return pltpu.make_async_copy(self.pairs[k][0], self.stage.at[k % 2], self.sem.at[k % 2])

    def take(self, n=None):
        n = len(self.pairs) - self.done if n is None else min(n, len(self.pairs) - self.done)
        for _ in range(n):
            k = self.done
            self._copy(k).wait()
            dst, idx = self.pairs[k][1]
            dst[idx] = self.stage[k % 2].astype(BF16)
            if k + 2 < len(self.pairs):
                self._copy(k + 2).start()
            self.done += 1


def _col_pairs(w_hbm, layer, w_ref, cols=None):
    if len(w_ref.shape) == 2:
        dst = lambda c: (slice(None), slice(c, c + W_STAGE))
        n = w_ref.shape[1]
    else:
        width = w_ref.shape[2]
        dst = lambda c: (c // width, slice(None), slice(c % width, c % width + W_STAGE))
        n = w_ref.shape[0] * width
    cols = range(0, n, W_STAGE) if cols is None else cols
    return [(w_hbm.at[layer, :, pl.ds(c, W_STAGE)], (w_ref, dst(c))) for c in cols]


def _row_pairs(w_hbm, layer, w_ref):
    return [(w_hbm.at[layer, pl.ds(r, W_STAGE), :], (w_ref, (slice(r, r + W_STAGE), slice(None))))
            for r in range(0, w_ref.shape[0], W_STAGE)]


def _load_by_cols(w_hbm, layer, w_ref, stage_ref, sem_ref):
    _WeightStream(_col_pairs(w_hbm, layer, w_ref), stage_ref, sem_ref).take()


COL_STAGE = (pltpu.VMEM((2, D_MODEL, W_STAGE), F32), pltpu.SemaphoreType.DMA((2,)))
ROW_STAGE = (pltpu.VMEM((2, W_STAGE, D_MODEL), F32), pltpu.SemaphoreType.DMA((2,)))
HBM = pl.BlockSpec(memory_space=pl.ANY)


def _ffn_kernel(*refs, layer, j, n_ctx, proj_layer, split_in, split_out):
    h_refs = refs[:2 if split_in else 1]
    pos = len(h_refs)
    mod_ref, g_ref, wgu_hbm, wd_hbm = refs[pos:pos + 4]
    pos += 4
    if proj_layer is not None:
        octx_ref, oa_ref, ob_ref, wo_hbm = refs[pos:pos + 4]
        pos += 4
    out_refs = refs[pos:pos + (2 if split_out else 1)]
    pos += len(out_refs)
    wgu_ref, wd_ref, stage_gu, sem_gu, stage_d, sem_d, act_ref = refs[pos:pos + 7]
    wo_ref = refs[pos + 7] if proj_layer is not None else None
    i = pl.program_id(0)
    is_ctx = i < n_ctx
    chunks = [(c, min(FFN_ACT_CHUNK, FFN_HIDDEN - c)) for c in range(0, FFN_HIDDEN, FFN_ACT_CHUNK)]

    def tile(first):
        if first:
            need = [col for c, w in chunks for base in (c, FFN_HIDDEN + c) for col in range(base, base + w, W_STAGE)]
            cols = _WeightStream(_col_pairs(wgu_hbm, layer, wgu_ref, need), stage_gu, sem_gu)
            proj = _row_pairs(wo_hbm, proj_layer, wo_ref) if proj_layer is not None else []
            rows = _WeightStream(proj + _row_pairs(wd_hbm, layer, wd_ref), stage_d, sem_d)
            rows.take(len(proj))
        if first or not split_in:
            h = h_refs[0][...]
        else:
            h = jnp.where(is_ctx, h_refs[0][...], h_refs[1][...])
        if proj_layer is not None:
            latent_heads = jnp.concatenate([oa_ref[...], ob_ref[...]], axis=1)
            heads = octx_ref[...] if first else jnp.where(is_ctx, octx_ref[...], latent_heads)
            h = h + _gate(mod_ref, 1) * _dot(heads, wo_ref[...])
        xn = _adaln(h, g_ref[...], mod_ref, j).astype(BF16)
        for c, w in chunks:
            if first:
                cols.take(2 * w // W_STAGE)
            gate = _dot(xn, wgu_ref[:, c:c + w])
            up = _dot(xn, wgu_ref[:, FFN_HIDDEN + c:FFN_HIDDEN + c + w])
            act_ref[:, c:c + w] = ((gate * jax.nn.sigmoid(gate)) * up).astype(BF16)
            if first:
                rows.take(2)
        if first:
            rows.take()
        res = h + (0.5 * _gate(mod_ref, j)) * _dot(act_ref[...], wd_ref[...])
        if first or not split_out:
            out_refs[0][...] = res
        else:
            @pl.when(is_ctx)
            def _():
                out_refs[0][...] = res

            @pl.when(jnp.logical_not(is_ctx))
            def _():
                out_refs[1][...] = res

    pl.when(i == 0)(lambda: tile(True))
    pl.when(i > 0)(lambda: tile(False))


def _ffn(h, mod, g, w_gu, w_down, layer, j, ctx_rows, dec_seq, attn=None, split_out=False):
    tm = FFN_TM
    split_in = isinstance(h, tuple)
    h_arrays = list(h) if split_in else [h]
    total_rows = sum(a.shape[0] for a in h_arrays)
    n_ctx, ctx_map, lat_map, cond = _two_stream_maps(tm, ctx_rows, dec_seq)
    assert n_ctx >= 1
    row_map = lambda i: (i, 0)
    row_spec = lambda m, width=D_MODEL: pl.BlockSpec((tm, width), m)
    in_specs = ([row_spec(ctx_map), row_spec(lat_map)] if split_in else [row_spec(row_map)]) + [
        _mod_spec(layer, cond),
        _norm_spec(layer, j),
        HBM, HBM]
    args = [*h_arrays, mod, g, w_gu, w_down]
    scratch = [pltpu.VMEM((D_MODEL, 2 * FFN_HIDDEN), BF16), pltpu.VMEM((FFN_HIDDEN, D_MODEL), BF16),
               *COL_STAGE, *ROW_STAGE, pltpu.VMEM((tm, FFN_HIDDEN), BF16)]
    proj_layer = None
    if attn is not None:
        o_ctx, oa, ob, w_out, proj_layer = attn
        in_specs += [row_spec(ctx_map, o_ctx.shape[1]), row_spec(lat_map, oa.shape[1]),
                     row_spec(lat_map, ob.shape[1]), HBM]
        args += [o_ctx, oa, ob, w_out]
        scratch.append(pltpu.VMEM((o_ctx.shape[1], D_MODEL), BF16))
    if split_out:
        out_specs = [row_spec(ctx_map), row_spec(lat_map)]
        out_shape = [jax.ShapeDtypeStruct((ctx_rows, D_MODEL), F32),
                     jax.ShapeDtypeStruct((total_rows - ctx_rows, D_MODEL), F32)]
    else:
        out_specs = row_spec(row_map)
        out_shape = jax.ShapeDtypeStruct((total_rows, D_MODEL), F32)
    return pl.pallas_call(
        functools.partial(_ffn_kernel, layer=layer, j=j, n_ctx=n_ctx, proj_layer=proj_layer,
                          split_in=split_in, split_out=split_out),
        grid=(total_rows // tm,),
        in_specs=in_specs,
        out_specs=out_specs,
        out_shape=out_shape,
        scratch_shapes=scratch,
        compiler_params=_params("arbitrary"),
        name=f"ffn{j}_l{layer}",
    )(*args)


def _head_mean_sq(x):
    w = x.shape[-1]
    r = lax.broadcasted_iota(jnp.int32, (MXU_COLS, MXU_COLS), 0) // HEAD_DIM
    c = lax.broadcasted_iota(jnp.int32, (MXU_COLS, MXU_COLS), 1) // HEAD_DIM
    bd = jnp.where(r == c, 1.0 / HEAD_DIM, 0.0).astype(BF16)
    sq = (x * x).astype(BF16)
    parts = []
    for s in range(0, w, MXU_COLS):
        e = min(s + MXU_COLS, w)
        parts.append(_dot(sq[:, s:e], bd[:e - s, :e - s]))
    return parts[0] if len(parts) == 1 else jnp.concatenate(parts, axis=-1)


def _low_half():
    return lax.broadcasted_iota(jnp.int32, (1, LANES), 1) < HEAD_DIM


def _head_rms(x, gain):
    return (x * lax.rsqrt(_head_mean_sq(x) + EPS)) * gain


def _rope(x, cos, sin_signed):
    lanes = cos.shape[-1]
    outs = []
    for s in range(0, x.shape[-1], lanes):
        xs = x[:, s:s + lanes]
        blk = lax.broadcasted_iota(jnp.int32, xs.shape, 1) // (HEAD_DIM // 4)
        partner = jnp.where(blk % 2 == 0,
                            pltpu.roll(xs, lanes - HEAD_DIM // 4, 1),
                            pltpu.roll(xs, HEAD_DIM // 4, 1))
        outs.append(xs * cos + partner * sin_signed)
    return outs[0] if len(outs) == 1 else jnp.concatenate(outs, axis=-1)


def _qkv_kernel(*refs, layer, rope, cache_out):
    h_ref, mod_ref, g_ref, w_hbm, gain_ref = refs[:5]
    w_ref, stage_ref, sem_ref = refs[-3:]
    pos = 5
    if rope:
        cos_ref, sin_ref = refs[pos:pos + 2]
        pos += 2
    q_ref, kt_ref, v_ref = refs[pos:pos + 3]
    pos += 3

    @pl.when(pl.program_id(0) == 0)
    def _load_weights():
        _load_by_cols(w_hbm, layer, w_ref, stage_ref, sem_ref)

    xn = _adaln(h_ref[...], g_ref[...], mod_ref, 1).astype(BF16)
    y = _dot(xn, w_ref[...])
    o = 0
    qa = _head_rms(y[:, o:o + NA_W], gain_ref[:, :NA_W]); o += NA_W
    ka = _head_rms(y[:, o:o + NA_W], gain_ref[:, NA_W:2 * NA_W]); o += NA_W
    va = y[:, o:o + NA_W]; o += NA_W
    qb = _head_rms(y[:, o:o + GQ_W], gain_ref[:, 2 * NA_W:2 * NA_W + GQ_W]); o += GQ_W
    kb = _head_rms(y[:, o:o + GKV_W], gain_ref[:, 2 * NA_W + GQ_W:]); o += GKV_W
    vb = y[:, o:o + GKV_W]
    if rope:
        cos, sin = cos_ref[...], sin_ref[...]
        qb = _rope(qb, cos, sin)
        kb = _rope(kb, cos, sin)
    q_ref[:, :NA_W] = (qa * Q_SCALE).astype(BF16)
    q_ref[:, NA_W:] = (qb * Q_SCALE).astype(BF16)
    low = _low_half()
    vb_swapped = pltpu.roll(vb, HEAD_DIM, 1)
    v_ref[:, :NA_W] = va.astype(BF16)
    v_ref[:, NA_W:NA_W + LANES] = jnp.where(low, vb, vb_swapped).astype(BF16)
    v_ref[:, NA_W + LANES:] = jnp.where(low, vb_swapped, vb).astype(BF16)
    n_seq, _, s = kt_ref.shape
    for i in range(n_seq):
        rows = slice(i * s, (i + 1) * s)
        ka_t = ka[rows].T
        kb_t = kb[rows].T
        k0, k1 = kb_t[:HEAD_DIM], kb_t[HEAD_DIM:]
        kt_ref[i, :NA_W, :] = ka_t.astype(BF16)
        kt_ref[i, NA_W:, :] = jnp.concatenate([k0, k0, k1, k1], axis=0).astype(BF16)
        if cache_out:
            kat_ref, vat_ref, kbt_ref, vbt_ref = refs[pos:pos + 4]
            kat_ref[i] = ka_t
            vat_ref[i] = va[rows].T
            kbt_ref[i] = kb_t
            vbt_ref[i] = vb[rows].T


def _qkv(h, row0, t, mod, mod_layer, cond0, g, w_in, layer, gains, cond_div, seq, rope_tabs=None,
         cache_out=False):
    tm = PROJ_TM
    rope = rope_tabs is not None
    assert not (rope and cache_out)
    row = lambda i: (i, 0)
    const = lambda i: (0, 0)
    in_specs = [
        pl.BlockSpec((tm, D_MODEL), lambda i: (i + row0 // tm, 0)),
        _mod_spec(mod_layer, lambda i: cond0 + i // cond_div),
        _norm_spec(mod_layer, 1),
        HBM,
        pl.BlockSpec((1, 2 * NA_W + GQ_W + GKV_W), const),
    ]
    args = [h, mod, g, w_in, gains]
    if rope:
        seq_tiles = rope_tabs[0].shape[0] // tm
        in_specs += [pl.BlockSpec((tm, 2 * HEAD_DIM), lambda i: (i % seq_tiles, 0))] * 2
        args += list(rope_tabs)
    if seq >= tm:
        per_seq = seq // tm
        t_block = lambda rows: pl.BlockSpec((1, rows, tm), lambda i: (i // per_seq, 0, i % per_seq))
    else:
        t_block = lambda rows: pl.BlockSpec((tm // seq, rows, seq), lambda i: (i, 0, 0))
    t_shape = lambda rows, dtype: jax.ShapeDtypeStruct((t // seq, rows, seq), dtype)
    out_specs = [pl.BlockSpec((tm, NA_W + GQ_W), row), t_block(KT_ROWS), pl.BlockSpec((tm, KT_ROWS), row)]
    out_shape = [jax.ShapeDtypeStruct((t, NA_W + GQ_W), BF16), t_shape(KT_ROWS, BF16),
                 jax.ShapeDtypeStruct((t, KT_ROWS), BF16)]
    if cache_out:
        out_specs += [t_block(NA_W), t_block(NA_W), t_block(GKV_W), t_block(GKV_W)]
        out_shape += [t_shape(NA_W, F32), t_shape(NA_W, F32), t_shape(GKV_W, F32), t_shape(GKV_W, F32)]
    return pl.pallas_call(
        functools.partial(_qkv_kernel, layer=layer, rope=rope, cache_out=cache_out),
        grid=(t // tm,),
        in_specs=in_specs,
        out_specs=out_specs,
        out_shape=out_shape,
        scratch_shapes=[pltpu.VMEM((D_MODEL, QKV_WIDTH), BF16), *COL_STAGE],
        compiler_params=_params("arbitrary"),
        name="qkv_latent" if rope else "qkv_context",
    )(*args)


def _split_pair(q2):
    low = _low_half()
    zero = jnp.zeros((), q2.dtype)
    return jnp.where(low, q2, zero), jnp.where(low, zero, q2)


def _with_ones(v2):
    return jnp.concatenate([v2, jnp.ones(v2.shape, v2.dtype)], axis=1)


def _join(blocks, axis):
    return blocks[0] if len(blocks) == 1 else jnp.concatenate(blocks, axis=axis)


def _attend(problems):
    scores = [_dot(qm, _join(kts, 1)) for qm, kts, _, _ in problems]
    probs = []
    for s, (_, kts, biases, _) in zip(scores, problems):
        parts, start = [], 0
        for kt, b in zip(kts, biases):
            part = s[:, start:start + kt.shape[1]]
            parts.append(part if b is None else part + b)
            start += kt.shape[1]
        m = functools.reduce(jnp.maximum, [jnp.max(part, axis=-1, keepdims=True) for part in parts])
        probs.append(_join([jnp.exp(part - m).astype(BF16) for part in parts], 1))
    return [_dot(p, _join(vexts, 0)) for p, (_, _, _, vexts) in zip(probs, problems)]


def _merge_pair(oe_even, oe_odd):
    even = oe_even[:, :LANES] * (1.0 / oe_even[:, LANES:])
    odd = oe_odd[:, :LANES] * (1.0 / oe_odd[:, LANES:])
    return jnp.where(_low_half(), even, odd)


def _gqa_groups(q_ref, q_off, o_ref, o_off, keys_values):
    tq = q_ref.shape[0]
    slabs = lambda g, off: [slice(off + (2 * g + j) * LANES, off + (2 * g + j + 1) * LANES) for j in range(2)]
    problems = []
    for g in range(GQA_KV_HEADS):
        qm = jnp.concatenate([part for sl in slabs(g, q_off) for part in _split_pair(q_ref[:, sl])], axis=0)
        problems.append((qm, *keys_values(slice(g * LANES, (g + 1) * LANES))))
    for g, oe in enumerate(_attend(problems)):
        for j, sl in enumerate(slabs(g, o_off)):
            o_ref[:, sl] = _merge_pair(oe[2 * j * tq:(2 * j + 1) * tq],
                                       oe[(2 * j + 1) * tq:(2 * j + 2) * tq]).astype(o_ref.dtype)


def _ctx_attn_kernel(q_blk, kt_blk, v_blk, o_blk):
    n_seq, _, seq = kt_blk.shape
    for b in range(n_seq):
        rows = pl.ds(b * seq, seq)
        q_ref, v_ref, o_ref, kt_ref = q_blk.at[rows], v_blk.at[rows], o_blk.at[rows], kt_blk.at[b]
        for p in range(NA_HEADS // 2):
            sl = slice(p * LANES, (p + 1) * LANES)
            vext = _with_ones(v_ref[:, sl])
            oe = [_attend([(qm, [kt_ref[sl, :]], [None], [vext])])[0] for qm in _split_pair(q_ref[:, sl])]
            o_ref[:, sl] = _merge_pair(*oe).astype(o_ref.dtype)
        _gqa_groups(q_ref, NA_W, o_ref, NA_W,
                    lambda kv: ([kt_ref[NA_W + kv.start:NA_W + kv.stop, :]], [None],
                                [_with_ones(v_ref[:, NA_W + kv.start:NA_W + kv.stop])]))


def _ctx_attention(q, kt, v, seq):
    t = q.shape[0]
    n_seq = CTX_SEQ_PER_STEP
    row = lambda b: (b, 0)
    return pl.pallas_call(
        _ctx_attn_kernel,
        grid=(t // (n_seq * seq),),
        in_specs=[pl.BlockSpec((n_seq * seq, NA_W + GQ_W), row),
                  pl.BlockSpec((n_seq, KT_ROWS, seq), lambda b: (b, 0, 0)),
                  pl.BlockSpec((n_seq * seq, KT_ROWS), row)],
        out_specs=pl.BlockSpec((n_seq * seq, NA_W + GQ_W), row),
        out_shape=jax.ShapeDtypeStruct((t, NA_W + GQ_W), BF16),
        compiler_params=_params("arbitrary"),
        name="ctx_attention",
    )(q, kt, v)


def _na_kernel(q_ref, kt0_ref, kt1_ref, kt2_ref, v0_ref, v1_ref, v2_ref, ckt_ref, cv_ref, bias_ref, o_ref):
    for p in range(NA_HEADS // 2):
        sl = slice(p * LANES, (p + 1) * LANES)
        kt_loc = jnp.concatenate([kt0_ref[sl, :], kt1_ref[sl, :], kt2_ref[sl, :]], axis=1)
        v_loc = _with_ones(jnp.concatenate([v0_ref[:, sl], v1_ref[:, sl], v2_ref[:, sl]], axis=0))
        v_ctx = _with_ones(cv_ref[:, sl])
        oe = _attend([(qm, [kt_loc, ckt_ref[sl, :]], [bias_ref[2 * p + half], None], [v_loc, v_ctx])
                      for half, qm in enumerate(_split_pair(q_ref[:, sl]))])
        o_ref[:, sl] = _merge_pair(*oe).astype(o_ref.dtype)


def _na_bias(rpb, rows):
    qc = np.arange(GRID_W)
    kc = np.arange(GRID_W)
    ws = np.clip(qc - NA_WIN_C // 2, 0, GRID_W - NA_WIN_C)
    col_ok = (kc[None, :] >= ws[:, None]) & (kc[None, :] < ws[:, None] + NA_WIN_C)
    dc = np.clip(kc[None, :] - qc[:, None] + NA_WIN_C - 1, 0, 2 * NA_WIN_C - 2)
    pick = ((dc[:, :, None] == np.arange(2 * NA_WIN_C - 1)) & col_ok[:, :, None]).astype(np.float32)
    tiles = jnp.einsum("hrd,qkd->hrqk", rpb.astype(F32), pick, precision=lax.Precision.HIGHEST)
    tiles = jnp.where(col_ok, tiles, NEG_INF)
    tiles = jnp.concatenate([tiles, tiles], axis=-1)
    n_off = 2 * NA_WIN_R - 1
    return pl.pallas_call(
        functools.partial(_na_bias_kernel, rows=rows),
        grid=(NA_HEADS,),
        in_specs=[pl.BlockSpec((None, n_off, GRID_W, LANES), lambda h: (h, 0, 0, 0))],
        out_specs=pl.BlockSpec((3, None, NA_BAND_Q, NA_BAND_K), lambda h: (0, h, 0, 0)),
        out_shape=jax.ShapeDtypeStruct((3, NA_HEADS, NA_BAND_Q, NA_BAND_K), F32),
        compiler_params=_params("arbitrary"),
        name="na_bias",
    )(tiles)


def _na_bias_kernel(tiles_ref, o_ref, *, rows):
    low = _low_half()
    masked = jnp.full((GRID_W, LANES), NEG_INF, F32)
    nb = rows // NA_BAND_ROWS
    for t, band in enumerate((0, 1, nb - 1)):
        r0 = band * NA_BAND_ROWS
        k0 = int(np.clip(r0 - NA_WIN_R // 2, 0, rows - NA_BAND_KEY_ROWS))
        for ri in range(NA_BAND_ROWS):
            r = r0 + ri
            start = int(np.clip(r - NA_WIN_R // 2, 0, rows - NA_WIN_R))
            tile = lambda kr: tiles_ref[kr - r + NA_WIN_R - 1] if start <= kr < start + NA_WIN_R else masked
            for m in range(NA_BAND_KEY_ROWS // 2):
                kr = k0 + 2 * m
                o_ref[t, ri * GRID_W:(ri + 1) * GRID_W, m * LANES:(m + 1) * LANES] = jnp.where(
                    low, tile(kr), tile(kr + 1))


def _na_attention(q, kt, v, ctx_kt, ctx_v, bias, batch, seq):
    rows = seq // GRID_W
    nb = rows // NA_BAND_ROWS
    past = ctx_v.shape[1]
    assert (rows - NA_BAND_KEY_ROWS) % NA_BAND_ROWS == 0
    n_chunks = NA_BAND_KEY_ROWS // NA_BAND_ROWS
    band_type = lambda j: jnp.where(j == 0, 0, jnp.where(j == nb - 1, 2, 1))
    chunk0 = lambda j: jnp.clip(j - NA_WIN_R // 2 // NA_BAND_ROWS, 0, nb - n_chunks)
    kt_spec = lambda c: pl.BlockSpec((None, NA_W, NA_BAND_Q), lambda b, j: (b, 0, chunk0(j) + c))
    v_spec = lambda c: pl.BlockSpec((NA_BAND_Q, NA_W), lambda b, j: (b * nb + chunk0(j) + c, 0))
    return pl.pallas_call(
        _na_kernel,
        grid=(batch, nb),
        in_specs=[pl.BlockSpec((NA_BAND_Q, NA_W), lambda b, j: (b * nb + j, 0))]
                 + [kt_spec(c) for c in range(n_chunks)] + [v_spec(c) for c in range(n_chunks)]
                 + [pl.BlockSpec((None, NA_W, past), lambda b, j: (b, 0, 0)),
                    pl.BlockSpec((None, past, NA_W), lambda b, j: (b, 0, 0)),
                    pl.BlockSpec((None, NA_HEADS, NA_BAND_Q, NA_BAND_K), lambda b, j: (band_type(j), 0, 0, 0))],
        out_specs=pl.BlockSpec((NA_BAND_Q, NA_W), lambda b, j: (b * nb + j, 0)),
        out_shape=jax.ShapeDtypeStruct((batch * seq, NA_W), BF16),
        compiler_params=_params("arbitrary", "arbitrary"),
        name="na_attention",
    )(q, *([kt] * n_chunks), *([v] * n_chunks), ctx_kt, ctx_v, bias)


def _gqa_kernel(q_ref, kt_ref, v_ref, ckt_ref, cv_ref, o_ref):
    _gqa_groups(q_ref, 0, o_ref, 0,
                lambda kv: ([kt_ref[kv, :], ckt_ref[kv, :]], [None, None],
                            [_with_ones(v_ref[:, kv]), _with_ones(cv_ref[:, kv])]))


def _gqa_attention(q, kt, v, ctx_kt, ctx_v, batch, seq):
    tq = GQA_TQ
    nt = seq // tq
    past = ctx_v.shape[1]
    dup = 2 * GKV_W
    return pl.pallas_call(
        _gqa_kernel,
        grid=(batch, nt),
        in_specs=[
            pl.BlockSpec((tq, GQ_W), lambda b, i: (b * nt + i, 1)),
            pl.BlockSpec((None, dup, seq), lambda b, i: (b, NA_W // dup, 0)),
            pl.BlockSpec((seq, dup), lambda b, i: (b, NA_W // dup)),
            pl.BlockSpec((None, dup, past), lambda b, i: (b, 0, 0)),
            pl.BlockSpec((None, past, dup), lambda b, i: (b, 0, 0)),
        ],
        out_specs=pl.BlockSpec((tq, GQ_W), lambda b, i: (b * nt + i, 0)),
        out_shape=jax.ShapeDtypeStruct((batch * seq, GQ_W), BF16),
        compiler_params=_params("arbitrary", "arbitrary"),
        name="gqa_attention",
    )(q, kt, v, ctx_kt, ctx_v)


def _sgu_kernel(h_ref, mod_ref, g_ref, win_hbm, vg_ref, ws_ref, bs_ref, wout_hbm, o_ref,
                win_ref, wout_ref, stage_in, sem_in, stage_out, sem_out, gated_ref, y_ref, *, layer):
    tm = h_ref.shape[0]
    n_chunks = y_ref.shape[0]
    per_chunk = SGU_ACT_CHUNK // SGU_GW

    def tile(first):
        if first:
            cols = _WeightStream(_col_pairs(win_hbm, layer, win_ref), stage_in, sem_in)
            rows = _WeightStream(_row_pairs(wout_hbm, layer, wout_ref), stage_out, sem_out)
        h = h_ref[...]
        xn = _adaln(h, g_ref[...], mod_ref, 1).astype(BF16)
        ssq = jnp.zeros((tm, 1), F32)
        for c in range(n_chunks):
            if first:
                cols.take(SGU_ACT_CHUNK // W_STAGE)
            y = _dot(xn, win_ref[c])
            y = 0.5 * y * (1.0 + lax.erf(y * math.sqrt(0.5)))
            y_ref[c] = y
            if c >= n_chunks // 2:
                ssq = ssq + jnp.sum(y * y, axis=-1, keepdims=True)
            if first:
                rows.take(1)
        if first:
            rows.take()
        v_scale = lax.rsqrt(ssq * (1.0 / SGU_WIDTH) + EPS)
        for c in range(tm // SGU_CHUNK):
            rs = slice(c * SGU_CHUNK, (c + 1) * SGU_CHUNK)
            for g in range(SGU_GROUPS):
                cs = slice(g * SGU_GW, (g + 1) * SGU_GW)
                in_chunk = slice((g % per_chunk) * SGU_GW, (g % per_chunk + 1) * SGU_GW)
                u = y_ref[g // per_chunk, rs, in_chunk]
                v = y_ref[n_chunks // 2 + g // per_chunk, rs, in_chunk]
                vn = ((v * v_scale[rs]) * vg_ref[:, cs]).astype(BF16)
                sv = _dot(ws_ref[g].astype(BF16), vn) + bs_ref[:, g:g + 1]
                gated_ref[rs, cs] = (u * sv).astype(BF16)
        o_ref[...] = h + _gate(mod_ref, 1) * _dot(gated_ref[...], wout_ref[...])

    pl.when(pl.program_id(0) == 0)(lambda: tile(True))
    pl.when(pl.program_id(0) > 0)(lambda: tile(False))


def _sgu(h, mod, mod_layer, g, w_in, v_g, w_s, b_s_t, w_out, layer, ctx_rows, dec_seq):
    tm = SGU_TM
    n_act = 2 * SGU_WIDTH // SGU_ACT_CHUNK
    _, _, _, cond = _two_stream_maps(tm, ctx_rows, dec_seq)
    row_spec = pl.BlockSpec((tm, D_MODEL), lambda i: (i, 0))
    const = lambda i: (0, 0)
    return pl.pallas_call(
        functools.partial(_sgu_kernel, layer=layer),
        grid=(h.shape[0] // tm,),
        in_specs=[
            row_spec,
            _mod_spec(mod_layer, cond),
            _norm_spec(mod_layer, 1),
            HBM,
            pl.BlockSpec((1, SGU_WIDTH), const),
            pl.BlockSpec((None, SGU_GROUPS, SGU_CHUNK, SGU_CHUNK), lambda i: (layer, 0, 0, 0)),
            pl.BlockSpec((SGU_CHUNK, SGU_GROUPS), const),
            HBM,
        ],
        out_specs=row_spec,
        out_shape=jax.ShapeDtypeStruct(h.shape, F32),
        scratch_shapes=[
            pltpu.VMEM((n_act, D_MODEL, SGU_ACT_CHUNK), BF16),
            pltpu.VMEM((SGU_WIDTH, D_MODEL), BF16),
            *COL_STAGE, *ROW_STAGE,
            pltpu.VMEM((tm, SGU_WIDTH), BF16),
            pltpu.VMEM((n_act, tm, SGU_ACT_CHUNK), F32),
        ],
        compiler_params=_params("arbitrary"),
        name="sgu",
    )(h, mod, g, w_in, v_g, w_s, b_s_t, w_out)


def _rope_tables(seq):
    half = HEAD_DIM // 2
    t = jnp.arange(seq)
    freqs = ROPE_BASE ** (-jnp.arange(0, half, 2, dtype=F32) / half)
    def tab(pos):
        ang = pos.astype(F32)[:, None] * freqs[None, :]
        cos, sin = jnp.cos(ang), jnp.sin(ang)
        return jnp.concatenate([cos, cos], -1), jnp.concatenate([-sin, sin], -1)
    cr, sr = tab(t // GRID_W)
    cc, sc = tab(t % GRID_W)
    cos = jnp.concatenate([cr, cc], -1)
    sin = jnp.concatenate([sr, sc], -1)
    return jnp.tile(cos, (1, 2)), jnp.tile(sin, (1, 2))


def kernel(x_prompt, x_sample, cache_na_k, cache_na_v, cache_gqa_k, cache_gqa_v, c, c_ctx, norm_g, mod_w, mod_b, ffn1_w_gu, ffn1_w_down, ffn2_w_gu, ffn2_w_down, attn_w_in, attn_w_out, na_q_g, na_k_g, na_rpb, gqa_q_g, gqa_k_g, sgu_w_in, sgu_v_g, sgu_w_s, sgu_b_s, sgu_w_out):
    batch, seq, _ = x_prompt.shape
    dec_batch, dec_seq, _ = x_sample.shape
    past = cache_na_k.shape[2]
    ctx_rows, lat_rows = batch * seq, dec_batch * dec_seq
    h = (x_prompt.reshape(ctx_rows, D_MODEL), x_sample.reshape(lat_rows, D_MODEL))

    assert 1 + dec_batch <= 8
    cond8 = jnp.concatenate([c_ctx[None], c, jnp.zeros((8 - 1 - dec_batch, D_MODEL), F32)], axis=0)
    mod = _modulation(cond8, mod_w, mod_b).reshape(DEPTH, 8, N_MOD, D_MODEL)

    g = norm_g.reshape(DEPTH * 3, 1, D_MODEL)
    new_cache = None
    for layer in range(DEPTH):
        h = _ffn(h, mod, g, ffn1_w_gu, ffn1_w_down, layer, 0, ctx_rows, dec_seq)

        if layer % 2 == 0:
            e = layer // 2
            gains = jnp.concatenate([jnp.tile(na_q_g[e], NA_HEADS), jnp.tile(na_k_g[e], NA_HEADS),
                                     jnp.tile(gqa_q_g[e], GQA_HEADS), jnp.tile(gqa_k_g[e], GQA_KV_HEADS)])[None]
            q, kt, v, *new_cache = _qkv(h, 0, ctx_rows, mod, layer, 0, g, attn_w_in, e, gains,
                                        ctx_rows // PROJ_TM, seq, cache_out=True)
            o_ctx = _ctx_attention(q, kt, v, seq)
            q, kt, v = _qkv(h, ctx_rows, lat_rows, mod, layer, 1, g, attn_w_in, e, gains,
                            dec_seq // PROJ_TM, dec_seq, rope_tabs=_rope_tables(dec_seq))
            to_kt = lambda ck, rep: jnp.repeat(jnp.transpose(ck[:, e], (0, 2, 3, 1)), rep, axis=1).reshape(
                dec_batch, -1, past).astype(BF16)
            to_v = lambda cv, rep: jnp.repeat(cv[:, e], rep, axis=2).reshape(dec_batch, past, -1).astype(BF16)
            oa = _na_attention(q, kt, v, to_kt(cache_na_k, 1), to_v(cache_na_v, 1),
                               _na_bias(na_rpb[e], dec_seq // GRID_W), dec_batch, dec_seq)
            ob = _gqa_attention(q, kt, v, to_kt(cache_gqa_k, 2), to_v(cache_gqa_v, 2), dec_batch, dec_seq)
            attn = (o_ctx, oa, ob, attn_w_out, e)
        else:
            attn = None
            o = layer // 2
            h = _sgu(h, mod, layer, g, sgu_w_in, sgu_v_g[o][None], sgu_w_s, sgu_b_s[o].T, sgu_w_out,
                     o, ctx_rows, dec_seq)

        h = _ffn(h, mod, g, ffn2_w_gu, ffn2_w_down, layer, 2, ctx_rows, dec_seq, attn=attn,
                 split_out=layer == DEPTH - 1)
    hp, hs = h

    n_attn = (DEPTH + 1) // 2
    assert n_attn == 1
    from_t = lambda x: jnp.transpose(x.reshape(batch, n_attn, -1, HEAD_DIM, seq), (0, 1, 4, 2, 3))
    return (hp.reshape(batch, seq, D_MODEL), hs.reshape(dec_batch, dec_seq, D_MODEL),
            *(from_t(x) for x in new_cache))
```

```python
import functools
import math

import jax
import jax.numpy as jnp
import numpy as np
from jax import lax
from jax.experimental import pallas as pl
from jax.experimental.pallas import tpu as pltpu

D_MODEL = 1024
DEPTH = 2
GRID_W = 64
HEAD_DIM = 64
NA_HEADS = 8
NA_WIN_R = 8
NA_WIN_C = 16
GQA_HEADS = 8
GQA_KV_HEADS = 2
GQA_GROUP = GQA_HEADS // GQA_KV_HEADS
NA_W = NA_HEADS * HEAD_DIM
GQ_W = GQA_HEADS * HEAD_DIM
GKV_W = GQA_KV_HEADS * HEAD_DIM
QKV_WIDTH = 3 * NA_W + GQ_W + 2 * GKV_W
ROPE_BASE = 10000.0
SGU_CHUNK = 128
SGU_GROUPS = 8
SGU_WIDTH = 2 * D_MODEL
SGU_GW = SGU_WIDTH // SGU_GROUPS
FFN_HIDDEN = 2816
N_MOD = 9
EPS = 1e-6
NEG_INF = -1e30
Q_SCALE = HEAD_DIM ** -0.5

NA_BAND_ROWS = 4
NA_BAND_KEY_ROWS = 12
NA_BAND_Q = NA_BAND_ROWS * GRID_W
NA_BAND_K = NA_BAND_KEY_ROWS * GRID_W

FFN_TM = 512
FFN_ACT_CHUNK = 512
SGU_ACT_CHUNK = 512
SGU_TM = 512
MOD_TN = 4608
GQA_TQ = 256
CTX_SEQ_PER_STEP = 8
PROJ_TM = 1024
W_STAGE = 256

LANES = 128
KT_ROWS = NA_W + 2 * GKV_W
MXU_COLS = 256
VMEM_LIMIT = 56 * 1024 * 1024

BF16 = jnp.bfloat16
F32 = jnp.float32


def _dot(a, b):
    return jnp.dot(a, b, preferred_element_type=F32)


def _params(*sem):
    return pltpu.CompilerParams(dimension_semantics=sem, vmem_limit_bytes=VMEM_LIMIT)


def _adaln(x, g, mod_ref, j):
    shift = mod_ref[0, 3 * j:3 * j + 1, :]
    scale = mod_ref[0, 3 * j + 1:3 * j + 2, :]
    y = x * lax.rsqrt(jnp.mean(x * x, axis=-1, keepdims=True) + EPS)
    return (y * g) * (1.0 + scale) + shift


def _gate(mod_ref, j):
    return mod_ref[0, 3 * j + 2:3 * j + 3, :]


def _mod_kernel(c_ref, w_ref, b_ref, o_ref):
    c = c_ref[...]
    a = (c * jax.nn.sigmoid(c)).astype(BF16)
    o_ref[...] = _dot(a, w_ref[...].astype(BF16)) + b_ref[...]


def _modulation(cond8, mod_w, mod_b):
    tn = MOD_TN
    n = N_MOD * D_MODEL
    return pl.pallas_call(
        _mod_kernel,
        grid=(DEPTH, n // tn),
        in_specs=[
            pl.BlockSpec((8, D_MODEL), lambda l, k: (0, 0)),
            pl.BlockSpec((None, D_MODEL, tn), lambda l, k: (l, 0, k)),
            pl.BlockSpec((None, 1, tn), lambda l, k: (l, 0, k)),
        ],
        out_specs=pl.BlockSpec((None, 8, tn), lambda l, k: (l, 0, k)),
        out_shape=jax.ShapeDtypeStruct((DEPTH, 8, n), F32),
        compiler_params=_params("arbitrary", "arbitrary"),
        name="modulation",
    )(cond8, mod_w, mod_b.reshape(DEPTH, 1, n))


def _two_stream_maps(tm, n_ctx_rows, dec_seq):
    n_ctx = n_ctx_rows // tm
    per_seq = dec_seq // tm
    ctx_map = lambda i: (jnp.minimum(i, n_ctx - 1), 0)
    lat_map = lambda i: (jnp.maximum(i - n_ctx, 0), 0)
    cond = lambda i: jnp.where(i < n_ctx, 0, 1 + (i - n_ctx) // per_seq)
    return n_ctx, ctx_map, lat_map, cond


def _norm_spec(layer, j):
    return pl.BlockSpec((None, 1, D_MODEL), lambda i: (3 * layer + j, 0, 0))


def _mod_spec(layer, cond):
    return pl.BlockSpec((None, 1, N_MOD, D_MODEL), lambda i: (layer, cond(i), 0, 0))


class _WeightStream:
    def __init__(self, pairs, stage_ref, sem_ref):
        self.pairs, self.stage, self.sem, self.done = pairs, stage_ref, sem_ref, 0
        for k in range(min(2, len(pairs))):
            self._copy(k).start()

    def _copy(self, k):
        return pltpu.make_async_copy(self.pairs[k][0], self.stage.at[k % 2], self.sem.at[k % 2])

    def take(self, n=None):
        n = len(self.pairs) - self.done if n is None else min(n, len(self.pairs) - self.done)
        for _ in range(n):
            k = self.done
            self._copy(k).wait()
            dst, idx = self.pairs[k][1]
            dst[idx] = self.stage[k % 2].astype(BF16)
            if k + 2 < len(self.pairs):
                self._copy(k + 2).start()
            self.done += 1


def _col_pairs(w_hbm, layer, w_ref, cols=None):
    if len(w_ref.shape) == 2:
        dst = lambda c: (slice(None), slice(c, c + W_STAGE))
        n = w_ref.shape[1]
    else:
        width = w_ref.shape[2]
        dst = lambda c: (c // width, slice(None), slice(c % width, c % width + W_STAGE))
        n = w_ref.shape[0] * width
    cols = range(0, n, W_STAGE) if cols is None else cols
    return [(w_hbm.at[layer, :, pl.ds(c, W_STAGE)], (w_ref, dst(c))) for c in cols]


def _row_pairs(w_hbm, layer, w_ref):
    return [(w_hbm.at[layer, pl.ds(r, W_STAGE), :], (w_ref, (slice(r, r + W_STAGE), slice(None))))
            for r in range(0, w_ref.shape[0], W_STAGE)]


def _load_by_cols(w_hbm, layer, w_ref, stage_ref, sem_ref):
    _WeightStream(_col_pairs(w_hbm, layer, w_ref), stage_ref, sem_ref).take()


COL_STAGE = (pltpu.VMEM((2, D_MODEL, W_STAGE), F32), pltpu.SemaphoreType.DMA((2,)))
ROW_STAGE = (pltpu.VMEM((2, W_STAGE, D_MODEL), F32), pltpu.SemaphoreType.DMA((2,)))
HBM = pl.BlockSpec(memory_space=pl.ANY)


def _ffn_kernel(*refs, layer, j, n_ctx, proj_layer, split_in, split_out):
    h_refs = refs[:2 if split_in else 1]
    pos = len(h_refs)
    mod_ref, g_ref, wgu_hbm, wd_hbm = refs[pos:pos + 4]
    pos += 4
    if proj_layer is not None:
        octx_ref, oa_ref, ob_ref, wo_hbm = refs[pos:pos + 4]
        pos += 4
    out_refs = refs[pos:pos + (2 if split_out else 1)]
    pos += len(out_refs)
    wgu_ref, wd_ref, stage_gu, sem_gu, stage_d, sem_d, act_ref = refs[pos:pos + 7]
    wo_ref = refs[pos + 7] if proj_layer is not None else None
    i = pl.program_id(0)
    is_ctx = i < n_ctx
    chunks = [(c, min(FFN_ACT_CHUNK, FFN_HIDDEN - c)) for c in range(0, FFN_HIDDEN, FFN_ACT_CHUNK)]

    def tile(first):
        if first:
            need = [col for c, w in chunks for base in (c, FFN_HIDDEN + c) for col in range(base, base + w, W_STAGE)]
            cols = _WeightStream(_col_pairs(wgu_hbm, layer, wgu_ref, need), stage_gu, sem_gu)
            proj = _row_pairs(wo_hbm, proj_layer, wo_ref) if proj_layer is not None else []
            rows = _WeightStream(proj + _row_pairs(wd_hbm, layer, wd_ref), stage_d, sem_d)
            rows.take(len(proj))
        if first or not split_in:
            h = h_refs[0][...]
        else:
            h = jnp.where(is_ctx, h_refs[0][...], h_refs[1][...])
        if proj_layer is not None:
            latent_heads = jnp.concatenate([oa_ref[...], ob_ref[...]], axis=1)
            heads = octx_ref[...] if first else jnp.where(is_ctx, octx_ref[...], latent_heads)
            h = h + _gate(mod_ref, 1) * _dot(heads, wo_ref[...])
        xn = _adaln(h, g_ref[...], mod_ref, j).astype(BF16)
        for c, w in chunks:
            if first:
                cols.take(2 * w // W_STAGE)
            gate = _dot(xn, wgu_ref[:, c:c + w])
            up = _dot(xn, wgu_ref[:, FFN_HIDDEN + c:FFN_HIDDEN + c + w])
            act_ref[:, c:c + w] = ((gate * jax.nn.sigmoid(gate)) * up).astype(BF16)
            if first:
                rows.take(2)
        if first:
            rows.take()
        res = h + (0.5 * _gate(mod_ref, j)) * _dot(act_ref[...], wd_ref[...])
        if first or not split_out:
            out_refs[0][...] = res
        else:
            @pl.when(is_ctx)
            def _():
                out_refs[0][...] = res

            @pl.when(jnp.logical_not(is_ctx))
            def _():
                out_refs[1][...] = res

    pl.when(i == 0)(lambda: tile(True))
    pl.when(i > 0)(lambda: tile(False))


def _ffn(h, mod, g, w_gu, w_down, layer, j, ctx_rows, dec_seq, attn=None, split_out=False):
    tm = FFN_TM
    split_in = isinstance(h, tuple)
    h_arrays = list(h) if split_in else [h]
    total_rows = sum(a.shape[0] for a in h_arrays)
    n_ctx, ctx_map, lat_map, cond = _two_stream_maps(tm, ctx_rows, dec_seq)
    assert n_ctx >= 1
    row_map = lambda i: (i, 0)
    row_spec = lambda m, width=D_MODEL: pl.BlockSpec((tm, width), m)
    in_specs = ([row_spec(ctx_map), row_spec(lat_map)] if split_in else [row_spec(row_map)]) + [
        _mod_spec(layer, cond),
        _norm_spec(layer, j),
        HBM, HBM]
    args = [*h_arrays, mod, g, w_gu, w_down]
    scratch = [pltpu.VMEM((D_MODEL, 2 * FFN_HIDDEN), BF16), pltpu.VMEM((FFN_HIDDEN, D_MODEL), BF16),
               *COL_STAGE, *ROW_STAGE, pltpu.VMEM((tm, FFN_HIDDEN), BF16)]
    proj_layer = None
    if attn is not None:
        o_ctx, oa, ob, w_out, proj_layer = attn
        in_specs += [row_spec(ctx_map, o_ctx.shape[1]), row_spec(lat_map, oa.shape[1]),
                     row_spec(lat_map, ob.shape[1]), HBM]
        args += [o_ctx, oa, ob, w_out]
        scratch.append(pltpu.VMEM((o_ctx.shape[1], D_MODEL), BF16))
    if split_out:
        out_specs = [row_spec(ctx_map), row_spec(lat_map)]
        out_shape = [jax.ShapeDtypeStruct((ctx_rows, D_MODEL), F32),
                     jax.ShapeDtypeStruct((total_rows - ctx_rows, D_MODEL), F32)]
    else:
        out_specs = row_spec(row_map)
        out_shape = jax.ShapeDtypeStruct((total_rows, D_MODEL), F32)
    return pl.pallas_call(
        functools.partial(_ffn_kernel, layer=layer, j=j, n_ctx=n_ctx, proj_layer=proj_layer,
                          split_in=split_in, split_out=split_out),
        grid=(total_rows // tm,),
        in_specs=in_specs,
        out_specs=out_specs,
        out_shape=out_shape,
        scratch_shapes=scratch,
        compiler_params=_params("arbitrary"),
        name=f"ffn{j}_l{layer}",
    )(*args)


def _head_mean_sq(x):
    w = x.shape[-1]
    r = lax.broadcasted_iota(jnp.int32, (MXU_COLS, MXU_COLS), 0) // HEAD_DIM
    c = lax.broadcasted_iota(jnp.int32, (MXU_COLS, MXU_COLS), 1) // HEAD_DIM
    bd = jnp.where(r == c, 1.0 / HEAD_DIM, 0.0).astype(BF16)
    sq = (x * x).astype(BF16)
    parts = []
    for s in range(0, w, MXU_COLS):
        e = min(s + MXU_COLS, w)
        parts.append(_dot(sq[:, s:e], bd[:e - s, :e - s]))
    return parts[0] if len(parts) == 1 else jnp.concatenate(parts, axis=-1)


def _low_half():
    return lax.broadcasted_iota(jnp.int32, (1, LANES), 1) < HEAD_DIM


def _head_rms(x, gain):
    return (x * lax.rsqrt(_head_mean_sq(x) + EPS)) * gain


def _rope(x, cos, sin_signed):
    lanes = cos.shape[-1]
    outs = []
    for s in range(0, x.shape[-1], lanes):
        xs = x[:, s:s + lanes]
        blk = lax.broadcasted_iota(jnp.int32, xs.shape, 1) // (HEAD_DIM // 4)
        partner = jnp.where(blk % 2 == 0,
                            pltpu.roll(xs, lanes - HEAD_DIM // 4, 1),
                            pltpu.roll(xs, HEAD_DIM // 4, 1))
        outs.append(xs * cos + partner * sin_signed)
    return outs[0] if len(outs) == 1 else jnp.concatenate(outs, axis=-1)


def _qkv_kernel(*refs, layer, rope, cache_out):
    h_ref, mod_ref, g_ref, w_hbm, gain_ref = refs[:5]
    w_ref, stage_ref, sem_ref = refs[-3:]
    pos = 5
    if rope:
        cos_ref, sin_ref = refs[pos:pos + 2]
        pos += 2
    q_ref, kt_ref, v_ref = refs[pos:pos + 3]
    pos += 3

    @pl.when(pl.program_id(0) == 0)
    def _load_weights():
        _load_by_cols(w_hbm, layer, w_ref, stage_ref, sem_ref)

    xn = _adaln(h_ref[...], g_ref[...], mod_ref, 1).astype(BF16)
    y = _dot(xn, w_ref[...])
    o = 0
    qa = _head_rms(y[:, o:o + NA_W], gain_ref[:, :NA_W]); o += NA_W
    ka = _head_rms(y[:, o:o + NA_W], gain_ref[:, NA_W:2 * NA_W]); o += NA_W
    va = y[:, o:o + NA_W]; o += NA_W
    qb = _head_rms(y[:, o:o + GQ_W], gain_ref[:, 2 * NA_W:2 * NA_W + GQ_W]); o += GQ_W
    kb = _head_rms(y[:, o:o + GKV_W], gain_ref[:, 2 * NA_W + GQ_W:]); o += GKV_W
    vb = y[:, o:o + GKV_W]
    if rope:
        cos, sin = cos_ref[...], sin_ref[...]
        qb = _rope(qb, cos, sin)
        kb = _rope(kb, cos, sin)
    q_ref[:, :NA_W] = (qa * Q_SCALE).astype(BF16)
    q_ref[:, NA_W:] = (qb * Q_SCALE).astype(BF16)
    low = _low_half()
    vb_swapped = pltpu.roll(vb, HEAD_DIM, 1)
    v_ref[:, :NA_W] = va.astype(BF16)
    v_ref[:, NA_W:NA_W + LANES] = jnp.where(low, vb, vb_swapped).astype(BF16)
    v_ref[:, NA_W + LANES:] = jnp.where(low, vb_swapped, vb).astype(BF16)
    n_seq, _, s = kt_ref.shape
    for i in range(n_seq):
        rows = slice(i * s, (i + 1) * s)
        ka_t = ka[rows].T
        kb_t = kb[rows].T
        k0, k1 = kb_t[:HEAD_DIM], kb_t[HEAD_DIM:]
        kt_ref[i, :NA_W, :] = ka_t.astype(BF16)
        kt_ref[i, NA_W:, :] = jnp.concatenate([k0, k0, k1, k1], axis=0).astype(BF16)
        if cache_out:
            kat_ref, vat_ref, kbt_ref, vbt_ref = refs[pos:pos + 4]
            kat_ref[i] = ka_t
            vat_ref[i] = va[rows].T
            kbt_ref[i] = kb_t
            vbt_ref[i] = vb[rows].T


def _qkv(h, row0, t, mod, mod_layer, cond0, g, w_in, layer, gains, cond_div, seq, rope_tabs=None,
         cache_out=False):
    tm = PROJ_TM
    rope = rope_tabs is not None
    assert not (rope and cache_out)
    row = lambda i: (i, 0)
    const = lambda i: (0, 0)
    in_specs = [
        pl.BlockSpec((tm, D_MODEL), lambda i: (i + row0 // tm, 0)),
        _mod_spec(mod_layer, lambda i: cond0 + i // cond_div),
        _norm_spec(mod_layer, 1),
        HBM,
        pl.BlockSpec((1, 2 * NA_W + GQ_W + GKV_W), const),
    ]
    args = [h, mod, g, w_in, gains]
    if rope:
        seq_tiles = rope_tabs[0].shape[0] // tm
        in_specs += [pl.BlockSpec((tm, 2 * HEAD_DIM), lambda i: (i % seq_tiles, 0))] * 2
        args += list(rope_tabs)
    if seq >= tm:
        per_seq = seq // tm
        t_block = lambda rows: pl.BlockSpec((1, rows, tm), lambda i: (i // per_seq, 0, i % per_seq))
    else:
        t_block = lambda rows: pl.BlockSpec((tm // seq, rows, seq), lambda i: (i, 0, 0))
    t_shape = lambda rows, dtype: jax.ShapeDtypeStruct((t // seq, rows, seq), dtype)
    out_specs = [pl.BlockSpec((tm, NA_W + GQ_W), row), t_block(KT_ROWS), pl.BlockSpec((tm, KT_ROWS), row)]
    out_shape = [jax.ShapeDtypeStruct((t, NA_W + GQ_W), BF16), t_shape(KT_ROWS, BF16),
                 jax.ShapeDtypeStruct((t, KT_ROWS), BF16)]
    if cache_out:
        out_specs += [t_block(NA_W), t_block(NA_W), t_block(GKV_W), t_block(GKV_W)]
        out_shape += [t_shape(NA_W, F32), t_shape(NA_W, F32), t_shape(GKV_W, F32), t_shape(GKV_W, F32)]
    return pl.pallas_call(
        functools.partial(_qkv_kernel, layer=layer, rope=rope, cache_out=cache_out),
        grid=(t // tm,),
        in_specs=in_specs,
        out_specs=out_specs,
        out_shape=out_shape,
        scratch_shapes=[pltpu.VMEM((D_MODEL, QKV_WIDTH), BF16), *COL_STAGE],
        compiler_params=_params("arbitrary"),
        name="qkv_latent" if rope else "qkv_context",
    )(*args)


def _split_pair(q2):
    low = _low_half()
    zero = jnp.zeros((), q2.dtype)
    return jnp.where(low, q2, zero), jnp.where(low, zero, q2)


def _with_ones(v2):
    return jnp.concatenate([v2, jnp.ones(v2.shape, v2.dtype)], axis=1)


def _join(blocks, axis):
    return blocks[0] if len(blocks) == 1 else jnp.concatenate(blocks, axis=axis)


def _attend(problems):
    scores = [_dot(qm, _join(kts, 1)) for qm, kts, _, _ in problems]
    probs = []
    for s, (_, kts, biases, _) in zip(scores, problems):
        parts, start = [], 0
        for kt, b in zip(kts, biases):
            part = s[:, start:start + kt.shape[1]]
            parts.append(part if b is None else part + b)
            start += kt.shape[1]
        m = functools.reduce(jnp.maximum, [jnp.max(part, axis=-1, keepdims=True) for part in parts])
        probs.append(_join([jnp.exp(part - m).astype(BF16) for part in parts], 1))
    return [_dot(p, _join(vexts, 0)) for p, (_, _, _, vexts) in zip(probs, problems)]


def _merge_pair(oe_even, oe_odd):
    even = oe_even[:, :LANES] * (1.0 / oe_even[:, LANES:])
    odd = oe_odd[:, :LANES] * (1.0 / oe_odd[:, LANES:])
    return jnp.where(_low_half(), even, odd)


def _gqa_groups(q_ref, q_off, o_ref, o_off, keys_values):
    tq = q_ref.shape[0]
    slabs = lambda g, off: [slice(off + (2 * g + j) * LANES, off + (2 * g + j + 1) * LANES) for j in range(2)]
    problems = []
    for g in range(GQA_KV_HEADS):
        qm = jnp.concatenate([part for sl in slabs(g, q_off) for part in _split_pair(q_ref[:, sl])], axis=0)
        problems.append((qm, *keys_values(slice(g * LANES, (g + 1) * LANES))))
    for g, oe in enumerate(_attend(problems)):
        for j, sl in enumerate(slabs(g, o_off)):
            o_ref[:, sl] = _merge_pair(oe[2 * j * tq:(2 * j + 1) * tq],
                                       oe[(2 * j + 1) * tq:(2 * j + 2) * tq]).astype(o_ref.dtype)


def _ctx_attn_kernel(q_blk, kt_blk, v_blk, o_blk):
    n_seq, _, seq = kt_blk.shape
    for b in range(n_seq):
        rows = pl.ds(b * seq, seq)
        q_ref, v_ref, o_ref, kt_ref = q_blk.at[rows], v_blk.at[rows], o_blk.at[rows], kt_blk.at[b]
        for p in range(NA_HEADS // 2):
            sl = slice(p * LANES, (p + 1) * LANES)
            vext = _with_ones(v_ref[:, sl])
            oe = [_attend([(qm, [kt_ref[sl, :]], [None], [vext])])[0] for qm in _split_pair(q_ref[:, sl])]
            o_ref[:, sl] = _merge_pair(*oe).astype(o_ref.dtype)
        _gqa_groups(q_ref, NA_W, o_ref, NA_W,
                    lambda kv: ([kt_ref[NA_W + kv.start:NA_W + kv.stop, :]], [None],
                                [_with_ones(v_ref[:, NA_W + kv.start:NA_W + kv.stop])]))


def _ctx_attention(q, kt, v, seq):
    t = q.shape[0]
    n_seq = CTX_SEQ_PER_STEP
    row = lambda b: (b, 0)
    return pl.pallas_call(
        _ctx_attn_kernel,
        grid=(t // (n_seq * seq),),
        in_specs=[pl.BlockSpec((n_seq * seq, NA_W + GQ_W), row),
                  pl.BlockSpec((n_seq, KT_ROWS, seq), lambda b: (b, 0, 0)),
                  pl.BlockSpec((n_seq * seq, KT_ROWS), row)],
        out_specs=pl.BlockSpec((n_seq * seq, NA_W + GQ_W), row),
        out_shape=jax.ShapeDtypeStruct((t, NA_W + GQ_W), BF16),
        compiler_params=_params("arbitrary"),
        name="ctx_attention",
    )(q, kt, v)


def _na_kernel(q_ref, kt0_ref, kt1_ref, kt2_ref, v0_ref, v1_ref, v2_ref, ckt_ref, cv_ref, bias_ref, o_ref):
    for p in range(NA_HEADS // 2):
        sl = slice(p * LANES, (p + 1) * LANES)
        kt_loc = jnp.concatenate([kt0_ref[sl, :], kt1_ref[sl, :], kt2_ref[sl, :]], axis=1)
        v_loc = _with_ones(jnp.concatenate([v0_ref[:, sl], v1_ref[:, sl], v2_ref[:, sl]], axis=0))
        v_ctx = _with_ones(cv_ref[:, sl])
        oe = _attend([(qm, [kt_loc, ckt_ref[sl, :]], [bias_ref[2 * p + half], None], [v_loc, v_ctx])
                      for half, qm in enumerate(_split_pair(q_ref[:, sl]))])
        o_ref[:, sl] = _merge_pair(*oe).astype(o_ref.dtype)


def _na_bias(rpb, rows):
    qc = np.arange(GRID_W)
    kc = np.arange(GRID_W)
    ws = np.clip(qc - NA_WIN_C // 2, 0, GRID_W - NA_WIN_C)
    col_ok = (kc[None, :] >= ws[:, None]) & (kc[None, :] < ws[:, None] + NA_WIN_C)
    dc = np.clip(kc[None, :] - qc[:, None] + NA_WIN_C - 1, 0, 2 * NA_WIN_C - 2)
    pick = ((dc[:, :, None] == np.arange(2 * NA_WIN_C - 1)) & col_ok[:, :, None]).astype(np.float32)
    tiles = jnp.einsum("hrd,qkd->hrqk", rpb.astype(F32), pick, precision=lax.Precision.HIGHEST)
    tiles = jnp.where(col_ok, tiles, NEG_INF)
    tiles = jnp.concatenate([tiles, tiles], axis=-1)
    n_off = 2 * NA_WIN_R - 1
    return pl.pallas_call(
        functools.partial(_na_bias_kernel, rows=rows),
        grid=(NA_HEADS,),
        in_specs=[pl.BlockSpec((None, n_off, GRID_W, LANES), lambda h: (h, 0, 0, 0))],
        out_specs=pl.BlockSpec((3, None, NA_BAND_Q, NA_BAND_K), lambda h: (0, h, 0, 0)),
        out_shape=jax.ShapeDtypeStruct((3, NA_HEADS, NA_BAND_Q, NA_BAND_K), F32),
        compiler_params=_params("arbitrary"),
        name="na_bias",
    )(tiles)


def _na_bias_kernel(tiles_ref, o_ref, *, rows):
    low = _low_half()
    masked = jnp.full((GRID_W, LANES), NEG_INF, F32)
    nb = rows // NA_BAND_ROWS
    for t, band in enumerate((0, 1, nb - 1)):
        r0 = band * NA_BAND_ROWS
        k0 = int(np.clip(r0 - NA_WIN_R // 2, 0, rows - NA_BAND_KEY_ROWS))
        for ri in range(NA_BAND_ROWS):
            r = r0 + ri
            start = int(np.clip(r - NA_WIN_R // 2, 0, rows - NA_WIN_R))
            tile = lambda kr: tiles_ref[kr - r + NA_WIN_R - 1] if start <= kr < start + NA_WIN_R else masked
            for m in range(NA_BAND_KEY_ROWS // 2):
                kr = k0 + 2 * m
                o_ref[t, ri * GRID_W:(ri + 1) * GRID_W, m * LANES:(m + 1) * LANES] = jnp.where(
                    low, tile(kr), tile(kr + 1))


def _na_attention(q, kt, v, ctx_kt, ctx_v, bias, batch, seq):
    rows = seq // GRID_W
    nb = rows // NA_BAND_ROWS
    past = ctx_v.shape[1]
    assert (rows - NA_BAND_KEY_ROWS) % NA_BAND_ROWS == 0
    n_chunks = NA_BAND_KEY_ROWS // NA_BAND_ROWS
    band_type = lambda j: jnp.where(j == 0, 0, jnp.where(j == nb - 1, 2, 1))
    chunk0 = lambda j: jnp.clip(j - NA_WIN_R // 2 // NA_BAND_ROWS, 0, nb - n_chunks)
    kt_spec = lambda c: pl.BlockSpec((None, NA_W, NA_BAND_Q), lambda b, j: (b, 0, chunk0(j) + c))
    v_spec = lambda c: pl.BlockSpec((NA_BAND_Q, NA_W), lambda b, j: (b * nb + chunk0(j) + c, 0))
    return pl.pallas_call(
        _na_kernel,
        grid=(batch, nb),
        in_specs=[pl.BlockSpec((NA_BAND_Q, NA_W), lambda b, j: (b * nb + j, 0))]
                 + [kt_spec(c) for c in range(n_chunks)] + [v_spec(c) for c in range(n_chunks)]
                 + [pl.BlockSpec((None, NA_W, past), lambda b, j: (b, 0, 0)),
                    pl.BlockSpec((None, past, NA_W), lambda b, j: (b, 0, 0)),
                    pl.BlockSpec((None, NA_HEADS, NA_BAND_Q, NA_BAND_K), lambda b, j: (band_type(j), 0, 0, 0))],
        out_specs=pl.BlockSpec((NA_BAND_Q, NA_W), lambda b, j: (b * nb + j, 0)),
        out_shape=jax.ShapeDtypeStruct((batch * seq, NA_W), BF16),
        compiler_params=_params("arbitrary", "arbitrary"),
        name="na_attention",
    )(q, *([kt] * n_chunks), *([v] * n_chunks), ctx_kt, ctx_v, bias)


def _gqa_kernel(q_ref, kt_ref, v_ref, ckt_ref, cv_ref, o_ref):
    _gqa_groups(q_ref, 0, o_ref, 0,
                lambda kv: ([kt_ref[kv, :], ckt_ref[kv, :]], [None, None],
                            [_with_ones(v_ref[:, kv]), _with_ones(cv_ref[:, kv])]))


def _gqa_attention(q, kt, v, ctx_kt, ctx_v, batch, seq):
    tq = GQA_TQ
    nt = seq // tq
    past = ctx_v.shape[1]
    dup = 2 * GKV_W
    return pl.pallas_call(
        _gqa_kernel,
        grid=(batch, nt),
        in_specs=[
            pl.BlockSpec((tq, GQ_W), lambda b, i: (b * nt + i, 1)),
            pl.BlockSpec((None, dup, seq), lambda b, i: (b, NA_W // dup, 0)),
            pl.BlockSpec((seq, dup), lambda b, i: (b, NA_W // dup)),
            pl.BlockSpec((None, dup, past), lambda b, i: (b, 0, 0)),
            pl.BlockSpec((None, past, dup), lambda b, i: (b, 0, 0)),
        ],
        out_specs=pl.BlockSpec((tq, GQ_W), lambda b, i: (b * nt + i, 0)),
        out_shape=jax.ShapeDtypeStruct((batch * seq, GQ_W), BF16),
        compiler_params=_params("arbitrary", "arbitrary"),
        name="gqa_attention",
    )(q, kt, v, ctx_kt, ctx_v)


def _sgu_kernel(h_ref, mod_ref, g_ref, win_hbm, vg_ref, ws_ref, bs_ref, wout_hbm, o_ref,
                win_ref, wout_ref, stage_in, sem_in, stage_out, sem_out, gated_ref, y_ref, *, layer):
    tm = h_ref.shape[0]
    n_chunks = y_ref.shape[0]
    per_chunk = SGU_ACT_CHUNK // SGU_GW

    def tile(first):
        if first:
            cols = _WeightStream(_col_pairs(win_hbm, layer, win_ref), stage_in, sem_in)
            rows = _WeightStream(_row_pairs(wout_hbm, layer, wout_ref), stage_out, sem_out)
        h = h_ref[...]
        xn = _adaln(h, g_ref[...], mod_ref, 1).astype(BF16)
        ssq = jnp.zeros((tm, 1), F32)
        for c in range(n_chunks):
            if first:
                cols.take(SGU_ACT_CHUNK // W_STAGE)
            y = _dot(xn, win_ref[c])
            y = 0.5 * y * (1.0 + lax.erf(y * math.sqrt(0.5)))
            y_ref[c] = y
            if c >= n_chunks // 2:
                ssq = ssq + jnp.sum(y * y, axis=-1, keepdims=True)
            if first:
                rows.take(1)
        if first:
            rows.take()
        v_scale = lax.rsqrt(ssq * (1.0 / SGU_WIDTH) + EPS)
        for c in range(tm // SGU_CHUNK):
            rs = slice(c * SGU_CHUNK, (c + 1) * SGU_CHUNK)
            for g in range(SGU_GROUPS):
                cs = slice(g * SGU_GW, (g + 1) * SGU_GW)
                in_chunk = slice((g % per_chunk) * SGU_GW, (g % per_chunk + 1) * SGU_GW)
                u = y_ref[g // per_chunk, rs, in_chunk]
                v = y_ref[n_chunks // 2 + g // per_chunk, rs, in_chunk]
                vn = ((v * v_scale[rs]) * vg_ref[:, cs]).astype(BF16)
                sv = _dot(ws_ref[g].astype(BF16), vn) + bs_ref[:, g:g + 1]
                gated_ref[rs, cs] = (u * sv).astype(BF16)
        o_ref[...] = h + _gate(mod_ref, 1) * _dot(gated_ref[...], wout_ref[...])

    pl.when(pl.program_id(0) == 0)(lambda: tile(True))
    pl.when(pl.program_id(0) > 0)(lambda: tile(False))


def _sgu(h, mod, mod_layer, g, w_in, v_g, w_s, b_s_t, w_out, layer, ctx_rows, dec_seq):
    tm = SGU_TM
    n_act = 2 * SGU_WIDTH // SGU_ACT_CHUNK
    _, _, _, cond = _two_stream_maps(tm, ctx_rows, dec_seq)
    row_spec = pl.BlockSpec((tm, D_MODEL), lambda i: (i, 0))
    const = lambda i: (0, 0)
    return pl.pallas_call(
        functools.partial(_sgu_kernel, layer=layer),
        grid=(h.shape[0] // tm,),
        in_specs=[
            row_spec,
            _mod_spec(mod_layer, cond),
            _norm_spec(mod_layer, 1),
            HBM,
            pl.BlockSpec((1, SGU_WIDTH), const),
            pl.BlockSpec((None, SGU_GROUPS, SGU_CHUNK, SGU_CHUNK), lambda i: (layer, 0, 0, 0)),
            pl.BlockSpec((SGU_CHUNK, SGU_GROUPS), const),
            HBM,
        ],
        out_specs=row_spec,
        out_shape=jax.ShapeDtypeStruct(h.shape, F32),
        scratch_shapes=[
            pltpu.VMEM((n_act, D_MODEL, SGU_ACT_CHUNK), BF16),
            pltpu.VMEM((SGU_WIDTH, D_MODEL), BF16),
            *COL_STAGE, *ROW_STAGE,
            pltpu.VMEM((tm, SGU_WIDTH), BF16),
            pltpu.VMEM((n_act, tm, SGU_ACT_CHUNK), F32),
        ],
        compiler_params=_params("arbitrary"),
        name="sgu",
    )(h, mod, g, w_in, v_g, w_s, b_s_t, w_out)


def _rope_tables(seq):
    half = HEAD_DIM // 2
    t = jnp.arange(seq)
    freqs = ROPE_BASE ** (-jnp.arange(0, half, 2, dtype=F32) / half)
    def tab(pos):
        ang = pos.astype(F32)[:, None] * freqs[None, :]
        cos, sin = jnp.cos(ang), jnp.sin(ang)
        return jnp.concatenate([cos, cos], -1), jnp.concatenate([-sin, sin], -1)
    cr, sr = tab(t // GRID_W)
    cc, sc = tab(t % GRID_W)
    cos = jnp.concatenate([cr, cc], -1)
    sin = jnp.concatenate([sr, sc], -1)
    return jnp.tile(cos, (1, 2)), jnp.tile(sin, (1, 2))


def kernel(x_prompt, x_sample, cache_na_k, cache_na_v, cache_gqa_k, cache_gqa_v, c, c_ctx, norm_g, mod_w, mod_b, ffn1_w_gu, ffn1_w_down, ffn2_w_gu, ffn2_w_down, attn_w_in, attn_w_out, na_q_g, na_k_g, na_rpb, gqa_q_g, gqa_k_g, sgu_w_in, sgu_v_g, sgu_w_s, sgu_b_s, sgu_w_out):
    batch, seq, _ = x_prompt.shape
    dec_batch, dec_seq, _ = x_sample.shape
    past = cache_na_k.shape[2]
    ctx_rows, lat_rows = batch * seq, dec_batch * dec_seq
    h = (x_prompt.reshape(ctx_rows, D_MODEL), x_sample.reshape(lat_rows, D_MODEL))

    assert 1 + dec_batch <= 8
    cond8 = jnp.concatenate([c_ctx[None], c, jnp.zeros((8 - 1 - dec_batch, D_MODEL), F32)], axis=0)
    mod = _modulation(cond8, mod_w, mod_b).reshape(DEPTH, 8, N_MOD, D_MODEL)

    g = norm_g.reshape(DEPTH * 3, 1, D_MODEL)
    new_cache = None
    for layer in range(DEPTH):
        h = _ffn(h, mod, g, ffn1_w_gu, ffn1_w_down, layer, 0, ctx_rows, dec_seq)

        if layer % 2 == 0:
            e = layer // 2
            gains = jnp.concatenate([jnp.tile(na_q_g[e], NA_HEADS), jnp.tile(na_k_g[e], NA_HEADS),
                                     jnp.tile(gqa_q_g[e], GQA_HEADS), jnp.tile(gqa_k_g[e], GQA_KV_HEADS)])[None]
            q, kt, v, *new_cache = _qkv(h, 0, ctx_rows, mod, layer, 0, g, attn_w_in, e, gains,
                                        ctx_rows // PROJ_TM, seq, cache_out=True)
            o_ctx = _ctx_attention(q, kt, v, seq)
            q, kt, v = _qkv(h, ctx_rows, lat_rows, mod, layer, 1, g, attn_w_in, e, gains,
                            dec_seq // PROJ_TM, dec_seq, rope_tabs=_rope_tables(dec_seq))
            to_kt = lambda ck, rep: jnp.repeat(jnp.transpose(ck[:, e], (0, 2, 3, 1)), rep, axis=1).reshape(
                dec_batch, -1, past).astype(BF16)
            to_v = lambda cv, rep: jnp.repeat(cv[:, e], rep, axis=2).reshape(dec_batch, past, -1).astype(BF16)
            oa = _na_attention(q, kt, v, to_kt(cache_na_k, 1), to_v(cache_na_v, 1),
                               _na_bias(na_rpb[e], dec_seq // GRID_W), dec_batch, dec_seq)
            ob = _gqa_attention(q, kt, v, to_kt(cache_gqa_k, 2), to_v(cache_gqa_v, 2), dec_batch, dec_seq)
            attn = (o_ctx, oa, ob, attn_w_out, e)
        else:
            attn = None
            o = layer // 2
            h = _sgu(h, mod, layer, g, sgu_w_in, sgu_v_g[o][None], sgu_w_s, sgu_b_s[o].T, sgu_w_out,
                     o, ctx_rows, dec_seq)

        h = _ffn(h, mod, g, ffn2_w_gu, ffn2_w_down, layer, 2, ctx_rows, dec_seq, attn=attn,
                 split_out=layer == DEPTH - 1)
    hp, hs = h

    n_attn = (DEPTH + 1) // 2
    assert n_attn == 1
    from_t = lambda x: jnp.transpose(x.reshape(batch, n_attn, -1, HEAD_DIM, seq), (0, 1, 4, 2, 3))
    return (hp.reshape(batch, seq, D_MODEL), hs.reshape(dec_batch, dec_seq, D_MODEL),
            *(from_t(x) for x in new_cache))
```

```python
import functools
import math

import jax
import jax.numpy as jnp
import numpy as np
from jax import lax
from jax.experimental import pallas as pl
from jax.experimental.pallas import tpu as pltpu

D_MODEL = 1024
DEPTH = 2
GRID_W = 64
HEAD_DIM = 64
NA_HEADS = 8
NA_WIN_R = 8
NA_WIN_C = 16
GQA_HEADS = 8
GQA_KV_HEADS = 2
GQA_GROUP = GQA_HEADS // GQA_KV_HEADS
NA_W = NA_HEADS * HEAD_DIM
GQ_W = GQA_HEADS * HEAD_DIM
GKV_W = GQA_KV_HEADS * HEAD_DIM
QKV_WIDTH = 3 * NA_W + GQ_W + 2 * GKV_W
ROPE_BASE = 10000.0
SGU_CHUNK = 128
SGU_GROUPS = 8
SGU_WIDTH = 2 * D_MODEL
SGU_GW = SGU_WIDTH // SGU_GROUPS
FFN_HIDDEN = 2816
N_MOD = 9
EPS = 1e-6
NEG_INF = -1e30
Q_SCALE = HEAD_DIM ** -0.5

NA_BAND_ROWS = 4
NA_BAND_KEY_ROWS = 12
NA_BAND_Q = NA_BAND_ROWS * GRID_W
NA_BAND_K = NA_BAND_KEY_ROWS * GRID_W

FFN_TM = 512
FFN_ACT_CHUNK = 512
SGU_ACT_CHUNK = 512
SGU_TM = 512
MOD_TN = 3072
CTX_SEQ_PER_STEP = 4
PROJ_TM = 1024
W_STAGE = 256

LANES = 128
KT_ROWS = NA_W + 2 * GKV_W
MXU_COLS = 256
VMEM_LIMIT = 56 * 1024 * 1024

BF16 = jnp.bfloat16
F32 = jnp.float32


def _dot(a, b):
    return jnp.dot(a, b, preferred_element_type=F32)


def _params(*sem):
    return pltpu.CompilerParams(dimension_semantics=sem, vmem_limit_bytes=VMEM_LIMIT)


def _adaln(x, g, mod_ref, j):
    shift = mod_ref[0, 3 * j:3 * j + 1, :]
    scale = mod_ref[0, 3 * j + 1:3 * j + 2, :]
    y = x * lax.rsqrt(jnp.mean(x * x, axis=-1, keepdims=True) + EPS)
    return (y * g) * (1.0 + scale) + shift


def _gate(mod_ref, j):
    return mod_ref[0, 3 * j + 2:3 * j + 3, :]


def _mod_kernel(c_ref, w_ref, b_ref, o_ref):
    c = c_ref[...]
    a = (c * jax.nn.sigmoid(c)).astype(BF16)
    o_ref[...] = _dot(a, w_ref[...].astype(BF16)) + b_ref[...]


def _modulation(cond8, mod_w, mod_b):
    tn = MOD_TN
    n = N_MOD * D_MODEL
    return pl.pallas_call(
        _mod_kernel,
        grid=(DEPTH, n // tn),
        in_specs=[
            pl.BlockSpec((8, D_MODEL), lambda l, k: (0, 0)),
            pl.BlockSpec((None, D_MODEL, tn), lambda l, k: (l, 0, k)),
            pl.BlockSpec((None, 1, tn), lambda l, k: (l, 0, k)),
        ],
        out_specs=pl.BlockSpec((None, 8, tn), lambda l, k: (l, 0, k)),
        out_shape=jax.ShapeDtypeStruct((DEPTH, 8, n), F32),
        compiler_params=_params("arbitrary", "arbitrary"),
        name="modulation",
    )(cond8, mod_w, mod_b.reshape(DEPTH, 1, n))


def _two_stream_maps(tm, n_ctx_rows, dec_seq):
    n_ctx = n_ctx_rows // tm
    per_seq = dec_seq // tm
    ctx_map = lambda i: (jnp.minimum(i, n_ctx - 1), 0)
    lat_map = lambda i: (jnp.maximum(i - n_ctx, 0), 0)
    cond = lambda i: jnp.where(i < n_ctx, 0, 1 + (i - n_ctx) // per_seq)
    return n_ctx, ctx_map, lat_map, cond


def _norm_spec(layer, j):
    return pl.BlockSpec((None, 1, D_MODEL), lambda i: (3 * layer + j, 0, 0))


def _mod_spec(layer, cond):
    return pl.BlockSpec((None, 1, N_MOD, D_MODEL), lambda i: (layer, cond(i), 0, 0))


class _WeightStream:
    def __init__(self, pairs, stage_ref, sem_ref):
        self.pairs, self.stage, self.sem, self.done = pairs, stage_ref, sem_ref, 0
        for k in range(min(2, len(pairs))):
            self._copy(k).start()

    def _copy(self, k):
        return pltpu.make_async_copy(self.pairs[k][0], self.stage.at[k % 2], self.sem.at[k % 2])

    def take(self, n=None):
        n = len(self.pairs) - self.done if n is None else min(n, len(self.pairs) - self.done)
        for _ in range(n):
            k = self.done
            self._copy(k).wait()
            dst, idx = self.pairs[k][1]
            dst[idx] = self.stage[k % 2].astype(BF16)
            if k + 2 < len(self.pairs):
                self._copy(k + 2).start()
            self.done += 1


def _col_pairs(w_hbm, layer, w_ref, cols=None):
    if len(w_ref.shape) == 2:
        dst = lambda c: (slice(None), slice(c, c + W_STAGE))
        n = w_ref.shape[1]
    else:
        width = w_ref.shape[2]
        dst = lambda c: (c // width, slice(None), slice(c % width, c % width + W_STAGE))
        n = w_ref.shape[0] * width
    cols = range(0, n, W_STAGE) if cols is None else cols
    return [(w_hbm.at[layer, :, pl.ds(c, W_STAGE)], (w_ref, dst(c))) for c in cols]


def _row_pairs(w_hbm, layer, w_ref):
    return [(w_hbm.at[layer, pl.ds(r, W_STAGE), :], (w_ref, (slice(r, r + W_STAGE), slice(None))))
            for r in range(0, w_ref.shape[0], W_STAGE)]


def _load_by_cols(w_hbm, layer, w_ref, stage_ref, sem_ref):
    _WeightStream(_col_pairs(w_hbm, layer, w_ref), stage_ref, sem_ref).take()


COL_STAGE = (pltpu.VMEM((2, D_MODEL, W_STAGE), F32), pltpu.SemaphoreType.DMA((2,)))
ROW_STAGE = (pltpu.VMEM((2, W_STAGE, D_MODEL), F32), pltpu.SemaphoreType.DMA((2,)))
HBM = pl.BlockSpec(memory_space=pl.ANY)


def _ffn_kernel(*refs, layer, j, n_ctx, proj_layer, split_in, split_out):
    h_refs = refs[:2 if split_in else 1]
    pos = len(h_refs)
    mod_ref, g_ref, wgu_hbm, wd_hbm = refs[pos:pos + 4]
    pos += 4
    if proj_layer is not None:
        octx_ref, oa_ref, ob_ref, wo_hbm = refs[pos:pos + 4]
        pos += 4
    out_refs = refs[pos:pos + (2 if split_out else 1)]
    pos += len(out_refs)
    wgu_ref, wd_ref, stage_gu, sem_gu, stage_d, sem_d, act_ref = refs[pos:pos + 7]
    wo_ref = refs[pos + 7] if proj_layer is not None else None
    i = pl.program_id(0)
    is_ctx = i < n_ctx
    chunks = [(c, min(FFN_ACT_CHUNK, FFN_HIDDEN - c)) for c in range(0, FFN_HIDDEN, FFN_ACT_CHUNK)]

    def tile(first):
        if first:
            need = [col for c, w in chunks for base in (c, FFN_HIDDEN + c) for col in range(base, base + w, W_STAGE)]
            cols = _WeightStream(_col_pairs(wgu_hbm, layer, wgu_ref, need), stage_gu, sem_gu)
            proj = _row_pairs(wo_hbm, proj_layer, wo_ref) if proj_layer is not None else []
            rows = _WeightStream(proj + _row_pairs(wd_hbm, layer, wd_ref), stage_d, sem_d)
            rows.take(len(proj))
        if first or not split_in:
            h = h_refs[0][...]
        else:
            h = jnp.where(is_ctx, h_refs[0][...], h_refs[1][...])
        if proj_layer is not None:
            latent_heads = jnp.concatenate([oa_ref[...], ob_ref[...]], axis=1)
            heads = octx_ref[...] if first else jnp.where(is_ctx, octx_ref[...], latent_heads)
            h = h + _gate(mod_ref, 1) * _dot(heads, wo_ref[...])
        xn = _adaln(h, g_ref[...], mod_ref, j).astype(BF16)
        for c, w in chunks:
            if first:
                cols.take(2 * w // W_STAGE)
            gate = _dot(xn, wgu_ref[:, c:c + w])
            up = _dot(xn, wgu_ref[:, FFN_HIDDEN + c:FFN_HIDDEN + c + w])
            act_ref[:, c:c + w] = ((gate * jax.nn.sigmoid(gate)) * up).astype(BF16)
            if first:
                rows.take(2)
        if first:
            rows.take()
        res = h + (0.5 * _gate(mod_ref, j)) * _dot(act_ref[...], wd_ref[...])
        if first or not split_out:
            out_refs[0][...] = res
        else:
            @pl.when(is_ctx)
            def _():
                out_refs[0][...] = res

            @pl.when(jnp.logical_not(is_ctx))
            def _():
                out_refs[1][...] = res

    pl.when(i == 0)(lambda: tile(True))
    pl.when(i > 0)(lambda: tile(False))


def _ffn(h, mod, g, w_gu, w_down, layer, j, ctx_rows, dec_seq, attn=None, split_out=False):
    tm = FFN_TM
    split_in = isinstance(h, tuple)
    h_arrays = list(h) if split_in else [h]
    total_rows = sum(a.shape[0] for a in h_arrays)
    n_ctx, ctx_map, lat_map, cond = _two_stream_maps(tm, ctx_rows, dec_seq)
    assert n_ctx >= 1
    row_map = lambda i: (i, 0)
    row_spec = lambda m, width=D_MODEL: pl.BlockSpec((tm, width), m)
    in_specs = ([row_spec(ctx_map), row_spec(lat_map)] if split_in else [row_spec(row_map)]) + [
        _mod_spec(layer, cond),
        _norm_spec(layer, j),
        HBM, HBM]
    args = [*h_arrays, mod, g, w_gu, w_down]
    scratch = [pltpu.VMEM((D_MODEL, 2 * FFN_HIDDEN), BF16), pltpu.VMEM((FFN_HIDDEN, D_MODEL), BF16),
               *COL_STAGE, *ROW_STAGE, pltpu.VMEM((tm, FFN_HIDDEN), BF16)]
    proj_layer = None
    if attn is not None:
        o_ctx, oa, ob, w_out, proj_layer = attn
        in_specs += [row_spec(ctx_map, o_ctx.shape[1]), row_spec(lat_map, oa.shape[1]),
                     row_spec(lat_map, ob.shape[1]), HBM]
        args += [o_ctx, oa, ob, w_out]
        scratch.append(pltpu.VMEM((o_ctx.shape[1], D_MODEL), BF16))
    if split_out:
        out_specs = [row_spec(ctx_map), row_spec(lat_map)]
        out_shape = [jax.ShapeDtypeStruct((ctx_rows, D_MODEL), F32),
                     jax.ShapeDtypeStruct((total_rows - ctx_rows, D_MODEL), F32)]
    else:
        out_specs = row_spec(row_map)
        out_shape = jax.ShapeDtypeStruct((total_rows, D_MODEL), F32)
    return pl.pallas_call(
        functools.partial(_ffn_kernel, layer=layer, j=j, n_ctx=n_ctx, proj_layer=proj_layer,
                          split_in=split_in, split_out=split_out),
        grid=(total_rows // tm,),
        in_specs=in_specs,
        out_specs=out_specs,
        out_shape=out_shape,
        scratch_shapes=scratch,
        compiler_params=_params("arbitrary"),
        name=f"ffn{j}_l{layer}",
    )(*args)


def _head_mean_sq(x):
    w = x.shape[-1]
    r = lax.broadcasted_iota(jnp.int32, (MXU_COLS, MXU_COLS), 0) // HEAD_DIM
    c = lax.broadcasted_iota(jnp.int32, (MXU_COLS, MXU_COLS), 1) // HEAD_DIM
    bd = jnp.where(r == c, 1.0 / HEAD_DIM, 0.0).astype(BF16)
    sq = (x * x).astype(BF16)
    parts = []
    for s in range(0, w, MXU_COLS):
        e = min(s + MXU_COLS, w)
        parts.append(_dot(sq[:, s:e], bd[:e - s, :e - s]))
    return parts[0] if len(parts) == 1 else jnp.concatenate(parts, axis=-1)


def _low_half():
    return lax.broadcasted_iota(jnp.int32, (1, LANES), 1) < HEAD_DIM


def _head_rms(x, gain):
    return (x * lax.rsqrt(_head_mean_sq(x) + EPS)) * gain


def _rope(x, cos, sin_signed):
    lanes = cos.shape[-1]
    outs = []
    for s in range(0, x.shape[-1], lanes):
        xs = x[:, s:s + lanes]
        blk = lax.broadcasted_iota(jnp.int32, xs.shape, 1) // (HEAD_DIM // 4)
        partner = jnp.where(blk % 2 == 0,
                            pltpu.roll(xs, lanes - HEAD_DIM // 4, 1),
                            pltpu.roll(xs, HEAD_DIM // 4, 1))
        outs.append(xs * cos + partner * sin_signed)
    return outs[0] if len(outs) == 1 else jnp.concatenate(outs, axis=-1)


def _qkv_kernel(*refs, layer, rope, cache_out):
    h_ref, mod_ref, g_ref, w_hbm, gain_ref = refs[:5]
    w_ref, stage_ref, sem_ref = refs[-3:]
    pos = 5
    if rope:
        cos_ref, sin_ref = refs[pos:pos + 2]
        pos += 2
    q_ref, kt_ref, v_ref = refs[pos:pos + 3]
    pos += 3

    @pl.when(pl.program_id(0) == 0)
    def _load_weights():
        _load_by_cols(w_hbm, layer, w_ref, stage_ref, sem_ref)

    xn = _adaln(h_ref[...], g_ref[...], mod_ref, 1).astype(BF16)
    y = _dot(xn, w_ref[...])
    o = 0
    qa = _head_rms(y[:, o:o + NA_W], gain_ref[:, :NA_W]); o += NA_W
    ka = _head_rms(y[:, o:o + NA_W], gain_ref[:, NA_W:2 * NA_W]); o += NA_W
    va = y[:, o:o + NA_W]; o += NA_W
    qb = _head_rms(y[:, o:o + GQ_W], gain_ref[:, 2 * NA_W:2 * NA_W + GQ_W]); o += GQ_W
    kb = _head_rms(y[:, o:o + GKV_W], gain_ref[:, 2 * NA_W + GQ_W:]); o += GKV_W
    vb = y[:, o:o + GKV_W]
    if rope:
        cos, sin = cos_ref[...], sin_ref[...]
        qb = _rope(qb, cos, sin)
        kb = _rope(kb, cos, sin)
    q_ref[:, :NA_W] = (qa * Q_SCALE).astype(BF16)
    q_ref[:, NA_W:] = (qb * Q_SCALE).astype(BF16)
    low = _low_half()
    vb_swapped = pltpu.roll(vb, HEAD_DIM, 1)
    v_ref[:, :NA_W] = va.astype(BF16)
    v_ref[:, NA_W:NA_W + LANES] = jnp.where(low, vb, vb_swapped).astype(BF16)
    v_ref[:, NA_W + LANES:] = jnp.where(low, vb_swapped, vb).astype(BF16)
    n_seq, _, s = kt_ref.shape
    for i in range(n_seq):
        rows = slice(i * s, (i + 1) * s)
        ka_t = ka[rows].T
        kb_t = kb[rows].T
        k0, k1 = kb_t[:HEAD_DIM], kb_t[HEAD_DIM:]
        kt_ref[i, :NA_W, :] = ka_t.astype(BF16)
        kt_ref[i, NA_W:, :] = jnp.concatenate([k0, k0, k1, k1], axis=0).astype(BF16)
        if cache_out:
            kat_ref, vat_ref, kbt_ref, vbt_ref = refs[pos:pos + 4]
            kat_ref[i] = ka_t
            vat_ref[i] = va[rows].T
            kbt_ref[i] = kb_t
            vbt_ref[i] = vb[rows].T


def _qkv(h, row0, t, mod, mod_layer, cond0, g, w_in, layer, gains, cond_div, seq, rope_tabs=None,
         cache_out=False):
    tm = PROJ_TM
    rope = rope_tabs is not None
    assert not (rope and cache_out)
    row = lambda i: (i, 0)
    const = lambda i: (0, 0)
    in_specs = [
        pl.BlockSpec((tm, D_MODEL), lambda i: (i + row0 // tm, 0)),
        _mod_spec(mod_layer, lambda i: cond0 + i // cond_div),
        _norm_spec(mod_layer, 1),
        HBM,
        pl.BlockSpec((1, 2 * NA_W + GQ_W + GKV_W), const),
    ]
    args = [h, mod, g, w_in, gains]
    if rope:
        seq_tiles = rope_tabs[0].shape[0] // tm
        in_specs += [pl.BlockSpec((tm, 2 * HEAD_DIM), lambda i: (i % seq_tiles, 0))] * 2
        args += list(rope_tabs)
    if seq >= tm:
        per_seq = seq // tm
        t_block = lambda rows: pl.BlockSpec((1, rows, tm), lambda i: (i // per_seq, 0, i % per_seq))
    else:
        t_block = lambda rows: pl.BlockSpec((tm // seq, rows, seq), lambda i: (i, 0, 0))
    t_shape = lambda rows, dtype: jax.ShapeDtypeStruct((t // seq, rows, seq), dtype)
    out_specs = [pl.BlockSpec((tm, NA_W + GQ_W), row), t_block(KT_ROWS), pl.BlockSpec((tm, KT_ROWS), row)]
    out_shape = [jax.ShapeDtypeStruct((t, NA_W + GQ_W), BF16), t_shape(KT_ROWS, BF16),
                 jax.ShapeDtypeStruct((t, KT_ROWS), BF16)]
    if cache_out:
        out_specs += [t_block(NA_W), t_block(NA_W), t_block(GKV_W), t_block(GKV_W)]
        out_shape += [t_shape(NA_W, F32), t_shape(NA_W, F32), t_shape(GKV_W, F32), t_shape(GKV_W, F32)]
    return pl.pallas_call(
        functools.partial(_qkv_kernel, layer=layer, rope=rope, cache_out=cache_out),
        grid=(t // tm,),
        in_specs=in_specs,
        out_specs=out_specs,
        out_shape=out_shape,
        scratch_shapes=[pltpu.VMEM((D_MODEL, QKV_WIDTH), BF16), *COL_STAGE],
        compiler_params=_params("arbitrary"),
        name="qkv_latent" if rope else "qkv_context",
    )(*args)


def _split_pair(q2):
    low = _low_half()
    zero = jnp.zeros((), q2.dtype)
    return jnp.where(low, q2, zero), jnp.where(low, zero, q2)


def _with_ones(v2):
    return jnp.concatenate([v2, jnp.ones(v2.shape, v2.dtype)], axis=1)


def _join(blocks, axis):
    return blocks[0] if len(blocks) == 1 else jnp.concatenate(blocks, axis=axis)


def _attend(problems):
    scores = [_dot(qm, _join(kts, 1)) for qm, kts, _, _ in problems]
    probs = []
    for s, (_, kts, biases, _) in zip(scores, problems):
        parts, start = [], 0
        for kt, b in zip(kts, biases):
            part = s[:, start:start + kt.shape[1]]
            parts.append(part if b is None else part + b)
            start += kt.shape[1]
        m = functools.reduce(jnp.maximum, [jnp.max(part, axis=-1, keepdims=True) for part in parts])
        probs.append(_join([jnp.exp(part - m).astype(BF16) for part in parts], 1))
    return [_dot(p, _join(vexts, 0)) for p, (_, _, _, vexts) in zip(probs, problems)]


def _merge_pair(oe_even, oe_odd):
    even = oe_even[:, :LANES] * (1.0 / oe_even[:, LANES:])
    odd = oe_odd[:, :LANES] * (1.0 / oe_odd[:, LANES:])
    return jnp.where(_low_half(), even, odd)


def _gqa_groups(q_ref, q_off, o_ref, o_off, keys_values):
    tq = q_ref.shape[0]
    slabs = lambda g, off: [slice(off + (2 * g + j) * LANES, off + (2 * g + j + 1) * LANES) for j in range(2)]
    problems = []
    for g in range(GQA_KV_HEADS):
        qm = jnp.concatenate([part for sl in slabs(g, q_off) for part in _split_pair(q_ref[:, sl])], axis=0)
        problems.append((qm, *keys_values(slice(g * LANES, (g + 1) * LANES))))
    for g, oe in enumerate(_attend(problems)):
        for j, sl in enumerate(slabs(g, o_off)):
            o_ref[:, sl] = _merge_pair(oe[2 * j * tq:(2 * j + 1) * tq],
                                       oe[(2 * j + 1) * tq:(2 * j + 2) * tq]).astype(o_ref.dtype)


def _ctx_attn_kernel(q_blk, kt_blk, v_blk, o_blk):
    n_seq, _, seq = kt_blk.shape
    for b in range(n_seq):
        rows = pl.ds(b * seq, seq)
        q_ref, v_ref, o_ref, kt_ref = q_blk.at[rows], v_blk.at[rows], o_blk.at[rows], kt_blk.at[b]
        for p in range(NA_HEADS // 2):
            sl = slice(p * LANES, (p + 1) * LANES)
            vext = _with_ones(v_ref[:, sl])
            oe = [_attend([(qm, [kt_ref[sl, :]], [None], [vext])])[0] for qm in _split_pair(q_ref[:, sl])]
            o_ref[:, sl] = _merge_pair(*oe).astype(o_ref.dtype)
        _gqa_groups(q_ref, NA_W, o_ref, NA_W,
                    lambda kv: ([kt_ref[NA_W + kv.start:NA_W + kv.stop, :]], [None],
                                [_with_ones(v_ref[:, NA_W + kv.start:NA_W + kv.stop])]))


def _ctx_attention(q, kt, v, seq):
    t = q.shape[0]
    n_seq = CTX_SEQ_PER_STEP
    row = lambda b: (b, 0)
    return pl.pallas_call(
        _ctx_attn_kernel,
        grid=(t // (n_seq * seq),),
        in_specs=[pl.BlockSpec((n_seq * seq, NA_W + GQ_W), row),
                  pl.BlockSpec((n_seq, KT_ROWS, seq), lambda b: (b, 0, 0)),
                  pl.BlockSpec((n_seq * seq, KT_ROWS), row)],
        out_specs=pl.BlockSpec((n_seq * seq, NA_W + GQ_W), row),
        out_shape=jax.ShapeDtypeStruct((t, NA_W + GQ_W), BF16),
        compiler_params=_params("arbitrary"),
        name="ctx_attention",
    )(q, kt, v)


def _na_kernel(q_ref, kt0_ref, kt1_ref, kt2_ref, v0_ref, v1_ref, v2_ref, ckt_ref, cv_ref, bias_ref, o_ref):
    for p in range(NA_HEADS // 2):
        sl = slice(p * LANES, (p + 1) * LANES)
        kt_loc = jnp.concatenate([kt0_ref[sl, :], kt1_ref[sl, :], kt2_ref[sl, :]], axis=1)
        v_loc = _with_ones(jnp.concatenate([v0_ref[:, sl], v1_ref[:, sl], v2_ref[:, sl]], axis=0))
        v_ctx = _with_ones(cv_ref[:, sl])
        oe = _attend([(qm, [kt_loc, ckt_ref[sl, :]], [bias_ref[2 * p + half], None], [v_loc, v_ctx])
                      for half, qm in enumerate(_split_pair(q_ref[:, sl]))])
        o_ref[:, sl] = _merge_pair(*oe).astype(o_ref.dtype)


def _na_bias(rpb, rows):
    qc = np.arange(GRID_W)
    kc = np.arange(GRID_W)
    ws = np.clip(qc - NA_WIN_C // 2, 0, GRID_W - NA_WIN_C)
    col_ok = (kc[None, :] >= ws[:, None]) & (kc[None, :] < ws[:, None] + NA_WIN_C)
    dc = np.clip(kc[None, :] - qc[:, None] + NA_WIN_C - 1, 0, 2 * NA_WIN_C - 2)
    pick = ((dc[:, :, None] == np.arange(2 * NA_WIN_C - 1)) & col_ok[:, :, None]).astype(np.float32)
    tiles = jnp.einsum("hrd,qkd->hrqk", rpb.astype(F32), pick, precision=lax.Precision.HIGHEST)
    tiles = jnp.where(col_ok, tiles, NEG_INF)
    tiles = jnp.concatenate([tiles, tiles], axis=-1)
    n_off = 2 * NA_WIN_R - 1
    return pl.pallas_call(
        functools.partial(_na_bias_kernel, rows=rows),
        grid=(NA_HEADS,),
        in_specs=[pl.BlockSpec((None, n_off, GRID_W, LANES), lambda h: (h, 0, 0, 0))],
        out_specs=pl.BlockSpec((3, None, NA_BAND_Q, NA_BAND_K), lambda h: (0, h, 0, 0)),
        out_shape=jax.ShapeDtypeStruct((3, NA_HEADS, NA_BAND_Q, NA_BAND_K), F32),
        compiler_params=_params("arbitrary"),
        name="na_bias",
    )(tiles)


def _na_bias_kernel(tiles_ref, o_ref, *, rows):
    low = _low_half()
    masked = jnp.full((GRID_W, LANES), NEG_INF, F32)
    nb = rows // NA_BAND_ROWS
    for t, band in enumerate((0, 1, nb - 1)):
        r0 = band * NA_BAND_ROWS
        k0 = int(np.clip(r0 - NA_WIN_R // 2, 0, rows - NA_BAND_KEY_ROWS))
        for ri in range(NA_BAND_ROWS):
            r = r0 + ri
            start = int(np.clip(r - NA_WIN_R // 2, 0, rows - NA_WIN_R))
            tile = lambda kr: tiles_ref[kr - r + NA_WIN_R - 1] if start <= kr < start + NA_WIN_R else masked
            for m in range(NA_BAND_KEY_ROWS // 2):
                kr = k0 + 2 * m
                o_ref[t, ri * GRID_W:(ri + 1) * GRID_W, m * LANES:(m + 1) * LANES] = jnp.where(
                    low, tile(kr), tile(kr + 1))


def _gqa_kernel(q_ref, kt_ref, v_ref, ckt_ref, cv_ref, o_ref):
    _gqa_groups(q_ref, 0, o_ref, 0,
                lambda kv: ([kt_ref[kv, :], ckt_ref[kv, :]], [None, None],
                            [_with_ones(v_ref[:, kv]), _with_ones(cv_ref[:, kv])]))


def _latent_attn_kernel(*refs):
    n_na = 10
    n_gqa = 5
    oa_ref, ob_ref = refs[n_na + n_gqa:]
    _gqa_kernel(*refs[n_na:n_na + n_gqa], ob_ref)
    _na_kernel(*refs[:n_na], oa_ref)


def _latent_attention(q, kt, v, na_ctx_kt, na_ctx_v, bias, gqa_ctx_kt, gqa_ctx_v, batch, seq):
    rows = seq // GRID_W
    nb = rows // NA_BAND_ROWS
    past = na_ctx_v.shape[1]
    assert (rows - NA_BAND_KEY_ROWS) % NA_BAND_ROWS == 0
    n_chunks = NA_BAND_KEY_ROWS // NA_BAND_ROWS
    band_type = lambda j: jnp.where(j == 0, 0, jnp.where(j == nb - 1, 2, 1))
    chunk0 = lambda j: jnp.clip(j - NA_WIN_R // 2 // NA_BAND_ROWS, 0, nb - n_chunks)
    kt_spec = lambda c: pl.BlockSpec((None, NA_W, NA_BAND_Q), lambda b, j: (b, 0, chunk0(j) + c))
    v_spec = lambda c: pl.BlockSpec((NA_BAND_Q, NA_W), lambda b, j: (b * nb + chunk0(j) + c, 0))
    tile = lambda lane_block: pl.BlockSpec((NA_BAND_Q, NA_W), lambda b, j: (b * nb + j, lane_block))
    dup = 2 * GKV_W
    assert NA_W == GQ_W
    na_specs = ([tile(0)] + [kt_spec(c) for c in range(n_chunks)] + [v_spec(c) for c in range(n_chunks)]
                + [pl.BlockSpec((None, NA_W, past), lambda b, j: (b, 0, 0)),
                   pl.BlockSpec((None, past, NA_W), lambda b, j: (b, 0, 0)),
                   pl.BlockSpec((None, NA_HEADS, NA_BAND_Q, NA_BAND_K), lambda b, j: (band_type(j), 0, 0, 0))])
    gqa_specs = [tile(1),
                 pl.BlockSpec((None, dup, seq), lambda b, j: (b, NA_W // dup, 0)),
                 pl.BlockSpec((seq, dup), lambda b, j: (b, NA_W // dup)),
                 pl.BlockSpec((None, dup, past), lambda b, j: (b, 0, 0)),
                 pl.BlockSpec((None, past, dup), lambda b, j: (b, 0, 0))]
    return pl.pallas_call(
        _latent_attn_kernel,
        grid=(batch, nb),
        in_specs=na_specs + gqa_specs,
        out_specs=[tile(0), tile(0)],
        out_shape=[jax.ShapeDtypeStruct((batch * seq, NA_W), BF16), jax.ShapeDtypeStruct((batch * seq, GQ_W), BF16)],
        compiler_params=_params("arbitrary", "arbitrary"),
        name="latent_attention",
    )(q, *([kt] * n_chunks), *([v] * n_chunks), na_ctx_kt, na_ctx_v, bias, q, kt, v, gqa_ctx_kt, gqa_ctx_v)


def _sgu_kernel(h_ref, mod_ref, g_ref, win_hbm, vg_ref, ws_ref, bs_ref, wout_hbm, o_ref,
                win_ref, wout_ref, stage_in, sem_in, stage_out, sem_out, gated_ref, y_ref, *, layer):
    tm = h_ref.shape[0]
    n_chunks = y_ref.shape[0]
    per_chunk = SGU_ACT_CHUNK // SGU_GW

    def tile(first):
        if first:
            cols = _WeightStream(_col_pairs(win_hbm, layer, win_ref), stage_in, sem_in)
            rows = _WeightStream(_row_pairs(wout_hbm, layer, wout_ref), stage_out, sem_out)
        h = h_ref[...]
        xn = _adaln(h, g_ref[...], mod_ref, 1).astype(BF16)
        ssq = jnp.zeros((tm, 1), F32)
        for c in range(n_chunks):
            if first:
                cols.take(SGU_ACT_CHUNK // W_STAGE)
            y = _dot(xn, win_ref[c])
            y = 0.5 * y * (1.0 + lax.erf(y * math.sqrt(0.5)))
            y_ref[c] = y
            if c >= n_chunks // 2:
                ssq = ssq + jnp.sum(y * y, axis=-1, keepdims=True)
            if first:
                rows.take(1)
        if first:
            rows.take()
        v_scale = lax.rsqrt(ssq * (1.0 / SGU_WIDTH) + EPS)
        for c in range(tm // SGU_CHUNK):
            rs = slice(c * SGU_CHUNK, (c + 1) * SGU_CHUNK)
            for g in range(SGU_GROUPS):
                cs = slice(g * SGU_GW, (g + 1) * SGU_GW)
                in_chunk = slice((g % per_chunk) * SGU_GW, (g % per_chunk + 1) * SGU_GW)
                u = y_ref[g // per_chunk, rs, in_chunk]
                v = y_ref[n_chunks // 2 + g // per_chunk, rs, in_chunk]
                vn = ((v * v_scale[rs]) * vg_ref[:, cs]).astype(BF16)
                sv = _dot(ws_ref[g].astype(BF16), vn) + bs_ref[:, g:g + 1]
                gated_ref[rs, cs] = (u * sv).astype(BF16)
        o_ref[...] = h + _gate(mod_ref, 1) * _dot(gated_ref[...], wout_ref[...])

    pl.when(pl.program_id(0) == 0)(lambda: tile(True))
    pl.when(pl.program_id(0) > 0)(lambda: tile(False))


def _sgu(h, mod, mod_layer, g, w_in, v_g, w_s, b_s_t, w_out, layer, ctx_rows, dec_seq):
    tm = SGU_TM
    n_act = 2 * SGU_WIDTH // SGU_ACT_CHUNK
    _, _, _, cond = _two_stream_maps(tm, ctx_rows, dec_seq)
    row_spec = pl.BlockSpec((tm, D_MODEL), lambda i: (i, 0))
    const = lambda i: (0, 0)
    return pl.pallas_call(
        functools.partial(_sgu_kernel, layer=layer),
        grid=(h.shape[0] // tm,),
        in_specs=[
            row_spec,
            _mod_spec(mod_layer, cond),
            _norm_spec(mod_layer, 1),
            HBM,
            pl.BlockSpec((1, SGU_WIDTH), const),
            pl.BlockSpec((None, SGU_GROUPS, SGU_CHUNK, SGU_CHUNK), lambda i: (layer, 0, 0, 0)),
            pl.BlockSpec((SGU_CHUNK, SGU_GROUPS), const),
            HBM,
        ],
        out_specs=row_spec,
        out_shape=jax.ShapeDtypeStruct(h.shape, F32),
        scratch_shapes=[
            pltpu.VMEM((n_act, D_MODEL, SGU_ACT_CHUNK), BF16),
            pltpu.VMEM((SGU_WIDTH, D_MODEL), BF16),
            *COL_STAGE, *ROW_STAGE,
            pltpu.VMEM((tm, SGU_WIDTH), BF16),
            pltpu.VMEM((n_act, tm, SGU_ACT_CHUNK), F32),
        ],
        compiler_params=_params("arbitrary"),
        name="sgu",
    )(h, mod, g, w_in, v_g, w_s, b_s_t, w_out)


def _rope_tables(seq):
    half = HEAD_DIM // 2
    t = jnp.arange(seq)
    freqs = ROPE_BASE ** (-jnp.arange(0, half, 2, dtype=F32) / half)
    def tab(pos):
        ang = pos.astype(F32)[:, None] * freqs[None, :]
        cos, sin = jnp.cos(ang), jnp.sin(ang)
        return jnp.concatenate([cos, cos], -1), jnp.concatenate([-sin, sin], -1)
    cr, sr = tab(t // GRID_W)
    cc, sc = tab(t % GRID_W)
    cos = jnp.concatenate([cr, cc], -1)
    sin = jnp.concatenate([sr, sc], -1)
    return jnp.tile(cos, (1, 2)), jnp.tile(sin, (1, 2))


def kernel(x_prompt, x_sample, cache_na_k, cache_na_v, cache_gqa_k, cache_gqa_v, c, c_ctx, norm_g, mod_w, mod_b, ffn1_w_gu, ffn1_w_down, ffn2_w_gu, ffn2_w_down, attn_w_in, attn_w_out, na_q_g, na_k_g, na_rpb, gqa_q_g, gqa_k_g, sgu_w_in, sgu_v_g, sgu_w_s, sgu_b_s, sgu_w_out):
    batch, seq, _ = x_prompt.shape
    dec_batch, dec_seq, _ = x_sample.shape
    past = cache_na_k.shape[2]
    ctx_rows, lat_rows = batch * seq, dec_batch * dec_seq
    h = (x_prompt.reshape(ctx_rows, D_MODEL), x_sample.reshape(lat_rows, D_MODEL))

    assert 1 + dec_batch <= 8
    cond8 = jnp.concatenate([c_ctx[None], c, jnp.zeros((8 - 1 - dec_batch, D_MODEL), F32)], axis=0)
    mod = _modulation(cond8, mod_w, mod_b).reshape(DEPTH, 8, N_MOD, D_MODEL)

    g = norm_g.reshape(DEPTH * 3, 1, D_MODEL)
    new_cache = None
    for layer in range(DEPTH):
        h = _ffn(h, mod, g, ffn1_w_gu, ffn1_w_down, layer, 0, ctx_rows, dec_seq)

        if layer % 2 == 0:
            e = layer // 2
            gains = jnp.concatenate([jnp.tile(na_q_g[e], NA_HEADS), jnp.tile(na_k_g[e], NA_HEADS),
                                     jnp.tile(gqa_q_g[e], GQA_HEADS), jnp.tile(gqa_k_g[e], GQA_KV_HEADS)])[None]
            q, kt, v, *new_cache = _qkv(h, 0, ctx_rows, mod, layer, 0, g, attn_w_in, e, gains,
                                        ctx_rows // PROJ_TM, seq, cache_out=True)
            o_ctx = _ctx_attention(q, kt, v, seq)
            q, kt, v = _qkv(h, ctx_rows, lat_rows, mod, layer, 1, g, attn_w_in, e, gains,
                            dec_seq // PROJ_TM, dec_seq, rope_tabs=_rope_tables(dec_seq))
            to_kt = lambda ck, rep: jnp.repeat(jnp.transpose(ck[:, e], (0, 2, 3, 1)), rep, axis=1).reshape(
                dec_batch, -1, past).astype(BF16)
            to_v = lambda cv, rep: jnp.repeat(cv[:, e], rep, axis=2).reshape(dec_batch, past, -1).astype(BF16)
            oa, ob = _latent_attention(q, kt, v, to_kt(cache_na_k, 1), to_v(cache_na_v, 1),
                                       _na_bias(na_rpb[e], dec_seq // GRID_W),
                                       to_kt(cache_gqa_k, 2), to_v(cache_gqa_v, 2), dec_batch, dec_seq)
            attn = (o_ctx, oa, ob, attn_w_out, e)
        else:
            attn = None
            o = layer // 2
            h = _sgu(h, mod, layer, g, sgu_w_in, sgu_v_g[o][None], sgu_w_s, sgu_b_s[o].T, sgu_w_out,
                     o, ctx_rows, dec_seq)

        h = _ffn(h, mod, g, ffn2_w_gu, ffn2_w_down, layer, 2, ctx_rows, dec_seq, attn=attn,
                 split_out=layer == DEPTH - 1)
    hp, hs = h

    n_attn = (DEPTH + 1) // 2
    assert n_attn == 1
    from_t = lambda x: jnp.transpose(x.reshape(batch, n_attn, -1, HEAD_DIM, seq), (0, 1, 4, 2, 3))
    return (hp.reshape(batch, seq, D_MODEL), hs.reshape(dec_batch, dec_seq, D_MODEL),
            *(from_t(x) for x in new_cache))
```

```python
import functools
import math

import jax
import jax.numpy as jnp
import numpy as np
from jax import lax
from jax.experimental import pallas as pl
from jax.experimental.pallas import tpu as pltpu

D_MODEL = 1024
DEPTH = 2
GRID_W = 64
HEAD_DIM = 64
NA_HEADS = 8
NA_WIN_R = 8
NA_WIN_C = 16
GQA_HEADS = 8
GQA_KV_HEADS = 2
GQA_GROUP = GQA_HEADS // GQA_KV_HEADS
NA_W = NA_HEADS * HEAD_DIM
GQ_W = GQA_HEADS * HEAD_DIM
GKV_W = GQA_KV_HEADS * HEAD_DIM
QKV_WIDTH = 3 * NA_W + GQ_W + 2 * GKV_W
ROPE_BASE = 10000.0
SGU_CHUNK = 128
SGU_GROUPS = 8
SGU_WIDTH = 2 * D_MODEL
SGU_GW = SGU_WIDTH // SGU_GROUPS
FFN_HIDDEN = 2816
N_MOD = 9
EPS = 1e-6
NEG_INF = -1e30
Q_SCALE = HEAD_DIM ** -0.5

NA_BAND_ROWS = 4
NA_BAND_KEY_ROWS = 12
NA_BAND_Q = NA_BAND_ROWS * GRID_W
NA_BAND_K = NA_BAND_KEY_ROWS * GRID_W

FFN_TM = 512
FFN_ACT_CHUNK = 512
SGU_ACT_CHUNK = 512
SGU_TM = 512
MOD_TN = 3072
CTX_SEQ_PER_STEP = 4
PROJ_TM = 1024
W_STAGE = 256

LANES = 128
KT_ROWS = NA_W + 2 * GKV_W
MXU_COLS = 256
VMEM_LIMIT = 56 * 1024 * 1024

BF16 = jnp.bfloat16
F32 = jnp.float32


def _dot(a, b):
    return jnp.dot(a, b, preferred_element_type=F32)


def _params(*sem):
    return pltpu.CompilerParams(dimension_semantics=sem, vmem_limit_bytes=VMEM_LIMIT)


def _adaln(x, g, mod_ref, j):
    shift = mod_ref[0, 3 * j:3 * j + 1, :]
    scale = mod_ref[0, 3 * j + 1:3 * j + 2, :]
    y = x * lax.rsqrt(jnp.mean(x * x, axis=-1, keepdims=True) + EPS)
    return (y * g) * (1.0 + scale) + shift


def _gate(mod_ref, j):
    return mod_ref[0, 3 * j + 2:3 * j + 3, :]


def _mod_kernel(c_ref, w_ref, b_ref, o_ref):
    c = c_ref[...]
    a = (c * jax.nn.sigmoid(c)).astype(BF16)
    o_ref[...] = _dot(a, w_ref[...].astype(BF16)) + b_ref[...]


def _modulation(cond8, mod_w, mod_b):
    tn = MOD_TN
    n = N_MOD * D_MODEL
    return pl.pallas_call(
        _mod_kernel,
        grid=(DEPTH, n // tn),
        in_specs=[
            pl.BlockSpec((8, D_MODEL), lambda l, k: (0, 0)),
            pl.BlockSpec((None, D_MODEL, tn), lambda l, k: (l, 0, k)),
            pl.BlockSpec((None, 1, tn), lambda l, k: (l, 0, k)),
        ],
        out_specs=pl.BlockSpec((None, 8, tn), lambda l, k: (l, 0, k)),
        out_shape=jax.ShapeDtypeStruct((DEPTH, 8, n), F32),
        compiler_params=_params("arbitrary", "arbitrary"),
        name="modulation",
    )(cond8, mod_w, mod_b.reshape(DEPTH, 1, n))


def _two_stream_maps(tm, n_ctx_rows, dec_seq):
    n_ctx = n_ctx_rows // tm
    per_seq = dec_seq // tm
    ctx_map = lambda i: (jnp.minimum(i, n_ctx - 1), 0)
    lat_map = lambda i: (jnp.maximum(i - n_ctx, 0), 0)
    cond = lambda i: jnp.where(i < n_ctx, 0, 1 + (i - n_ctx) // per_seq)
    return n_ctx, ctx_map, lat_map, cond


def _norm_spec(layer, j):
    return pl.BlockSpec((None, 1, D_MODEL), lambda i: (3 * layer + j, 0, 0))


def _mod_spec(layer, cond):
    return pl.BlockSpec((None, 1, N_MOD, D_MODEL), lambda i: (layer, cond(i), 0, 0))


class _WeightStream:
    def __init__(self, pairs, stage_ref, sem_ref):
        self.pairs, self.stage, self.sem, self.done = pairs, stage_ref, sem_ref, 0
        for k in range(min(2, len(pairs))):
            self._copy(k).start()

    def _copy(self, k):
        return pltpu.make_async_copy(self.pairs[k][0], self.stage.at[k % 2], self.sem.at[k % 2])

    def take(self, n=None):
        n = len(self.pairs) - self.done if n is None else min(n, len(self.pairs) - self.done)
        for _ in range(n):
            k = self.done
            self._copy(k).wait()
            dst, idx = self.pairs[k][1]
            dst[idx] = self.stage[k % 2].astype(BF16)
            if k + 2 < len(self.pairs):
                self._copy(k + 2).start()
            self.done += 1


def _col_pairs(w_hbm, layer, w_ref, cols=None):
    if len(w_ref.shape) == 2:
        dst = lambda c: (slice(None), slice(c, c + W_STAGE))
        n = w_ref.shape[1]
    else:
        width = w_ref.shape[2]
        dst = lambda c: (c // width, slice(None), slice(c % width, c % width + W_STAGE))
        n = w_ref.shape[0] * width
    cols = range(0, n, W_STAGE) if cols is None else cols
    return [(w_hbm.at[layer, :, pl.ds(c, W_STAGE)], (w_ref, dst(c))) for c in cols]


def _row_pairs(w_hbm, layer, w_ref):
    return [(w_hbm.at[layer, pl.ds(r, W_STAGE), :], (w_ref, (slice(r, r + W_STAGE), slice(None))))
            for r in range(0, w_ref.shape[0], W_STAGE)]


def _load_by_cols(w_hbm, layer, w_ref, stage_ref, sem_ref):
    _WeightStream(_col_pairs(w_hbm, layer, w_ref), stage_ref, sem_ref).take()


COL_STAGE = (pltpu.VMEM((2, D_MODEL, W_STAGE), F32), pltpu.SemaphoreType.DMA((2,)))
ROW_STAGE = (pltpu.VMEM((2, W_STAGE, D_MODEL), F32), pltpu.SemaphoreType.DMA((2,)))
HBM = pl.BlockSpec(memory_space=pl.ANY)


def _ffn_kernel(*refs, layer, j, n_ctx, proj_layer, split_in, split_out):
    h_refs = refs[:2 if split_in else 1]
    pos = len(h_refs)
    mod_ref, g_ref, wgu_hbm, wd_hbm = refs[pos:pos + 4]
    pos += 4
    if proj_layer is not None:
        octx_ref, oa_ref, ob_ref, wo_hbm = refs[pos:pos + 4]
        pos += 4
    out_refs = refs[pos:pos + (2 if split_out else 1)]
    pos += len(out_refs)
    wgu_ref, wd_ref, stage_gu, sem_gu, stage_d, sem_d, act_ref = refs[pos:pos + 7]
    wo_ref = refs[pos + 7] if proj_layer is not None else None
    i = pl.program_id(0)
    is_ctx = i < n_ctx
    chunks = [(c, min(FFN_ACT_CHUNK, FFN_HIDDEN - c)) for c in range(0, FFN_HIDDEN, FFN_ACT_CHUNK)]

    def tile(first):
        if first:
            need = [col for c, w in chunks for base in (c, FFN_HIDDEN + c) for col in range(base, base + w, W_STAGE)]
            cols = _WeightStream(_col_pairs(wgu_hbm, layer, wgu_ref, need), stage_gu, sem_gu)
            proj = _row_pairs(wo_hbm, proj_layer, wo_ref) if proj_layer is not None else []
            rows = _WeightStream(proj + _row_pairs(wd_hbm, layer, wd_ref), stage_d, sem_d)
            rows.take(len(proj))
        if first or not split_in:
            h = h_refs[0][...]
        else:
            h = jnp.where(is_ctx, h_refs[0][...], h_refs[1][...])
        if proj_layer is not None:
            latent_heads = jnp.concatenate([oa_ref[...], ob_ref[...]], axis=1)
            heads = octx_ref[...] if first else jnp.where(is_ctx, octx_ref[...], latent_heads)
            h = h + _gate(mod_ref, 1) * _dot(heads, wo_ref[...])
        xn = _adaln(h, g_ref[...], mod_ref, j).astype(BF16)
        for c, w in chunks:
            if first:
                cols.take(2 * w // W_STAGE)
            gate = _dot(xn, wgu_ref[:, c:c + w])
            up = _dot(xn, wgu_ref[:, FFN_HIDDEN + c:FFN_HIDDEN + c + w])
            act_ref[:, c:c + w] = ((gate * jax.nn.sigmoid(gate)) * up).astype(BF16)
            if first:
                rows.take(2)
        if first:
            rows.take()
        res = h + (0.5 * _gate(mod_ref, j)) * _dot(act_ref[...], wd_ref[...])
        if first or not split_out:
            out_refs[0][...] = res
        else:
            @pl.when(is_ctx)
            def _():
                out_refs[0][...] = res

            @pl.when(jnp.logical_not(is_ctx))
            def _():
                out_refs[1][...] = res

    pl.when(i == 0)(lambda: tile(True))
    pl.when(i > 0)(lambda: tile(False))


def _ffn(h, mod, g, w_gu, w_down, layer, j, ctx_rows, dec_seq, attn=None, split_out=False):
    tm = FFN_TM
    split_in = isinstance(h, tuple)
    h_arrays = list(h) if split_in else [h]
    total_rows = sum(a.shape[0] for a in h_arrays)
    n_ctx, ctx_map, lat_map, cond = _two_stream_maps(tm, ctx_rows, dec_seq)
    assert n_ctx >= 1
    row_map = lambda i: (i, 0)
    row_spec = lambda m, width=D_MODEL: pl.BlockSpec((tm, width), m)
    in_specs = ([row_spec(ctx_map), row_spec(lat_map)] if split_in else [row_spec(row_map)]) + [
        _mod_spec(layer, cond),
        _norm_spec(layer, j),
        HBM, HBM]
    args = [*h_arrays, mod, g, w_gu, w_down]
    scratch = [pltpu.VMEM((D_MODEL, 2 * FFN_HIDDEN), BF16), pltpu.VMEM((FFN_HIDDEN, D_MODEL), BF16),
               *COL_STAGE, *ROW_STAGE, pltpu.VMEM((tm, FFN_HIDDEN), BF16)]
    proj_layer = None
    if attn is not None:
        o_ctx, oa, ob, w_out, proj_layer = attn
        in_specs += [row_spec(ctx_map, o_ctx.shape[1]), row_spec(lat_map, oa.shape[1]),
                     row_spec(lat_map, ob.shape[1]), HBM]
        args += [o_ctx, oa, ob, w_out]
        scratch.append(pltpu.VMEM((o_ctx.shape[1], D_MODEL), BF16))
    if split_out:
        out_specs = [row_spec(ctx_map), row_spec(lat_map)]
        out_shape = [jax.ShapeDtypeStruct((ctx_rows, D_MODEL), F32),
                     jax.ShapeDtypeStruct((total_rows - ctx_rows, D_MODEL), F32)]
    else:
        out_specs = row_spec(row_map)
        out_shape = jax.ShapeDtypeStruct((total_rows, D_MODEL), F32)
    return pl.pallas_call(
        functools.partial(_ffn_kernel, layer=layer, j=j, n_ctx=n_ctx, proj_layer=proj_layer,
                          split_in=split_in, split_out=split_out),
        grid=(total_rows // tm,),
        in_specs=in_specs,
        out_specs=out_specs,
        out_shape=out_shape,
        scratch_shapes=scratch,
        compiler_params=_params("arbitrary"),
        name=f"ffn{j}_l{layer}",
    )(*args)


def _head_mean_sq(x):
    w = x.shape[-1]
    r = lax.broadcasted_iota(jnp.int32, (MXU_COLS, MXU_COLS), 0) // HEAD_DIM
    c = lax.broadcasted_iota(jnp.int32, (MXU_COLS, MXU_COLS), 1) // HEAD_DIM
    bd = jnp.where(r == c, 1.0 / HEAD_DIM, 0.0).astype(BF16)
    sq = (x * x).astype(BF16)
    parts = []
    for s in range(0, w, MXU_COLS):
        e = min(s + MXU_COLS, w)
        parts.append(_dot(sq[:, s:e], bd[:e - s, :e - s]))
    return parts[0] if len(parts) == 1 else jnp.concatenate(parts, axis=-1)


def _low_half():
    return lax.broadcasted_iota(jnp.int32, (1, LANES), 1) < HEAD_DIM


def _head_rms(x, gain):
    return (x * lax.rsqrt(_head_mean_sq(x) + EPS)) * gain


def _rope(x, cos, sin_signed):
    lanes = cos.shape[-1]
    outs = []
    for s in range(0, x.shape[-1], lanes):
        xs = x[:, s:s + lanes]
        blk = lax.broadcasted_iota(jnp.int32, xs.shape, 1) // (HEAD_DIM // 4)
        partner = jnp.where(blk % 2 == 0,
                            pltpu.roll(xs, lanes - HEAD_DIM // 4, 1),
                            pltpu.roll(xs, HEAD_DIM // 4, 1))
        outs.append(xs * cos + partner * sin_signed)
    return outs[0] if len(outs) == 1 else jnp.concatenate(outs, axis=-1)


def _qkv_kernel(*refs, layer, rope, cache_out):
    h_ref, mod_ref, g_ref, w_hbm, gain_ref = refs[:5]
    w_ref, stage_ref, sem_ref = refs[-3:]
    pos = 5
    if rope:
        cos_ref, sin_ref = refs[pos:pos + 2]
        pos += 2
    q_ref, kt_ref, v_ref = refs[pos:pos + 3]
    pos += 3

    @pl.when(pl.program_id(0) == 0)
    def _load_weights():
        _load_by_cols(w_hbm, layer, w_ref, stage_ref, sem_ref)

    xn = _adaln(h_ref[...], g_ref[...], mod_ref, 1).astype(BF16)
    y = _dot(xn, w_ref[...])
    o = 0
    qa = _head_rms(y[:, o:o + NA_W], gain_ref[:, :NA_W]); o += NA_W
    ka = _head_rms(y[:, o:o + NA_W], gain_ref[:, NA_W:2 * NA_W]); o += NA_W
    va = y[:, o:o + NA_W]; o += NA_W
    qb = _head_rms(y[:, o:o + GQ_W], gain_ref[:, 2 * NA_W:2 * NA_W + GQ_W]); o += GQ_W
    kb = _head_rms(y[:, o:o + GKV_W], gain_ref[:, 2 * NA_W + GQ_W:]); o += GKV_W
    vb = y[:, o:o + GKV_W]
    if rope:
        cos, sin = cos_ref[...], sin_ref[...]
        qb = _rope(qb, cos, sin)
        kb = _rope(kb, cos, sin)
    q_ref[:, :NA_W] = (qa * Q_SCALE).astype(BF16)
    q_ref[:, NA_W:] = (qb * Q_SCALE).astype(BF16)
    low = _low_half()
    vb_swapped = pltpu.roll(vb, HEAD_DIM, 1)
    v_ref[:, :NA_W] = va.astype(BF16)
    v_ref[:, NA_W:NA_W + LANES] = jnp.where(low, vb, vb_swapped).astype(BF16)
    v_ref[:, NA_W + LANES:] = jnp.where(low, vb_swapped, vb).astype(BF16)
    n_seq, _, s = kt_ref.shape
    for i in range(n_seq):
        rows = slice(i * s, (i + 1) * s)
        ka_t = ka[rows].T
        kb_t = kb[rows].T
        k0, k1 = kb_t[:HEAD_DIM], kb_t[HEAD_DIM:]
        kt_ref[i, :NA_W, :] = ka_t.astype(BF16)
        kt_ref[i, NA_W:, :] = jnp.concatenate([k0, k0, k1, k1], axis=0).astype(BF16)
        if cache_out:
            kat_ref, vat_ref, kbt_ref, vbt_ref = refs[pos:pos + 4]
            kat_ref[i] = ka_t
            vat_ref[i] = va[rows].T
            kbt_ref[i] = kb_t
            vbt_ref[i] = vb[rows].T


def _qkv(h, row0, t, mod, mod_layer, cond0, g, w_in, layer, gains, cond_div, seq, rope_tabs=None,
         cache_out=False):
    tm = PROJ_TM
    rope = rope_tabs is not None
    assert not (rope and cache_out)
    row = lambda i: (i, 0)
    const = lambda i: (0, 0)
    in_specs = [
        pl.BlockSpec((tm, D_MODEL), lambda i: (i + row0 // tm, 0)),
        _mod_spec(mod_layer, lambda i: cond0 + i // cond_div),
        _norm_spec(mod_layer, 1),
        HBM,
        pl.BlockSpec((1, 2 * NA_W + GQ_W + GKV_W), const),
    ]
    args = [h, mod, g, w_in, gains]
    if rope:
        seq_tiles = rope_tabs[0].shape[0] // tm
        in_specs += [pl.BlockSpec((tm, 2 * HEAD_DIM), lambda i: (i % seq_tiles, 0))] * 2
        args += list(rope_tabs)
    if seq >= tm:
        per_seq = seq // tm
        t_block = lambda rows: pl.BlockSpec((1, rows, tm), lambda i: (i // per_seq, 0, i % per_seq))
    else:
        t_block = lambda rows: pl.BlockSpec((tm // seq, rows, seq), lambda i: (i, 0, 0))
    t_shape = lambda rows, dtype: jax.ShapeDtypeStruct((t // seq, rows, seq), dtype)
    out_specs = [pl.BlockSpec((tm, NA_W + GQ_W), row), t_block(KT_ROWS), pl.BlockSpec((tm, KT_ROWS), row)]
    out_shape = [jax.ShapeDtypeStruct((t, NA_W + GQ_W), BF16), t_shape(KT_ROWS, BF16),
                 jax.ShapeDtypeStruct((t, KT_ROWS), BF16)]
    if cache_out:
        out_specs += [t_block(NA_W), t_block(NA_W), t_block(GKV_W), t_block(GKV_W)]
        out_shape += [t_shape(NA_W, F32), t_shape(NA_W, F32), t_shape(GKV_W, F32), t_shape(GKV_W, F32)]
    return pl.pallas_call(
        functools.partial(_qkv_kernel, layer=layer, rope=rope, cache_out=cache_out),
        grid=(t // tm,),
        in_specs=in_specs,
        out_specs=out_specs,
        out_shape=out_shape,
        scratch_shapes=[pltpu.VMEM((D_MODEL, QKV_WIDTH), BF16), *COL_STAGE],
        compiler_params=_params("arbitrary"),
        name="qkv_latent" if rope else "qkv_context",
    )(*args)


def _split_pair(q2):
    low = _low_half()
    zero = jnp.zeros((), q2.dtype)
    return jnp.where(low, q2, zero), jnp.where(low, zero, q2)


def _with_ones(v2):
    return jnp.concatenate([v2, jnp.ones(v2.shape, v2.dtype)], axis=1)


def _join(blocks, axis):
    return blocks[0] if len(blocks) == 1 else jnp.concatenate(blocks, axis=axis)


def _attend(problems):
    scores = [_dot(qm, _join(kts, 1)) for qm, kts, _, _ in problems]
    probs = []
    for s, (_, kts, biases, _) in zip(scores, problems):
        parts, start = [], 0
        for kt, b in zip(kts, biases):
            part = s[:, start:start + kt.shape[1]]
            parts.append(part if b is None else part + b)
            start += kt.shape[1]
        m = functools.reduce(jnp.maximum, [jnp.max(part, axis=-1, keepdims=True) for part in parts])
        probs.append(_join([jnp.exp(part - m).astype(BF16) for part in parts], 1))
    return [_dot(p, _join(vexts, 0)) for p, (_, _, _, vexts) in zip(probs, problems)]


def _merge_pair(oe_even, oe_odd):
    even = oe_even[:, :LANES] * (1.0 / oe_even[:, LANES:])
    odd = oe_odd[:, :LANES] * (1.0 / oe_odd[:, LANES:])
    return jnp.where(_low_half(), even, odd)


def _gqa_groups(q_ref, q_off, o_ref, o_off, keys_values):
    tq = q_ref.shape[0]
    slabs = lambda g, off: [slice(off + (2 * g + j) * LANES, off + (2 * g + j + 1) * LANES) for j in range(2)]
    problems = []
    for g in range(GQA_KV_HEADS):
        qm = jnp.concatenate([part for sl in slabs(g, q_off) for part in _split_pair(q_ref[:, sl])], axis=0)
        problems.append((qm, *keys_values(slice(g * LANES, (g + 1) * LANES))))
    for g, oe in enumerate(_attend(problems)):
        for j, sl in enumerate(slabs(g, o_off)):
            o_ref[:, sl] = _merge_pair(oe[2 * j * tq:(2 * j + 1) * tq],
                                       oe[(2 * j + 1) * tq:(2 * j + 2) * tq]).astype(o_ref.dtype)


def _ctx_attn_kernel(q_blk, kt_blk, v_blk, o_blk):
    n_seq, _, seq = kt_blk.shape
    for b in range(n_seq):
        rows = pl.ds(b * seq, seq)
        q_ref, v_ref, o_ref, kt_ref = q_blk.at[rows], v_blk.at[rows], o_blk.at[rows], kt_blk.at[b]
        for p in range(NA_HEADS // 2):
            sl = slice(p * LANES, (p + 1) * LANES)
            vext = _with_ones(v_ref[:, sl])
            oe = [_attend([(qm, [kt_ref[sl, :]], [None], [vext])])[0] for qm in _split_pair(q_ref[:, sl])]
            o_ref[:, sl] = _merge_pair(*oe).astype(o_ref.dtype)
        _gqa_groups(q_ref, NA_W, o_ref, NA_W,
                    lambda kv: ([kt_ref[NA_W + kv.start:NA_W + kv.stop, :]], [None],
                                [_with_ones(v_ref[:, NA_W + kv.start:NA_W + kv.stop])]))


def _ctx_attention(q, kt, v, seq):
    t = q.shape[0]
    n_seq = CTX_SEQ_PER_STEP
    row = lambda b: (b, 0)
    return pl.pallas_call(
        _ctx_attn_kernel,
        grid=(t // (n_seq * seq),),
        in_specs=[pl.BlockSpec((n_seq * seq, NA_W + GQ_W), row),
                  pl.BlockSpec((n_seq, KT_ROWS, seq), lambda b: (b, 0, 0)),
                  pl.BlockSpec((n_seq * seq, KT_ROWS), row)],
        out_specs=pl.BlockSpec((n_seq * seq, NA_W + GQ_W), row),
        out_shape=jax.ShapeDtypeStruct((t, NA_W + GQ_W), BF16),
        compiler_params=_params("arbitrary"),
        name="ctx_attention",
    )(q, kt, v)


def _na_kernel(q_ref, kt0_ref, kt1_ref, kt2_ref, v0_ref, v1_ref, v2_ref, ckt_ref, cv_ref, bias_ref, o_ref):
    for p in range(NA_HEADS // 2):
        sl = slice(p * LANES, (p + 1) * LANES)
        kt_loc = jnp.concatenate([kt0_ref[sl, :], kt1_ref[sl, :], kt2_ref[sl, :]], axis=1)
        v_loc = _with_ones(jnp.concatenate([v0_ref[:, sl], v1_ref[:, sl], v2_ref[:, sl]], axis=0))
        v_ctx = _with_ones(cv_ref[:, sl])
        oe = _attend([(qm, [kt_loc, ckt_ref[sl, :]], [bias_ref[2 * p + half], None], [v_loc, v_ctx])
                      for half, qm in enumerate(_split_pair(q_ref[:, sl]))])
        o_ref[:, sl] = _merge_pair(*oe).astype(o_ref.dtype)


def _na_bias_tiles(rpb):
    qc = np.arange(GRID_W)
    kc = np.arange(GRID_W)
    ws = np.clip(qc - NA_WIN_C // 2, 0, GRID_W - NA_WIN_C)
    col_ok = (kc[None, :] >= ws[:, None]) & (kc[None, :] < ws[:, None] + NA_WIN_C)
    dc = np.clip(kc[None, :] - qc[:, None] + NA_WIN_C - 1, 0, 2 * NA_WIN_C - 2)
    pick = ((dc[:, :, None] == np.arange(2 * NA_WIN_C - 1)) & col_ok[:, :, None]).astype(np.float32)
    tiles = jnp.einsum("hrd,qkd->hrqk", rpb.astype(F32), pick, precision=lax.Precision.HIGHEST)
    tiles = jnp.where(col_ok, tiles, NEG_INF)
    return jnp.concatenate([tiles, tiles], axis=-1)


def _fill_band_bias(tiles_ref, bias_ref, band, rows):
    low = _low_half()
    masked = jnp.full((GRID_W, LANES), NEG_INF, F32)
    r0 = band * NA_BAND_ROWS
    k0 = int(np.clip(r0 - NA_WIN_R // 2, 0, rows - NA_BAND_KEY_ROWS))
    for h in range(NA_HEADS):
        for ri in range(NA_BAND_ROWS):
            r = r0 + ri
            start = int(np.clip(r - NA_WIN_R // 2, 0, rows - NA_WIN_R))
            tile = lambda kr: tiles_ref[h, kr - r + NA_WIN_R - 1] if start <= kr < start + NA_WIN_R else masked
            for m in range(NA_BAND_KEY_ROWS // 2):
                kr = k0 + 2 * m
                bias_ref[h, ri * GRID_W:(ri + 1) * GRID_W, m * LANES:(m + 1) * LANES] = jnp.where(
                    low, tile(kr), tile(kr + 1))


def _gqa_kernel(q_ref, kt_ref, v_ref, ckt_ref, cv_ref, o_ref):
    _gqa_groups(q_ref, 0, o_ref, 0,
                lambda kv: ([kt_ref[kv, :], ckt_ref[kv, :]], [None, None],
                            [_with_ones(v_ref[:, kv]), _with_ones(cv_ref[:, kv])]))


def _latent_attn_kernel(*refs, rows):
    n_na = 9
    n_gqa = 5
    tiles_ref = refs[n_na]
    oa_ref, ob_ref, bias_ref = refs[n_na + 1 + n_gqa:]
    nb = rows // NA_BAND_ROWS
    for band in sorted({0, 1, nb - 1}):
        pl.when(pl.program_id(1) == band)(functools.partial(_fill_band_bias, tiles_ref, bias_ref, band, rows))
    _gqa_kernel(*refs[n_na + 1:n_na + 1 + n_gqa], ob_ref)
    _na_kernel(*refs[:n_na], bias_ref, oa_ref)


def _latent_attention(q, kt, v, na_ctx_kt, na_ctx_v, bias_tiles, gqa_ctx_kt, gqa_ctx_v, batch, seq):
    rows = seq // GRID_W
    nb = rows // NA_BAND_ROWS
    past = na_ctx_v.shape[1]
    assert (rows - NA_BAND_KEY_ROWS) % NA_BAND_ROWS == 0
    n_chunks = NA_BAND_KEY_ROWS // NA_BAND_ROWS
    assert nb >= 3
    chunk0 = lambda j: jnp.clip(j - NA_WIN_R // 2 // NA_BAND_ROWS, 0, nb - n_chunks)
    kt_spec = lambda c: pl.BlockSpec((None, NA_W, NA_BAND_Q), lambda b, j: (b, 0, chunk0(j) + c))
    v_spec = lambda c: pl.BlockSpec((NA_BAND_Q, NA_W), lambda b, j: (b * nb + chunk0(j) + c, 0))
    tile = lambda lane_block: pl.BlockSpec((NA_BAND_Q, NA_W), lambda b, j: (b * nb + j, lane_block))
    dup = 2 * GKV_W
    assert NA_W == GQ_W
    na_specs = ([tile(0)] + [kt_spec(c) for c in range(n_chunks)] + [v_spec(c) for c in range(n_chunks)]
                + [pl.BlockSpec((None, NA_W, past), lambda b, j: (b, 0, 0)),
                   pl.BlockSpec((None, past, NA_W), lambda b, j: (b, 0, 0)),
                   pl.BlockSpec(bias_tiles.shape, lambda b, j: (0, 0, 0, 0))])
    gqa_specs = [tile(1),
                 pl.BlockSpec((None, dup, seq), lambda b, j: (b, NA_W // dup, 0)),
                 pl.BlockSpec((seq, dup), lambda b, j: (b, NA_W // dup)),
                 pl.BlockSpec((None, dup, past), lambda b, j: (b, 0, 0)),
                 pl.BlockSpec((None, past, dup), lambda b, j: (b, 0, 0))]
    return pl.pallas_call(
        functools.partial(_latent_attn_kernel, rows=rows),
        grid=(batch, nb),
        in_specs=na_specs + gqa_specs,
        out_specs=[tile(0), tile(0)],
        out_shape=[jax.ShapeDtypeStruct((batch * seq, NA_W), BF16), jax.ShapeDtypeStruct((batch * seq, GQ_W), BF16)],
        scratch_shapes=[pltpu.VMEM((NA_HEADS, NA_BAND_Q, NA_BAND_K), F32)],
        compiler_params=_params("arbitrary", "arbitrary"),
        name="latent_attention",
    )(q, *([kt] * n_chunks), *([v] * n_chunks), na_ctx_kt, na_ctx_v, bias_tiles, q, kt, v, gqa_ctx_kt, gqa_ctx_v)


def _sgu_kernel(h_ref, mod_ref, g_ref, win_hbm, vg_ref, ws_ref, bs_ref, wout_hbm, o_ref,
                win_ref, wout_ref, stage_in, sem_in, stage_out, sem_out, gated_ref, y_ref, *, layer):
    tm = h_ref.shape[0]
    n_chunks = y_ref.shape[0]
    per_chunk = SGU_ACT_CHUNK // SGU_GW

    def tile(first):
        if first:
            cols = _WeightStream(_col_pairs(win_hbm, layer, win_ref), stage_in, sem_in)
            rows = _WeightStream(_row_pairs(wout_hbm, layer, wout_ref), stage_out, sem_out)
        h = h_ref[...]
        xn = _adaln(h, g_ref[...], mod_ref, 1).astype(BF16)
        ssq = jnp.zeros((tm, 1), F32)
        for c in range(n_chunks):
            if first:
                cols.take(SGU_ACT_CHUNK // W_STAGE)
            y = _dot(xn, win_ref[c])
            y = 0.5 * y * (1.0 + lax.erf(y * math.sqrt(0.5)))
            y_ref[c] = y
            if c >= n_chunks // 2:
                ssq = ssq + jnp.sum(y * y, axis=-1, keepdims=True)
            if first:
                rows.take(1)
        if first:
            rows.take()
        v_scale = lax.rsqrt(ssq * (1.0 / SGU_WIDTH) + EPS)
        for c in range(tm // SGU_CHUNK):
            rs = slice(c * SGU_CHUNK, (c + 1) * SGU_CHUNK)
            for g in range(SGU_GROUPS):
                cs = slice(g * SGU_GW, (g + 1) * SGU_GW)
                in_chunk = slice((g % per_chunk) * SGU_GW, (g % per_chunk + 1) * SGU_GW)
                u = y_ref[g // per_chunk, rs, in_chunk]
                v = y_ref[n_chunks // 2 + g // per_chunk, rs, in_chunk]
                vn = ((v * v_scale[rs]) * vg_ref[:, cs]).astype(BF16)
                sv = _dot(ws_ref[g].astype(BF16), vn) + bs_ref[:, g:g + 1]
                gated_ref[rs, cs] = (u * sv).astype(BF16)
        o_ref[...] = h + _gate(mod_ref, 1) * _dot(gated_ref[...], wout_ref[...])

    pl.when(pl.program_id(0) == 0)(lambda: tile(True))
    pl.when(pl.program_id(0) > 0)(lambda: tile(False))


def _sgu(h, mod, mod_layer, g, w_in, v_g, w_s, b_s_t, w_out, layer, ctx_rows, dec_seq):
    tm = SGU_TM
    n_act = 2 * SGU_WIDTH // SGU_ACT_CHUNK
    _, _, _, cond = _two_stream_maps(tm, ctx_rows, dec_seq)
    row_spec = pl.BlockSpec((tm, D_MODEL), lambda i: (i, 0))
    const = lambda i: (0, 0)
    return pl.pallas_call(
        functools.partial(_sgu_kernel, layer=layer),
        grid=(h.shape[0] // tm,),
        in_specs=[
            row_spec,
            _mod_spec(mod_layer, cond),
            _norm_spec(mod_layer, 1),
            HBM,
            pl.BlockSpec((1, SGU_WIDTH), const),
            pl.BlockSpec((None, SGU_GROUPS, SGU_CHUNK, SGU_CHUNK), lambda i: (layer, 0, 0, 0)),
            pl.BlockSpec((SGU_CHUNK, SGU_GROUPS), const),
            HBM,
        ],
        out_specs=row_spec,
        out_shape=jax.ShapeDtypeStruct(h.shape, F32),
        scratch_shapes=[
            pltpu.VMEM((n_act, D_MODEL, SGU_ACT_CHUNK), BF16),
            pltpu.VMEM((SGU_WIDTH, D_MODEL), BF16),
            *COL_STAGE, *ROW_STAGE,
            pltpu.VMEM((tm, SGU_WIDTH), BF16),
            pltpu.VMEM((n_act, tm, SGU_ACT_CHUNK), F32),
        ],
        compiler_params=_params("arbitrary"),
        name="sgu",
    )(h, mod, g, w_in, v_g, w_s, b_s_t, w_out)


def _rope_tables(seq):
    half = HEAD_DIM // 2
    t = np.arange(seq)
    freqs = np.float32(ROPE_BASE) ** (-np.arange(0, half, 2, dtype=np.float32) / np.float32(half))
    def tab(pos):
        ang = pos.astype(np.float32)[:, None] * freqs[None, :]
        cos, sin = np.cos(ang), np.sin(ang)
        return np.concatenate([cos, cos], -1), np.concatenate([-sin, sin], -1)
    cr, sr = tab(t // GRID_W)
    cc, sc = tab(t % GRID_W)
    cos = np.concatenate([cr, cc], -1)
    sin = np.concatenate([sr, sc], -1)
    return jnp.asarray(np.tile(cos, (1, 2)), F32), jnp.asarray(np.tile(sin, (1, 2)), F32)


def kernel(x_prompt, x_sample, cache_na_k, cache_na_v, cache_gqa_k, cache_gqa_v, c, c_ctx, norm_g, mod_w, mod_b, ffn1_w_gu, ffn1_w_down, ffn2_w_gu, ffn2_w_down, attn_w_in, attn_w_out, na_q_g, na_k_g, na_rpb, gqa_q_g, gqa_k_g, sgu_w_in, sgu_v_g, sgu_w_s, sgu_b_s, sgu_w_out):
    batch, seq, _ = x_prompt.shape
    dec_batch, dec_seq, _ = x_sample.shape
    past = cache_na_k.shape[2]
    ctx_rows, lat_rows = batch * seq, dec_batch * dec_seq
    h = (x_prompt.reshape(ctx_rows, D_MODEL), x_sample.reshape(lat_rows, D_MODEL))

    assert 1 + dec_batch <= 8
    cond8 = jnp.concatenate([c_ctx[None], c, jnp.zeros((8 - 1 - dec_batch, D_MODEL), F32)], axis=0)
    mod = _modulation(cond8, mod_w, mod_b).reshape(DEPTH, 8, N_MOD, D_MODEL)

    g = norm_g.reshape(DEPTH * 3, 1, D_MODEL)
    new_cache = None
    for layer in range(DEPTH):
        h = _ffn(h, mod, g, ffn1_w_gu, ffn1_w_down, layer, 0, ctx_rows, dec_seq)

        if layer % 2 == 0:
            e = layer // 2
            gains = jnp.concatenate([jnp.tile(na_q_g[e], NA_HEADS), jnp.tile(na_k_g[e], NA_HEADS),
                                     jnp.tile(gqa_q_g[e], GQA_HEADS), jnp.tile(gqa_k_g[e], GQA_KV_HEADS)])[None]
            q, kt, v, *new_cache = _qkv(h, 0, ctx_rows, mod, layer, 0, g, attn_w_in, e, gains,
                                        ctx_rows // PROJ_TM, seq, cache_out=True)
            o_ctx = _ctx_attention(q, kt, v, seq)
            q, kt, v = _qkv(h, ctx_rows, lat_rows, mod, layer, 1, g, attn_w_in, e, gains,
                            dec_seq // PROJ_TM, dec_seq, rope_tabs=_rope_tables(dec_seq))
            to_kt = lambda ck, rep: jnp.repeat(jnp.transpose(ck[:, e], (0, 2, 3, 1)), rep, axis=1).reshape(
                dec_batch, -1, past).astype(BF16)
            to_v = lambda cv, rep: jnp.repeat(cv[:, e], rep, axis=2).reshape(dec_batch, past, -1).astype(BF16)
            oa, ob = _latent_attention(q, kt, v, to_kt(cache_na_k, 1), to_v(cache_na_v, 1),
                                       _na_bias_tiles(na_rpb[e]),
                                       to_kt(cache_gqa_k, 2), to_v(cache_gqa_v, 2), dec_batch, dec_seq)
            attn = (o_ctx, oa, ob, attn_w_out, e)
        else:
            attn = None
            o = layer // 2
            h = _sgu(h, mod, layer, g, sgu_w_in, sgu_v_g[o][None], sgu_w_s, sgu_b_s[o].T, sgu_w_out,
                     o, ctx_rows, dec_seq)

        h = _ffn(h, mod, g, ffn2_w_gu, ffn2_w_down, layer, 2, ctx_rows, dec_seq, attn=attn,
                 split_out=layer == DEPTH - 1)
    hp, hs = h

    n_attn = (DEPTH + 1) // 2
    assert n_attn == 1
    from_t = lambda x: jnp.transpose(x.reshape(batch, n_attn, -1, HEAD_DIM, seq), (0, 1, 4, 2, 3))
    return (hp.reshape(batch, seq, D_MODEL), hs.reshape(dec_batch, dec_seq, D_MODEL),
            *(from_t(x) for x in new_cache))
```

```python
import functools
import math

import jax
import jax.numpy as jnp
import numpy as np
from jax import lax
from jax.experimental import pallas as pl
from jax.experimental.pallas import tpu as pltpu

D_MODEL = 1024
DEPTH = 2
GRID_W = 64
HEAD_DIM = 64
NA_HEADS = 8
NA_WIN_R = 8
NA_WIN_C = 16
GQA_HEADS = 8
GQA_KV_HEADS = 2
GQA_GROUP = GQA_HEADS // GQA_KV_HEADS
NA_W = NA_HEADS * HEAD_DIM
GQ_W = GQA_HEADS * HEAD_DIM
GKV_W = GQA_KV_HEADS * HEAD_DIM
QKV_WIDTH = 3 * NA_W + GQ_W + 2 * GKV_W
ROPE_BASE = 10000.0
SGU_CHUNK = 128
SGU_GROUPS = 8
SGU_WIDTH = 2 * D_MODEL
SGU_GW = SGU_WIDTH // SGU_GROUPS
FFN_HIDDEN = 2816
N_MOD = 9
EPS = 1e-6
NEG_INF = -1e30
Q_SCALE = HEAD_DIM ** -0.5

NA_BAND_ROWS = 4
NA_BAND_KEY_ROWS = 12
NA_BAND_Q = NA_BAND_ROWS * GRID_W
NA_BAND_K = NA_BAND_KEY_ROWS * GRID_W

FFN_TM = 512
FFN_ACT_CHUNK = 256
SGU_ACT_CHUNK = 512
SGU_TM = 512
MOD_TN = 3072
CTX_SEQ_PER_STEP = 4
PROJ_TM = 1024
W_STAGE = 256

LANES = 128
KT_ROWS = NA_W + 2 * GKV_W
MXU_COLS = 256
VMEM_LIMIT = 56 * 1024 * 1024

BF16 = jnp.bfloat16
F32 = jnp.float32


def _dot(a, b):
    return jnp.dot(a, b, preferred_element_type=F32)


def _params(*sem):
    return pltpu.CompilerParams(dimension_semantics=sem, vmem_limit_bytes=VMEM_LIMIT)


def _adaln(x, g, mod_ref, j):
    shift = mod_ref[0, 3 * j:3 * j + 1, :]
    scale = mod_ref[0, 3 * j + 1:3 * j + 2, :]
    y = x * lax.rsqrt(jnp.mean(x * x, axis=-1, keepdims=True) + EPS)
    return (y * g) * (1.0 + scale) + shift


def _gate(mod_ref, j):
    return mod_ref[0, 3 * j + 2:3 * j + 3, :]


def _mod_kernel(c_ref, w_ref, b_ref, o_ref):
    c = c_ref[...]
    a = (c * jax.nn.sigmoid(c)).astype(BF16)
    o_ref[...] = _dot(a, w_ref[...].astype(BF16)) + b_ref[...]


def _modulation(cond8, mod_w, mod_b):
    tn = MOD_TN
    n = N_MOD * D_MODEL
    return pl.pallas_call(
        _mod_kernel,
        grid=(DEPTH, n // tn),
        in_specs=[
            pl.BlockSpec((8, D_MODEL), lambda l, k: (0, 0)),
            pl.BlockSpec((None, D_MODEL, tn), lambda l, k: (l, 0, k)),
            pl.BlockSpec((None, 1, tn), lambda l, k: (l, 0, k)),
        ],
        out_specs=pl.BlockSpec((None, 8, tn), lambda l, k: (l, 0, k)),
        out_shape=jax.ShapeDtypeStruct((DEPTH, 8, n), F32),
        compiler_params=_params("arbitrary", "arbitrary"),
        name="modulation",
    )(cond8, mod_w, mod_b.reshape(DEPTH, 1, n))


def _two_stream_maps(tm, n_ctx_rows, dec_seq):
    n_ctx = n_ctx_rows // tm
    per_seq = dec_seq // tm
    ctx_map = lambda i: (jnp.minimum(i, n_ctx - 1), 0)
    lat_map = lambda i: (jnp.maximum(i - n_ctx, 0), 0)
    cond = lambda i: jnp.where(i < n_ctx, 0, 1 + (i - n_ctx) // per_seq)
    return n_ctx, ctx_map, lat_map, cond


def _norm_spec(layer, j):
    return pl.BlockSpec((None, 1, D_MODEL), lambda i: (3 * layer + j, 0, 0))


def _mod_spec(layer, cond):
    return pl.BlockSpec((None, 1, N_MOD, D_MODEL), lambda i: (layer, cond(i), 0, 0))


class _WeightStream:
    def __init__(self, pairs, stage_ref, sem_ref):
        self.pairs, self.stage, self.sem, self.done = pairs, stage_ref, sem_ref, 0
        for k in range(min(2, len(pairs))):
            self._copy(k).start()

    def _copy(self, k):
        return pltpu.make_async_copy(self.pairs[k][0], self.stage.at[k % 2], self.sem.at[k % 2])

    def take(self, n=None):
        n = len(self.pairs) - self.done if n is None else min(n, len(self.pairs) - self.done)
        for _ in range(n):
            k = self.done
            self._copy(k).wait()
            dst, idx = self.pairs[k][1]
            dst[idx] = self.stage[k % 2].astype(BF16)
            if k + 2 < len(self.pairs):
                self._copy(k + 2).start()
            self.done += 1


def _col_pairs(w_hbm, layer, w_ref, cols=None):
    if len(w_ref.shape) == 2:
        dst = lambda c: (slice(None), slice(c, c + W_STAGE))
        n = w_ref.shape[1]
    else:
        width = w_ref.shape[2]
        dst = lambda c: (c // width, slice(None), slice(c % width, c % width + W_STAGE))
        n = w_ref.shape[0] * width
    cols = range(0, n, W_STAGE) if cols is None else cols
    return [(w_hbm.at[layer, :, pl.ds(c, W_STAGE)], (w_ref, dst(c))) for c in cols]


def _row_pairs(w_hbm, layer, w_ref):
    return [(w_hbm.at[layer, pl.ds(r, W_STAGE), :], (w_ref, (slice(r, r + W_STAGE), slice(None))))
            for r in range(0, w_ref.shape[0], W_STAGE)]


def _load_by_cols(w_hbm, layer, w_ref, stage_ref, sem_ref):
    _WeightStream(_col_pairs(w_hbm, layer, w_ref), stage_ref, sem_ref).take()


COL_STAGE = (pltpu.VMEM((2, D_MODEL, W_STAGE), F32), pltpu.SemaphoreType.DMA((2,)))
ROW_STAGE = (pltpu.VMEM((2, W_STAGE, D_MODEL), F32), pltpu.SemaphoreType.DMA((2,)))
HBM = pl.BlockSpec(memory_space=pl.ANY)


def _ffn_kernel(*refs, layer, j, n_ctx, proj_layer, split_in, split_out):
    h_refs = refs[:2 if split_in else 1]
    pos = len(h_refs)
    mod_ref, g_ref, wgu_hbm, wd_hbm = refs[pos:pos + 4]
    pos += 4
    if proj_layer is not None:
        octx_ref, oa_ref, ob_ref, wo_hbm = refs[pos:pos + 4]
        pos += 4
    out_refs = refs[pos:pos + (2 if split_out else 1)]
    pos += len(out_refs)
    wgu_ref, wd_ref, stage_gu, sem_gu, stage_d, sem_d, act_ref = refs[pos:pos + 7]
    wo_ref = refs[pos + 7] if proj_layer is not None else None
    i = pl.program_id(0)
    is_ctx = i < n_ctx
    chunks = [(c, min(FFN_ACT_CHUNK, FFN_HIDDEN - c)) for c in range(0, FFN_HIDDEN, FFN_ACT_CHUNK)]

    def tile(first):
        if first:
            need = [col for c, w in chunks for base in (c, FFN_HIDDEN + c) for col in range(base, base + w, W_STAGE)]
            cols = _WeightStream(_col_pairs(wgu_hbm, layer, wgu_ref, need), stage_gu, sem_gu)
            proj = _row_pairs(wo_hbm, proj_layer, wo_ref) if proj_layer is not None else []
            rows = _WeightStream(proj + _row_pairs(wd_hbm, layer, wd_ref), stage_d, sem_d)
            rows.take(len(proj))
        if first or not split_in:
            h = h_refs[0][...]
        else:
            h = jnp.where(is_ctx, h_refs[0][...], h_refs[1][...])
        if proj_layer is not None:
            latent_heads = jnp.concatenate([oa_ref[...], ob_ref[...]], axis=1)
            heads = octx_ref[...] if first else jnp.where(is_ctx, octx_ref[...], latent_heads)
            h = h + _gate(mod_ref, 1) * _dot(heads, wo_ref[...])
        xn = _adaln(h, g_ref[...], mod_ref, j).astype(BF16)
        for c, w in chunks:
            if first:
                cols.take(2 * w // W_STAGE)
            gate = _dot(xn, wgu_ref[:, c:c + w])
            up = _dot(xn, wgu_ref[:, FFN_HIDDEN + c:FFN_HIDDEN + c + w])
            act_ref[:, c:c + w] = ((gate * jax.nn.sigmoid(gate)) * up).astype(BF16)
            if first:
                rows.take(2)
        if first:
            rows.take()
        res = h + (0.5 * _gate(mod_ref, j)) * _dot(act_ref[...], wd_ref[...])
        if first or not split_out:
            out_refs[0][...] = res
        else:
            @pl.when(is_ctx)
            def _():
                out_refs[0][...] = res

            @pl.when(jnp.logical_not(is_ctx))
            def _():
                out_refs[1][...] = res

    pl.when(i == 0)(lambda: tile(True))
    pl.when(i > 0)(lambda: tile(False))


def _ffn(h, mod, g, w_gu, w_down, layer, j, ctx_rows, dec_seq, attn=None, split_out=False):
    tm = FFN_TM
    split_in = isinstance(h, tuple)
    h_arrays = list(h) if split_in else [h]
    total_rows = sum(a.shape[0] for a in h_arrays)
    n_ctx, ctx_map, lat_map, cond = _two_stream_maps(tm, ctx_rows, dec_seq)
    assert n_ctx >= 1
    row_map = lambda i: (i, 0)
    row_spec = lambda m, width=D_MODEL: pl.BlockSpec((tm, width), m)
    in_specs = ([row_spec(ctx_map), row_spec(lat_map)] if split_in else [row_spec(row_map)]) + [
        _mod_spec(layer, cond),
        _norm_spec(layer, j),
        HBM, HBM]
    args = [*h_arrays, mod, g, w_gu, w_down]
    scratch = [pltpu.VMEM((D_MODEL, 2 * FFN_HIDDEN), BF16), pltpu.VMEM((FFN_HIDDEN, D_MODEL), BF16),
               *COL_STAGE, *ROW_STAGE, pltpu.VMEM((tm, FFN_HIDDEN), BF16)]
    proj_layer = None
    if attn is not None:
        o_ctx, oa, ob, w_out, proj_layer = attn
        in_specs += [row_spec(ctx_map, o_ctx.shape[1]), row_spec(lat_map, oa.shape[1]),
                     row_spec(lat_map, ob.shape[1]), HBM]
        args += [o_ctx, oa, ob, w_out]
        scratch.append(pltpu.VMEM((o_ctx.shape[1], D_MODEL), BF16))
    if split_out:
        out_specs = [row_spec(ctx_map), row_spec(lat_map)]
        out_shape = [jax.ShapeDtypeStruct((ctx_rows, D_MODEL), F32),
                     jax.ShapeDtypeStruct((total_rows - ctx_rows, D_MODEL), F32)]
    else:
        out_specs = row_spec(row_map)
        out_shape = jax.ShapeDtypeStruct((total_rows, D_MODEL), F32)
    return pl.pallas_call(
        functools.partial(_ffn_kernel, layer=layer, j=j, n_ctx=n_ctx, proj_layer=proj_layer,
                          split_in=split_in, split_out=split_out),
        grid=(total_rows // tm,),
        in_specs=in_specs,
        out_specs=out_specs,
        out_shape=out_shape,
        scratch_shapes=scratch,
        compiler_params=_params("arbitrary"),
        name=f"ffn{j}_l{layer}",
    )(*args)


def _head_mean_sq(x):
    w = x.shape[-1]
    r = lax.broadcasted_iota(jnp.int32, (MXU_COLS, MXU_COLS), 0) // HEAD_DIM
    c = lax.broadcasted_iota(jnp.int32, (MXU_COLS, MXU_COLS), 1) // HEAD_DIM
    bd = jnp.where(r == c, 1.0 / HEAD_DIM, 0.0).astype(BF16)
    sq = (x * x).astype(BF16)
    parts = []
    for s in range(0, w, MXU_COLS):
        e = min(s + MXU_COLS, w)
        parts.append(_dot(sq[:, s:e], bd[:e - s, :e - s]))
    return parts[0] if len(parts) == 1 else jnp.concatenate(parts, axis=-1)


def _low_half():
    return lax.broadcasted_iota(jnp.int32, (1, LANES), 1) < HEAD_DIM


def _head_rms(x, gain):
    return (x * lax.rsqrt(_head_mean_sq(x) + EPS)) * gain


def _rope(x, cos, sin_signed):
    lanes = cos.shape[-1]
    outs = []
    for s in range(0, x.shape[-1], lanes):
        xs = x[:, s:s + lanes]
        blk = lax.broadcasted_iota(jnp.int32, xs.shape, 1) // (HEAD_DIM // 4)
        partner = jnp.where(blk % 2 == 0,
                            pltpu.roll(xs, lanes - HEAD_DIM // 4, 1),
                            pltpu.roll(xs, HEAD_DIM // 4, 1))
        outs.append(xs * cos + partner * sin_signed)
    return outs[0] if len(outs) == 1 else jnp.concatenate(outs, axis=-1)


def _qkv_kernel(*refs, layer, rope, cache_out):
    h_ref, mod_ref, g_ref, w_hbm, gain_ref = refs[:5]
    w_ref, stage_ref, sem_ref = refs[-3:]
    pos = 5
    if rope:
        cos_ref, sin_ref = refs[pos:pos + 2]
        pos += 2
    q_ref, kt_ref, v_ref = refs[pos:pos + 3]
    pos += 3

    @pl.when(pl.program_id(0) == 0)
    def _load_weights():
        _load_by_cols(w_hbm, layer, w_ref, stage_ref, sem_ref)

    xn = _adaln(h_ref[...], g_ref[...], mod_ref, 1).astype(BF16)
    y = _dot(xn, w_ref[...])
    o = 0
    qa = _head_rms(y[:, o:o + NA_W], gain_ref[:, :NA_W]); o += NA_W
    ka = _head_rms(y[:, o:o + NA_W], gain_ref[:, NA_W:2 * NA_W]); o += NA_W
    va = y[:, o:o + NA_W]; o += NA_W
    qb = _head_rms(y[:, o:o + GQ_W], gain_ref[:, 2 * NA_W:2 * NA_W + GQ_W]); o += GQ_W
    kb = _head_rms(y[:, o:o + GKV_W], gain_ref[:, 2 * NA_W + GQ_W:]); o += GKV_W
    vb = y[:, o:o + GKV_W]
    if rope:
        cos, sin = cos_ref[...], sin_ref[...]
        qb = _rope(qb, cos, sin)
        kb = _rope(kb, cos, sin)
    q_ref[:, :NA_W] = (qa * Q_SCALE).astype(BF16)
    q_ref[:, NA_W:] = (qb * Q_SCALE).astype(BF16)
    low = _low_half()
    vb_swapped = pltpu.roll(vb, HEAD_DIM, 1)
    v_ref[:, :NA_W] = va.astype(BF16)
    v_ref[:, NA_W:NA_W + LANES] = jnp.where(low, vb, vb_swapped).astype(BF16)
    v_ref[:, NA_W + LANES:] = jnp.where(low, vb_swapped, vb).astype(BF16)
    n_seq, _, s = kt_ref.shape
    for i in range(n_seq):
        rows = slice(i * s, (i + 1) * s)
        ka_t = ka[rows].T
        kb_t = kb[rows].T
        k0, k1 = kb_t[:HEAD_DIM], kb_t[HEAD_DIM:]
        kt_ref[i, :NA_W, :] = ka_t.astype(BF16)
        kt_ref[i, NA_W:, :] = jnp.concatenate([k0, k0, k1, k1], axis=0).astype(BF16)
        if cache_out:
            kat_ref, vat_ref, kbt_ref, vbt_ref = refs[pos:pos + 4]
            kat_ref[i] = ka_t
            vat_ref[i] = va[rows].T
            kbt_ref[i] = kb_t
            vbt_ref[i] = vb[rows].T


def _qkv(h, row0, t, mod, mod_layer, cond0, g, w_in, layer, gains, cond_div, seq, rope_tabs=None,
         cache_out=False):
    tm = PROJ_TM
    rope = rope_tabs is not None
    assert not (rope and cache_out)
    row = lambda i: (i, 0)
    const = lambda i: (0, 0)
    in_specs = [
        pl.BlockSpec((tm, D_MODEL), lambda i: (i + row0 // tm, 0)),
        _mod_spec(mod_layer, lambda i: cond0 + i // cond_div),
        _norm_spec(mod_layer, 1),
        HBM,
        pl.BlockSpec((1, 2 * NA_W + GQ_W + GKV_W), const),
    ]
    args = [h, mod, g, w_in, gains]
    if rope:
        seq_tiles = rope_tabs[0].shape[0] // tm
        in_specs += [pl.BlockSpec((tm, 2 * HEAD_DIM), lambda i: (i % seq_tiles, 0))] * 2
        args += list(rope_tabs)
    if seq >= tm:
        per_seq = seq // tm
        t_block = lambda rows: pl.BlockSpec((1, rows, tm), lambda i: (i // per_seq, 0, i % per_seq))
    else:
        t_block = lambda rows: pl.BlockSpec((tm // seq, rows, seq), lambda i: (i, 0, 0))
    t_shape = lambda rows, dtype: jax.ShapeDtypeStruct((t // seq, rows, seq), dtype)
    out_specs = [pl.BlockSpec((tm, NA_W + GQ_W), row), t_block(KT_ROWS), pl.BlockSpec((tm, KT_ROWS), row)]
    out_shape = [jax.ShapeDtypeStruct((t, NA_W + GQ_W), BF16), t_shape(KT_ROWS, BF16),
                 jax.ShapeDtypeStruct((t, KT_ROWS), BF16)]
    if cache_out:
        out_specs += [t_block(NA_W), t_block(NA_W), t_block(GKV_W), t_block(GKV_W)]
        out_shape += [t_shape(NA_W, F32), t_shape(NA_W, F32), t_shape(GKV_W, F32), t_shape(GKV_W, F32)]
    return pl.pallas_call(
        functools.partial(_qkv_kernel, layer=layer, rope=rope, cache_out=cache_out),
        grid=(t // tm,),
        in_specs=in_specs,
        out_specs=out_specs,
        out_shape=out_shape,
        scratch_shapes=[pltpu.VMEM((D_MODEL, QKV_WIDTH), BF16), *COL_STAGE],
        compiler_params=_params("arbitrary"),
        name="qkv_latent" if rope else "qkv_context",
    )(*args)


def _split_pair(q2):
    low = _low_half()
    zero = jnp.zeros((), q2.dtype)
    return jnp.where(low, q2, zero), jnp.where(low, zero, q2)


def _with_ones(v2):
    return jnp.concatenate([v2, jnp.ones(v2.shape, v2.dtype)], axis=1)


def _join(blocks, axis):
    return blocks[0] if len(blocks) == 1 else jnp.concatenate(blocks, axis=axis)


def _attend(problems):
    scores = [_dot(qm, _join(kts, 1)) for qm, kts, _, _ in problems]
    probs = []
    for s, (_, kts, biases, _) in zip(scores, problems):
        parts, start = [], 0
        for kt, b in zip(kts, biases):
            part = s[:, start:start + kt.shape[1]]
            parts.append(part if b is None else part + b)
            start += kt.shape[1]
        m = functools.reduce(jnp.maximum, [jnp.max(part, axis=-1, keepdims=True) for part in parts])
        probs.append(_join([jnp.exp(part - m).astype(BF16) for part in parts], 1))
    return [_dot(p, _join(vexts, 0)) for p, (_, _, _, vexts) in zip(probs, problems)]


def _merge_pair(oe_even, oe_odd):
    even = oe_even[:, :LANES] * (1.0 / oe_even[:, LANES:])
    odd = oe_odd[:, :LANES] * (1.0 / oe_odd[:, LANES:])
    return jnp.where(_low_half(), even, odd)


def _gqa_groups(q_ref, q_off, o_ref, o_off, keys_values):
    tq = q_ref.shape[0]
    slabs = lambda g, off: [slice(off + (2 * g + j) * LANES, off + (2 * g + j + 1) * LANES) for j in range(2)]
    problems = []
    for g in range(GQA_KV_HEADS):
        qm = jnp.concatenate([part for sl in slabs(g, q_off) for part in _split_pair(q_ref[:, sl])], axis=0)
        problems.append((qm, *keys_values(slice(g * LANES, (g + 1) * LANES))))
    for g, oe in enumerate(_attend(problems)):
        for j, sl in enumerate(slabs(g, o_off)):
            o_ref[:, sl] = _merge_pair(oe[2 * j * tq:(2 * j + 1) * tq],
                                       oe[(2 * j + 1) * tq:(2 * j + 2) * tq]).astype(o_ref.dtype)


def _ctx_attn_kernel(q_blk, kt_blk, v_blk, o_blk):
    n_seq, _, seq = kt_blk.shape
    for b in range(n_seq):
        rows = pl.ds(b * seq, seq)
        q_ref, v_ref, o_ref, kt_ref = q_blk.at[rows], v_blk.at[rows], o_blk.at[rows], kt_blk.at[b]
        for p in range(NA_HEADS // 2):
            sl = slice(p * LANES, (p + 1) * LANES)
            vext = _with_ones(v_ref[:, sl])
            oe = [_attend([(qm, [kt_ref[sl, :]], [None], [vext])])[0] for qm in _split_pair(q_ref[:, sl])]
            o_ref[:, sl] = _merge_pair(*oe).astype(o_ref.dtype)
        _gqa_groups(q_ref, NA_W, o_ref, NA_W,
                    lambda kv: ([kt_ref[NA_W + kv.start:NA_W + kv.stop, :]], [None],
                                [_with_ones(v_ref[:, NA_W + kv.start:NA_W + kv.stop])]))


def _ctx_attention(q, kt, v, seq):
    t = q.shape[0]
    n_seq = CTX_SEQ_PER_STEP
    row = lambda b: (b, 0)
    return pl.pallas_call(
        _ctx_attn_kernel,
        grid=(t // (n_seq * seq),),
        in_specs=[pl.BlockSpec((n_seq * seq, NA_W + GQ_W), row),
                  pl.BlockSpec((n_seq, KT_ROWS, seq), lambda b: (b, 0, 0)),
                  pl.BlockSpec((n_seq * seq, KT_ROWS), row)],
        out_specs=pl.BlockSpec((n_seq * seq, NA_W + GQ_W), row),
        out_shape=jax.ShapeDtypeStruct((t, NA_W + GQ_W), BF16),
        compiler_params=_params("arbitrary"),
        name="ctx_attention",
    )(q, kt, v)


def _na_kernel(q_ref, kt0_ref, kt1_ref, kt2_ref, v0_ref, v1_ref, v2_ref, ckt_ref, cv_ref, bias_ref, o_ref):
    for p in range(NA_HEADS // 2):
        sl = slice(p * LANES, (p + 1) * LANES)
        kt_loc = jnp.concatenate([kt0_ref[sl, :], kt1_ref[sl, :], kt2_ref[sl, :]], axis=1)
        v_loc = _with_ones(jnp.concatenate([v0_ref[:, sl], v1_ref[:, sl], v2_ref[:, sl]], axis=0))
        v_ctx = _with_ones(cv_ref[:, sl])
        oe = _attend([(qm, [kt_loc, ckt_ref[sl, :]], [bias_ref[2 * p + half], None], [v_loc, v_ctx])
                      for half, qm in enumerate(_split_pair(q_ref[:, sl]))])
        o_ref[:, sl] = _merge_pair(*oe).astype(o_ref.dtype)


def _na_bias_tiles(rpb):
    qc = np.arange(GRID_W)
    kc = np.arange(GRID_W)
    ws = np.clip(qc - NA_WIN_C // 2, 0, GRID_W - NA_WIN_C)
    col_ok = (kc[None, :] >= ws[:, None]) & (kc[None, :] < ws[:, None] + NA_WIN_C)
    dc = np.clip(kc[None, :] - qc[:, None] + NA_WIN_C - 1, 0, 2 * NA_WIN_C - 2)
    pick = ((dc[:, :, None] == np.arange(2 * NA_WIN_C - 1)) & col_ok[:, :, None]).astype(np.float32)
    tiles = jnp.einsum("hrd,qkd->hrqk", rpb.astype(F32), pick, precision=lax.Precision.HIGHEST)
    tiles = jnp.where(col_ok, tiles, NEG_INF)
    return jnp.concatenate([tiles, tiles], axis=-1)


def _fill_band_bias(tiles_ref, bias_ref, band, rows):
    low = _low_half()
    masked = jnp.full((GRID_W, LANES), NEG_INF, F32)
    r0 = band * NA_BAND_ROWS
    k0 = int(np.clip(r0 - NA_WIN_R // 2, 0, rows - NA_BAND_KEY_ROWS))
    for h in range(NA_HEADS):
        for ri in range(NA_BAND_ROWS):
            r = r0 + ri
            start = int(np.clip(r - NA_WIN_R // 2, 0, rows - NA_WIN_R))
            tile = lambda kr: tiles_ref[h, kr - r + NA_WIN_R - 1] if start <= kr < start + NA_WIN_R else masked
            for m in range(NA_BAND_KEY_ROWS // 2):
                kr = k0 + 2 * m
                bias_ref[h, ri * GRID_W:(ri + 1) * GRID_W, m * LANES:(m + 1) * LANES] = jnp.where(
                    low, tile(kr), tile(kr + 1))


def _gqa_kernel(q_ref, kt_ref, v_ref, ckt_ref, cv_ref, o_ref):
    _gqa_groups(q_ref, 0, o_ref, 0,
                lambda kv: ([kt_ref[kv, :], ckt_ref[kv, :]], [None, None],
                            [_with_ones(v_ref[:, kv]), _with_ones(cv_ref[:, kv])]))


def _latent_attn_kernel(*refs, rows):
    n_na = 9
    n_gqa = 5
    tiles_ref = refs[n_na]
    oa_ref, ob_ref, bias_ref = refs[n_na + 1 + n_gqa:]
    nb = rows // NA_BAND_ROWS
    for band in sorted({0, 1, nb - 1}):
        pl.when(pl.program_id(1) == band)(functools.partial(_fill_band_bias, tiles_ref, bias_ref, band, rows))
    _gqa_kernel(*refs[n_na + 1:n_na + 1 + n_gqa], ob_ref)
    _na_kernel(*refs[:n_na], bias_ref, oa_ref)


def _latent_attention(q, kt, v, na_ctx_kt, na_ctx_v, bias_tiles, gqa_ctx_kt, gqa_ctx_v, batch, seq):
    rows = seq // GRID_W
    nb = rows // NA_BAND_ROWS
    past = na_ctx_v.shape[1]
    assert (rows - NA_BAND_KEY_ROWS) % NA_BAND_ROWS == 0
    n_chunks = NA_BAND_KEY_ROWS // NA_BAND_ROWS
    assert nb >= 3
    chunk0 = lambda j: jnp.clip(j - NA_WIN_R // 2 // NA_BAND_ROWS, 0, nb - n_chunks)
    kt_spec = lambda c: pl.BlockSpec((None, NA_W, NA_BAND_Q), lambda b, j: (b, 0, chunk0(j) + c))
    v_spec = lambda c: pl.BlockSpec((NA_BAND_Q, NA_W), lambda b, j: (b * nb + chunk0(j) + c, 0))
    tile = lambda lane_block: pl.BlockSpec((NA_BAND_Q, NA_W), lambda b, j: (b * nb + j, lane_block))
    dup = 2 * GKV_W
    assert NA_W == GQ_W
    na_specs = ([tile(0)] + [kt_spec(c) for c in range(n_chunks)] + [v_spec(c) for c in range(n_chunks)]
                + [pl.BlockSpec((None, NA_W, past), lambda b, j: (b, 0, 0)),
                   pl.BlockSpec((None, past, NA_W), lambda b, j: (b, 0, 0)),
                   pl.BlockSpec(bias_tiles.shape, lambda b, j: (0, 0, 0, 0))])
    gqa_specs = [tile(1),
                 pl.BlockSpec((None, dup, seq), lambda b, j: (b, NA_W // dup, 0)),
                 pl.BlockSpec((seq, dup), lambda b, j: (b, NA_W // dup)),
                 pl.BlockSpec((None, dup, past), lambda b, j: (b, 0, 0)),
                 pl.BlockSpec((None, past, dup), lambda b, j: (b, 0, 0))]
    return pl.pallas_call(
        functools.partial(_latent_attn_kernel, rows=rows),
        grid=(batch, nb),
        in_specs=na_specs + gqa_specs,
        out_specs=[tile(0), tile(0)],
        out_shape=[jax.ShapeDtypeStruct((batch * seq, NA_W), BF16), jax.ShapeDtypeStruct((batch * seq, GQ_W), BF16)],
        scratch_shapes=[pltpu.VMEM((NA_HEADS, NA_BAND_Q, NA_BAND_K), F32)],
        compiler_params=_params("arbitrary", "arbitrary"),
        name="latent_attention",
    )(q, *([kt] * n_chunks), *([v] * n_chunks), na_ctx_kt, na_ctx_v, bias_tiles, q, kt, v, gqa_ctx_kt, gqa_ctx_v)


def _sgu_kernel(h_ref, mod_ref, g_ref, win_hbm, vg_ref, ws_ref, bs_ref, wout_hbm, o_ref,
                win_ref, wout_ref, stage_in, sem_in, stage_out, sem_out, gated_ref, y_ref, *, layer):
    tm = h_ref.shape[0]
    n_chunks = y_ref.shape[0]
    per_chunk = SGU_ACT_CHUNK // SGU_GW

    def tile(first):
        if first:
            cols = _WeightStream(_col_pairs(win_hbm, layer, win_ref), stage_in, sem_in)
            rows = _WeightStream(_row_pairs(wout_hbm, layer, wout_ref), stage_out, sem_out)
        h = h_ref[...]
        xn = _adaln(h, g_ref[...], mod_ref, 1).astype(BF16)
        ssq = jnp.zeros((tm, 1), F32)
        for c in range(n_chunks):
            if first:
                cols.take(SGU_ACT_CHUNK // W_STAGE)
            y = _dot(xn, win_ref[c])
            y = 0.5 * y * (1.0 + lax.erf(y * math.sqrt(0.5)))
            y_ref[c] = y
            if c >= n_chunks // 2:
                ssq = ssq + jnp.sum(y * y, axis=-1, keepdims=True)
            if first:
                rows.take(1)
        if first:
            rows.take()
        v_scale = lax.rsqrt(ssq * (1.0 / SGU_WIDTH) + EPS)
        for c in range(tm // SGU_CHUNK):
            rs = slice(c * SGU_CHUNK, (c + 1) * SGU_CHUNK)
            for g in range(SGU_GROUPS):
                cs = slice(g * SGU_GW, (g + 1) * SGU_GW)
                in_chunk = slice((g % per_chunk) * SGU_GW, (g % per_chunk + 1) * SGU_GW)
                u = y_ref[g // per_chunk, rs, in_chunk]
                v = y_ref[n_chunks // 2 + g // per_chunk, rs, in_chunk]
                vn = ((v * v_scale[rs]) * vg_ref[:, cs]).astype(BF16)
                sv = _dot(ws_ref[g].astype(BF16), vn) + bs_ref[:, g:g + 1]
                gated_ref[rs, cs] = (u * sv).astype(BF16)
        o_ref[...] = h + _gate(mod_ref, 1) * _dot(gated_ref[...], wout_ref[...])

    pl.when(pl.program_id(0) == 0)(lambda: tile(True))
    pl.when(pl.program_id(0) > 0)(lambda: tile(False))


def _sgu(h, mod, mod_layer, g, w_in, v_g, w_s, b_s_t, w_out, layer, ctx_rows, dec_seq):
    tm = SGU_TM
    n_act = 2 * SGU_WIDTH // SGU_ACT_CHUNK
    _, _, _, cond = _two_stream_maps(tm, ctx_rows, dec_seq)
    row_spec = pl.BlockSpec((tm, D_MODEL), lambda i: (i, 0))
    const = lambda i: (0, 0)
    return pl.pallas_call(
        functools.partial(_sgu_kernel, layer=layer),
        grid=(h.shape[0] // tm,),
        in_specs=[
            row_spec,
            _mod_spec(mod_layer, cond),
            _norm_spec(mod_layer, 1),
            HBM,
            pl.BlockSpec((1, SGU_WIDTH), const),
            pl.BlockSpec((None, SGU_GROUPS, SGU_CHUNK, SGU_CHUNK), lambda i: (layer, 0, 0, 0)),
            pl.BlockSpec((SGU_CHUNK, SGU_GROUPS), const),
            HBM,
        ],
        out_specs=row_spec,
        out_shape=jax.ShapeDtypeStruct(h.shape, F32),
        scratch_shapes=[
            pltpu.VMEM((n_act, D_MODEL, SGU_ACT_CHUNK), BF16),
            pltpu.VMEM((SGU_WIDTH, D_MODEL), BF16),
            *COL_STAGE, *ROW_STAGE,
            pltpu.VMEM((tm, SGU_WIDTH), BF16),
            pltpu.VMEM((n_act, tm, SGU_ACT_CHUNK), F32),
        ],
        compiler_params=_params("arbitrary"),
        name="sgu",
    )(h, mod, g, w_in, v_g, w_s, b_s_t, w_out)


def _rope_tables(seq):
    half = HEAD_DIM // 2
    t = np.arange(seq)
    freqs = np.float32(ROPE_BASE) ** (-np.arange(0, half, 2, dtype=np.float32) / np.float32(half))
    def tab(pos):
        ang = pos.astype(np.float32)[:, None] * freqs[None, :]
        cos, sin = np.cos(ang), np.sin(ang)
        return np.concatenate([cos, cos], -1), np.concatenate([-sin, sin], -1)
    cr, sr = tab(t // GRID_W)
    cc, sc = tab(t % GRID_W)
    cos = np.concatenate([cr, cc], -1)
    sin = np.concatenate([sr, sc], -1)
    return jnp.asarray(np.tile(cos, (1, 2)), F32), jnp.asarray(np.tile(sin, (1, 2)), F32)


def kernel(x_prompt, x_sample, cache_na_k, cache_na_v, cache_gqa_k, cache_gqa_v, c, c_ctx, norm_g, mod_w, mod_b, ffn1_w_gu, ffn1_w_down, ffn2_w_gu, ffn2_w_down, attn_w_in, attn_w_out, na_q_g, na_k_g, na_rpb, gqa_q_g, gqa_k_g, sgu_w_in, sgu_v_g, sgu_w_s, sgu_b_s, sgu_w_out):
    batch, seq, _ = x_prompt.shape
    dec_batch, dec_seq, _ = x_sample.shape
    past = cache_na_k.shape[2]
    ctx_rows, lat_rows = batch * seq, dec_batch * dec_seq
    h = (x_prompt.reshape(ctx_rows, D_MODEL), x_sample.reshape(lat_rows, D_MODEL))

    assert 1 + dec_batch <= 8
    cond8 = jnp.concatenate([c_ctx[None], c, jnp.zeros((8 - 1 - dec_batch, D_MODEL), F32)], axis=0)
    mod = _modulation(cond8, mod_w, mod_b).reshape(DEPTH, 8, N_MOD, D_MODEL)

    g = norm_g.reshape(DEPTH * 3, 1, D_MODEL)
    new_cache = None
    for layer in range(DEPTH):
        h = _ffn(h, mod, g, ffn1_w_gu, ffn1_w_down, layer, 0, ctx_rows, dec_seq)

        if layer % 2 == 0:
            e = layer // 2
            gains = jnp.concatenate([jnp.tile(na_q_g[e], NA_HEADS), jnp.tile(na_k_g[e], NA_HEADS),
                                     jnp.tile(gqa_q_g[e], GQA_HEADS), jnp.tile(gqa_k_g[e], GQA_KV_HEADS)])[None]
            q, kt, v, *new_cache = _qkv(h, 0, ctx_rows, mod, layer, 0, g, attn_w_in, e, gains,
                                        ctx_rows // PROJ_TM, seq, cache_out=True)
            o_ctx = _ctx_attention(q, kt, v, seq)
            q, kt, v = _qkv(h, ctx_rows, lat_rows, mod, layer, 1, g, attn_w_in, e, gains,
                            dec_seq // PROJ_TM, dec_seq, rope_tabs=_rope_tables(dec_seq))
            to_kt = lambda ck, rep: jnp.repeat(jnp.transpose(ck[:, e], (0, 2, 3, 1)), rep, axis=1).reshape(
                dec_batch, -1, past).astype(BF16)
            to_v = lambda cv, rep: jnp.repeat(cv[:, e], rep, axis=2).reshape(dec_batch, past, -1).astype(BF16)
            oa, ob = _latent_attention(q, kt, v, to_kt(cache_na_k, 1), to_v(cache_na_v, 1),
                                       _na_bias_tiles(na_rpb[e]),
                                       to_kt(cache_gqa_k, 2), to_v(cache_gqa_v, 2), dec_batch, dec_seq)
            attn = (o_ctx, oa, ob, attn_w_out, e)
        else:
            attn = None
            o = layer // 2
            h = _sgu(h, mod, layer, g, sgu_w_in, sgu_v_g[o][None], sgu_w_s, sgu_b_s[o].T, sgu_w_out,
                     o, ctx_rows, dec_seq)

        h = _ffn(h, mod, g, ffn2_w_gu, ffn2_w_down, layer, 2, ctx_rows, dec_seq, attn=attn,
                 split_out=layer == DEPTH - 1)
    hp, hs = h

    n_attn = (DEPTH + 1) // 2
    assert n_attn == 1
    from_t = lambda x: jnp.transpose(x.reshape(batch, n_attn, -1, HEAD_DIM, seq), (0, 1, 4, 2, 3))
    return (hp.reshape(batch, seq, D_MODEL), hs.reshape(dec_batch, dec_seq, D_MODEL),
            *(from_t(x) for x in new_cache))
```

```python
import functools
import math

import jax
import jax.numpy as jnp
import numpy as np
from jax import lax
from jax.experimental import pallas as pl
from jax.experimental.pallas import tpu as pltpu

D_MODEL = 1024
DEPTH = 2
GRID_W = 64
HEAD_DIM = 64
NA_HEADS = 8
NA_WIN_R = 8
NA_WIN_C = 16
GQA_HEADS = 8
GQA_KV_HEADS = 2
GQA_GROUP = GQA_HEADS // GQA_KV_HEADS
NA_W = NA_HEADS * HEAD_DIM
GQ_W = GQA_HEADS * HEAD_DIM
GKV_W = GQA_KV_HEADS * HEAD_DIM
QKV_WIDTH = 3 * NA_W + GQ_W + 2 * GKV_W
ROPE_BASE = 10000.0
SGU_CHUNK = 128
SGU_GROUPS = 8
SGU_WIDTH = 2 * D_MODEL
SGU_GW = SGU_WIDTH // SGU_GROUPS
FFN_HIDDEN = 2816
N_MOD = 9
EPS = 1e-6
NEG_INF = -1e30
Q_SCALE = HEAD_DIM ** -0.5

NA_BAND_ROWS = 4
NA_BAND_KEY_ROWS = 12
NA_BAND_Q = NA_BAND_ROWS * GRID_W
NA_BAND_K = NA_BAND_KEY_ROWS * GRID_W
NA_PAIRS_TOGETHER = 2

FFN_TM = 512
FFN_ACT_CHUNK = 256
SGU_ACT_CHUNK = 512
SGU_TM = 512
MOD_TN = 3072
CTX_SEQ_PER_STEP = 4
PROJ_TM = 1024
W_STAGE = 256

LANES = 128
KT_ROWS = NA_W + 2 * GKV_W
MXU_COLS = 256
VMEM_LIMIT = 56 * 1024 * 1024

BF16 = jnp.bfloat16
F32 = jnp.float32


def _dot(a, b):
    return jnp.dot(a, b, preferred_element_type=F32)


def _params(*sem):
    return pltpu.CompilerParams(dimension_semantics=sem, vmem_limit_bytes=VMEM_LIMIT)


def _adaln(x, g, mod_ref, j):
    shift = mod_ref[0, 3 * j:3 * j + 1, :]
    scale = mod_ref[0, 3 * j + 1:3 * j + 2, :]
    y = x * lax.rsqrt(jnp.mean(x * x, axis=-1, keepdims=True) + EPS)
    return (y * g) * (1.0 + scale) + shift


def _gate(mod_ref, j):
    return mod_ref[0, 3 * j + 2:3 * j + 3, :]


def _mod_kernel(c_ref, w_ref, b_ref, o_ref):
    c = c_ref[...]
    a = (c * jax.nn.sigmoid(c)).astype(BF16)
    o_ref[...] = _dot(a, w_ref[...].astype(BF16)) + b_ref[...]


def _modulation(cond8, mod_w, mod_b):
    tn = MOD_TN
    n = N_MOD * D_MODEL
    return pl.pallas_call(
        _mod_kernel,
        grid=(DEPTH, n // tn),
        in_specs=[
            pl.BlockSpec((8, D_MODEL), lambda l, k: (0, 0)),
            pl.BlockSpec((None, D_MODEL, tn), lambda l, k: (l, 0, k)),
            pl.BlockSpec((None, 1, tn), lambda l, k: (l, 0, k)),
        ],
        out_specs=pl.BlockSpec((None, 8, tn), lambda l, k: (l, 0, k)),
        out_shape=jax.ShapeDtypeStruct((DEPTH, 8, n), F32),
        compiler_params=_params("arbitrary", "arbitrary"),
        name="modulation",
    )(cond8, mod_w, mod_b.reshape(DEPTH, 1, n))


def _two_stream_maps(tm, n_ctx_rows, dec_seq):
    n_ctx = n_ctx_rows // tm
    per_seq = dec_seq // tm
    ctx_map = lambda i: (jnp.minimum(i, n_ctx - 1), 0)
    lat_map = lambda i: (jnp.maximum(i - n_ctx, 0), 0)
    cond = lambda i: jnp.where(i < n_ctx, 0, 1 + (i - n_ctx) // per_seq)
    return n_ctx, ctx_map, lat_map, cond


def _norm_spec(layer, j):
    return pl.BlockSpec((None, 1, D_MODEL), lambda i: (3 * layer + j, 0, 0))


def _mod_spec(layer, cond):
    return pl.BlockSpec((None, 1, N_MOD, D_MODEL), lambda i: (layer, cond(i), 0, 0))


class _WeightStream:
    def __init__(self, pairs, stage_ref, sem_ref):
        self.pairs, self.stage, self.sem, self.done = pairs, stage_ref, sem_ref, 0
        for k in range(min(2, len(pairs))):
            self._copy(k).start()

    def _copy(self, k):
        return pltpu.make_async_copy(self.pairs[k][0], self.stage.at[k % 2], self.sem.at[k % 2])

    def take(self, n=None):
        n = len(self.pairs) - self.done if n is None else min(n, len(self.pairs) - self.done)
        for _ in range(n):
            k = self.done
            self._copy(k).wait()
            dst, idx = self.pairs[k][1]
            dst[idx] = self.stage[k % 2].astype(BF16)
            if k + 2 < len(self.pairs):
                self._copy(k + 2).start()
            self.done += 1


def _col_pairs(w_hbm, layer, w_ref, cols=None):
    if len(w_ref.shape) == 2:
        dst = lambda c: (slice(None), slice(c, c + W_STAGE))
        n = w_ref.shape[1]
    else:
        width = w_ref.shape[2]
        dst = lambda c: (c // width, slice(None), slice(c % width, c % width + W_STAGE))
        n = w_ref.shape[0] * width
    cols = range(0, n, W_STAGE) if cols is None else cols
    return [(w_hbm.at[layer, :, pl.ds(c, W_STAGE)], (w_ref, dst(c))) for c in cols]


def _row_pairs(w_hbm, layer, w_ref):
    return [(w_hbm.at[layer, pl.ds(r, W_STAGE), :], (w_ref, (slice(r, r + W_STAGE), slice(None))))
            for r in range(0, w_ref.shape[0], W_STAGE)]


def _load_by_cols(w_hbm, layer, w_ref, stage_ref, sem_ref):
    _WeightStream(_col_pairs(w_hbm, layer, w_ref), stage_ref, sem_ref).take()


COL_STAGE = (pltpu.VMEM((2, D_MODEL, W_STAGE), F32), pltpu.SemaphoreType.DMA((2,)))
ROW_STAGE = (pltpu.VMEM((2, W_STAGE, D_MODEL), F32), pltpu.SemaphoreType.DMA((2,)))
HBM = pl.BlockSpec(memory_space=pl.ANY)


def _ffn_kernel(*refs, layer, j, n_ctx, proj_layer, split_in, split_out):
    h_refs = refs[:2 if split_in else 1]
    pos = len(h_refs)
    mod_ref, g_ref, wgu_hbm, wd_hbm = refs[pos:pos + 4]
    pos += 4
    if proj_layer is not None:
        octx_ref, oa_ref, ob_ref, wo_hbm = refs[pos:pos + 4]
        pos += 4
    out_refs = refs[pos:pos + (2 if split_out else 1)]
    pos += len(out_refs)
    wgu_ref, wd_ref, stage_gu, sem_gu, stage_d, sem_d, act_ref = refs[pos:pos + 7]
    wo_ref = refs[pos + 7] if proj_layer is not None else None
    i = pl.program_id(0)
    is_ctx = i < n_ctx
    chunks = [(c, min(FFN_ACT_CHUNK, FFN_HIDDEN - c)) for c in range(0, FFN_HIDDEN, FFN_ACT_CHUNK)]

    def tile(first):
        if first:
            need = [col for c, w in chunks for base in (c, FFN_HIDDEN + c) for col in range(base, base + w, W_STAGE)]
            cols = _WeightStream(_col_pairs(wgu_hbm, layer, wgu_ref, need), stage_gu, sem_gu)
            proj = _row_pairs(wo_hbm, proj_layer, wo_ref) if proj_layer is not None else []
            rows = _WeightStream(proj + _row_pairs(wd_hbm, layer, wd_ref), stage_d, sem_d)
            rows.take(len(proj))
        if first or not split_in:
            h = h_refs[0][...]
        else:
            h = jnp.where(is_ctx, h_refs[0][...], h_refs[1][...])
        if proj_layer is not None:
            latent_heads = jnp.concatenate([oa_ref[...], ob_ref[...]], axis=1)
            heads = octx_ref[...] if first else jnp.where(is_ctx, octx_ref[...], latent_heads)
            h = h + _gate(mod_ref, 1) * _dot(heads, wo_ref[...])
        xn = _adaln(h, g_ref[...], mod_ref, j).astype(BF16)
        for c, w in chunks:
            if first:
                cols.take(2 * w // W_STAGE)
            gate = _dot(xn, wgu_ref[:, c:c + w])
            up = _dot(xn, wgu_ref[:, FFN_HIDDEN + c:FFN_HIDDEN + c + w])
            act_ref[:, c:c + w] = ((gate * jax.nn.sigmoid(gate)) * up).astype(BF16)
            if first:
                rows.take(2)
        if first:
            rows.take()
        res = h + (0.5 * _gate(mod_ref, j)) * _dot(act_ref[...], wd_ref[...])
        if first or not split_out:
            out_refs[0][...] = res
        else:
            @pl.when(is_ctx)
            def _():
                out_refs[0][...] = res

            @pl.when(jnp.logical_not(is_ctx))
            def _():
                out_refs[1][...] = res

    pl.when(i == 0)(lambda: tile(True))
    pl.when(i > 0)(lambda: tile(False))


def _ffn(h, mod, g, w_gu, w_down, layer, j, ctx_rows, dec_seq, attn=None, split_out=False):
    tm = FFN_TM
    split_in = isinstance(h, tuple)
    h_arrays = list(h) if split_in else [h]
    total_rows = sum(a.shape[0] for a in h_arrays)
    n_ctx, ctx_map, lat_map, cond = _two_stream_maps(tm, ctx_rows, dec_seq)
    assert n_ctx >= 1
    row_map = lambda i: (i, 0)
    row_spec = lambda m, width=D_MODEL: pl.BlockSpec((tm, width), m)
    in_specs = ([row_spec(ctx_map), row_spec(lat_map)] if split_in else [row_spec(row_map)]) + [
        _mod_spec(layer, cond),
        _norm_spec(layer, j),
        HBM, HBM]
    args = [*h_arrays, mod, g, w_gu, w_down]
    scratch = [pltpu.VMEM((D_MODEL, 2 * FFN_HIDDEN), BF16), pltpu.VMEM((FFN_HIDDEN, D_MODEL), BF16),
               *COL_STAGE, *ROW_STAGE, pltpu.VMEM((tm, FFN_HIDDEN), BF16)]
    proj_layer = None
    if attn is not None:
        o_ctx, oa, ob, w_out, proj_layer = attn
        in_specs += [row_spec(ctx_map, o_ctx.shape[1]), row_spec(lat_map, oa.shape[1]),
                     row_spec(lat_map, ob.shape[1]), HBM]
        args += [o_ctx, oa, ob, w_out]
        scratch.append(pltpu.VMEM((o_ctx.shape[1], D_MODEL), BF16))
    if split_out:
        out_specs = [row_spec(ctx_map), row_spec(lat_map)]
        out_shape = [jax.ShapeDtypeStruct((ctx_rows, D_MODEL), F32),
                     jax.ShapeDtypeStruct((total_rows - ctx_rows, D_MODEL), F32)]
    else:
        out_specs = row_spec(row_map)
        out_shape = jax.ShapeDtypeStruct((total_rows, D_MODEL), F32)
    return pl.pallas_call(
        functools.partial(_ffn_kernel, layer=layer, j=j, n_ctx=n_ctx, proj_layer=proj_layer,
                          split_in=split_in, split_out=split_out),
        grid=(total_rows // tm,),
        in_specs=in_specs,
        out_specs=out_specs,
        out_shape=out_shape,
        scratch_shapes=scratch,
        compiler_params=_params("arbitrary"),
        name=f"ffn{j}_l{layer}",
    )(*args)


def _head_mean_sq(x):
    w = x.shape[-1]
    r = lax.broadcasted_iota(jnp.int32, (MXU_COLS, MXU_COLS), 0) // HEAD_DIM
    c = lax.broadcasted_iota(jnp.int32, (MXU_COLS, MXU_COLS), 1) // HEAD_DIM
    bd = jnp.where(r == c, 1.0 / HEAD_DIM, 0.0).astype(BF16)
    sq = (x * x).astype(BF16)
    parts = []
    for s in range(0, w, MXU_COLS):
        e = min(s + MXU_COLS, w)
        parts.append(_dot(sq[:, s:e], bd[:e - s, :e - s]))
    return parts[0] if len(parts) == 1 else jnp.concatenate(parts, axis=-1)


def _low_half():
    return lax.broadcasted_iota(jnp.int32, (1, LANES), 1) < HEAD_DIM


def _head_rms(x, gain):
    return (x * lax.rsqrt(_head_mean_sq(x) + EPS)) * gain


def _rope(x, cos, sin_signed):
    lanes = cos.shape[-1]
    outs = []
    for s in range(0, x.shape[-1], lanes):
        xs = x[:, s:s + lanes]
        blk = lax.broadcasted_iota(jnp.int32, xs.shape, 1) // (HEAD_DIM // 4)
        partner = jnp.where(blk % 2 == 0,
                            pltpu.roll(xs, lanes - HEAD_DIM // 4, 1),
                            pltpu.roll(xs, HEAD_DIM // 4, 1))
        outs.append(xs * cos + partner * sin_signed)
    return outs[0] if len(outs) == 1 else jnp.concatenate(outs, axis=-1)


def _qkv_kernel(*refs, layer, rope, cache_out):
    h_ref, mod_ref, g_ref, w_hbm, gain_ref = refs[:5]
    w_ref, stage_ref, sem_ref = refs[-3:]
    pos = 5
    if rope:
        cos_ref, sin_ref = refs[pos:pos + 2]
        pos += 2
    q_ref, kt_ref, v_ref = refs[pos:pos + 3]
    pos += 3

    @pl.when(pl.program_id(0) == 0)
    def _load_weights():
        _load_by_cols(w_hbm, layer, w_ref, stage_ref, sem_ref)

    xn = _adaln(h_ref[...], g_ref[...], mod_ref, 1).astype(BF16)
    y = _dot(xn, w_ref[...])
    o = 0
    qa = _head_rms(y[:, o:o + NA_W], gain_ref[:, :NA_W]); o += NA_W
    ka = _head_rms(y[:, o:o + NA_W], gain_ref[:, NA_W:2 * NA_W]); o += NA_W
    va = y[:, o:o + NA_W]; o += NA_W
    qb = _head_rms(y[:, o:o + GQ_W], gain_ref[:, 2 * NA_W:2 * NA_W + GQ_W]); o += GQ_W
    kb = _head_rms(y[:, o:o + GKV_W], gain_ref[:, 2 * NA_W + GQ_W:]); o += GKV_W
    vb = y[:, o:o + GKV_W]
    if rope:
        cos, sin = cos_ref[...], sin_ref[...]
        qb = _rope(qb, cos, sin)
        kb = _rope(kb, cos, sin)
    q_ref[:, :NA_W] = (qa * Q_SCALE).astype(BF16)
    q_ref[:, NA_W:] = (qb * Q_SCALE).astype(BF16)
    low = _low_half()
    vb_swapped = pltpu.roll(vb, HEAD_DIM, 1)
    v_ref[:, :NA_W] = va.astype(BF16)
    v_ref[:, NA_W:NA_W + LANES] = jnp.where(low, vb, vb_swapped).astype(BF16)
    v_ref[:, NA_W + LANES:] = jnp.where(low, vb_swapped, vb).astype(BF16)
    n_seq, _, s = kt_ref.shape
    for i in range(n_seq):
        rows = slice(i * s, (i + 1) * s)
        ka_t = ka[rows].T
        kb_t = kb[rows].T
        k0, k1 = kb_t[:HEAD_DIM], kb_t[HEAD_DIM:]
        kt_ref[i, :NA_W, :] = ka_t.astype(BF16)
        kt_ref[i, NA_W:, :] = jnp.concatenate([k0, k0, k1, k1], axis=0).astype(BF16)
        if cache_out:
            kat_ref, vat_ref, kbt_ref, vbt_ref = refs[pos:pos + 4]
            kat_ref[i] = ka_t
            vat_ref[i] = va[rows].T
            kbt_ref[i] = kb_t
            vbt_ref[i] = vb[rows].T


def _qkv(h, row0, t, mod, mod_layer, cond0, g, w_in, layer, gains, cond_div, seq, rope_tabs=None,
         cache_out=False):
    tm = PROJ_TM
    rope = rope_tabs is not None
    assert not (rope and cache_out)
    row = lambda i: (i, 0)
    const = lambda i: (0, 0)
    in_specs = [
        pl.BlockSpec((tm, D_MODEL), lambda i: (i + row0 // tm, 0)),
        _mod_spec(mod_layer, lambda i: cond0 + i // cond_div),
        _norm_spec(mod_layer, 1),
        HBM,
        pl.BlockSpec((1, 2 * NA_W + GQ_W + GKV_W), const),
    ]
    args = [h, mod, g, w_in, gains]
    if rope:
        seq_tiles = rope_tabs[0].shape[0] // tm
        in_specs += [pl.BlockSpec((tm, 2 * HEAD_DIM), lambda i: (i % seq_tiles, 0))] * 2
        args += list(rope_tabs)
    if seq >= tm:
        per_seq = seq // tm
        t_block = lambda rows: pl.BlockSpec((1, rows, tm), lambda i: (i // per_seq, 0, i % per_seq))
    else:
        t_block = lambda rows: pl.BlockSpec((tm // seq, rows, seq), lambda i: (i, 0, 0))
    t_shape = lambda rows, dtype: jax.ShapeDtypeStruct((t // seq, rows, seq), dtype)
    out_specs = [pl.BlockSpec((tm, NA_W + GQ_W), row), t_block(KT_ROWS), pl.BlockSpec((tm, KT_ROWS), row)]
    out_shape = [jax.ShapeDtypeStruct((t, NA_W + GQ_W), BF16), t_shape(KT_ROWS, BF16),
                 jax.ShapeDtypeStruct((t, KT_ROWS), BF16)]
    if cache_out:
        out_specs += [t_block(NA_W), t_block(NA_W), t_block(GKV_W), t_block(GKV_W)]
        out_shape += [t_shape(NA_W, F32), t_shape(NA_W, F32), t_shape(GKV_W, F32), t_shape(GKV_W, F32)]
    return pl.pallas_call(
        functools.partial(_qkv_kernel, layer=layer, rope=rope, cache_out=cache_out),
        grid=(t // tm,),
        in_specs=in_specs,
        out_specs=out_specs,
        out_shape=out_shape,
        scratch_shapes=[pltpu.VMEM((D_MODEL, QKV_WIDTH), BF16), *COL_STAGE],
        compiler_params=_params("arbitrary"),
        name="qkv_latent" if rope else "qkv_context",
    )(*args)


def _split_pair(q2):
    low = _low_half()
    zero = jnp.zeros((), q2.dtype)
    return jnp.where(low, q2, zero), jnp.where(low, zero, q2)


def _with_ones(v2):
    return jnp.concatenate([v2, jnp.ones(v2.shape, v2.dtype)], axis=1)


def _join(blocks, axis):
    return blocks[0] if len(blocks) == 1 else jnp.concatenate(blocks, axis=axis)


def _attend(problems):
    scores = [_dot(qm, _join(kts, 1)) for qm, kts, _, _ in problems]
    probs = []
    for s, (_, kts, biases, _) in zip(scores, problems):
        parts, start = [], 0
        for kt, b in zip(kts, biases):
            part = s[:, start:start + kt.shape[1]]
            parts.append(part if b is None else part + b)
            start += kt.shape[1]
        m = functools.reduce(jnp.maximum, [jnp.max(part, axis=-1, keepdims=True) for part in parts])
        probs.append(_join([jnp.exp(part - m).astype(BF16) for part in parts], 1))
    return [_dot(p, _join(vexts, 0)) for p, (_, _, _, vexts) in zip(probs, problems)]


def _merge_pair(oe_even, oe_odd):
    even = oe_even[:, :LANES] * (1.0 / oe_even[:, LANES:])
    odd = oe_odd[:, :LANES] * (1.0 / oe_odd[:, LANES:])
    return jnp.where(_low_half(), even, odd)


def _gqa_groups(q_ref, q_off, o_ref, o_off, keys_values):
    tq = q_ref.shape[0]
    slabs = lambda g, off: [slice(off + (2 * g + j) * LANES, off + (2 * g + j + 1) * LANES) for j in range(2)]
    problems = []
    for g in range(GQA_KV_HEADS):
        qm = jnp.concatenate([part for sl in slabs(g, q_off) for part in _split_pair(q_ref[:, sl])], axis=0)
        problems.append((qm, *keys_values(slice(g * LANES, (g + 1) * LANES))))
    for g, oe in enumerate(_attend(problems)):
        for j, sl in enumerate(slabs(g, o_off)):
            o_ref[:, sl] = _merge_pair(oe[2 * j * tq:(2 * j + 1) * tq],
                                       oe[(2 * j + 1) * tq:(2 * j + 2) * tq]).astype(o_ref.dtype)


def _ctx_attn_kernel(q_blk, kt_blk, v_blk, o_blk):
    n_seq, _, seq = kt_blk.shape
    for b in range(n_seq):
        rows = pl.ds(b * seq, seq)
        q_ref, v_ref, o_ref, kt_ref = q_blk.at[rows], v_blk.at[rows], o_blk.at[rows], kt_blk.at[b]
        for p in range(NA_HEADS // 2):
            sl = slice(p * LANES, (p + 1) * LANES)
            vext = _with_ones(v_ref[:, sl])
            oe = [_attend([(qm, [kt_ref[sl, :]], [None], [vext])])[0] for qm in _split_pair(q_ref[:, sl])]
            o_ref[:, sl] = _merge_pair(*oe).astype(o_ref.dtype)
        _gqa_groups(q_ref, NA_W, o_ref, NA_W,
                    lambda kv: ([kt_ref[NA_W + kv.start:NA_W + kv.stop, :]], [None],
                                [_with_ones(v_ref[:, NA_W + kv.start:NA_W + kv.stop])]))


def _ctx_attention(q, kt, v, seq):
    t = q.shape[0]
    n_seq = CTX_SEQ_PER_STEP
    row = lambda b: (b, 0)
    return pl.pallas_call(
        _ctx_attn_kernel,
        grid=(t // (n_seq * seq),),
        in_specs=[pl.BlockSpec((n_seq * seq, NA_W + GQ_W), row),
                  pl.BlockSpec((n_seq, KT_ROWS, seq), lambda b: (b, 0, 0)),
                  pl.BlockSpec((n_seq * seq, KT_ROWS), row)],
        out_specs=pl.BlockSpec((n_seq * seq, NA_W + GQ_W), row),
        out_shape=jax.ShapeDtypeStruct((t, NA_W + GQ_W), BF16),
        compiler_params=_params("arbitrary"),
        name="ctx_attention",
    )(q, kt, v)


def _na_kernel(q_ref, kt0_ref, kt1_ref, kt2_ref, v0_ref, v1_ref, v2_ref, ckt_ref, cv_ref, bias_ref, o_ref):
    for p0 in range(0, NA_HEADS // 2, NA_PAIRS_TOGETHER):
        problems, slabs = [], []
        for p in range(p0, p0 + NA_PAIRS_TOGETHER):
            sl = slice(p * LANES, (p + 1) * LANES)
            kt_loc = jnp.concatenate([kt0_ref[sl, :], kt1_ref[sl, :], kt2_ref[sl, :]], axis=1)
            v_loc = _with_ones(jnp.concatenate([v0_ref[:, sl], v1_ref[:, sl], v2_ref[:, sl]], axis=0))
            v_ctx = _with_ones(cv_ref[:, sl])
            problems += [(qm, [kt_loc, ckt_ref[sl, :]], [bias_ref[2 * p + half], None], [v_loc, v_ctx])
                         for half, qm in enumerate(_split_pair(q_ref[:, sl]))]
            slabs.append(sl)
        oe = _attend(problems)
        for k, sl in enumerate(slabs):
            o_ref[:, sl] = _merge_pair(oe[2 * k], oe[2 * k + 1]).astype(o_ref.dtype)


def _na_bias_tiles(rpb):
    qc = np.arange(GRID_W)
    kc = np.arange(GRID_W)
    ws = np.clip(qc - NA_WIN_C // 2, 0, GRID_W - NA_WIN_C)
    col_ok = (kc[None, :] >= ws[:, None]) & (kc[None, :] < ws[:, None] + NA_WIN_C)
    dc = np.clip(kc[None, :] - qc[:, None] + NA_WIN_C - 1, 0, 2 * NA_WIN_C - 2)
    pick = ((dc[:, :, None] == np.arange(2 * NA_WIN_C - 1)) & col_ok[:, :, None]).astype(np.float32)
    tiles = jnp.einsum("hrd,qkd->hrqk", rpb.astype(F32), pick, precision=lax.Precision.HIGHEST)
    tiles = jnp.where(col_ok, tiles, NEG_INF)
    return jnp.concatenate([tiles, tiles], axis=-1)


def _fill_band_bias(tiles_ref, bias_ref, band, rows):
    low = _low_half()
    masked = jnp.full((GRID_W, LANES), NEG_INF, F32)
    r0 = band * NA_BAND_ROWS
    k0 = int(np.clip(r0 - NA_WIN_R // 2, 0, rows - NA_BAND_KEY_ROWS))
    for h in range(NA_HEADS):
        for ri in range(NA_BAND_ROWS):
            r = r0 + ri
            start = int(np.clip(r - NA_WIN_R // 2, 0, rows - NA_WIN_R))
            tile = lambda kr: tiles_ref[h, kr - r + NA_WIN_R - 1] if start <= kr < start + NA_WIN_R else masked
            for m in range(NA_BAND_KEY_ROWS // 2):
                kr = k0 + 2 * m
                bias_ref[h, ri * GRID_W:(ri + 1) * GRID_W, m * LANES:(m + 1) * LANES] = jnp.where(
                    low, tile(kr), tile(kr + 1))


def _gqa_kernel(q_ref, kt_ref, v_ref, ckt_ref, cv_ref, o_ref):
    _gqa_groups(q_ref, 0, o_ref, 0,
                lambda kv: ([kt_ref[kv, :], ckt_ref[kv, :]], [None, None],
                            [_with_ones(v_ref[:, kv]), _with_ones(cv_ref[:, kv])]))


def _latent_attn_kernel(*refs, rows):
    n_na = 9
    n_gqa = 5
    tiles_ref = refs[n_na]
    oa_ref, ob_ref, bias_ref = refs[n_na + 1 + n_gqa:]
    nb = rows // NA_BAND_ROWS
    for band in sorted({0, 1, nb - 1}):
        pl.when(pl.program_id(1) == band)(functools.partial(_fill_band_bias, tiles_ref, bias_ref, band, rows))
    _gqa_kernel(*refs[n_na + 1:n_na + 1 + n_gqa], ob_ref)
    _na_kernel(*refs[:n_na], bias_ref, oa_ref)


def _latent_attention(q, kt, v, na_ctx_kt, na_ctx_v, bias_tiles, gqa_ctx_kt, gqa_ctx_v, batch, seq):
    rows = seq // GRID_W
    nb = rows // NA_BAND_ROWS
    past = na_ctx_v.shape[1]
    assert (rows - NA_BAND_KEY_ROWS) % NA_BAND_ROWS == 0
    n_chunks = NA_BAND_KEY_ROWS // NA_BAND_ROWS
    assert nb >= 3
    chunk0 = lambda j: jnp.clip(j - NA_WIN_R // 2 // NA_BAND_ROWS, 0, nb - n_chunks)
    kt_spec = lambda c: pl.BlockSpec((None, NA_W, NA_BAND_Q), lambda b, j: (b, 0, chunk0(j) + c))
    v_spec = lambda c: pl.BlockSpec((NA_BAND_Q, NA_W), lambda b, j: (b * nb + chunk0(j) + c, 0))
    tile = lambda lane_block: pl.BlockSpec((NA_BAND_Q, NA_W), lambda b, j: (b * nb + j, lane_block))
    dup = 2 * GKV_W
    assert NA_W == GQ_W
    na_specs = ([tile(0)] + [kt_spec(c) for c in range(n_chunks)] + [v_spec(c) for c in range(n_chunks)]
                + [pl.BlockSpec((None, NA_W, past), lambda b, j: (b, 0, 0)),
                   pl.BlockSpec((None, past, NA_W), lambda b, j: (b, 0, 0)),
                   pl.BlockSpec(bias_tiles.shape, lambda b, j: (0, 0, 0, 0))])
    gqa_specs = [tile(1),
                 pl.BlockSpec((None, dup, seq), lambda b, j: (b, NA_W // dup, 0)),
                 pl.BlockSpec((seq, dup), lambda b, j: (b, NA_W // dup)),
                 pl.BlockSpec((None, dup, past), lambda b, j: (b, 0, 0)),
                 pl.BlockSpec((None, past, dup), lambda b, j: (b, 0, 0))]
    return pl.pallas_call(
        functools.partial(_latent_attn_kernel, rows=rows),
        grid=(batch, nb),
        in_specs=na_specs + gqa_specs,
        out_specs=[tile(0), tile(0)],
        out_shape=[jax.ShapeDtypeStruct((batch * seq, NA_W), BF16), jax.ShapeDtypeStruct((batch * seq, GQ_W), BF16)],
        scratch_shapes=[pltpu.VMEM((NA_HEADS, NA_BAND_Q, NA_BAND_K), F32)],
        compiler_params=_params("arbitrary", "arbitrary"),
        name="latent_attention",
    )(q, *([kt] * n_chunks), *([v] * n_chunks), na_ctx_kt, na_ctx_v, bias_tiles, q, kt, v, gqa_ctx_kt, gqa_ctx_v)


def _sgu_kernel(h_ref, mod_ref, g_ref, win_hbm, vg_ref, ws_ref, bs_ref, wout_hbm, o_ref,
                win_ref, wout_ref, stage_in, sem_in, stage_out, sem_out, gated_ref, y_ref, *, layer):
    tm = h_ref.shape[0]
    n_chunks = y_ref.shape[0]
    per_chunk = SGU_ACT_CHUNK // SGU_GW

    def tile(first):
        if first:
            cols = _WeightStream(_col_pairs(win_hbm, layer, win_ref), stage_in, sem_in)
            rows = _WeightStream(_row_pairs(wout_hbm, layer, wout_ref), stage_out, sem_out)
        h = h_ref[...]
        xn = _adaln(h, g_ref[...], mod_ref, 1).astype(BF16)
        ssq = jnp.zeros((tm, 1), F32)
        for c in range(n_chunks):
            if first:
                cols.take(SGU_ACT_CHUNK // W_STAGE)
            y = _dot(xn, win_ref[c])
            y = 0.5 * y * (1.0 + lax.erf(y * math.sqrt(0.5)))
            y_ref[c] = y
            if c >= n_chunks // 2:
                ssq = ssq + jnp.sum(y * y, axis=-1, keepdims=True)
            if first:
                rows.take(1)
        if first:
            rows.take()
        v_scale = lax.rsqrt(ssq * (1.0 / SGU_WIDTH) + EPS)
        for c in range(tm // SGU_CHUNK):
            rs = slice(c * SGU_CHUNK, (c + 1) * SGU_CHUNK)
            for g in range(SGU_GROUPS):
                cs = slice(g * SGU_GW, (g + 1) * SGU_GW)
                in_chunk = slice((g % per_chunk) * SGU_GW, (g % per_chunk + 1) * SGU_GW)
                u = y_ref[g // per_chunk, rs, in_chunk]
                v = y_ref[n_chunks // 2 + g // per_chunk, rs, in_chunk]
                vn = ((v * v_scale[rs]) * vg_ref[:, cs]).astype(BF16)
                sv = _dot(ws_ref[g].astype(BF16), vn) + bs_ref[:, g:g + 1]
                gated_ref[rs, cs] = (u * sv).astype(BF16)
        o_ref[...] = h + _gate(mod_ref, 1) * _dot(gated_ref[...], wout_ref[...])

    pl.when(pl.program_id(0) == 0)(lambda: tile(True))
    pl.when(pl.program_id(0) > 0)(lambda: tile(False))


def _sgu(h, mod, mod_layer, g, w_in, v_g, w_s, b_s_t, w_out, layer, ctx_rows, dec_seq):
    tm = SGU_TM
    n_act = 2 * SGU_WIDTH // SGU_ACT_CHUNK
    _, _, _, cond = _two_stream_maps(tm, ctx_rows, dec_seq)
    row_spec = pl.BlockSpec((tm, D_MODEL), lambda i: (i, 0))
    const = lambda i: (0, 0)
    return pl.pallas_call(
        functools.partial(_sgu_kernel, layer=layer),
        grid=(h.shape[0] // tm,),
        in_specs=[
            row_spec,
            _mod_spec(mod_layer, cond),
            _norm_spec(mod_layer, 1),
            HBM,
            pl.BlockSpec((1, SGU_WIDTH), const),
            pl.BlockSpec((None, SGU_GROUPS, SGU_CHUNK, SGU_CHUNK), lambda i: (layer, 0, 0, 0)),
            pl.BlockSpec((SGU_CHUNK, SGU_GROUPS), const),
            HBM,
        ],
        out_specs=row_spec,
        out_shape=jax.ShapeDtypeStruct(h.shape, F32),
        scratch_shapes=[
            pltpu.VMEM((n_act, D_MODEL, SGU_ACT_CHUNK), BF16),
            pltpu.VMEM((SGU_WIDTH, D_MODEL), BF16),
            *COL_STAGE, *ROW_STAGE,
            pltpu.VMEM((tm, SGU_WIDTH), BF16),
            pltpu.VMEM((n_act, tm, SGU_ACT_CHUNK), F32),
        ],
        compiler_params=_params("arbitrary"),
        name="sgu",
    )(h, mod, g, w_in, v_g, w_s, b_s_t, w_out)


def _rope_tables(seq):
    half = HEAD_DIM // 2
    t = np.arange(seq)
    freqs = np.float32(ROPE_BASE) ** (-np.arange(0, half, 2, dtype=np.float32) / np.float32(half))
    def tab(pos):
        ang = pos.astype(np.float32)[:, None] * freqs[None, :]
        cos, sin = np.cos(ang), np.sin(ang)
        return np.concatenate([cos, cos], -1), np.concatenate([-sin, sin], -1)
    cr, sr = tab(t // GRID_W)
    cc, sc = tab(t % GRID_W)
    cos = np.concatenate([cr, cc], -1)
    sin = np.concatenate([sr, sc], -1)
    return jnp.asarray(np.tile(cos, (1, 2)), F32), jnp.asarray(np.tile(sin, (1, 2)), F32)


def kernel(x_prompt, x_sample, cache_na_k, cache_na_v, cache_gqa_k, cache_gqa_v, c, c_ctx, norm_g, mod_w, mod_b, ffn1_w_gu, ffn1_w_down, ffn2_w_gu, ffn2_w_down, attn_w_in, attn_w_out, na_q_g, na_k_g, na_rpb, gqa_q_g, gqa_k_g, sgu_w_in, sgu_v_g, sgu_w_s, sgu_b_s, sgu_w_out):
    batch, seq, _ = x_prompt.shape
    dec_batch, dec_seq, _ = x_sample.shape
    past = cache_na_k.shape[2]
    ctx_rows, lat_rows = batch * seq, dec_batch * dec_seq
    h = (x_prompt.reshape(ctx_rows, D_MODEL), x_sample.reshape(lat_rows, D_MODEL))

    assert 1 + dec_batch <= 8
    cond8 = jnp.concatenate([c_ctx[None], c, jnp.zeros((8 - 1 - dec_batch, D_MODEL), F32)], axis=0)
    mod = _modulation(cond8, mod_w, mod_b).reshape(DEPTH, 8, N_MOD, D_MODEL)

    g = norm_g.reshape(DEPTH * 3, 1, D_MODEL)
    new_cache = None
    for layer in range(DEPTH):
        h = _ffn(h, mod, g, ffn1_w_gu, ffn1_w_down, layer, 0, ctx_rows, dec_seq)

        if layer % 2 == 0:
            e = layer // 2
            gains = jnp.concatenate([jnp.tile(na_q_g[e], NA_HEADS), jnp.tile(na_k_g[e], NA_HEADS),
                                     jnp.tile(gqa_q_g[e], GQA_HEADS), jnp.tile(gqa_k_g[e], GQA_KV_HEADS)])[None]
            q, kt, v, *new_cache = _qkv(h, 0, ctx_rows, mod, layer, 0, g, attn_w_in, e, gains,
                                        ctx_rows // PROJ_TM, seq, cache_out=True)
            o_ctx = _ctx_attention(q, kt, v, seq)
            q, kt, v = _qkv(h, ctx_rows, lat_rows, mod, layer, 1, g, attn_w_in, e, gains,
                            dec_seq // PROJ_TM, dec_seq, rope_tabs=_rope_tables(dec_seq))
            to_kt = lambda ck, rep: jnp.repeat(jnp.transpose(ck[:, e], (0, 2, 3, 1)), rep, axis=1).reshape(
                dec_batch, -1, past).astype(BF16)
            to_v = lambda cv, rep: jnp.repeat(cv[:, e], rep, axis=2).reshape(dec_batch, past, -1).astype(BF16)
            oa, ob = _latent_attention(q, kt, v, to_kt(cache_na_k, 1), to_v(cache_na_v, 1),
                                       _na_bias_tiles(na_rpb[e]),
                                       to_kt(cache_gqa_k, 2), to_v(cache_gqa_v, 2), dec_batch, dec_seq)
            attn = (o_ctx, oa, ob, attn_w_out, e)
        else:
            attn = None
            o = layer // 2
            h = _sgu(h, mod, layer, g, sgu_w_in, sgu_v_g[o][None], sgu_w_s, sgu_b_s[o].T, sgu_w_out,
                     o, ctx_rows, dec_seq)

        h = _ffn(h, mod, g, ffn2_w_gu, ffn2_w_down, layer, 2, ctx_rows, dec_seq, attn=attn,
                 split_out=layer == DEPTH - 1)
    hp, hs = h

    n_attn = (DEPTH + 1) // 2
    assert n_attn == 1
    from_t = lambda x: jnp.transpose(x.reshape(batch, n_attn, -1, HEAD_DIM, seq), (0, 1, 4, 2, 3))
    return (hp.reshape(batch, seq, D_MODEL), hs.reshape(dec_batch, dec_seq, D_MODEL),
            *(from_t(x) for x in new_cache))
```

```python
import functools
import math

import jax
import jax.numpy as jnp
import numpy as np
from jax import lax
from jax.experimental import pallas as pl
from jax.experimental.pallas import tpu as pltpu

D_MODEL = 1024
DEPTH = 2
GRID_W = 64
HEAD_DIM = 64
NA_HEADS = 8
NA_WIN_R = 8
NA_WIN_C = 16
GQA_HEADS = 8
GQA_KV_HEADS = 2
GQA_GROUP = GQA_HEADS // GQA_KV_HEADS
NA_W = NA_HEADS * HEAD_DIM
GQ_W = GQA_HEADS * HEAD_DIM
GKV_W = GQA_KV_HEADS * HEAD_DIM
QKV_WIDTH = 3 * NA_W + GQ_W + 2 * GKV_W
ROPE_BASE = 10000.0
SGU_CHUNK = 128
SGU_GROUPS = 8
SGU_WIDTH = 2 * D_MODEL
SGU_GW = SGU_WIDTH // SGU_GROUPS
FFN_HIDDEN = 2816
N_MOD = 9
EPS = 1e-6
NEG_INF = -1e30
Q_SCALE = HEAD_DIM ** -0.5

NA_BAND_ROWS = 4
NA_BAND_KEY_ROWS = 12
NA_BAND_Q = NA_BAND_ROWS * GRID_W
NA_BAND_K = NA_BAND_KEY_ROWS * GRID_W
NA_PAIRS_TOGETHER = 2

FFN_TM = 512
FFN_ACT_CHUNK = 256
FFN_PROJ_ACT_CHUNK = 512
SGU_ACT_CHUNK = 512
SGU_TM = 512
MOD_TN = 3072
CTX_SEQ_PER_STEP = 4
PROJ_TM = 1024
W_STAGE = 256

LANES = 128
KT_ROWS = NA_W + 2 * GKV_W
MXU_COLS = 256
VMEM_LIMIT = 56 * 1024 * 1024

BF16 = jnp.bfloat16
F32 = jnp.float32


def _dot(a, b):
    return jnp.dot(a, b, preferred_element_type=F32)


def _params(*sem):
    return pltpu.CompilerParams(dimension_semantics=sem, vmem_limit_bytes=VMEM_LIMIT)


def _adaln(x, g, mod_ref, j):
    shift = mod_ref[0, 3 * j:3 * j + 1, :]
    scale = mod_ref[0, 3 * j + 1:3 * j + 2, :]
    y = x * lax.rsqrt(jnp.mean(x * x, axis=-1, keepdims=True) + EPS)
    return (y * g) * (1.0 + scale) + shift


def _gate(mod_ref, j):
    return mod_ref[0, 3 * j + 2:3 * j + 3, :]


def _mod_kernel(c_ref, w_ref, b_ref, o_ref):
    c = c_ref[...]
    a = (c * jax.nn.sigmoid(c)).astype(BF16)
    o_ref[...] = _dot(a, w_ref[...].astype(BF16)) + b_ref[...]


def _modulation(cond8, mod_w, mod_b):
    tn = MOD_TN
    n = N_MOD * D_MODEL
    return pl.pallas_call(
        _mod_kernel,
        grid=(DEPTH, n // tn),
        in_specs=[
            pl.BlockSpec((8, D_MODEL), lambda l, k: (0, 0)),
            pl.BlockSpec((None, D_MODEL, tn), lambda l, k: (l, 0, k)),
            pl.BlockSpec((None, 1, tn), lambda l, k: (l, 0, k)),
        ],
        out_specs=pl.BlockSpec((None, 8, tn), lambda l, k: (l, 0, k)),
        out_shape=jax.ShapeDtypeStruct((DEPTH, 8, n), F32),
        compiler_params=_params("arbitrary", "arbitrary"),
        name="modulation",
    )(cond8, mod_w, mod_b.reshape(DEPTH, 1, n))


def _two_stream_maps(tm, n_ctx_rows, dec_seq):
    n_ctx = n_ctx_rows // tm
    per_seq = dec_seq // tm
    ctx_map = lambda i: (jnp.minimum(i, n_ctx - 1), 0)
    lat_map = lambda i: (jnp.maximum(i - n_ctx, 0), 0)
    cond = lambda i: jnp.where(i < n_ctx, 0, 1 + (i - n_ctx) // per_seq)
    return n_ctx, ctx_map, lat_map, cond


def _norm_spec(layer, j):
    return pl.BlockSpec((None, 1, D_MODEL), lambda i: (3 * layer + j, 0, 0))


def _mod_spec(layer, cond):
    return pl.BlockSpec((None, 1, N_MOD, D_MODEL), lambda i: (layer, cond(i), 0, 0))


class _WeightStream:
    def __init__(self, pairs, stage_ref, sem_ref):
        self.pairs, self.stage, self.sem, self.done = pairs, stage_ref, sem_ref, 0
        for k in range(min(2, len(pairs))):
            self._copy(k).start()

    def _copy(self, k):
        return pltpu.make_async_copy(self.pairs[k][0], self.stage.at[k % 2], self.sem.at[k % 2])

    def take(self, n=None):
        n = len(self.pairs) - self.done if n is None else min(n, len(self.pairs) - self.done)
        for _ in range(n):
            k = self.done
            self._copy(k).wait()
            dst, idx = self.pairs[k][1]
            dst[idx] = self.stage[k % 2].astype(BF16)
            if k + 2 < len(self.pairs):
                self._copy(k + 2).start()
            self.done += 1


def _col_pairs(w_hbm, layer, w_ref, cols=None):
    if len(w_ref.shape) == 2:
        dst = lambda c: (slice(None), slice(c, c + W_STAGE))
        n = w_ref.shape[1]
    else:
        width = w_ref.shape[2]
        dst = lambda c: (c // width, slice(None), slice(c % width, c % width + W_STAGE))
        n = w_ref.shape[0] * width
    cols = range(0, n, W_STAGE) if cols is None else cols
    return [(w_hbm.at[layer, :, pl.ds(c, W_STAGE)], (w_ref, dst(c))) for c in cols]


def _row_pairs(w_hbm, layer, w_ref):
    return [(w_hbm.at[layer, pl.ds(r, W_STAGE), :], (w_ref, (slice(r, r + W_STAGE), slice(None))))
            for r in range(0, w_ref.shape[0], W_STAGE)]


def _load_by_cols(w_hbm, layer, w_ref, stage_ref, sem_ref):
    _WeightStream(_col_pairs(w_hbm, layer, w_ref), stage_ref, sem_ref).take()


COL_STAGE = (pltpu.VMEM((2, D_MODEL, W_STAGE), F32), pltpu.SemaphoreType.DMA((2,)))
ROW_STAGE = (pltpu.VMEM((2, W_STAGE, D_MODEL), F32), pltpu.SemaphoreType.DMA((2,)))
HBM = pl.BlockSpec(memory_space=pl.ANY)


def _ffn_kernel(*refs, layer, j, n_ctx, proj_layer, split_in, split_out):
    h_refs = refs[:2 if split_in else 1]
    pos = len(h_refs)
    mod_ref, g_ref, wgu_hbm, wd_hbm = refs[pos:pos + 4]
    pos += 4
    if proj_layer is not None:
        octx_ref, oa_ref, ob_ref, wo_hbm = refs[pos:pos + 4]
        pos += 4
    out_refs = refs[pos:pos + (2 if split_out else 1)]
    pos += len(out_refs)
    wgu_ref, wd_ref, stage_gu, sem_gu, stage_d, sem_d, act_ref = refs[pos:pos + 7]
    wo_ref = refs[pos + 7] if proj_layer is not None else None
    i = pl.program_id(0)
    is_ctx = i < n_ctx
    act_chunk = FFN_ACT_CHUNK if proj_layer is None else FFN_PROJ_ACT_CHUNK
    chunks = [(c, min(act_chunk, FFN_HIDDEN - c)) for c in range(0, FFN_HIDDEN, act_chunk)]

    def tile(first):
        if first:
            need = [col for c, w in chunks for base in (c, FFN_HIDDEN + c) for col in range(base, base + w, W_STAGE)]
            cols = _WeightStream(_col_pairs(wgu_hbm, layer, wgu_ref, need), stage_gu, sem_gu)
            proj = _row_pairs(wo_hbm, proj_layer, wo_ref) if proj_layer is not None else []
            rows = _WeightStream(proj + _row_pairs(wd_hbm, layer, wd_ref), stage_d, sem_d)
            rows.take(len(proj))
        if first or not split_in:
            h = h_refs[0][...]
        else:
            h = jnp.where(is_ctx, h_refs[0][...], h_refs[1][...])
        if proj_layer is not None:
            latent_heads = jnp.concatenate([oa_ref[...], ob_ref[...]], axis=1)
            heads = octx_ref[...] if first else jnp.where(is_ctx, octx_ref[...], latent_heads)
            h = h + _gate(mod_ref, 1) * _dot(heads, wo_ref[...])
        xn = _adaln(h, g_ref[...], mod_ref, j).astype(BF16)
        for c, w in chunks:
            if first:
                cols.take(2 * w // W_STAGE)
            gate = _dot(xn, wgu_ref[:, c:c + w])
            up = _dot(xn, wgu_ref[:, FFN_HIDDEN + c:FFN_HIDDEN + c + w])
            act_ref[:, c:c + w] = ((gate * jax.nn.sigmoid(gate)) * up).astype(BF16)
            if first:
                rows.take(2)
        if first:
            rows.take()
        res = h + (0.5 * _gate(mod_ref, j)) * _dot(act_ref[...], wd_ref[...])
        if first or not split_out:
            out_refs[0][...] = res
        else:
            @pl.when(is_ctx)
            def _():
                out_refs[0][...] = res

            @pl.when(jnp.logical_not(is_ctx))
            def _():
                out_refs[1][...] = res

    pl.when(i == 0)(lambda: tile(True))
    pl.when(i > 0)(lambda: tile(False))


def _ffn(h, mod, g, w_gu, w_down, layer, j, ctx_rows, dec_seq, attn=None, split_out=False):
    tm = FFN_TM
    split_in = isinstance(h, tuple)
    h_arrays = list(h) if split_in else [h]
    total_rows = sum(a.shape[0] for a in h_arrays)
    n_ctx, ctx_map, lat_map, cond = _two_stream_maps(tm, ctx_rows, dec_seq)
    assert n_ctx >= 1
    row_map = lambda i: (i, 0)
    row_spec = lambda m, width=D_MODEL: pl.BlockSpec((tm, width), m)
    in_specs = ([row_spec(ctx_map), row_spec(lat_map)] if split_in else [row_spec(row_map)]) + [
        _mod_spec(layer, cond),
        _norm_spec(layer, j),
        HBM, HBM]
    args = [*h_arrays, mod, g, w_gu, w_down]
    scratch = [pltpu.VMEM((D_MODEL, 2 * FFN_HIDDEN), BF16), pltpu.VMEM((FFN_HIDDEN, D_MODEL), BF16),
               *COL_STAGE, *ROW_STAGE, pltpu.VMEM((tm, FFN_HIDDEN), BF16)]
    proj_layer = None
    if attn is not None:
        o_ctx, oa, ob, w_out, proj_layer = attn
        in_specs += [row_spec(ctx_map, o_ctx.shape[1]), row_spec(lat_map, oa.shape[1]),
                     row_spec(lat_map, ob.shape[1]), HBM]
        args += [o_ctx, oa, ob, w_out]
        scratch.append(pltpu.VMEM((o_ctx.shape[1], D_MODEL), BF16))
    if split_out:
        out_specs = [row_spec(ctx_map), row_spec(lat_map)]
        out_shape = [jax.ShapeDtypeStruct((ctx_rows, D_MODEL), F32),
                     jax.ShapeDtypeStruct((total_rows - ctx_rows, D_MODEL), F32)]
    else:
        out_specs = row_spec(row_map)
        out_shape = jax.ShapeDtypeStruct((total_rows, D_MODEL), F32)
    return pl.pallas_call(
        functools.partial(_ffn_kernel, layer=layer, j=j, n_ctx=n_ctx, proj_layer=proj_layer,
                          split_in=split_in, split_out=split_out),
        grid=(total_rows // tm,),
        in_specs=in_specs,
        out_specs=out_specs,
        out_shape=out_shape,
        scratch_shapes=scratch,
        compiler_params=_params("arbitrary"),
        name=f"ffn{j}_l{layer}",
    )(*args)


def _head_mean_sq(x):
    w = x.shape[-1]
    r = lax.broadcasted_iota(jnp.int32, (MXU_COLS, MXU_COLS), 0) // HEAD_DIM
    c = lax.broadcasted_iota(jnp.int32, (MXU_COLS, MXU_COLS), 1) // HEAD_DIM
    bd = jnp.where(r == c, 1.0 / HEAD_DIM, 0.0).astype(BF16)
    sq = (x * x).astype(BF16)
    parts = []
    for s in range(0, w, MXU_COLS):
        e = min(s + MXU_COLS, w)
        parts.append(_dot(sq[:, s:e], bd[:e - s, :e - s]))
    return parts[0] if len(parts) == 1 else jnp.concatenate(parts, axis=-1)


def _low_half():
    return lax.broadcasted_iota(jnp.int32, (1, LANES), 1) < HEAD_DIM


def _head_rms(x, gain):
    return (x * lax.rsqrt(_head_mean_sq(x) + EPS)) * gain


def _rope(x, cos, sin_signed):
    lanes = cos.shape[-1]
    outs = []
    for s in range(0, x.shape[-1], lanes):
        xs = x[:, s:s + lanes]
        blk = lax.broadcasted_iota(jnp.int32, xs.shape, 1) // (HEAD_DIM // 4)
        partner = jnp.where(blk % 2 == 0,
                            pltpu.roll(xs, lanes - HEAD_DIM // 4, 1),
                            pltpu.roll(xs, HEAD_DIM // 4, 1))
        outs.append(xs * cos + partner * sin_signed)
    return outs[0] if len(outs) == 1 else jnp.concatenate(outs, axis=-1)


def _qkv_kernel(*refs, layer, rope, cache_out):
    h_ref, mod_ref, g_ref, w_hbm, gain_ref = refs[:5]
    w_ref, stage_ref, sem_ref = refs[-3:]
    pos = 5
    if rope:
        cos_ref, sin_ref = refs[pos:pos + 2]
        pos += 2
    q_ref, kt_ref, v_ref = refs[pos:pos + 3]
    pos += 3

    @pl.when(pl.program_id(0) == 0)
    def _load_weights():
        _load_by_cols(w_hbm, layer, w_ref, stage_ref, sem_ref)

    xn = _adaln(h_ref[...], g_ref[...], mod_ref, 1).astype(BF16)
    y = _dot(xn, w_ref[...])
    o = 0
    qa = _head_rms(y[:, o:o + NA_W], gain_ref[:, :NA_W]); o += NA_W
    ka = _head_rms(y[:, o:o + NA_W], gain_ref[:, NA_W:2 * NA_W]); o += NA_W
    va = y[:, o:o + NA_W]; o += NA_W
    qb = _head_rms(y[:, o:o + GQ_W], gain_ref[:, 2 * NA_W:2 * NA_W + GQ_W]); o += GQ_W
    kb = _head_rms(y[:, o:o + GKV_W], gain_ref[:, 2 * NA_W + GQ_W:]); o += GKV_W
    vb = y[:, o:o + GKV_W]
    if rope:
        cos, sin = cos_ref[...], sin_ref[...]
        qb = _rope(qb, cos, sin)
        kb = _rope(kb, cos, sin)
    q_ref[:, :NA_W] = (qa * Q_SCALE).astype(BF16)
    q_ref[:, NA_W:] = (qb * Q_SCALE).astype(BF16)
    low = _low_half()
    vb_swapped = pltpu.roll(vb, HEAD_DIM, 1)
    v_ref[:, :NA_W] = va.astype(BF16)
    v_ref[:, NA_W:NA_W + LANES] = jnp.where(low, vb, vb_swapped).astype(BF16)
    v_ref[:, NA_W + LANES:] = jnp.where(low, vb_swapped, vb).astype(BF16)
    n_seq, _, s = kt_ref.shape
    for i in range(n_seq):
        rows = slice(i * s, (i + 1) * s)
        ka_t = ka[rows].T
        kb_t = kb[rows].T
        k0, k1 = kb_t[:HEAD_DIM], kb_t[HEAD_DIM:]
        kt_ref[i, :NA_W, :] = ka_t.astype(BF16)
        kt_ref[i, NA_W:, :] = jnp.concatenate([k0, k0, k1, k1], axis=0).astype(BF16)
        if cache_out:
            kat_ref, vat_ref, kbt_ref, vbt_ref = refs[pos:pos + 4]
            kat_ref[i] = ka_t
            vat_ref[i] = va[rows].T
            kbt_ref[i] = kb_t
            vbt_ref[i] = vb[rows].T


def _qkv(h, row0, t, mod, mod_layer, cond0, g, w_in, layer, gains, cond_div, seq, rope_tabs=None,
         cache_out=False):
    tm = PROJ_TM
    rope = rope_tabs is not None
    assert not (rope and cache_out)
    row = lambda i: (i, 0)
    const = lambda i: (0, 0)
    in_specs = [
        pl.BlockSpec((tm, D_MODEL), lambda i: (i + row0 // tm, 0)),
        _mod_spec(mod_layer, lambda i: cond0 + i // cond_div),
        _norm_spec(mod_layer, 1),
        HBM,
        pl.BlockSpec((1, 2 * NA_W + GQ_W + GKV_W), const),
    ]
    args = [h, mod, g, w_in, gains]
    if rope:
        seq_tiles = rope_tabs[0].shape[0] // tm
        in_specs += [pl.BlockSpec((tm, 2 * HEAD_DIM), lambda i: (i % seq_tiles, 0))] * 2
        args += list(rope_tabs)
    if seq >= tm:
        per_seq = seq // tm
        t_block = lambda rows: pl.BlockSpec((1, rows, tm), lambda i: (i // per_seq, 0, i % per_seq))
    else:
        t_block = lambda rows: pl.BlockSpec((tm // seq, rows, seq), lambda i: (i, 0, 0))
    t_shape = lambda rows, dtype: jax.ShapeDtypeStruct((t // seq, rows, seq), dtype)
    out_specs = [pl.BlockSpec((tm, NA_W + GQ_W), row), t_block(KT_ROWS), pl.BlockSpec((tm, KT_ROWS), row)]
    out_shape = [jax.ShapeDtypeStruct((t, NA_W + GQ_W), BF16), t_shape(KT_ROWS, BF16),
                 jax.ShapeDtypeStruct((t, KT_ROWS), BF16)]
    if cache_out:
        out_specs += [t_block(NA_W), t_block(NA_W), t_block(GKV_W), t_block(GKV_W)]
        out_shape += [t_shape(NA_W, F32), t_shape(NA_W, F32), t_shape(GKV_W, F32), t_shape(GKV_W, F32)]
    return pl.pallas_call(
        functools.partial(_qkv_kernel, layer=layer, rope=rope, cache_out=cache_out),
        grid=(t // tm,),
        in_specs=in_specs,
        out_specs=out_specs,
        out_shape=out_shape,
        scratch_shapes=[pltpu.VMEM((D_MODEL, QKV_WIDTH), BF16), *COL_STAGE],
        compiler_params=_params("arbitrary"),
        name="qkv_latent" if rope else "qkv_context",
    )(*args)


def _split_pair(q2):
    low = _low_half()
    zero = jnp.zeros((), q2.dtype)
    return jnp.where(low, q2, zero), jnp.where(low, zero, q2)


def _with_ones(v2):
    return jnp.concatenate([v2, jnp.ones(v2.shape, v2.dtype)], axis=1)


def _join(blocks, axis):
    return blocks[0] if len(blocks) == 1 else jnp.concatenate(blocks, axis=axis)


def _attend(problems):
    scores = [_dot(qm, _join(kts, 1)) for qm, kts, _, _ in problems]
    probs = []
    for s, (_, kts, biases, _) in zip(scores, problems):
        parts, start = [], 0
        for kt, b in zip(kts, biases):
            part = s[:, start:start + kt.shape[1]]
            parts.append(part if b is None else part + b)
            start += kt.shape[1]
        m = functools.reduce(jnp.maximum, [jnp.max(part, axis=-1, keepdims=True) for part in parts])
        probs.append(_join([jnp.exp(part - m).astype(BF16) for part in parts], 1))
    return [_dot(p, _join(vexts, 0)) for p, (_, _, _, vexts) in zip(probs, problems)]


def _merge_pair(oe_even, oe_odd):
    even = oe_even[:, :LANES] * (1.0 / oe_even[:, LANES:])
    odd = oe_odd[:, :LANES] * (1.0 / oe_odd[:, LANES:])
    return jnp.where(_low_half(), even, odd)


def _gqa_groups(q_ref, q_off, o_ref, o_off, keys_values):
    tq = q_ref.shape[0]
    slabs = lambda g, off: [slice(off + (2 * g + j) * LANES, off + (2 * g + j + 1) * LANES) for j in range(2)]
    problems = []
    for g in range(GQA_KV_HEADS):
        qm = jnp.concatenate([part for sl in slabs(g, q_off) for part in _split_pair(q_ref[:, sl])], axis=0)
        problems.append((qm, *keys_values(slice(g * LANES, (g + 1) * LANES))))
    for g, oe in enumerate(_attend(problems)):
        for j, sl in enumerate(slabs(g, o_off)):
            o_ref[:, sl] = _merge_pair(oe[2 * j * tq:(2 * j + 1) * tq],
                                       oe[(2 * j + 1) * tq:(2 * j + 2) * tq]).astype(o_ref.dtype)


def _ctx_attn_kernel(q_blk, kt_blk, v_blk, o_blk):
    n_seq, _, seq = kt_blk.shape
    for b in range(n_seq):
        rows = pl.ds(b * seq, seq)
        q_ref, v_ref, o_ref, kt_ref = q_blk.at[rows], v_blk.at[rows], o_blk.at[rows], kt_blk.at[b]
        for p in range(NA_HEADS // 2):
            sl = slice(p * LANES, (p + 1) * LANES)
            vext = _with_ones(v_ref[:, sl])
            oe = [_attend([(qm, [kt_ref[sl, :]], [None], [vext])])[0] for qm in _split_pair(q_ref[:, sl])]
            o_ref[:, sl] = _merge_pair(*oe).astype(o_ref.dtype)
        _gqa_groups(q_ref, NA_W, o_ref, NA_W,
                    lambda kv: ([kt_ref[NA_W + kv.start:NA_W + kv.stop, :]], [None],
                                [_with_ones(v_ref[:, NA_W + kv.start:NA_W + kv.stop])]))


def _ctx_attention(q, kt, v, seq):
    t = q.shape[0]
    n_seq = CTX_SEQ_PER_STEP
    row = lambda b: (b, 0)
    return pl.pallas_call(
        _ctx_attn_kernel,
        grid=(t // (n_seq * seq),),
        in_specs=[pl.BlockSpec((n_seq * seq, NA_W + GQ_W), row),
                  pl.BlockSpec((n_seq, KT_ROWS, seq), lambda b: (b, 0, 0)),
                  pl.BlockSpec((n_seq * seq, KT_ROWS), row)],
        out_specs=pl.BlockSpec((n_seq * seq, NA_W + GQ_W), row),
        out_shape=jax.ShapeDtypeStruct((t, NA_W + GQ_W), BF16),
        compiler_params=_params("arbitrary"),
        name="ctx_attention",
    )(q, kt, v)


def _na_kernel(q_ref, kt0_ref, kt1_ref, kt2_ref, v0_ref, v1_ref, v2_ref, ckt_ref, cv_ref, bias_ref, o_ref):
    for p0 in range(0, NA_HEADS // 2, NA_PAIRS_TOGETHER):
        problems, slabs = [], []
        for p in range(p0, p0 + NA_PAIRS_TOGETHER):
            sl = slice(p * LANES, (p + 1) * LANES)
            kt_loc = jnp.concatenate([kt0_ref[sl, :], kt1_ref[sl, :], kt2_ref[sl, :]], axis=1)
            v_loc = _with_ones(jnp.concatenate([v0_ref[:, sl], v1_ref[:, sl], v2_ref[:, sl]], axis=0))
            v_ctx = _with_ones(cv_ref[:, sl])
            problems += [(qm, [kt_loc, ckt_ref[sl, :]], [bias_ref[2 * p + half], None], [v_loc, v_ctx])
                         for half, qm in enumerate(_split_pair(q_ref[:, sl]))]
            slabs.append(sl)
        oe = _attend(problems)
        for k, sl in enumerate(slabs):
            o_ref[:, sl] = _merge_pair(oe[2 * k], oe[2 * k + 1]).astype(o_ref.dtype)


def _na_bias_tiles(rpb):
    qc = np.arange(GRID_W)
    kc = np.arange(GRID_W)
    ws = np.clip(qc - NA_WIN_C // 2, 0, GRID_W - NA_WIN_C)
    col_ok = (kc[None, :] >= ws[:, None]) & (kc[None, :] < ws[:, None] + NA_WIN_C)
    dc = np.clip(kc[None, :] - qc[:, None] + NA_WIN_C - 1, 0, 2 * NA_WIN_C - 2)
    pick = ((dc[:, :, None] == np.arange(2 * NA_WIN_C - 1)) & col_ok[:, :, None]).astype(np.float32)
    tiles = jnp.einsum("hrd,qkd->hrqk", rpb.astype(F32), pick, precision=lax.Precision.HIGHEST)
    tiles = jnp.where(col_ok, tiles, NEG_INF)
    return jnp.concatenate([tiles, tiles], axis=-1)


def _fill_band_bias(tiles_ref, bias_ref, band, rows):
    low = _low_half()
    masked = jnp.full((GRID_W, LANES), NEG_INF, F32)
    r0 = band * NA_BAND_ROWS
    k0 = int(np.clip(r0 - NA_WIN_R // 2, 0, rows - NA_BAND_KEY_ROWS))
    for h in range(NA_HEADS):
        for ri in range(NA_BAND_ROWS):
            r = r0 + ri
            start = int(np.clip(r - NA_WIN_R // 2, 0, rows - NA_WIN_R))
            tile = lambda kr: tiles_ref[h, kr - r + NA_WIN_R - 1] if start <= kr < start + NA_WIN_R else masked
            for m in range(NA_BAND_KEY_ROWS // 2):
                kr = k0 + 2 * m
                bias_ref[h, ri * GRID_W:(ri + 1) * GRID_W, m * LANES:(m + 1) * LANES] = jnp.where(
                    low, tile(kr), tile(kr + 1))


def _gqa_kernel(q_ref, kt_ref, v_ref, ckt_ref, cv_ref, o_ref):
    _gqa_groups(q_ref, 0, o_ref, 0,
                lambda kv: ([kt_ref[kv, :], ckt_ref[kv, :]], [None, None],
                            [_with_ones(v_ref[:, kv]), _with_ones(cv_ref[:, kv])]))


def _latent_attn_kernel(*refs, rows):
    n_na = 9
    n_gqa = 5
    tiles_ref = refs[n_na]
    oa_ref, ob_ref, bias_ref = refs[n_na + 1 + n_gqa:]
    nb = rows // NA_BAND_ROWS
    for band in sorted({0, 1, nb - 1}):
        pl.when(pl.program_id(1) == band)(functools.partial(_fill_band_bias, tiles_ref, bias_ref, band, rows))
    _gqa_kernel(*refs[n_na + 1:n_na + 1 + n_gqa], ob_ref)
    _na_kernel(*refs[:n_na], bias_ref, oa_ref)


def _latent_attention(q, kt, v, na_ctx_kt, na_ctx_v, bias_tiles, gqa_ctx_kt, gqa_ctx_v, batch, seq):
    rows = seq // GRID_W
    nb = rows // NA_BAND_ROWS
    past = na_ctx_v.shape[1]
    assert (rows - NA_BAND_KEY_ROWS) % NA_BAND_ROWS == 0
    n_chunks = NA_BAND_KEY_ROWS // NA_BAND_ROWS
    assert nb >= 3
    chunk0 = lambda j: jnp.clip(j - NA_WIN_R // 2 // NA_BAND_ROWS, 0, nb - n_chunks)
    kt_spec = lambda c: pl.BlockSpec((None, NA_W, NA_BAND_Q), lambda b, j: (b, 0, chunk0(j) + c))
    v_spec = lambda c: pl.BlockSpec((NA_BAND_Q, NA_W), lambda b, j: (b * nb + chunk0(j) + c, 0))
    tile = lambda lane_block: pl.BlockSpec((NA_BAND_Q, NA_W), lambda b, j: (b * nb + j, lane_block))
    dup = 2 * GKV_W
    assert NA_W == GQ_W
    na_specs = ([tile(0)] + [kt_spec(c) for c in range(n_chunks)] + [v_spec(c) for c in range(n_chunks)]
                + [pl.BlockSpec((None, NA_W, past), lambda b, j: (b, 0, 0)),
                   pl.BlockSpec((None, past, NA_W), lambda b, j: (b, 0, 0)),
                   pl.BlockSpec(bias_tiles.shape, lambda b, j: (0, 0, 0, 0))])
    gqa_specs = [tile(1),
                 pl.BlockSpec((None, dup, seq), lambda b, j: (b, NA_W // dup, 0)),
                 pl.BlockSpec((seq, dup), lambda b, j: (b, NA_W // dup)),
                 pl.BlockSpec((None, dup, past), lambda b, j: (b, 0, 0)),
                 pl.BlockSpec((None, past, dup), lambda b, j: (b, 0, 0))]
    return pl.pallas_call(
        functools.partial(_latent_attn_kernel, rows=rows),
        grid=(batch, nb),
        in_specs=na_specs + gqa_specs,
        out_specs=[tile(0), tile(0)],
        out_shape=[jax.ShapeDtypeStruct((batch * seq, NA_W), BF16), jax.ShapeDtypeStruct((batch * seq, GQ_W), BF16)],
        scratch_shapes=[pltpu.VMEM((NA_HEADS, NA_BAND_Q, NA_BAND_K), F32)],
        compiler_params=_params("arbitrary", "arbitrary"),
        name="latent_attention",
    )(q, *([kt] * n_chunks), *([v] * n_chunks), na_ctx_kt, na_ctx_v, bias_tiles, q, kt, v, gqa_ctx_kt, gqa_ctx_v)


def _sgu_kernel(h_ref, mod_ref, g_ref, win_hbm, vg_ref, ws_ref, bs_ref, wout_hbm, o_ref,
                win_ref, wout_ref, stage_in, sem_in, stage_out, sem_out, gated_ref, y_ref, *, layer):
    tm = h_ref.shape[0]
    n_chunks = y_ref.shape[0]
    per_chunk = SGU_ACT_CHUNK // SGU_GW

    def tile(first):
        if first:
            cols = _WeightStream(_col_pairs(win_hbm, layer, win_ref), stage_in, sem_in)
            rows = _WeightStream(_row_pairs(wout_hbm, layer, wout_ref), stage_out, sem_out)
        h = h_ref[...]
        xn = _adaln(h, g_ref[...], mod_ref, 1).astype(BF16)
        ssq = jnp.zeros((tm, 1), F32)
        for c in range(n_chunks):
            if first:
                cols.take(SGU_ACT_CHUNK // W_STAGE)
            y = _dot(xn, win_ref[c])
            y = 0.5 * y * (1.0 + lax.erf(y * math.sqrt(0.5)))
            y_ref[c] = y
            if c >= n_chunks // 2:
                ssq = ssq + jnp.sum(y * y, axis=-1, keepdims=True)
            if first:
                rows.take(1)
        if first:
            rows.take()
        v_scale = lax.rsqrt(ssq * (1.0 / SGU_WIDTH) + EPS)
        for c in range(tm // SGU_CHUNK):
            rs = slice(c * SGU_CHUNK, (c + 1) * SGU_CHUNK)
            for g in range(SGU_GROUPS):
                cs = slice(g * SGU_GW, (g + 1) * SGU_GW)
                in_chunk = slice((g % per_chunk) * SGU_GW, (g % per_chunk + 1) * SGU_GW)
                u = y_ref[g // per_chunk, rs, in_chunk]
                v = y_ref[n_chunks // 2 + g // per_chunk, rs, in_chunk]
                vn = ((v * v_scale[rs]) * vg_ref[:, cs]).astype(BF16)
                sv = _dot(ws_ref[g].astype(BF16), vn) + bs_ref[:, g:g + 1]
                gated_ref[rs, cs] = (u * sv).astype(BF16)
        o_ref[...] = h + _gate(mod_ref, 1) * _dot(gated_ref[...], wout_ref[...])

    pl.when(pl.program_id(0) == 0)(lambda: tile(True))
    pl.when(pl.program_id(0) > 0)(lambda: tile(False))


def _sgu(h, mod, mod_layer, g, w_in, v_g, w_s, b_s_t, w_out, layer, ctx_rows, dec_seq):
    tm = SGU_TM
    n_act = 2 * SGU_WIDTH // SGU_ACT_CHUNK
    _, _, _, cond = _two_stream_maps(tm, ctx_rows, dec_seq)
    row_spec = pl.BlockSpec((tm, D_MODEL), lambda i: (i, 0))
    const = lambda i: (0, 0)
    return pl.pallas_call(
        functools.partial(_sgu_kernel, layer=layer),
        grid=(h.shape[0] // tm,),
        in_specs=[
            row_spec,
            _mod_spec(mod_layer, cond),
            _norm_spec(mod_layer, 1),
            HBM,
            pl.BlockSpec((1, SGU_WIDTH), const),
            pl.BlockSpec((None, SGU_GROUPS, SGU_CHUNK, SGU_CHUNK), lambda i: (layer, 0, 0, 0)),
            pl.BlockSpec((SGU_CHUNK, SGU_GROUPS), const),
            HBM,
        ],
        out_specs=row_spec,
        out_shape=jax.ShapeDtypeStruct(h.shape, F32),
        scratch_shapes=[
            pltpu.VMEM((n_act, D_MODEL, SGU_ACT_CHUNK), BF16),
            pltpu.VMEM((SGU_WIDTH, D_MODEL), BF16),
            *COL_STAGE, *ROW_STAGE,
            pltpu.VMEM((tm, SGU_WIDTH), BF16),
            pltpu.VMEM((n_act, tm, SGU_ACT_CHUNK), F32),
        ],
        compiler_params=_params("arbitrary"),
        name="sgu",
    )(h, mod, g, w_in, v_g, w_s, b_s_t, w_out)


def _rope_tables(seq):
    half = HEAD_DIM // 2
    t = np.arange(seq)
    freqs = np.float32(ROPE_BASE) ** (-np.arange(0, half, 2, dtype=np.float32) / np.float32(half))
    def tab(pos):
        ang = pos.astype(np.float32)[:, None] * freqs[None, :]
        cos, sin = np.cos(ang), np.sin(ang)
        return np.concatenate([cos, cos], -1), np.concatenate([-sin, sin], -1)
    cr, sr = tab(t // GRID_W)
    cc, sc = tab(t % GRID_W)
    cos = np.concatenate([cr, cc], -1)
    sin = np.concatenate([sr, sc], -1)
    return jnp.asarray(np.tile(cos, (1, 2)), F32), jnp.asarray(np.tile(sin, (1, 2)), F32)


def kernel(x_prompt, x_sample, cache_na_k, cache_na_v, cache_gqa_k, cache_gqa_v, c, c_ctx, norm_g, mod_w, mod_b, ffn1_w_gu, ffn1_w_down, ffn2_w_gu, ffn2_w_down, attn_w_in, attn_w_out, na_q_g, na_k_g, na_rpb, gqa_q_g, gqa_k_g, sgu_w_in, sgu_v_g, sgu_w_s, sgu_b_s, sgu_w_out):
    batch, seq, _ = x_prompt.shape
    dec_batch, dec_seq, _ = x_sample.shape
    past = cache_na_k.shape[2]
    ctx_rows, lat_rows = batch * seq, dec_batch * dec_seq
    h = (x_prompt.reshape(ctx_rows, D_MODEL), x_sample.reshape(lat_rows, D_MODEL))

    assert 1 + dec_batch <= 8
    cond8 = jnp.concatenate([c_ctx[None], c, jnp.zeros((8 - 1 - dec_batch, D_MODEL), F32)], axis=0)
    mod = _modulation(cond8, mod_w, mod_b).reshape(DEPTH, 8, N_MOD, D_MODEL)

    g = norm_g.reshape(DEPTH * 3, 1, D_MODEL)
    new_cache = None
    for layer in range(DEPTH):
        h = _ffn(h, mod, g, ffn1_w_gu, ffn1_w_down, layer, 0, ctx_rows, dec_seq)

        if layer % 2 == 0:
            e = layer // 2
            gains = jnp.concatenate([jnp.tile(na_q_g[e], NA_HEADS), jnp.tile(na_k_g[e], NA_HEADS),
                                     jnp.tile(gqa_q_g[e], GQA_HEADS), jnp.tile(gqa_k_g[e], GQA_KV_HEADS)])[None]
            q, kt, v, *new_cache = _qkv(h, 0, ctx_rows, mod, layer, 0, g, attn_w_in, e, gains,
                                        ctx_rows // PROJ_TM, seq, cache_out=True)
            o_ctx = _ctx_attention(q, kt, v, seq)
            q, kt, v = _qkv(h, ctx_rows, lat_rows, mod, layer, 1, g, attn_w_in, e, gains,
                            dec_seq // PROJ_TM, dec_seq, rope_tabs=_rope_tables(dec_seq))
            to_kt = lambda ck, rep: jnp.repeat(jnp.transpose(ck[:, e], (0, 2, 3, 1)), rep, axis=1).reshape(
                dec_batch, -1, past).astype(BF16)
            to_v = lambda cv, rep: jnp.repeat(cv[:, e], rep, axis=2).reshape(dec_batch, past, -1).astype(BF16)
            oa, ob = _latent_attention(q, kt, v, to_kt(cache_na_k, 1), to_v(cache_na_v, 1),
                                       _na_bias_tiles(na_rpb[e]),
                                       to_kt(cache_gqa_k, 2), to_v(cache_gqa_v, 2), dec_batch, dec_seq)
            attn = (o_ctx, oa, ob, attn_w_out, e)
        else:
            attn = None
            o = layer // 2
            h = _sgu(h, mod, layer, g, sgu_w_in, sgu_v_g[o][None], sgu_w_s, sgu_b_s[o].T, sgu_w_out,
                     o, ctx_rows, dec_seq)

        h = _ffn(h, mod, g, ffn2_w_gu, ffn2_w_down, layer, 2, ctx_rows, dec_seq, attn=attn,
                 split_out=layer == DEPTH - 1)
    hp, hs = h

    n_attn = (DEPTH + 1) // 2
    assert n_attn == 1
    from_t = lambda x: jnp.transpose(x.reshape(batch, n_attn, -1, HEAD_DIM, seq), (0, 1, 4, 2, 3))
    return (hp.reshape(batch, seq, D_MODEL), hs.reshape(dec_batch, dec_seq, D_MODEL),
            *(from_t(x) for x in new_cache))
```

```python
import functools
import math

import jax
import jax.numpy as jnp
import numpy as np
from jax import lax
from jax.experimental import pallas as pl
from jax.experimental.pallas import tpu as pltpu

D_MODEL = 1024
DEPTH = 2
GRID_W = 64
HEAD_DIM = 64
NA_HEADS = 8
NA_WIN_R = 8
NA_WIN_C = 16
GQA_HEADS = 8
GQA_KV_HEADS = 2
GQA_GROUP = GQA_HEADS // GQA_KV_HEADS
NA_W = NA_HEADS * HEAD_DIM
GQ_W = GQA_HEADS * HEAD_DIM
GKV_W = GQA_KV_HEADS * HEAD_DIM
QKV_WIDTH = 3 * NA_W + GQ_W + 2 * GKV_W
ROPE_BASE = 10000.0
SGU_CHUNK = 128
SGU_GROUPS = 8
SGU_WIDTH = 2 * D_MODEL
SGU_GW = SGU_WIDTH // SGU_GROUPS
FFN_HIDDEN = 2816
N_MOD = 9
EPS = 1e-6
NEG_INF = -1e30
Q_SCALE = HEAD_DIM ** -0.5

NA_BAND_ROWS = 4
NA_BAND_KEY_ROWS = 12
NA_BAND_Q = NA_BAND_ROWS * GRID_W
NA_BAND_K = NA_BAND_KEY_ROWS * GRID_W
NA_PAIRS_TOGETHER = 2

FFN_TM = 512
FFN_ACT_CHUNK = 256
SGU_ACT_CHUNK = 512
SGU_TM = 512
MOD_TN = 3072
CTX_SEQ_PER_STEP = 4
PROJ_TM = 1024
W_STAGE = 256

LANES = 128
KT_ROWS = NA_W + 2 * GKV_W
MXU_COLS = 256
VMEM_LIMIT = 56 * 1024 * 1024

BF16 = jnp.bfloat16
F32 = jnp.float32


def _dot(a, b):
    return jnp.dot(a, b, preferred_element_type=F32)


def _params(*sem):
    return pltpu.CompilerParams(dimension_semantics=sem, vmem_limit_bytes=VMEM_LIMIT)


def _adaln(x, g, mod_ref, j):
    shift = mod_ref[0, 3 * j:3 * j + 1, :]
    scale = mod_ref[0, 3 * j + 1:3 * j + 2, :]
    y = x * lax.rsqrt(jnp.mean(x * x, axis=-1, keepdims=True) + EPS)
    return y * (g * (1.0 + scale)) + shift


def _gate(mod_ref, j):
    return mod_ref[0, 3 * j + 2:3 * j + 3, :]


def _mod_kernel(c_ref, w_ref, b_ref, o_ref):
    c = c_ref[...]
    a = (c * jax.nn.sigmoid(c)).astype(BF16)
    o_ref[...] = _dot(a, w_ref[...].astype(BF16)) + b_ref[...]


def _modulation(cond8, mod_w, mod_b):
    tn = MOD_TN
    n = N_MOD * D_MODEL
    return pl.pallas_call(
        _mod_kernel,
        grid=(DEPTH, n // tn),
        in_specs=[
            pl.BlockSpec((8, D_MODEL), lambda l, k: (0, 0)),
            pl.BlockSpec((None, D_MODEL, tn), lambda l, k: (l, 0, k)),
            pl.BlockSpec((None, 1, tn), lambda l, k: (l, 0, k)),
        ],
        out_specs=pl.BlockSpec((None, 8, tn), lambda l, k: (l, 0, k)),
        out_shape=jax.ShapeDtypeStruct((DEPTH, 8, n), F32),
        compiler_params=_params("arbitrary", "arbitrary"),
        name="modulation",
    )(cond8, mod_w, mod_b.reshape(DEPTH, 1, n))


def _two_stream_maps(tm, n_ctx_rows, dec_seq):
    n_ctx = n_ctx_rows // tm
    per_seq = dec_seq // tm
    ctx_map = lambda i: (jnp.minimum(i, n_ctx - 1), 0)
    lat_map = lambda i: (jnp.maximum(i - n_ctx, 0), 0)
    cond = lambda i: jnp.where(i < n_ctx, 0, 1 + (i - n_ctx) // per_seq)
    return n_ctx, ctx_map, lat_map, cond


def _norm_spec(layer, j):
    return pl.BlockSpec((None, 1, D_MODEL), lambda i: (3 * layer + j, 0, 0))


def _mod_spec(layer, cond):
    return pl.BlockSpec((None, 1, N_MOD, D_MODEL), lambda i: (layer, cond(i), 0, 0))


class _WeightStream:
    def __init__(self, pairs, stage_ref, sem_ref):
        self.pairs, self.stage, self.sem, self.done = pairs, stage_ref, sem_ref, 0
        for k in range(min(2, len(pairs))):
            self._copy(k).start()

    def _copy(self, k):
        return pltpu.make_async_copy(self.pairs[k][0], self.stage.at[k % 2], self.sem.at[k % 2])

    def take(self, n=None):
        n = len(self.pairs) - self.done if n is None else min(n, len(self.pairs) - self.done)
        for _ in range(n):
            k = self.done
            self._copy(k).wait()
            dst, idx = self.pairs[k][1]
            dst[idx] = self.stage[k % 2].astype(BF16)
            if k + 2 < len(self.pairs):
                self._copy(k + 2).start()
            self.done += 1


def _col_pairs(w_hbm, layer, w_ref, cols=None):
    if len(w_ref.shape) == 2:
        dst = lambda c: (slice(None), slice(c, c + W_STAGE))
        n = w_ref.shape[1]
    else:
        width = w_ref.shape[2]
        dst = lambda c: (c // width, slice(None), slice(c % width, c % width + W_STAGE))
        n = w_ref.shape[0] * width
    cols = range(0, n, W_STAGE) if cols is None else cols
    return [(w_hbm.at[layer, :, pl.ds(c, W_STAGE)], (w_ref, dst(c))) for c in cols]


def _row_pairs(w_hbm, layer, w_ref):
    return [(w_hbm.at[layer, pl.ds(r, W_STAGE), :], (w_ref, (slice(r, r + W_STAGE), slice(None))))
            for r in range(0, w_ref.shape[0], W_STAGE)]


def _load_by_cols(w_hbm, layer, w_ref, stage_ref, sem_ref):
    _WeightStream(_col_pairs(w_hbm, layer, w_ref), stage_ref, sem_ref).take()


COL_STAGE = (pltpu.VMEM((2, D_MODEL, W_STAGE), F32), pltpu.SemaphoreType.DMA((2,)))
ROW_STAGE = (pltpu.VMEM((2, W_STAGE, D_MODEL), F32), pltpu.SemaphoreType.DMA((2,)))
HBM = pl.BlockSpec(memory_space=pl.ANY)


def _ffn_kernel(*refs, layer, j, n_ctx, proj_layer, split_in, split_out):
    h_refs = refs[:2 if split_in else 1]
    pos = len(h_refs)
    mod_ref, g_ref, wgu_hbm, wd_hbm = refs[pos:pos + 4]
    pos += 4
    if proj_layer is not None:
        octx_ref, oa_ref, ob_ref, wo_hbm = refs[pos:pos + 4]
        pos += 4
    out_refs = refs[pos:pos + (2 if split_out else 1)]
    pos += len(out_refs)
    wgu_ref, wd_ref, stage_gu, sem_gu, stage_d, sem_d, act_ref = refs[pos:pos + 7]
    wo_ref = refs[pos + 7] if proj_layer is not None else None
    i = pl.program_id(0)
    is_ctx = i < n_ctx
    chunks = [(c, min(FFN_ACT_CHUNK, FFN_HIDDEN - c)) for c in range(0, FFN_HIDDEN, FFN_ACT_CHUNK)]

    def tile(first):
        if first:
            need = [col for c, w in chunks for base in (c, FFN_HIDDEN + c) for col in range(base, base + w, W_STAGE)]
            cols = _WeightStream(_col_pairs(wgu_hbm, layer, wgu_ref, need), stage_gu, sem_gu)
            proj = _row_pairs(wo_hbm, proj_layer, wo_ref) if proj_layer is not None else []
            rows = _WeightStream(proj + _row_pairs(wd_hbm, layer, wd_ref), stage_d, sem_d)
            rows.take(len(proj))
        if first or not split_in:
            h = h_refs[0][...]
        else:
            h = jnp.where(is_ctx, h_refs[0][...], h_refs[1][...])
        if proj_layer is not None:
            latent_heads = jnp.concatenate([oa_ref[...], ob_ref[...]], axis=1)
            heads = octx_ref[...] if first else jnp.where(is_ctx, octx_ref[...], latent_heads)
            h = h + _gate(mod_ref, 1) * _dot(heads, wo_ref[...])
        xn = _adaln(h, g_ref[...], mod_ref, j).astype(BF16)
        for c, w in chunks:
            if first:
                cols.take(2 * w // W_STAGE)
            gate = _dot(xn, wgu_ref[:, c:c + w])
            up = _dot(xn, wgu_ref[:, FFN_HIDDEN + c:FFN_HIDDEN + c + w])
            act_ref[:, c:c + w] = ((gate * jax.nn.sigmoid(gate)) * up).astype(BF16)
            if first:
                rows.take(2)
        if first:
            rows.take()
        res = h + (0.5 * _gate(mod_ref, j)) * _dot(act_ref[...], wd_ref[...])
        if first or not split_out:
            out_refs[0][...] = res
        else:
            @pl.when(is_ctx)
            def _():
                out_refs[0][...] = res

            @pl.when(jnp.logical_not(is_ctx))
            def _():
                out_refs[1][...] = res

    pl.when(i == 0)(lambda: tile(True))
    pl.when(i > 0)(lambda: tile(False))


def _ffn(h, mod, g, w_gu, w_down, layer, j, ctx_rows, dec_seq, attn=None, split_out=False):
    tm = FFN_TM
    split_in = isinstance(h, tuple)
    h_arrays = list(h) if split_in else [h]
    total_rows = sum(a.shape[0] for a in h_arrays)
    n_ctx, ctx_map, lat_map, cond = _two_stream_maps(tm, ctx_rows, dec_seq)
    assert n_ctx >= 1
    row_map = lambda i: (i, 0)
    row_spec = lambda m, width=D_MODEL: pl.BlockSpec((tm, width), m)
    in_specs = ([row_spec(ctx_map), row_spec(lat_map)] if split_in else [row_spec(row_map)]) + [
        _mod_spec(layer, cond),
        _norm_spec(layer, j),
        HBM, HBM]
    args = [*h_arrays, mod, g, w_gu, w_down]
    scratch = [pltpu.VMEM((D_MODEL, 2 * FFN_HIDDEN), BF16), pltpu.VMEM((FFN_HIDDEN, D_MODEL), BF16),
               *COL_STAGE, *ROW_STAGE, pltpu.VMEM((tm, FFN_HIDDEN), BF16)]
    proj_layer = None
    if attn is not None:
        o_ctx, oa, ob, w_out, proj_layer = attn
        in_specs += [row_spec(ctx_map, o_ctx.shape[1]), row_spec(lat_map, oa.shape[1]),
                     row_spec(lat_map, ob.shape[1]), HBM]
        args += [o_ctx, oa, ob, w_out]
        scratch.append(pltpu.VMEM((o_ctx.shape[1], D_MODEL), BF16))
    if split_out:
        out_specs = [row_spec(ctx_map), row_spec(lat_map)]
        out_shape = [jax.ShapeDtypeStruct((ctx_rows, D_MODEL), F32),
                     jax.ShapeDtypeStruct((total_rows - ctx_rows, D_MODEL), F32)]
    else:
        out_specs = row_spec(row_map)
        out_shape = jax.ShapeDtypeStruct((total_rows, D_MODEL), F32)
    return pl.pallas_call(
        functools.partial(_ffn_kernel, layer=layer, j=j, n_ctx=n_ctx, proj_layer=proj_layer,
                          split_in=split_in, split_out=split_out),
        grid=(total_rows // tm,),
        in_specs=in_specs,
        out_specs=out_specs,
        out_shape=out_shape,
        scratch_shapes=scratch,
        compiler_params=_params("arbitrary"),
        name=f"ffn{j}_l{layer}",
    )(*args)


def _head_mean_sq(x):
    w = x.shape[-1]
    r = lax.broadcasted_iota(jnp.int32, (MXU_COLS, MXU_COLS), 0) // HEAD_DIM
    c = lax.broadcasted_iota(jnp.int32, (MXU_COLS, MXU_COLS), 1) // HEAD_DIM
    bd = jnp.where(r == c, 1.0 / HEAD_DIM, 0.0).astype(BF16)
    sq = (x * x).astype(BF16)
    parts = []
    for s in range(0, w, MXU_COLS):
        e = min(s + MXU_COLS, w)
        parts.append(_dot(sq[:, s:e], bd[:e - s, :e - s]))
    return parts[0] if len(parts) == 1 else jnp.concatenate(parts, axis=-1)


def _low_half():
    return lax.broadcasted_iota(jnp.int32, (1, LANES), 1) < HEAD_DIM


def _head_rms(x, gain):
    return (x * lax.rsqrt(_head_mean_sq(x) + EPS)) * gain


def _rope(x, cos, sin_signed):
    lanes = cos.shape[-1]
    outs = []
    for s in range(0, x.shape[-1], lanes):
        xs = x[:, s:s + lanes]
        blk = lax.broadcasted_iota(jnp.int32, xs.shape, 1) // (HEAD_DIM // 4)
        partner = jnp.where(blk % 2 == 0,
                            pltpu.roll(xs, lanes - HEAD_DIM // 4, 1),
                            pltpu.roll(xs, HEAD_DIM // 4, 1))
        outs.append(xs * cos + partner * sin_signed)
    return outs[0] if len(outs) == 1 else jnp.concatenate(outs, axis=-1)


def _qkv_kernel(*refs, layer, rope, cache_out):
    h_ref, mod_ref, g_ref, w_hbm, gain_ref = refs[:5]
    w_ref, stage_ref, sem_ref = refs[-3:]
    pos = 5
    if rope:
        cos_ref, sin_ref = refs[pos:pos + 2]
        pos += 2
    q_ref, kt_ref, v_ref = refs[pos:pos + 3]
    pos += 3

    @pl.when(pl.program_id(0) == 0)
    def _load_weights():
        _load_by_cols(w_hbm, layer, w_ref, stage_ref, sem_ref)

    xn = _adaln(h_ref[...], g_ref[...], mod_ref, 1).astype(BF16)
    y = _dot(xn, w_ref[...])
    o = 0
    qa = _head_rms(y[:, o:o + NA_W], gain_ref[:, :NA_W]); o += NA_W
    ka = _head_rms(y[:, o:o + NA_W], gain_ref[:, NA_W:2 * NA_W]); o += NA_W
    va = y[:, o:o + NA_W]; o += NA_W
    qb = _head_rms(y[:, o:o + GQ_W], gain_ref[:, 2 * NA_W:2 * NA_W + GQ_W]); o += GQ_W
    kb = _head_rms(y[:, o:o + GKV_W], gain_ref[:, 2 * NA_W + GQ_W:]); o += GKV_W
    vb = y[:, o:o + GKV_W]
    if rope:
        cos, sin = cos_ref[...], sin_ref[...]
        qb = _rope(qb, cos, sin)
        kb = _rope(kb, cos, sin)
    q_ref[:, :NA_W] = (qa * Q_SCALE).astype(BF16)
    q_ref[:, NA_W:] = (qb * Q_SCALE).astype(BF16)
    low = _low_half()
    vb_swapped = pltpu.roll(vb, HEAD_DIM, 1)
    v_ref[:, :NA_W] = va.astype(BF16)
    v_ref[:, NA_W:NA_W + LANES] = jnp.where(low, vb, vb_swapped).astype(BF16)
    v_ref[:, NA_W + LANES:] = jnp.where(low, vb_swapped, vb).astype(BF16)
    n_seq, _, s = kt_ref.shape
    for i in range(n_seq):
        rows = slice(i * s, (i + 1) * s)
        ka_t = ka[rows].T
        kb_t = kb[rows].T
        k0, k1 = kb_t[:HEAD_DIM], kb_t[HEAD_DIM:]
        kt_ref[i, :NA_W, :] = ka_t.astype(BF16)
        kt_ref[i, NA_W:, :] = jnp.concatenate([k0, k0, k1, k1], axis=0).astype(BF16)
        if cache_out:
            kat_ref, vat_ref, kbt_ref, vbt_ref = refs[pos:pos + 4]
            kat_ref[i] = ka_t
            vat_ref[i] = va[rows].T
            kbt_ref[i] = kb_t
            vbt_ref[i] = vb[rows].T


def _qkv(h, row0, t, mod, mod_layer, cond0, g, w_in, layer, gains, cond_div, seq, rope_tabs=None,
         cache_out=False):
    tm = PROJ_TM
    rope = rope_tabs is not None
    assert not (rope and cache_out)
    row = lambda i: (i, 0)
    const = lambda i: (0, 0)
    in_specs = [
        pl.BlockSpec((tm, D_MODEL), lambda i: (i + row0 // tm, 0)),
        _mod_spec(mod_layer, lambda i: cond0 + i // cond_div),
        _norm_spec(mod_layer, 1),
        HBM,
        pl.BlockSpec((1, 2 * NA_W + GQ_W + GKV_W), const),
    ]
    args = [h, mod, g, w_in, gains]
    if rope:
        seq_tiles = rope_tabs[0].shape[0] // tm
        in_specs += [pl.BlockSpec((tm, 2 * HEAD_DIM), lambda i: (i % seq_tiles, 0))] * 2
        args += list(rope_tabs)
    if seq >= tm:
        per_seq = seq // tm
        t_block = lambda rows: pl.BlockSpec((1, rows, tm), lambda i: (i // per_seq, 0, i % per_seq))
    else:
        t_block = lambda rows: pl.BlockSpec((tm // seq, rows, seq), lambda i: (i, 0, 0))
    t_shape = lambda rows, dtype: jax.ShapeDtypeStruct((t // seq, rows, seq), dtype)
    out_specs = [pl.BlockSpec((tm, NA_W + GQ_W), row), t_block(KT_ROWS), pl.BlockSpec((tm, KT_ROWS), row)]
    out_shape = [jax.ShapeDtypeStruct((t, NA_W + GQ_W), BF16), t_shape(KT_ROWS, BF16),
                 jax.ShapeDtypeStruct((t, KT_ROWS), BF16)]
    if cache_out:
        out_specs += [t_block(NA_W), t_block(NA_W), t_block(GKV_W), t_block(GKV_W)]
        out_shape += [t_shape(NA_W, F32), t_shape(NA_W, F32), t_shape(GKV_W, F32), t_shape(GKV_W, F32)]
    return pl.pallas_call(
        functools.partial(_qkv_kernel, layer=layer, rope=rope, cache_out=cache_out),
        grid=(t // tm,),
        in_specs=in_specs,
        out_specs=out_specs,
        out_shape=out_shape,
        scratch_shapes=[pltpu.VMEM((D_MODEL, QKV_WIDTH), BF16), *COL_STAGE],
        compiler_params=_params("arbitrary"),
        name="qkv_latent" if rope else "qkv_context",
    )(*args)


def _split_pair(q2):
    low = _low_half()
    zero = jnp.zeros((), q2.dtype)
    return jnp.where(low, q2, zero), jnp.where(low, zero, q2)


def _with_ones(v2):
    return jnp.concatenate([v2, jnp.ones(v2.shape, v2.dtype)], axis=1)


def _join(blocks, axis):
    return blocks[0] if len(blocks) == 1 else jnp.concatenate(blocks, axis=axis)


def _attend(problems):
    scores = [_dot(qm, _join(kts, 1)) for qm, kts, _, _ in problems]
    probs = []
    for s, (_, kts, biases, _) in zip(scores, problems):
        parts, start = [], 0
        for kt, b in zip(kts, biases):
            part = s[:, start:start + kt.shape[1]]
            parts.append(part if b is None else part + b)
            start += kt.shape[1]
        m = functools.reduce(jnp.maximum, [jnp.max(part, axis=-1, keepdims=True) for part in parts])
        probs.append(_join([jnp.exp(part - m).astype(BF16) for part in parts], 1))
    return [_dot(p, _join(vexts, 0)) for p, (_, _, _, vexts) in zip(probs, problems)]


def _merge_pair(oe_even, oe_odd):
    even = oe_even[:, :LANES] * (1.0 / oe_even[:, LANES:])
    odd = oe_odd[:, :LANES] * (1.0 / oe_odd[:, LANES:])
    return jnp.where(_low_half(), even, odd)


def _gqa_groups(q_ref, q_off, o_ref, o_off, keys_values):
    tq = q_ref.shape[0]
    slabs = lambda g, off: [slice(off + (2 * g + j) * LANES, off + (2 * g + j + 1) * LANES) for j in range(2)]
    problems = []
    for g in range(GQA_KV_HEADS):
        qm = jnp.concatenate([part for sl in slabs(g, q_off) for part in _split_pair(q_ref[:, sl])], axis=0)
        problems.append((qm, *keys_values(slice(g * LANES, (g + 1) * LANES))))
    for g, oe in enumerate(_attend(problems)):
        for j, sl in enumerate(slabs(g, o_off)):
            o_ref[:, sl] = _merge_pair(oe[2 * j * tq:(2 * j + 1) * tq],
                                       oe[(2 * j + 1) * tq:(2 * j + 2) * tq]).astype(o_ref.dtype)


def _ctx_attn_kernel(q_blk, kt_blk, v_blk, o_blk):
    n_seq, _, seq = kt_blk.shape
    for b in range(n_seq):
        rows = pl.ds(b * seq, seq)
        q_ref, v_ref, o_ref, kt_ref = q_blk.at[rows], v_blk.at[rows], o_blk.at[rows], kt_blk.at[b]
        for p in range(NA_HEADS // 2):
            sl = slice(p * LANES, (p + 1) * LANES)
            vext = _with_ones(v_ref[:, sl])
            oe = [_attend([(qm, [kt_ref[sl, :]], [None], [vext])])[0] for qm in _split_pair(q_ref[:, sl])]
            o_ref[:, sl] = _merge_pair(*oe).astype(o_ref.dtype)
        _gqa_groups(q_ref, NA_W, o_ref, NA_W,
                    lambda kv: ([kt_ref[NA_W + kv.start:NA_W + kv.stop, :]], [None],
                                [_with_ones(v_ref[:, NA_W + kv.start:NA_W + kv.stop])]))


def _ctx_attention(q, kt, v, seq):
    t = q.shape[0]
    n_seq = CTX_SEQ_PER_STEP
    row = lambda b: (b, 0)
    return pl.pallas_call(
        _ctx_attn_kernel,
        grid=(t // (n_seq * seq),),
        in_specs=[pl.BlockSpec((n_seq * seq, NA_W + GQ_W), row),
                  pl.BlockSpec((n_seq, KT_ROWS, seq), lambda b: (b, 0, 0)),
                  pl.BlockSpec((n_seq * seq, KT_ROWS), row)],
        out_specs=pl.BlockSpec((n_seq * seq, NA_W + GQ_W), row),
        out_shape=jax.ShapeDtypeStruct((t, NA_W + GQ_W), BF16),
        compiler_params=_params("arbitrary"),
        name="ctx_attention",
    )(q, kt, v)


def _na_kernel(q_ref, kt0_ref, kt1_ref, kt2_ref, v0_ref, v1_ref, v2_ref, ckt_ref, cv_ref, bias_ref, o_ref):
    for p0 in range(0, NA_HEADS // 2, NA_PAIRS_TOGETHER):
        problems, slabs = [], []
        for p in range(p0, p0 + NA_PAIRS_TOGETHER):
            sl = slice(p * LANES, (p + 1) * LANES)
            kt_loc = jnp.concatenate([kt0_ref[sl, :], kt1_ref[sl, :], kt2_ref[sl, :]], axis=1)
            v_loc = _with_ones(jnp.concatenate([v0_ref[:, sl], v1_ref[:, sl], v2_ref[:, sl]], axis=0))
            v_ctx = _with_ones(cv_ref[:, sl])
            problems += [(qm, [kt_loc, ckt_ref[sl, :]], [bias_ref[2 * p + half], None], [v_loc, v_ctx])
                         for half, qm in enumerate(_split_pair(q_ref[:, sl]))]
            slabs.append(sl)
        oe = _attend(problems)
        for k, sl in enumerate(slabs):
            o_ref[:, sl] = _merge_pair(oe[2 * k], oe[2 * k + 1]).astype(o_ref.dtype)


def _na_bias_tiles(rpb):
    qc = np.arange(GRID_W)
    kc = np.arange(GRID_W)
    ws = np.clip(qc - NA_WIN_C // 2, 0, GRID_W - NA_WIN_C)
    col_ok = (kc[None, :] >= ws[:, None]) & (kc[None, :] < ws[:, None] + NA_WIN_C)
    dc = np.clip(kc[None, :] - qc[:, None] + NA_WIN_C - 1, 0, 2 * NA_WIN_C - 2)
    pick = ((dc[:, :, None] == np.arange(2 * NA_WIN_C - 1)) & col_ok[:, :, None]).astype(np.float32)
    tiles = jnp.einsum("hrd,qkd->hrqk", rpb.astype(F32), pick, precision=lax.Precision.HIGHEST)
    tiles = jnp.where(col_ok, tiles, NEG_INF)
    return jnp.concatenate([tiles, tiles], axis=-1)


def _fill_band_bias(tiles_ref, bias_ref, band, rows):
    low = _low_half()
    masked = jnp.full((GRID_W, LANES), NEG_INF, F32)
    r0 = band * NA_BAND_ROWS
    k0 = int(np.clip(r0 - NA_WIN_R // 2, 0, rows - NA_BAND_KEY_ROWS))
    for h in range(NA_HEADS):
        for ri in range(NA_BAND_ROWS):
            r = r0 + ri
            start = int(np.clip(r - NA_WIN_R // 2, 0, rows - NA_WIN_R))
            tile = lambda kr: tiles_ref[h, kr - r + NA_WIN_R - 1] if start <= kr < start + NA_WIN_R else masked
            for m in range(NA_BAND_KEY_ROWS // 2):
                kr = k0 + 2 * m
                bias_ref[h, ri * GRID_W:(ri + 1) * GRID_W, m * LANES:(m + 1) * LANES] = jnp.where(
                    low, tile(kr), tile(kr + 1))


def _gqa_kernel(q_ref, kt_ref, v_ref, ckt_ref, cv_ref, o_ref):
    _gqa_groups(q_ref, 0, o_ref, 0,
                lambda kv: ([kt_ref[kv, :], ckt_ref[kv, :]], [None, None],
                            [_with_ones(v_ref[:, kv]), _with_ones(cv_ref[:, kv])]))


def _latent_attn_kernel(*refs, rows):
    n_na = 9
    n_gqa = 5
    tiles_ref = refs[n_na]
    oa_ref, ob_ref, bias_ref = refs[n_na + 1 + n_gqa:]
    nb = rows // NA_BAND_ROWS
    for band in sorted({0, 1, nb - 1}):
        pl.when(pl.program_id(1) == band)(functools.partial(_fill_band_bias, tiles_ref, bias_ref, band, rows))
    _gqa_kernel(*refs[n_na + 1:n_na + 1 + n_gqa], ob_ref)
    _na_kernel(*refs[:n_na], bias_ref, oa_ref)


def _latent_attention(q, kt, v, na_ctx_kt, na_ctx_v, bias_tiles, gqa_ctx_kt, gqa_ctx_v, batch, seq):
    rows = seq // GRID_W
    nb = rows // NA_BAND_ROWS
    past = na_ctx_v.shape[1]
    assert (rows - NA_BAND_KEY_ROWS) % NA_BAND_ROWS == 0
    n_chunks = NA_BAND_KEY_ROWS // NA_BAND_ROWS
    assert nb >= 3
    chunk0 = lambda j: jnp.clip(j - NA_WIN_R // 2 // NA_BAND_ROWS, 0, nb - n_chunks)
    kt_spec = lambda c: pl.BlockSpec((None, NA_W, NA_BAND_Q), lambda b, j: (b, 0, chunk0(j) + c))
    v_spec = lambda c: pl.BlockSpec((NA_BAND_Q, NA_W), lambda b, j: (b * nb + chunk0(j) + c, 0))
    tile = lambda lane_block: pl.BlockSpec((NA_BAND_Q, NA_W), lambda b, j: (b * nb + j, lane_block))
    dup = 2 * GKV_W
    assert NA_W == GQ_W
    na_specs = ([tile(0)] + [kt_spec(c) for c in range(n_chunks)] + [v_spec(c) for c in range(n_chunks)]
                + [pl.BlockSpec((None, NA_W, past), lambda b, j: (b, 0, 0)),
                   pl.BlockSpec((None, past, NA_W), lambda b, j: (b, 0, 0)),
                   pl.BlockSpec(bias_tiles.shape, lambda b, j: (0, 0, 0, 0))])
    gqa_specs = [tile(1),
                 pl.BlockSpec((None, dup, seq), lambda b, j: (b, NA_W // dup, 0)),
                 pl.BlockSpec((seq, dup), lambda b, j: (b, NA_W // dup)),
                 pl.BlockSpec((None, dup, past), lambda b, j: (b, 0, 0)),
                 pl.BlockSpec((None, past, dup), lambda b, j: (b, 0, 0))]
    return pl.pallas_call(
        functools.partial(_latent_attn_kernel, rows=rows),
        grid=(batch, nb),
        in_specs=na_specs + gqa_specs,
        out_specs=[tile(0), tile(0)],
        out_shape=[jax.ShapeDtypeStruct((batch * seq, NA_W), BF16), jax.ShapeDtypeStruct((batch * seq, GQ_W), BF16)],
        scratch_shapes=[pltpu.VMEM((NA_HEADS, NA_BAND_Q, NA_BAND_K), F32)],
        compiler_params=_params("arbitrary", "arbitrary"),
        name="latent_attention",
    )(q, *([kt] * n_chunks), *([v] * n_chunks), na_ctx_kt, na_ctx_v, bias_tiles, q, kt, v, gqa_ctx_kt, gqa_ctx_v)


def _sgu_kernel(h_ref, mod_ref, g_ref, win_hbm, vg_ref, ws_ref, bs_ref, wout_hbm, o_ref,
                win_ref, wout_ref, stage_in, sem_in, stage_out, sem_out, gated_ref, y_ref, *, layer):
    tm = h_ref.shape[0]
    n_chunks = y_ref.shape[0]
    per_chunk = SGU_ACT_CHUNK // SGU_GW

    def tile(first):
        if first:
            cols = _WeightStream(_col_pairs(win_hbm, layer, win_ref), stage_in, sem_in)
            rows = _WeightStream(_row_pairs(wout_hbm, layer, wout_ref), stage_out, sem_out)
        h = h_ref[...]
        xn = _adaln(h, g_ref[...], mod_ref, 1).astype(BF16)
        ssq = jnp.zeros((tm, 1), F32)
        for c in range(n_chunks):
            if first:
                cols.take(SGU_ACT_CHUNK // W_STAGE)
            y = _dot(xn, win_ref[c])
            y = 0.5 * y * (1.0 + lax.erf(y * math.sqrt(0.5)))
            y_ref[c] = y
            if c >= n_chunks // 2:
                ssq = ssq + jnp.sum(y * y, axis=-1, keepdims=True)
            if first:
                rows.take(1)
        if first:
            rows.take()
        v_scale = lax.rsqrt(ssq * (1.0 / SGU_WIDTH) + EPS)
        for c in range(tm // SGU_CHUNK):
            rs = slice(c * SGU_CHUNK, (c + 1) * SGU_CHUNK)
            for g in range(SGU_GROUPS):
                cs = slice(g * SGU_GW, (g + 1) * SGU_GW)
                in_chunk = slice((g % per_chunk) * SGU_GW, (g % per_chunk + 1) * SGU_GW)
                u = y_ref[g // per_chunk, rs, in_chunk]
                v = y_ref[n_chunks // 2 + g // per_chunk, rs, in_chunk]
                vn = ((v * v_scale[rs]) * vg_ref[:, cs]).astype(BF16)
                sv = _dot(ws_ref[g].astype(BF16), vn) + bs_ref[:, g:g + 1]
                gated_ref[rs, cs] = (u * sv).astype(BF16)
        o_ref[...] = h + _gate(mod_ref, 1) * _dot(gated_ref[...], wout_ref[...])

    pl.when(pl.program_id(0) == 0)(lambda: tile(True))
    pl.when(pl.program_id(0) > 0)(lambda: tile(False))


def _sgu(h, mod, mod_layer, g, w_in, v_g, w_s, b_s_t, w_out, layer, ctx_rows, dec_seq):
    tm = SGU_TM
    n_act = 2 * SGU_WIDTH // SGU_ACT_CHUNK
    _, _, _, cond = _two_stream_maps(tm, ctx_rows, dec_seq)
    row_spec = pl.BlockSpec((tm, D_MODEL), lambda i: (i, 0))
    const = lambda i: (0, 0)
    return pl.pallas_call(
        functools.partial(_sgu_kernel, layer=layer),
        grid=(h.shape[0] // tm,),
        in_specs=[
            row_spec,
            _mod_spec(mod_layer, cond),
            _norm_spec(mod_layer, 1),
            HBM,
            pl.BlockSpec((1, SGU_WIDTH), const),
            pl.BlockSpec((None, SGU_GROUPS, SGU_CHUNK, SGU_CHUNK), lambda i: (layer, 0, 0, 0)),
            pl.BlockSpec((SGU_CHUNK, SGU_GROUPS), const),
            HBM,
        ],
        out_specs=row_spec,
        out_shape=jax.ShapeDtypeStruct(h.shape, F32),
        scratch_shapes=[
            pltpu.VMEM((n_act, D_MODEL, SGU_ACT_CHUNK), BF16),
            pltpu.VMEM((SGU_WIDTH, D_MODEL), BF16),
            *COL_STAGE, *ROW_STAGE,
            pltpu.VMEM((tm, SGU_WIDTH), BF16),
            pltpu.VMEM((n_act, tm, SGU_ACT_CHUNK), F32),
        ],
        compiler_params=_params("arbitrary"),
        name="sgu",
    )(h, mod, g, w_in, v_g, w_s, b_s_t, w_out)


def _rope_tables(seq):
    half = HEAD_DIM // 2
    t = np.arange(seq)
    freqs = np.float32(ROPE_BASE) ** (-np.arange(0, half, 2, dtype=np.float32) / np.float32(half))
    def tab(pos):
        ang = pos.astype(np.float32)[:, None] * freqs[None, :]
        cos, sin = np.cos(ang), np.sin(ang)
        return np.concatenate([cos, cos], -1), np.concatenate([-sin, sin], -1)
    cr, sr = tab(t // GRID_W)
    cc, sc = tab(t % GRID_W)
    cos = np.concatenate([cr, cc], -1)
    sin = np.concatenate([sr, sc], -1)
    return jnp.asarray(np.tile(cos, (1, 2)), F32), jnp.asarray(np.tile(sin, (1, 2)), F32)


def kernel(x_prompt, x_sample, cache_na_k, cache_na_v, cache_gqa_k, cache_gqa_v, c, c_ctx, norm_g, mod_w, mod_b, ffn1_w_gu, ffn1_w_down, ffn2_w_gu, ffn2_w_down, attn_w_in, attn_w_out, na_q_g, na_k_g, na_rpb, gqa_q_g, gqa_k_g, sgu_w_in, sgu_v_g, sgu_w_s, sgu_b_s, sgu_w_out):
    batch, seq, _ = x_prompt.shape
    dec_batch, dec_seq, _ = x_sample.shape
    past = cache_na_k.shape[2]
    ctx_rows, lat_rows = batch * seq, dec_batch * dec_seq
    h = (x_prompt.reshape(ctx_rows, D_MODEL), x_sample.reshape(lat_rows, D_MODEL))

    assert 1 + dec_batch <= 8
    cond8 = jnp.concatenate([c_ctx[None], c, jnp.zeros((8 - 1 - dec_batch, D_MODEL), F32)], axis=0)
    mod = _modulation(cond8, mod_w, mod_b).reshape(DEPTH, 8, N_MOD, D_MODEL)

    g = norm_g.reshape(DEPTH * 3, 1, D_MODEL)
    new_cache = None
    for layer in range(DEPTH):
        h = _ffn(h, mod, g, ffn1_w_gu, ffn1_w_down, layer, 0, ctx_rows, dec_seq)

        if layer % 2 == 0:
            e = layer // 2
            gains = jnp.concatenate([jnp.tile(na_q_g[e], NA_HEADS), jnp.tile(na_k_g[e], NA_HEADS),
                                     jnp.tile(gqa_q_g[e], GQA_HEADS), jnp.tile(gqa_k_g[e], GQA_KV_HEADS)])[None]
            q, kt, v, *new_cache = _qkv(h, 0, ctx_rows, mod, layer, 0, g, attn_w_in, e, gains,
                                        ctx_rows // PROJ_TM, seq, cache_out=True)
            o_ctx = _ctx_attention(q, kt, v, seq)
            q, kt, v = _qkv(h, ctx_rows, lat_rows, mod, layer, 1, g, attn_w_in, e, gains,
                            dec_seq // PROJ_TM, dec_seq, rope_tabs=_rope_tables(dec_seq))
            to_kt = lambda ck, rep: jnp.repeat(jnp.transpose(ck[:, e], (0, 2, 3, 1)), rep, axis=1).reshape(
                dec_batch, -1, past).astype(BF16)
            to_v = lambda cv, rep: jnp.repeat(cv[:, e], rep, axis=2).reshape(dec_batch, past, -1).astype(BF16)
            oa, ob = _latent_attention(q, kt, v, to_kt(cache_na_k, 1), to_v(cache_na_v, 1),
                                       _na_bias_tiles(na_rpb[e]),
                                       to_kt(cache_gqa_k, 2), to_v(cache_gqa_v, 2), dec_batch, dec_seq)
            attn = (o_ctx, oa, ob, attn_w_out, e)
        else:
            attn = None
            o = layer // 2
            h = _sgu(h, mod, layer, g, sgu_w_in, sgu_v_g[o][None], sgu_w_s, sgu_b_s[o].T, sgu_w_out,
                     o, ctx_rows, dec_seq)

        h = _ffn(h, mod, g, ffn2_w_gu, ffn2_w_down, layer, 2, ctx_rows, dec_seq, attn=attn,
                 split_out=layer == DEPTH - 1)
    hp, hs = h

    n_attn = (DEPTH + 1) // 2
    assert n_attn == 1
    from_t = lambda x: jnp.transpose(x.reshape(batch, n_attn, -1, HEAD_DIM, seq), (0, 1, 4, 2, 3))
    return (hp.reshape(batch, seq, D_MODEL), hs.reshape(dec_batch, dec_seq, D_MODEL),
            *(from_t(x) for x in new_cache))
```
